```python
import math
import jax, jax.numpy as jnp
from jax import lax
import numpy as np

D_MODEL = 1024
BATCH = 1
SEQ = 16384
DEPTH = 1

N_HEADS = 8
HEAD_DIM = 64
ATTN_WIDTH = N_HEADS * HEAD_DIM
ATTN_SCALE = HEAD_DIM ** -0.5
IDX_HEADS = 8
IDX_DIM = 64
IDX_SCALE = (IDX_HEADS ** -0.5) * (IDX_DIM ** -0.5)
TOPK_MAX = 256
Q_BLOCK = 128
CONV_CH = 512
CONV_WIDTH = 31
N_BRANCH = 2
N_BUCKETS = 32
MAX_DISTANCE = 128
N_EXPERTS = 32
TOP_K_EXPERTS = 4
D_EXPERT = 1024
SWIGLU_LIMIT = 7.0
SWIGLU_ALPHA = 1.702
PLE_DIM = 256
LN_EPS = 1e-5
DEEPNORM_ALPHA = (2 * DEPTH) ** 0.25
DEEPNORM_BETA = (8 * DEPTH) ** -0.25

IN_SPLITS = [ATTN_WIDTH, ATTN_WIDTH, ATTN_WIDTH, IDX_HEADS * IDX_DIM, IDX_DIM, IDX_HEADS,
             2 * CONV_CH, D_MODEL, D_MODEL]
IN_COLS = sum(IN_SPLITS)

kernel_name = "hybrid_dsa_conformer_moe_block"


def layer_norm(x, g, b):
    xf = x.astype(jnp.float32)
    mu = jnp.mean(xf, axis=-1, keepdims=True)
    var = jnp.mean(jnp.square(xf - mu), axis=-1, keepdims=True)
    return ((xf - mu) * lax.rsqrt(var + LN_EPS) * g.astype(jnp.float32) + b.astype(jnp.float32)).astype(x.dtype)


def t5_bucket(rel):
    max_exact = N_BUCKETS // 2
    rel_f = jnp.maximum(rel, 1).astype(jnp.float32)
    large = max_exact + (jnp.log(rel_f / max_exact) / math.log(MAX_DISTANCE / max_exact)
                         * (N_BUCKETS - max_exact)).astype(jnp.int32)
    large = jnp.minimum(large, N_BUCKETS - 1)
    return jnp.where(rel < max_exact, rel, large)


def dsa_attention(q, k, v, q_idx, k_idx, w_idx, rel_bias):
    B, S = q.shape[0], q.shape[1]
    top_k = min(TOPK_MAX, S // 4)
    n_blocks = S // Q_BLOCK
    key_pos = jnp.arange(S, dtype=jnp.int32)
    gather_rows = jax.vmap(lambda arr, idx: arr[idx])

    def block(i):
        start = i * Q_BLOCK
        q_pos = start + jnp.arange(Q_BLOCK, dtype=jnp.int32)
        qb = lax.dynamic_slice_in_dim(q, start, Q_BLOCK, axis=1)
        qib = lax.dynamic_slice_in_dim(q_idx, start, Q_BLOCK, axis=1)
        wb = lax.dynamic_slice_in_dim(w_idx, start, Q_BLOCK, axis=1)
        dots = jnp.einsum('bqhd,bsd->bqhs', qib, k_idx, preferred_element_type=jnp.float32)
        score = jnp.einsum('bqh,bqhs->bqs', wb.astype(jnp.float32) * IDX_SCALE, jax.nn.relu(dots))
        causal = key_pos[None, :] <= q_pos[:, None]
        score = jnp.where(causal[None], score, -jnp.inf)
        _, sel = lax.top_k(score, top_k)
        valid = sel <= q_pos[None, :, None]
        k_sel = gather_rows(k, sel)
        v_sel = gather_rows(v, sel)
        rel = jnp.maximum(q_pos[None, :, None] - sel, 0)
        bias = jnp.transpose(rel_bias[t5_bucket(rel)], (0, 3, 1, 2))
        logits = jnp.einsum('bqhd,bqkhd->bhqk', qb, k_sel,
                            preferred_element_type=jnp.float32) * ATTN_SCALE + bias.astype(jnp.float32)
        logits = jnp.where(valid[:, None], logits, -jnp.inf)
        probs = jax.nn.softmax(logits, axis=-1)
        return jnp.einsum('bhqk,bqkhd->bqhd', probs.astype(v.dtype), v_sel)

    out = lax.map(block, jnp.arange(n_blocks, dtype=jnp.int32))
    return jnp.transpose(out, (1, 0, 2, 3, 4)).reshape(B, S, ATTN_WIDTH)


def conformer_conv(conv_in, w_dw, b_dw, ln_g, ln_b):
    a, g = jnp.split(conv_in, 2, axis=-1)
    u = a * jax.nn.sigmoid(g)
    y = lax.conv_general_dilated(u, w_dw[:, None, :], window_strides=(1,),
                                 padding=((CONV_WIDTH - 1, 0),),
                                 dimension_numbers=('NWC', 'WIO', 'NWC'),
                                 feature_group_count=CONV_CH) + b_dw
    y = layer_norm(y, ln_g, ln_b)
    return jax.nn.silu(y)


def mixing_sublayer(x, w_in, b_in, w_o_attn, w_dw, b_dw, conv_ln_g, conv_ln_b, w_o_conv, w_out, rel_bias):
    B, S, _ = x.shape
    proj = x @ w_in + b_in
    offsets = np.cumsum(IN_SPLITS)[:-1].tolist()
    q, k, v, qi, ki, wi, conv_in, g_attn, g_conv = jnp.split(proj, offsets, axis=-1)
    q = q.reshape(B, S, N_HEADS, HEAD_DIM)
    k = k.reshape(B, S, N_HEADS, HEAD_DIM)
    v = v.reshape(B, S, N_HEADS, HEAD_DIM)
    qi = qi.reshape(B, S, IDX_HEADS, IDX_DIM)
    y_attn = dsa_attention(q, k, v, qi, ki, wi, rel_bias) @ w_o_attn
    y_conv = conformer_conv(conv_in, w_dw, b_dw, conv_ln_g, conv_ln_b) @ w_o_conv
    merged = jax.nn.sigmoid(g_attn) * y_attn + jax.nn.sigmoid(g_conv) * y_conv
    return merged @ w_out


def moe_ffn(h, w_router, b_router, w_gate_up, b_gate_up, w_down, b_down):
    B, S, D = h.shape
    t = h.reshape(B * S, D)
    logits = (t @ w_router + b_router).astype(jnp.float32)
    top_vals, top_idx = lax.top_k(logits, TOP_K_EXPERTS)
    gates = jax.nn.softmax(top_vals, axis=-1)
    rows = jnp.arange(B * S)[:, None]
    combine = jnp.zeros((B * S, N_EXPERTS), jnp.float32).at[rows, top_idx].set(gates)
    out = jnp.zeros_like(t)
    for e in range(N_EXPERTS):
        gu = t @ w_gate_up[e] + b_gate_up[e]
        g, u = gu[:, :D_EXPERT], gu[:, D_EXPERT:]
        g = jnp.minimum(g, SWIGLU_LIMIT)
        u = jnp.clip(u, -SWIGLU_LIMIT, SWIGLU_LIMIT)
        act = (u + 1) * (g * jax.nn.sigmoid(SWIGLU_ALPHA * g))
        y = act @ w_down[e] + b_down[e]
        out = out + combine[:, e:e + 1].astype(t.dtype) * y
    return out.reshape(B, S, D)


def setup_inputs(seed: int = 0) -> dict:
    key = jax.random.key(seed)
    ks = jax.random.split(key, 26)
    f32 = jnp.float32

    def nrm(k, shape, fan_in, scale=1.0):
        return jax.random.normal(k, shape, f32) * (scale * fan_in ** -0.5)

    def small(k, shape, s=0.02):
        return jax.random.normal(k, shape, f32) * s

    L = DEPTH
    return {
        "x": jax.random.normal(ks[0], (BATCH, SEQ, D_MODEL), f32),
        "p": jax.random.normal(ks[1], (DEPTH, BATCH, SEQ, PLE_DIM), f32),
        "w_in": nrm(ks[2], (L, D_MODEL, IN_COLS), D_MODEL),
        "b_in": small(ks[3], (L, IN_COLS)),
        "w_o_attn": nrm(ks[4], (L, ATTN_WIDTH, D_MODEL), ATTN_WIDTH),
        "w_dw": nrm(ks[5], (L, CONV_WIDTH, CONV_CH), CONV_WIDTH),
        "b_dw": small(ks[6], (L, CONV_CH)),
        "conv_ln_g": 1.0 + small(ks[7], (L, CONV_CH)),
        "conv_ln_b": small(ks[8], (L, CONV_CH)),
        "w_o_conv": nrm(ks[9], (L, CONV_CH, D_MODEL), CONV_CH),
        "w_out": nrm(ks[10], (L, D_MODEL, D_MODEL), D_MODEL, DEEPNORM_BETA),
        "ln1_g": 1.0 + small(ks[11], (L, D_MODEL)),
        "ln1_b": small(ks[12], (L, D_MODEL)),
        "w_router": nrm(ks[13], (L, D_MODEL, N_EXPERTS), D_MODEL),
        "b_router": small(ks[14], (L, N_EXPERTS), 0.01),
        "w_gate_up": nrm(ks[15], (L, N_EXPERTS, D_MODEL, 2 * D_EXPERT), D_MODEL),
        "b_gate_up": small(ks[16], (L, N_EXPERTS, 2 * D_EXPERT)),
        "w_down": nrm(ks[17], (L, N_EXPERTS, D_EXPERT, D_MODEL), D_EXPERT, DEEPNORM_BETA),
        "b_down": small(ks[18], (L, N_EXPERTS, D_MODEL)),
        "w_ple_gate": nrm(ks[19], (L, D_MODEL, D_MODEL), D_MODEL),
        "w_ple_proj": nrm(ks[20], (L, PLE_DIM, D_MODEL), PLE_DIM, DEEPNORM_BETA),
        "ln2_g": 1.0 + small(ks[21], (L, D_MODEL)),
        "ln2_b": small(ks[22], (L, D_MODEL)),
        "rel_bias": small(ks[23], (N_BUCKETS, N_HEADS), 0.5),
    }


def reference(x, p, w_in, b_in, w_o_attn, w_dw, b_dw, conv_ln_g, conv_ln_b, w_o_conv, w_out,
              ln1_g, ln1_b, w_router, b_router, w_gate_up, b_gate_up, w_down, b_down,
              w_ple_gate, w_ple_proj, ln2_g, ln2_b, rel_bias):
    for i in range(DEPTH):
        mix = mixing_sublayer(x, w_in[i], b_in[i], w_o_attn[i], w_dw[i], b_dw[i], conv_ln_g[i],
                              conv_ln_b[i], w_o_conv[i], w_out[i], rel_bias)
        x = layer_norm(DEEPNORM_ALPHA * x + mix, ln1_g[i], ln1_b[i])
        h = DEEPNORM_ALPHA * x + moe_ffn(x, w_router[i], b_router[i], w_gate_up[i], b_gate_up[i],
                                         w_down[i], b_down[i])
        ple = jax.nn.sigmoid(h @ w_ple_gate[i]) * (p[i] @ w_ple_proj[i])
        x = layer_norm(h + ple, ln2_g[i], ln2_b[i])
    return x
```

```python
import functools
import math

import jax
import jax.numpy as jnp
import numpy as np
from jax import lax
from jax.experimental import pallas as pl
from jax.experimental.pallas import tpu as pltpu

F32 = jnp.float32
BF16 = jnp.bfloat16

D_MODEL = 1024
N_HEADS = 8
HEAD_DIM = 64
ATTN_WIDTH = N_HEADS * HEAD_DIM
ATTN_SCALE = HEAD_DIM ** -0.5
IDX_HEADS = 8
IDX_DIM = 64
IDX_SCALE = (IDX_HEADS ** -0.5) * (IDX_DIM ** -0.5)
TOPK_MAX = 256
CONV_CH = 512
CONV_WIDTH = 31
N_BUCKETS = 32
MAX_DISTANCE = 128
N_EXPERTS = 32
TOP_K_EXPERTS = 4
D_EXPERT = 1024
SWIGLU_LIMIT = 7.0
SWIGLU_ALPHA = 1.702
PLE_DIM = 256
LN_EPS = 1e-5
DEPTH = 1
DEEPNORM_ALPHA = (2 * DEPTH) ** 0.25

LANES = 128
NEG_BIG = -1e30
VMEM_LIMIT = 56 * 1024 * 1024

TM = 512
TQI = 256
TKI = 512
TA = 512
SUB = 128
TME = 256
CONV_HALO = 32


def _params(sem, vmem=VMEM_LIMIT):
    return pltpu.CompilerParams(dimension_semantics=sem, vmem_limit_bytes=vmem)


def _sigmoid(x):
    return 1.0 / (1.0 + jnp.exp(-x))


def _layer_norm(x, g, b):
    mu = jnp.mean(x, axis=-1, keepdims=True)
    xc = x - mu
    var = jnp.mean(xc * xc, axis=-1, keepdims=True)
    return xc * lax.rsqrt(var + LN_EPS) * g + b


def _dot(a, b):
    return jnp.dot(a, b, preferred_element_type=F32)


def _dot_nt(a, b):
    return lax.dot_general(a, b, (((1,), (1,)), ((), ())), preferred_element_type=F32)


def _proj_kernel(x_ref, wqkv_ref, bqkv_ref, wqi_ref, bqi_ref, wkw_ref, bkw_ref, wcv_ref, bcv_ref,
                 wg_ref, bg_ref, q_ref, k_ref, v_ref, qi_ref, kw_ref, u_ref, sg_ref):
    xb = x_ref[...].astype(BF16)
    qkv = _dot(xb, wqkv_ref[...]) + bqkv_ref[...]
    q_ref[...] = (qkv[:, :ATTN_WIDTH] * ATTN_SCALE).astype(BF16)
    k_ref[...] = qkv[:, ATTN_WIDTH:2 * ATTN_WIDTH].astype(BF16)
    v_ref[...] = qkv[:, 2 * ATTN_WIDTH:].astype(BF16)
    qi_ref[...] = (_dot(xb, wqi_ref[...]) + bqi_ref[...]).astype(BF16)
    kw = _dot(xb, wkw_ref[...]) + bkw_ref[...]
    lane = lax.broadcasted_iota(jnp.int32, kw.shape, 1)
    kw_ref[...] = jnp.where(lane >= IDX_DIM, kw * IDX_SCALE, kw)
    cv = _dot(xb, wcv_ref[...]) + bcv_ref[...]
    u_ref[...] = cv[:, :CONV_CH] * _sigmoid(cv[:, CONV_CH:])
    sg_ref[...] = _sigmoid(_dot(xb, wg_ref[...]) + bg_ref[...])


def _proj(x, wqkv, bqkv, wqi, bqi, wkw, bkw, wcv, bcv, wg, bg):
    S = x.shape[0]
    full = lambda a: pl.BlockSpec(a.shape, lambda i: (0, 0))
    row = lambda n: pl.BlockSpec((TM, n), lambda i: (i, 0))
    outs = [(ATTN_WIDTH, BF16)] * 3 + [(IDX_HEADS * IDX_DIM, BF16), (LANES, F32), (CONV_CH, F32),
                                        (2 * D_MODEL, F32)]
    return pl.pallas_call(
        _proj_kernel,
        grid=(S // TM,),
        in_specs=[row(D_MODEL)] + [full(a) for a in (wqkv, bqkv, wqi, bqi, wkw, bkw, wcv, bcv, wg, bg)],
        out_specs=[row(n) for n, _ in outs],
        out_shape=[jax.ShapeDtypeStruct((S, n), dt) for n, dt in outs],
        compiler_params=_params(("parallel",)),
        name="proj",
    )(x, wqkv, bqkv, wqi, bqi, wkw, bkw, wcv, bcv, wg, bg)


def _conv_kernel(u_ref, wdw_ref, bdw_ref, lng_ref, lnb_ref, wo_ref, sgc_ref, o_ref, buf_ref):
    @pl.when(pl.program_id(0) == 0)
    def _():
        buf_ref[0:CONV_HALO, :] = jnp.zeros((CONV_HALO, CONV_CH), F32)

    buf_ref[CONV_HALO:CONV_HALO + TM, :] = u_ref[...]
    base = CONV_HALO - (CONV_WIDTH - 1)
    acc = jnp.zeros((TM, CONV_CH), F32) + bdw_ref[...]
    for j in range(CONV_WIDTH):
        acc = acc + wdw_ref[j:j + 1, :] * buf_ref[base + j:base + j + TM, :]
    buf_ref[0:CONV_HALO, :] = buf_ref[TM:TM + CONV_HALO, :]
    y = _layer_norm(acc, lng_ref[...], lnb_ref[...])
    y = y * _sigmoid(y)
    o_ref[...] = sgc_ref[...] * _dot(y.astype(BF16), wo_ref[...])


def _conv(u, sg, wdw, bdw, lng, lnb, wo):
    S = u.shape[0]
    full = lambda a: pl.BlockSpec(a.shape, lambda i: (0, 0))
    return pl.pallas_call(
        _conv_kernel,
        grid=(S // TM,),
        in_specs=[pl.BlockSpec((TM, CONV_CH), lambda i: (i, 0)), full(wdw), full(bdw), full(lng), full(lnb),
                  full(wo), pl.BlockSpec((TM, D_MODEL), lambda i: (i, 1))],
        out_specs=pl.BlockSpec((TM, D_MODEL), lambda i: (i, 0)),
        out_shape=jax.ShapeDtypeStruct((S, D_MODEL), F32),
        scratch_shapes=[pltpu.VMEM((TM + CONV_HALO, CONV_CH), F32)],
        compiler_params=_params(("arbitrary",)),
        name="conv",
    )(u, wdw, bdw, lng, lnb, wo, sg)


def _float_key(f):
    b = lax.bitcast_convert_type(f, jnp.int32)
    return b ^ ((b >> 31) & jnp.int32(0x7FFFFFFF))


def _key_float(k):
    b = k ^ ((k >> 31) & jnp.int32(0x7FFFFFFF))
    return lax.bitcast_convert_type(b, F32)


def _index_kernel(qi_ref, kw_ref, ki_ref, mask_ref, sc_ref, qh_ref, wb_ref, *, top_k):
    i = pl.program_id(0)
    nkt = (i * TQI + TQI + TKI - 1) // TKI
    ntile = sc_ref.shape[0]
    row_g = i * TQI + lax.broadcasted_iota(jnp.int32, (TQI, 1), 0)
    lane = lax.broadcasted_iota(jnp.int32, (TQI, LANES), 1)

    for h in range(IDX_HEADS):
        qp = qi_ref[:, (h // 2) * LANES:(h // 2 + 1) * LANES]
        keep = (lane < IDX_DIM) if h % 2 == 0 else (lane >= IDX_DIM)
        qh_ref[h] = jnp.where(keep, qp, jnp.zeros_like(qp))
        wb_ref[h] = jnp.broadcast_to(kw_ref[:, IDX_DIM + h:IDX_DIM + h + 1], (TQI, LANES))

    def score_tile(kt, causal):
        kb = ki_ref[pl.ds(pl.multiple_of(kt * TKI, TKI), TKI), :]
        acc = jnp.zeros((TQI, TKI), F32)
        for h in range(IDX_HEADS):
            d = _dot_nt(qh_ref[h], kb)
            w = wb_ref[h]
            acc = acc + jnp.concatenate([w] * (TKI // LANES), axis=1) * jnp.maximum(d, 0.0)
        if causal:
            col_g = kt * TKI + lax.broadcasted_iota(jnp.int32, (TQI, TKI), 1)
            ok = col_g <= row_g
            lo_src = jnp.where(ok, acc, jnp.inf)
            acc = jnp.where(ok, acc, -jnp.inf)
        else:
            lo_src = acc
        sc_ref[kt] = acc
        return jnp.max(acc, axis=1, keepdims=True), jnp.min(lo_src, axis=1, keepdims=True)

    def score_body(kt, carry):
        mx, mn = carry
        tmx, tmn = score_tile(kt, False)
        return jnp.maximum(mx, tmx), jnp.minimum(mn, tmn)

    mx0 = jnp.full((TQI, 1), -jnp.inf, F32)
    mn0 = jnp.full((TQI, 1), jnp.inf, F32)
    mx, mn = lax.fori_loop(0, nkt - 1, score_body, (mx0, mn0))
    tmx, tmn = score_tile(nkt - 1, True)
    mx = jnp.maximum(mx, tmx)
    mn = jnp.minimum(mn, tmn)

    def count(pivot, strict):
        pb = jnp.broadcast_to(pivot, (TQI, LANES))

        def body(kt, cnt):
            s = sc_ref[kt]
            for c in range(TKI // LANES):
                blk = s[:, c * LANES:(c + 1) * LANES]
                hit = (blk > pb) if strict else (blk >= pb)
                cnt = cnt + jnp.where(hit, 1.0, 0.0)
            return cnt

        cnt = lax.fori_loop(0, nkt, body, jnp.zeros((TQI, LANES), F32))
        return jnp.sum(cnt, axis=1, keepdims=True)

    kf = float(top_k)
    n_causal = (row_g + 1).astype(F32)
    all_sel = n_causal <= kf
    state0 = dict(
        it=jnp.int32(0),
        lo=_float_key(mn), hi=_float_key(mx) + 1,
        clo=n_causal, chi=jnp.zeros((TQI, 1), F32),
        thr=jnp.where(all_sel, NEG_BIG, 0.0).astype(F32),
        done=all_sel.astype(F32), tie=jnp.zeros((TQI, 1), F32),
        bis=jnp.zeros((TQI, 1), F32), side=jnp.zeros((TQI, 1), F32),
    )

    def cond(st):
        return jnp.logical_and(st["it"] < 100, jnp.min(st["done"]) < 0.5)

    def step(st):
        lo, hi, clo, chi = st["lo"], st["hi"], st["clo"], st["chi"]
        active = st["done"] < 0.5
        adjacent = lo + 1 >= hi
        lo_f, hi_f = _key_float(lo), _key_float(hi)
        frac = (jnp.log(clo) - math.log(kf + 0.5)) / (jnp.log(clo) - jnp.log(jnp.maximum(chi, 0.5)))
        p_int = _float_key(lo_f + frac * (hi_f - lo_f))
        p_bis = (lo >> 1) + (hi >> 1) + (lo & hi & 1)
        pk = jnp.where(st["bis"] > 0.5, p_bis, p_int)
        pk = jnp.minimum(jnp.maximum(pk, lo + 1), hi - 1)
        pk = jnp.where(jnp.logical_and(active, jnp.logical_not(adjacent)), pk, lo)
        pf = _key_float(pk)
        c = count(pf, False)
        probe = jnp.logical_and(active, jnp.logical_not(adjacent))
        hit = jnp.logical_and(probe, c == kf)
        up = jnp.logical_and(probe, c > kf)
        dn = jnp.logical_and(probe, c < kf)
        tie_now = jnp.logical_and(active, adjacent)
        side_now = jnp.where(up, 1.0, jnp.where(dn, -1.0, 0.0))
        was_bis = st["bis"] > 0.5
        return dict(
            it=st["it"] + 1,
            lo=jnp.where(up, pk, lo), hi=jnp.where(dn, pk, hi),
            clo=jnp.where(up, c, clo), chi=jnp.where(dn, c, chi),
            thr=jnp.where(hit, pf, jnp.where(tie_now, lo_f, st["thr"])),
            done=jnp.where(jnp.logical_or(hit, tie_now), 1.0, st["done"]),
            tie=jnp.where(tie_now, 1.0, st["tie"]),
            bis=jnp.where(jnp.logical_and(jnp.logical_not(was_bis), side_now == st["side"]), 1.0, 0.0),
            side=jnp.where(was_bis, 0.0, side_now),
        )

    st = lax.while_loop(cond, step, state0)
    unfinished = st["done"] < 0.5
    thr = jnp.where(unfinished, _key_float(st["lo"]), st["thr"])
    tie = jnp.where(unfinished, 1.0, st["tie"])
    thr_b = jnp.broadcast_to(thr, (TQI, LANES))
    any_tie = jnp.max(tie) > 0.5

    def emit(kt, sel):
        mask_ref[0, kt] = jnp.where(sel, 0.0, NEG_BIG).astype(mask_ref.dtype)

    @pl.when(jnp.logical_not(any_tie))
    def _():
        def body(kt, carry):
            s = sc_ref[kt]
            emit(kt, s >= jnp.concatenate([thr_b] * (TKI // LANES), axis=1))
            return carry
        lax.fori_loop(0, nkt, body, 0)

    @pl.when(any_tie)
    def _():
        need = jnp.where(tie > 0.5, kf - count(thr, True), float(2 * ntile * TKI))
        r = lax.broadcasted_iota(jnp.int32, (TKI, TKI), 0)
        c = lax.broadcasted_iota(jnp.int32, (TKI, TKI), 1)
        prefix = jnp.where(r <= c, 1.0, 0.0).astype(BF16)
        thr_w = jnp.concatenate([thr_b] * (TKI // LANES), axis=1)

        def body(kt, seen):
            s = sc_ref[kt]
            eq = s == thr_w
            rank = seen + _dot(jnp.where(eq, 1.0, 0.0).astype(BF16), prefix)
            emit(kt, jnp.logical_or(s > thr_w, jnp.logical_and(eq, rank <= need)))
            return seen + jnp.sum(jnp.where(eq, 1.0, 0.0), axis=1, keepdims=True)
        lax.fori_loop(0, nkt, body, jnp.zeros((TQI, 1), F32))

    def fill(kt, carry):
        mask_ref[0, kt] = jnp.full((TQI, TKI), NEG_BIG, mask_ref.dtype)
        return carry
    lax.fori_loop(nkt, ntile, fill, 0)


def _index_mask(qi, kw, ki2, top_k):
    S = qi.shape[0]
    nq, nk = S // TQI, S // TKI
    return pl.pallas_call(
        functools.partial(_index_kernel, top_k=top_k),
        grid=(nq,),
        in_specs=[pl.BlockSpec((TQI, IDX_HEADS * IDX_DIM), lambda i: (i, 0)),
                  pl.BlockSpec((TQI, LANES), lambda i: (i, 0)),
                  pl.BlockSpec((S, LANES), lambda i: (0, 0))],
        out_specs=pl.BlockSpec((1, nk, TQI, TKI), lambda i: (i, 0, 0, 0)),
        out_shape=jax.ShapeDtypeStruct((nq, nk, TQI, TKI), BF16),
        scratch_shapes=[pltpu.VMEM((nk, TQI, TKI), F32),
                        pltpu.VMEM((IDX_HEADS, TQI, LANES), BF16),
                        pltpu.VMEM((IDX_HEADS, TQI, LANES), F32)],
        compiler_params=_params(("parallel",)),
        name="index_mask",
    )(qi, kw, ki2)


def _attn_kernel(qt_ref, kt_ref, q_ref, k_ref, v_ref, m_ref, bd_ref, be_ref, o_ref, s_sc, m_sc, l_sc, acc_sc):
    step = pl.program_id(0)
    qi = qt_ref[step]
    ki = kt_ref[step]
    nsub = TA // SUB

    @pl.when(ki == 0)
    def _():
        m_sc[...] = jnp.full(m_sc.shape, NEG_BIG, F32)
        l_sc[...] = jnp.zeros(l_sc.shape, F32)
        acc_sc[...] = jnp.zeros(acc_sc.shape, F32)

    maskf = jnp.concatenate([m_ref[a, 0] for a in range(TA // TQI)], axis=0).astype(F32)
    lane = lax.broadcasted_iota(jnp.int32, (TA, LANES), 1)
    first = lane < HEAD_DIM

    for p in range(N_HEADS // 2):
        cols = slice(p * LANES, (p + 1) * LANES)
        qp = q_ref[:, cols]
        kp = k_ref[:, cols]
        vp = v_ref[:, cols]
        upd = []
        for hh in range(2):
            h = 2 * p + hh
            qh = jnp.where(first if hh == 0 else jnp.logical_not(first), qp, jnp.zeros_like(qp))
            s_sc[...] = _dot_nt(qh, kp) + maskf

            @pl.when(ki == qi)
            def _():
                for a in range(nsub):
                    rows = slice(a * SUB, (a + 1) * SUB)
                    s_sc[rows, rows] = s_sc[rows, rows] + bd_ref[h]
                    if a > 0:
                        prev = slice((a - 1) * SUB, a * SUB)
                        s_sc[rows, prev] = s_sc[rows, prev] + be_ref[h]

            @pl.when(ki == qi - 1)
            def _():
                s_sc[0:SUB, TA - SUB:TA] = s_sc[0:SUB, TA - SUB:TA] + be_ref[h]

            s = s_sc[...]
            m_prev = m_sc[h]
            m_next = jnp.maximum(m_prev, jnp.max(s, axis=1, keepdims=True))
            alpha = jnp.exp(m_prev - m_next)
            pexp = jnp.exp(s - m_next[:, :1])
            l_sc[h] = alpha * l_sc[h] + jnp.sum(pexp, axis=1, keepdims=True)
            m_sc[h] = m_next
            upd.append((alpha, _dot(pexp.astype(BF16), vp)))
        acc = acc_sc[p]
        acc_sc[p] = jnp.where(first, upd[0][0] * acc + upd[0][1], upd[1][0] * acc + upd[1][1])

    @pl.when(ki == qi)
    def _():
        for p in range(N_HEADS // 2):
            inv = jnp.where(first, 1.0 / l_sc[2 * p], 1.0 / l_sc[2 * p + 1])
            o_ref[:, p * LANES:(p + 1) * LANES] = (acc_sc[p] * inv).astype(o_ref.dtype)


def _attention(q, k, v, mask4, bias_d, bias_e):
    S = q.shape[0]
    nb = S // TA
    pairs = [(a, b) for a in range(nb) for b in range(a + 1)]
    qtab = jnp.asarray(np.array([a for a, _ in pairs], np.int32))
    ktab = jnp.asarray(np.array([b for _, b in pairs], np.int32))
    grid_spec = pltpu.PrefetchScalarGridSpec(
        num_scalar_prefetch=2,
        grid=(len(pairs),),
        in_specs=[pl.BlockSpec((TA, ATTN_WIDTH), lambda s, qt, kt: (qt[s], 0)),
                  pl.BlockSpec((TA, ATTN_WIDTH), lambda s, qt, kt: (kt[s], 0)),
                  pl.BlockSpec((TA, ATTN_WIDTH), lambda s, qt, kt: (kt[s], 0)),
                  pl.BlockSpec((TA // TQI, TA // TKI, TQI, TKI), lambda s, qt, kt: (qt[s], kt[s], 0, 0)),
                  pl.BlockSpec(bias_d.shape, lambda s, qt, kt: (0, 0, 0)),
                  pl.BlockSpec(bias_e.shape, lambda s, qt, kt: (0, 0, 0))],
        out_specs=pl.BlockSpec((TA, ATTN_WIDTH), lambda s, qt, kt: (qt[s], 0)),
        scratch_shapes=[pltpu.VMEM((TA, TA), F32),
                        pltpu.VMEM((N_HEADS, TA, LANES), F32),
                        pltpu.VMEM((N_HEADS, TA, LANES), F32),
                        pltpu.VMEM((N_HEADS // 2, TA, LANES), F32)],
    )
    return pl.pallas_call(
        _attn_kernel,
        grid_spec=grid_spec,
        out_shape=jax.ShapeDtypeStruct((S, ATTN_WIDTH), BF16),
        compiler_params=_params(("arbitrary",)),
        name="attn",
    )(qtab, ktab, q, k, v, mask4, bias_d, bias_e)


def _relative_bias_blocks(rel_bias):
    dist = jnp.arange(2 * SUB, dtype=jnp.int32)
    max_exact = N_BUCKETS // 2
    dist_f = jnp.maximum(dist, 1).astype(F32)
    large = max_exact + (jnp.log(dist_f / max_exact) / math.log(MAX_DISTANCE / max_exact)
                         * (N_BUCKETS - max_exact)).astype(jnp.int32)
    bucket = jnp.where(dist < max_exact, dist, jnp.minimum(large, N_BUCKETS - 1))
    table = (rel_bias[bucket] - rel_bias[N_BUCKETS - 1]).astype(F32)
    i = jnp.arange(SUB)[:, None]
    j = jnp.arange(SUB)[None, :]
    diag = jnp.transpose(table[jnp.clip(i - j, 0, 2 * SUB - 1)], (2, 0, 1))
    sub = jnp.transpose(table[SUB + i - j], (2, 0, 1))
    return diag, sub


def _pack_bf16_pair(a, b):
    def rnd(x):
        bits = lax.bitcast_convert_type(x, jnp.uint32)
        return bits + jnp.uint32(0x7FFF) + ((bits >> 16) & jnp.uint32(1))
    return (rnd(a) >> 16) | (rnd(b) & jnp.uint32(0xFFFF0000))


def _unpack_bf16_pair(p):
    lo = lax.bitcast_convert_type(p << 16, F32)
    hi = lax.bitcast_convert_type(p & jnp.uint32(0xFFFF0000), F32)
    return lo.astype(BF16), hi.astype(BF16)


def _post_kernel(ya_ref, sga_ref, pc_ref, x_ref, woa_ref, wout_ref, g_ref, b_ref, wr_ref, br_ref,
                 x1_ref, x1p_ref, ridx_ref, rgate_ref):
    y_attn = _dot(ya_ref[...], woa_ref[...])
    merged = sga_ref[...] * y_attn + pc_ref[...]
    mix = _dot(merged.astype(BF16), wout_ref[...])
    x1 = _layer_norm(DEEPNORM_ALPHA * x_ref[...] + mix, g_ref[...], b_ref[...])
    x1_ref[...] = x1
    half = D_MODEL // 2
    x1p_ref[...] = _pack_bf16_pair(x1[:, :half], x1[:, half:])

    logits = jnp.dot(x1, wr_ref[...], preferred_element_type=F32, precision=lax.Precision.HIGHEST) + br_ref[...]
    lane = lax.broadcasted_iota(jnp.int32, logits.shape, 1).astype(F32)
    cur = logits
    vals, idxs = [], []
    for _ in range(TOP_K_EXPERTS):
        m = jnp.max(cur, axis=1, keepdims=True)
        ix = jnp.min(jnp.where(cur == m, lane, float(LANES)), axis=1, keepdims=True)
        vals.append(m)
        idxs.append(ix)
        cur = jnp.where(lane == ix, -jnp.inf, cur)
    exps = [jnp.exp(v - vals[0]) for v in vals]
    denom = exps[0]
    for e in exps[1:]:
        denom = denom + e
    ridx = jnp.zeros_like(logits)
    rgate = jnp.zeros_like(logits)
    for j in range(TOP_K_EXPERTS):
        ridx = jnp.where(lane == float(j), idxs[j], ridx)
        rgate = jnp.where(lane == float(j), exps[j] / denom, rgate)
    ridx_ref[...] = ridx.astype(jnp.int32)
    rgate_ref[...] = rgate


def _post(ya, sg, pc, x, woa, wout, g, b, wr, br):
    S = x.shape[0]
    full = lambda a: pl.BlockSpec(a.shape, lambda i: (0, 0))
    row = lambda n: pl.BlockSpec((TM, n), lambda i: (i, 0))
    return pl.pallas_call(
        _post_kernel,
        grid=(S // TM,),
        in_specs=[row(ATTN_WIDTH), row(D_MODEL), row(D_MODEL), row(D_MODEL)] + [full(a) for a in (woa, wout, g, b, wr, br)],
        out_specs=[row(D_MODEL), row(D_MODEL // 2), row(LANES), row(LANES)],
        out_shape=[jax.ShapeDtypeStruct((S, D_MODEL), F32), jax.ShapeDtypeStruct((S, D_MODEL // 2), jnp.uint32),
                   jax.ShapeDtypeStruct((S, LANES), jnp.int32), jax.ShapeDtypeStruct((S, LANES), F32)],
        compiler_params=_params(("parallel",)),
        name="post",
    )(ya, sg, pc, x, woa, wout, g, b, wr, br)


def _row_copy(src_ref, dst_ref, src_row, dst_row, sem):
    return pltpu.make_async_copy(src_ref.at[pl.ds(src_row, 1), :], dst_ref.at[pl.ds(dst_row, 1), :], sem)


def _gather_kernel(nused_ref, idx_ref, src_ref, dst_ref, sem):
    t = pl.program_id(0)

    @pl.when(t < nused_ref[0])
    def _():
        def issue(r, carry):
            _row_copy(src_ref, dst_ref, idx_ref[0, 0, r], t * TME + r, sem).start()
            return carry
        lax.fori_loop(0, TME, issue, 0)

        def drain(r, carry):
            _row_copy(src_ref, dst_ref, 0, t * TME + r, sem).wait()
            return carry
        lax.fori_loop(0, TME, drain, 0)

    @pl.when(t >= nused_ref[0])
    def _():
        filler = pltpu.make_async_copy(src_ref.at[pl.ds(0, TME), :], dst_ref.at[pl.ds(t * TME, TME), :], sem)
        filler.start()
        filler.wait()


def _gather_rows(nused, row_token, src, n_tiles):
    grid_spec = pltpu.PrefetchScalarGridSpec(
        num_scalar_prefetch=1,
        grid=(n_tiles,),
        in_specs=[pl.BlockSpec((1, 1, TME), lambda t, nu: (t, 0, 0), memory_space=pltpu.SMEM),
                  pl.BlockSpec(memory_space=pl.ANY)],
        out_specs=pl.BlockSpec(memory_space=pl.ANY),
        scratch_shapes=[pltpu.SemaphoreType.DMA(())],
    )
    return pl.pallas_call(
        _gather_kernel,
        grid_spec=grid_spec,
        out_shape=jax.ShapeDtypeStruct((n_tiles * TME, src.shape[1]), src.dtype),
        compiler_params=_params(("arbitrary",)),
        name="moe_gather",
    )(nused, row_token.reshape(n_tiles, 1, TME), src)


def _scatter_kernel(nused_ref, dst_idx_ref, src_ref, dst_ref, sem, *, n_dst):
    t = pl.program_id(0)

    @pl.when(t < nused_ref[0])
    def _():
        def issue(r, carry):
            d = dst_idx_ref[0, 0, r]

            @pl.when(d < n_dst)
            def _():
                _row_copy(src_ref, dst_ref, t * TME + r, d, sem).start()
            return carry
        lax.fori_loop(0, TME, issue, 0)

        def drain(r, carry):
            @pl.when(dst_idx_ref[0, 0, r] < n_dst)
            def _():
                _row_copy(src_ref, dst_ref, t * TME + r, 0, sem).wait()
            return carry
        lax.fori_loop(0, TME, drain, 0)


def _scatter_rows(nused, row_dst, src, n_dst, n_tiles):
    grid_spec = pltpu.PrefetchScalarGridSpec(
        num_scalar_prefetch=1,
        grid=(n_tiles,),
        in_specs=[pl.BlockSpec((1, 1, TME), lambda t, nu: (t, 0, 0), memory_space=pltpu.SMEM),
                  pl.BlockSpec(memory_space=pl.ANY)],
        out_specs=pl.BlockSpec(memory_space=pl.ANY),
        scratch_shapes=[pltpu.SemaphoreType.DMA(())],
    )
    return pl.pallas_call(
        functools.partial(_scatter_kernel, n_dst=n_dst),
        grid_spec=grid_spec,
        out_shape=jax.ShapeDtypeStruct((n_dst, src.shape[1]), src.dtype),
        compiler_params=_params(("arbitrary",)),
        name="moe_scatter",
    )(nused, row_dst.reshape(n_tiles, 1, TME), src)


def _expert_kernel(te_ref, nused_ref, xs_ref, wgu_ref, bgu_ref, wd_ref, bd_ref, ys_ref, wgu_sc, wd_sc):
    t = pl.program_id(0)
    e = te_ref[t]
    e_prev = te_ref[jnp.maximum(t - 1, 0)]

    @pl.when(jnp.logical_or(t == 0, e != e_prev))
    def _():
        wgu_sc[...] = wgu_ref[0].astype(BF16)
        wd_sc[...] = wd_ref[0].astype(BF16)

    @pl.when(t < nused_ref[0])
    def _():
        half = D_MODEL // 2
        lo, hi = _unpack_bf16_pair(xs_ref[...])
        gu = _dot(lo, wgu_sc[0:half, :]) + _dot(hi, wgu_sc[half:, :]) + bgu_ref[0]
        g = jnp.minimum(gu[:, :D_EXPERT], SWIGLU_LIMIT)
        u = jnp.clip(gu[:, D_EXPERT:], -SWIGLU_LIMIT, SWIGLU_LIMIT)
        act = (u + 1.0) * (g * _sigmoid(SWIGLU_ALPHA * g))
        ys_ref[...] = _dot(act.astype(BF16), wd_sc[...]) + bd_ref[0]

    @pl.when(t >= nused_ref[0])
    def _():
        ys_ref[...] = jnp.zeros(ys_ref.shape, F32)


def _experts(tile_e, nused, xs, wgu, bgu, wd, bd, n_tiles):
    grid_spec = pltpu.PrefetchScalarGridSpec(
        num_scalar_prefetch=2,
        grid=(n_tiles,),
        in_specs=[pl.BlockSpec((TME, D_MODEL // 2), lambda t, te, nu: (t, 0)),
                  pl.BlockSpec((1, D_MODEL, 2 * D_EXPERT), lambda t, te, nu: (te[t], 0, 0)),
                  pl.BlockSpec((1, 1, 2 * D_EXPERT), lambda t, te, nu: (te[t], 0, 0)),
                  pl.BlockSpec((1, D_EXPERT, D_MODEL), lambda t, te, nu: (te[t], 0, 0)),
                  pl.BlockSpec((1, 1, D_MODEL), lambda t, te, nu: (te[t], 0, 0))],
        out_specs=pl.BlockSpec((TME, D_MODEL), lambda t, te, nu: (t, 0)),
        scratch_shapes=[pltpu.VMEM((D_MODEL, 2 * D_EXPERT), BF16), pltpu.VMEM((D_EXPERT, D_MODEL), BF16)],
    )
    return pl.pallas_call(
        _expert_kernel,
        grid_spec=grid_spec,
        out_shape=jax.ShapeDtypeStruct((n_tiles * TME, D_MODEL), F32),
        compiler_params=_params(("arbitrary",)),
        name="moe_experts",
    )(tile_e, nused, xs, wgu, bgu, wd, bd)


def _routing_tables(ridx, S):
    n_flat = S * TOP_K_EXPERTS
    n_tiles = n_flat // TME + N_EXPERTS
    flat_e = ridx.reshape(n_flat)
    order = jnp.argsort(flat_e, stable=True).astype(jnp.int32)
    counts = jnp.sum(flat_e[:, None] == jnp.arange(N_EXPERTS, dtype=jnp.int32)[None, :], axis=0, dtype=jnp.int32)
    padded = ((counts + TME - 1) // TME) * TME
    pad_end = jnp.cumsum(padded)
    pad_start = pad_end - padded
    grp_start = jnp.cumsum(counts) - counts
    nused = (pad_end[-1] // TME).astype(jnp.int32).reshape(1)
    tile_row0 = jnp.arange(n_tiles, dtype=jnp.int32) * TME
    tile_e = jnp.minimum(jnp.sum(tile_row0[:, None] >= pad_end[None, :], axis=1), N_EXPERTS - 1).astype(jnp.int32)
    rows = jnp.arange(n_tiles * TME, dtype=jnp.int32)
    row_e = jnp.repeat(tile_e, TME)
    rank = rows - pad_start[row_e]
    valid = jnp.logical_and(rank < counts[row_e], rows < pad_end[-1])
    src_flat = order[jnp.clip(grp_start[row_e] + rank, 0, n_flat - 1)]
    row_token = jnp.where(valid, src_flat // TOP_K_EXPERTS, 0).astype(jnp.int32)
    row_dst = jnp.where(valid, (src_flat % TOP_K_EXPERTS) * S + src_flat // TOP_K_EXPERTS, n_flat).astype(jnp.int32)
    return tile_e, nused, row_token, row_dst, n_tiles


def _final_kernel(x1_ref, ys_ref, rg_ref, p_ref, wpg_ref, wpp_ref, g_ref, b_ref, o_ref):
    h = DEEPNORM_ALPHA * x1_ref[...]
    rg = rg_ref[...]
    for j in range(TOP_K_EXPERTS):
        h = h + rg[:, j:j + 1] * ys_ref[j]
    ple = _sigmoid(_dot(h.astype(BF16), wpg_ref[...])) * _dot(p_ref[...].astype(BF16), wpp_ref[...])
    o_ref[...] = _layer_norm(h + ple, g_ref[...], b_ref[...])


def _final(x1, ys, rgate, p, wpg, wpp, g, b):
    S = x1.shape[0]
    full = lambda a: pl.BlockSpec(a.shape, lambda i: (0, 0))
    row = lambda n: pl.BlockSpec((TM, n), lambda i: (i, 0))
    return pl.pallas_call(
        _final_kernel,
        grid=(S // TM,),
        in_specs=[row(D_MODEL), pl.BlockSpec((TOP_K_EXPERTS, TM, D_MODEL), lambda i: (0, i, 0)), row(LANES),
                  row(PLE_DIM), full(wpg), full(wpp), full(g), full(b)],
        out_specs=row(D_MODEL),
        out_shape=jax.ShapeDtypeStruct((S, D_MODEL), F32),
        compiler_params=_params(("parallel",)),
        name="final",
    )(x1, ys, rgate, p, wpg, wpp, g, b)


def _layer(x, p, w_in, b_in, w_o_attn, w_dw, b_dw, conv_ln_g, conv_ln_b, w_o_conv, w_out, ln1_g, ln1_b,
           w_router, b_router, w_gate_up, b_gate_up, w_down, b_down, w_ple_gate, w_ple_proj, ln2_g, ln2_b,
           rel_bias):
    S = x.shape[0]
    assert S % TM == 0 and S % TA == 0 and S % TKI == 0 and (S * TOP_K_EXPERTS) % TME == 0
    top_k = min(TOPK_MAX, S // 4)
    row2 = lambda a: a.reshape(1, -1).astype(F32)

    o_q, o_qi, o_ki, o_cv, o_g = 0, 3 * ATTN_WIDTH, 3 * ATTN_WIDTH + IDX_HEADS * IDX_DIM, \
        3 * ATTN_WIDTH + IDX_HEADS * IDX_DIM + IDX_DIM + IDX_HEADS, \
        3 * ATTN_WIDTH + IDX_HEADS * IDX_DIM + IDX_DIM + IDX_HEADS + 2 * CONV_CH
    kw_pad = LANES - (IDX_DIM + IDX_HEADS)
    wkw = jnp.pad(w_in[:, o_ki:o_cv], ((0, 0), (0, kw_pad)))
    bkw = jnp.pad(b_in[o_ki:o_cv], (0, kw_pad))
    q, k, v, qi, kw, u, sg = _proj(
        x, w_in[:, o_q:o_qi].astype(BF16), row2(b_in[o_q:o_qi]),
        w_in[:, o_qi:o_ki].astype(BF16), row2(b_in[o_qi:o_ki]),
        wkw.astype(BF16), row2(bkw),
        w_in[:, o_cv:o_g].astype(BF16), row2(b_in[o_cv:o_g]),
        w_in[:, o_g:].astype(BF16), row2(b_in[o_g:]))

    part_conv = _conv(u, sg, w_dw, row2(b_dw), row2(conv_ln_g), row2(conv_ln_b), w_o_conv.astype(BF16))

    ki = kw[:, :IDX_DIM].astype(BF16)
    mask4 = _index_mask(qi, kw, jnp.concatenate([ki, ki], axis=1), top_k)
    bias_d, bias_e = _relative_bias_blocks(rel_bias)
    y_attn = _attention(q, k, v, mask4, bias_d, bias_e)

    wr = jnp.pad(w_router, ((0, 0), (0, LANES - N_EXPERTS)))
    br = jnp.pad(b_router, (0, LANES - N_EXPERTS), constant_values=-jnp.inf)
    x1, x1p, ridx, rgate = _post(y_attn, sg, part_conv, x, w_o_attn.astype(BF16), w_out.astype(BF16),
                                 row2(ln1_g), row2(ln1_b), wr, row2(br))

    tile_e, nused, row_token, row_dst, n_tiles = _routing_tables(ridx[:, :TOP_K_EXPERTS], S)
    xs = _gather_rows(nused, row_token, x1p, n_tiles)
    ys_sorted = _experts(tile_e, nused, xs, w_gate_up, b_gate_up.reshape(N_EXPERTS, 1, -1), w_down,
                         b_down.reshape(N_EXPERTS, 1, -1), n_tiles)
    ys = _scatter_rows(nused, row_dst, ys_sorted, S * TOP_K_EXPERTS, n_tiles)
    return _final(x1, ys.reshape(TOP_K_EXPERTS, S, D_MODEL), rgate, p, w_ple_gate.astype(BF16),
                  w_ple_proj.astype(BF16), row2(ln2_g), row2(ln2_b))


def kernel(x, p, w_in, b_in, w_o_attn, w_dw, b_dw, conv_ln_g, conv_ln_b, w_o_conv, w_out, ln1_g, ln1_b, w_router, b_router, w_gate_up, b_gate_up, w_down, b_down, w_ple_gate, w_ple_proj, ln2_g, ln2_b, rel_bias):
    assert x.shape[0] == 1 and p.shape[0] == DEPTH
    out = _layer(x[0], p[0, 0], w_in[0], b_in[0], w_o_attn[0], w_dw[0], b_dw[0], conv_ln_g[0], conv_ln_b[0],
                 w_o_conv[0], w_out[0], ln1_g[0], ln1_b[0], w_router[0], b_router[0], w_gate_up[0], b_gate_up[0],
                 w_down[0], b_down[0], w_ple_gate[0], w_ple_proj[0], ln2_g[0], ln2_b[0], rel_bias)
    return out[None]
```

```python
import functools
import math

import jax
import jax.numpy as jnp
import numpy as np
from jax import lax
from jax.experimental import pallas as pl
from jax.experimental.pallas import tpu as pltpu

F32 = jnp.float32
BF16 = jnp.bfloat16

D_MODEL = 1024
N_HEADS = 8
HEAD_DIM = 64
ATTN_WIDTH = N_HEADS * HEAD_DIM
ATTN_SCALE = HEAD_DIM ** -0.5
IDX_HEADS = 8
IDX_DIM = 64
IDX_SCALE = (IDX_HEADS ** -0.5) * (IDX_DIM ** -0.5)
TOPK_MAX = 256
CONV_CH = 512
CONV_WIDTH = 31
N_BUCKETS = 32
MAX_DISTANCE = 128
N_EXPERTS = 32
TOP_K_EXPERTS = 4
D_EXPERT = 1024
SWIGLU_LIMIT = 7.0
SWIGLU_ALPHA = 1.702
PLE_DIM = 256
LN_EPS = 1e-5
DEPTH = 1
DEEPNORM_ALPHA = (2 * DEPTH) ** 0.25

LANES = 128
NEG_BIG = -1e30
VMEM_LIMIT = 56 * 1024 * 1024

TM = 512
TQI = 256
TKI = 512
CH = 64
TA = 512
SUB = 128
TME = 256
CONV_HALO = 32


def _params(sem, vmem=VMEM_LIMIT):
    return pltpu.CompilerParams(dimension_semantics=sem, vmem_limit_bytes=vmem)


def _sigmoid(x):
    return 1.0 / (1.0 + jnp.exp(-x))


def _layer_norm(x, g, b):
    mu = jnp.mean(x, axis=-1, keepdims=True)
    xc = x - mu
    var = jnp.mean(xc * xc, axis=-1, keepdims=True)
    return xc * lax.rsqrt(var + LN_EPS) * g + b


def _dot(a, b):
    return jnp.dot(a, b, preferred_element_type=F32)


def _dot_nt(a, b):
    return lax.dot_general(a, b, (((1,), (1,)), ((), ())), preferred_element_type=F32)


def _proj_kernel(x_ref, wqkv_ref, bqkv_ref, wqi_ref, bqi_ref, wkw_ref, bkw_ref, wcv_ref, bcv_ref,
                 wg_ref, bg_ref, q_ref, k_ref, v_ref, qi_ref, kw_ref, u_ref, sg_ref):
    xb = x_ref[...].astype(BF16)
    qkv = _dot(xb, wqkv_ref[...]) + bqkv_ref[...]
    q_ref[...] = (qkv[:, :ATTN_WIDTH] * ATTN_SCALE).astype(BF16)
    k_ref[...] = qkv[:, ATTN_WIDTH:2 * ATTN_WIDTH].astype(BF16)
    v_ref[...] = qkv[:, 2 * ATTN_WIDTH:].astype(BF16)
    qi_ref[...] = (_dot(xb, wqi_ref[...]) + bqi_ref[...]).astype(BF16)
    kw = _dot(xb, wkw_ref[...]) + bkw_ref[...]
    lane = lax.broadcasted_iota(jnp.int32, kw.shape, 1)
    kw_ref[...] = jnp.where(lane >= IDX_DIM, kw * IDX_SCALE, kw)
    cv = _dot(xb, wcv_ref[...]) + bcv_ref[...]
    u_ref[...] = cv[:, :CONV_CH] * _sigmoid(cv[:, CONV_CH:])
    sg_ref[...] = _sigmoid(_dot(xb, wg_ref[...]) + bg_ref[...])


def _proj(x, wqkv, bqkv, wqi, bqi, wkw, bkw, wcv, bcv, wg, bg):
    S = x.shape[0]
    full = lambda a: pl.BlockSpec(a.shape, lambda i: (0, 0))
    row = lambda n: pl.BlockSpec((TM, n), lambda i: (i, 0))
    outs = [(ATTN_WIDTH, BF16)] * 3 + [(IDX_HEADS * IDX_DIM, BF16), (LANES, F32), (CONV_CH, F32),
                                        (2 * D_MODEL, F32)]
    return pl.pallas_call(
        _proj_kernel,
        grid=(S // TM,),
        in_specs=[row(D_MODEL)] + [full(a) for a in (wqkv, bqkv, wqi, bqi, wkw, bkw, wcv, bcv, wg, bg)],
        out_specs=[row(n) for n, _ in outs],
        out_shape=[jax.ShapeDtypeStruct((S, n), dt) for n, dt in outs],
        compiler_params=_params(("parallel",)),
        name="proj",
    )(x, wqkv, bqkv, wqi, bqi, wkw, bkw, wcv, bcv, wg, bg)


def _conv_kernel(u_ref, wdw_ref, bdw_ref, lng_ref, lnb_ref, wo_ref, sgc_ref, o_ref, buf_ref):
    @pl.when(pl.program_id(0) == 0)
    def _():
        buf_ref[0:CONV_HALO, :] = jnp.zeros((CONV_HALO, CONV_CH), F32)

    buf_ref[CONV_HALO:CONV_HALO + TM, :] = u_ref[...]
    base = CONV_HALO - (CONV_WIDTH - 1)
    acc = jnp.zeros((TM, CONV_CH), F32) + bdw_ref[...]
    for j in range(CONV_WIDTH):
        acc = acc + wdw_ref[j:j + 1, :] * buf_ref[base + j:base + j + TM, :]
    buf_ref[0:CONV_HALO, :] = buf_ref[TM:TM + CONV_HALO, :]
    y = _layer_norm(acc, lng_ref[...], lnb_ref[...])
    y = y * _sigmoid(y)
    o_ref[...] = sgc_ref[...] * _dot(y.astype(BF16), wo_ref[...])


def _conv(u, sg, wdw, bdw, lng, lnb, wo):
    S = u.shape[0]
    full = lambda a: pl.BlockSpec(a.shape, lambda i: (0, 0))
    return pl.pallas_call(
        _conv_kernel,
        grid=(S // TM,),
        in_specs=[pl.BlockSpec((TM, CONV_CH), lambda i: (i, 0)), full(wdw), full(bdw), full(lng), full(lnb),
                  full(wo), pl.BlockSpec((TM, D_MODEL), lambda i: (i, 1))],
        out_specs=pl.BlockSpec((TM, D_MODEL), lambda i: (i, 0)),
        out_shape=jax.ShapeDtypeStruct((S, D_MODEL), F32),
        scratch_shapes=[pltpu.VMEM((TM + CONV_HALO, CONV_CH), F32)],
        compiler_params=_params(("arbitrary",)),
        name="conv",
    )(u, wdw, bdw, lng, lnb, wo, sg)


def _float_key(f):
    b = lax.bitcast_convert_type(f, jnp.int32)
    return b ^ ((b >> 31) & jnp.int32(0x7FFFFFFF))


def _key_float(k):
    b = k ^ ((k >> 31) & jnp.int32(0x7FFFFFFF))
    return lax.bitcast_convert_type(b, F32)


def _index_kernel(qi_ref, kw_ref, ki_ref, mask_ref, sc_ref, qh_ref, wb_ref, mx_ref, mn_ref, thr_ref, tie_ref, *,
                  top_k):
    i = pl.program_id(0)
    nkt = (i * TQI + TQI + TKI - 1) // TKI
    ntile = sc_ref.shape[0]
    row_g = i * TQI + lax.broadcasted_iota(jnp.int32, (TQI, 1), 0)
    lane = lax.broadcasted_iota(jnp.int32, (TQI, LANES), 1)

    for h in range(IDX_HEADS):
        qp = qi_ref[:, (h // 2) * LANES:(h // 2 + 1) * LANES]
        keep = (lane < IDX_DIM) if h % 2 == 0 else (lane >= IDX_DIM)
        qh_ref[h] = jnp.where(keep, qp, jnp.zeros_like(qp))
        wb_ref[h] = jnp.broadcast_to(kw_ref[:, IDX_DIM + h:IDX_DIM + h + 1], (TQI, LANES))

    def score_tile(kt, causal):
        kb = ki_ref[pl.ds(pl.multiple_of(kt * TKI, TKI), TKI), :]
        acc = jnp.zeros((TQI, TKI), F32)
        for h in range(IDX_HEADS):
            d = _dot_nt(qh_ref[h], kb)
            w = wb_ref[h]
            acc = acc + jnp.concatenate([w] * (TKI // LANES), axis=1) * jnp.maximum(d, 0.0)
        if causal:
            col_g = kt * TKI + lax.broadcasted_iota(jnp.int32, (TQI, TKI), 1)
            ok = col_g <= row_g
            lo_src = jnp.where(ok, acc, jnp.inf)
            acc = jnp.where(ok, acc, -jnp.inf)
        else:
            lo_src = acc
        sc_ref[kt] = acc
        return jnp.max(acc, axis=1, keepdims=True), jnp.min(lo_src, axis=1, keepdims=True)

    def score_body(kt, carry):
        mx, mn = carry
        tmx, tmn = score_tile(kt, False)
        return jnp.maximum(mx, tmx), jnp.minimum(mn, tmn)

    mx0 = jnp.full((TQI, 1), -jnp.inf, F32)
    mn0 = jnp.full((TQI, 1), jnp.inf, F32)
    mx, mn = lax.fori_loop(0, nkt - 1, score_body, (mx0, mn0))
    tmx, tmn = score_tile(nkt - 1, True)
    mx_ref[...] = jnp.maximum(mx, tmx)
    mn_ref[...] = jnp.minimum(mn, tmn)
    kf = float(top_k)

    def count(rows, pivot, strict):
        pb = jnp.broadcast_to(pivot, (CH, LANES))

        def body(kt, cnt):
            s = sc_ref[kt, rows, :]
            for c in range(TKI // LANES):
                blk = s[:, c * LANES:(c + 1) * LANES]
                hit = (blk > pb) if strict else (blk >= pb)
                cnt = cnt + jnp.where(hit, 1.0, 0.0)
            return cnt

        cnt = lax.fori_loop(0, nkt, body, jnp.zeros((CH, LANES), F32))
        return jnp.sum(cnt, axis=1, keepdims=True)

    def search_chunk(ci, carry):
        r0 = pl.multiple_of(ci * CH, CH)
        rows = pl.ds(r0, CH)
        n_causal = (i * TQI + r0 + lax.broadcasted_iota(jnp.int32, (CH, 1), 0) + 1).astype(F32)
        all_sel = n_causal <= kf
        state0 = dict(
            it=jnp.int32(0),
            lo=mn_ref[rows, :], hi=_key_float(_float_key(mx_ref[rows, :]) + 1),
            glo=jnp.log(jnp.maximum(n_causal, kf + 1.0) / kf), ghi=jnp.full((CH, 1), math.log(0.5 / kf), F32),
            thr=jnp.where(all_sel, NEG_BIG, 0.0).astype(F32),
            done=all_sel.astype(F32), tie=jnp.zeros((CH, 1), F32), side=jnp.zeros((CH, 1), F32),
        )

        def cond(st):
            return jnp.logical_and(st["it"] < 100, jnp.min(st["done"]) < 0.5)

        def step(st):
            it, lo, hi, glo, ghi = st["it"], st["lo"], st["hi"], st["glo"], st["ghi"]
            lo_k, hi_k = _float_key(lo), _float_key(hi)
            probe = jnp.logical_and(st["done"] < 0.5, lo_k + 1 < hi_k)
            tie_now = jnp.logical_and(st["done"] < 0.5, lo_k + 1 >= hi_k)
            frac = jnp.where(it < 24, glo / (glo - ghi), 0.5)
            pk = _float_key(lo + (hi - lo) * frac)
            pk = jnp.where(it < 64, pk, (lo_k >> 1) + (hi_k >> 1) + (lo_k & hi_k & 1))
            pk = jnp.where(probe, jnp.minimum(jnp.maximum(pk, lo_k + 1), hi_k - 1), lo_k)
            pf = _key_float(pk)
            c = count(rows, pf, False)
            hit = jnp.logical_and(probe, c == kf)
            up = jnp.logical_and(probe, c > kf)
            dn = jnp.logical_and(probe, c < kf)
            g = jnp.log(jnp.maximum(c, 0.5) / kf)
            return dict(
                it=it + 1,
                lo=jnp.where(up, pf, lo), hi=jnp.where(dn, pf, hi),
                glo=jnp.where(up, g, jnp.where(jnp.logical_and(dn, st["side"] < -0.5), 0.5 * glo, glo)),
                ghi=jnp.where(dn, g, jnp.where(jnp.logical_and(up, st["side"] > 0.5), 0.5 * ghi, ghi)),
                thr=jnp.where(hit, pf, jnp.where(tie_now, lo, st["thr"])),
                done=jnp.where(jnp.logical_or(hit, tie_now), 1.0, st["done"]),
                tie=jnp.where(tie_now, 1.0, st["tie"]),
                side=jnp.where(up, 1.0, jnp.where(dn, -1.0, st["side"])),
            )

        st = lax.while_loop(cond, step, state0)
        thr_ref[rows, :] = jnp.broadcast_to(st["thr"], (CH, LANES))
        tie_ref[rows, :] = st["tie"]
        return carry

    lax.fori_loop(0, TQI // CH, search_chunk, 0)
    thr_w = jnp.concatenate([thr_ref[...]] * (TKI // LANES), axis=1)
    tie = tie_ref[...]
    any_tie = jnp.max(tie) > 0.5

    def emit(kt, sel):
        mask_ref[0, kt] = jnp.where(sel, 0.0, NEG_BIG).astype(mask_ref.dtype)

    @pl.when(jnp.logical_not(any_tie))
    def _():
        def body(kt, carry):
            emit(kt, sc_ref[kt] >= thr_w)
            return carry
        lax.fori_loop(0, nkt, body, 0)

    @pl.when(any_tie)
    def _():
        thr = thr_ref[:, 0:1]
        above = jnp.concatenate([count(pl.ds(c * CH, CH), thr[c * CH:(c + 1) * CH], True)
                                 for c in range(TQI // CH)], axis=0)
        need = jnp.where(tie > 0.5, kf - above, float(2 * ntile * TKI))
        r = lax.broadcasted_iota(jnp.int32, (TKI, TKI), 0)
        c = lax.broadcasted_iota(jnp.int32, (TKI, TKI), 1)
        prefix = jnp.where(r <= c, 1.0, 0.0).astype(BF16)

        def body(kt, seen):
            s = sc_ref[kt]
            eq = s == thr_w
            rank = seen + _dot(jnp.where(eq, 1.0, 0.0).astype(BF16), prefix)
            emit(kt, jnp.logical_or(s > thr_w, jnp.logical_and(eq, rank <= need)))
            return seen + jnp.sum(jnp.where(eq, 1.0, 0.0), axis=1, keepdims=True)
        lax.fori_loop(0, nkt, body, jnp.zeros((TQI, 1), F32))

    def fill(kt, carry):
        mask_ref[0, kt] = jnp.full((TQI, TKI), NEG_BIG, mask_ref.dtype)
        return carry
    lax.fori_loop(nkt, ntile, fill, 0)


def _index_mask(qi, kw, ki2, top_k):
    S = qi.shape[0]
    nq, nk = S // TQI, S // TKI
    return pl.pallas_call(
        functools.partial(_index_kernel, top_k=top_k),
        grid=(nq,),
        in_specs=[pl.BlockSpec((TQI, IDX_HEADS * IDX_DIM), lambda i: (i, 0)),
                  pl.BlockSpec((TQI, LANES), lambda i: (i, 0)),
                  pl.BlockSpec((S, LANES), lambda i: (0, 0))],
        out_specs=pl.BlockSpec((1, nk, TQI, TKI), lambda i: (i, 0, 0, 0)),
        out_shape=jax.ShapeDtypeStruct((nq, nk, TQI, TKI), BF16),
        scratch_shapes=[pltpu.VMEM((nk, TQI, TKI), F32),
                        pltpu.VMEM((IDX_HEADS, TQI, LANES), BF16),
                        pltpu.VMEM((IDX_HEADS, TQI, LANES), F32),
                        pltpu.VMEM((TQI, 1), F32), pltpu.VMEM((TQI, 1), F32),
                        pltpu.VMEM((TQI, LANES), F32), pltpu.VMEM((TQI, 1), F32)],
        compiler_params=_params(("parallel",)),
        name="index_mask",
    )(qi, kw, ki2)


def _attn_kernel(qt_ref, kt_ref, q_ref, k_ref, v_ref, m_ref, bd_ref, be_ref, o_ref, s_sc, m_sc, l_sc, acc_sc):
    step = pl.program_id(0)
    qi = qt_ref[step]
    ki = kt_ref[step]
    nsub = TA // SUB

    @pl.when(ki == 0)
    def _():
        m_sc[...] = jnp.full(m_sc.shape, NEG_BIG, F32)
        l_sc[...] = jnp.zeros(l_sc.shape, F32)
        acc_sc[...] = jnp.zeros(acc_sc.shape, F32)

    maskf = jnp.concatenate([m_ref[a, 0] for a in range(TA // TQI)], axis=0).astype(F32)
    lane = lax.broadcasted_iota(jnp.int32, (TA, LANES), 1)
    first = lane < HEAD_DIM

    for p in range(N_HEADS // 2):
        cols = slice(p * LANES, (p + 1) * LANES)
        qp = q_ref[:, cols]
        kp = k_ref[:, cols]
        vp = v_ref[:, cols]
        upd = []
        for hh in range(2):
            h = 2 * p + hh
            qh = jnp.where(first if hh == 0 else jnp.logical_not(first), qp, jnp.zeros_like(qp))
            s_sc[...] = _dot_nt(qh, kp) + maskf

            @pl.when(ki == qi)
            def _():
                for a in range(nsub):
                    rows = slice(a * SUB, (a + 1) * SUB)
                    s_sc[rows, rows] = s_sc[rows, rows] + bd_ref[h]
                    if a > 0:
                        prev = slice((a - 1) * SUB, a * SUB)
                        s_sc[rows, prev] = s_sc[rows, prev] + be_ref[h]

            @pl.when(ki == qi - 1)
            def _():
                s_sc[0:SUB, TA - SUB:TA] = s_sc[0:SUB, TA - SUB:TA] + be_ref[h]

            s = s_sc[...]
            m_prev = m_sc[h]
            m_next = jnp.maximum(m_prev, jnp.max(s, axis=1, keepdims=True))
            alpha = jnp.exp(m_prev - m_next)
            pexp = jnp.exp(s - m_next[:, :1])
            l_sc[h] = alpha * l_sc[h] + jnp.sum(pexp, axis=1, keepdims=True)
            m_sc[h] = m_next
            upd.append((alpha, _dot(pexp.astype(BF16), vp)))
        acc = acc_sc[p]
        acc_sc[p] = jnp.where(first, upd[0][0] * acc + upd[0][1], upd[1][0] * acc + upd[1][1])

    @pl.when(ki == qi)
    def _():
        for p in range(N_HEADS // 2):
            inv = jnp.where(first, 1.0 / l_sc[2 * p], 1.0 / l_sc[2 * p + 1])
            o_ref[:, p * LANES:(p + 1) * LANES] = (acc_sc[p] * inv).astype(o_ref.dtype)


def _attention(q, k, v, mask4, bias_d, bias_e):
    S = q.shape[0]
    nb = S // TA
    pairs = [(a, b) for a in range(nb) for b in range(a + 1)]
    qtab = jnp.asarray(np.array([a for a, _ in pairs], np.int32))
    ktab = jnp.asarray(np.array([b for _, b in pairs], np.int32))
    grid_spec = pltpu.PrefetchScalarGridSpec(
        num_scalar_prefetch=2,
        grid=(len(pairs),),
        in_specs=[pl.BlockSpec((TA, ATTN_WIDTH), lambda s, qt, kt: (qt[s], 0)),
                  pl.BlockSpec((TA, ATTN_WIDTH), lambda s, qt, kt: (kt[s], 0)),
                  pl.BlockSpec((TA, ATTN_WIDTH), lambda s, qt, kt: (kt[s], 0)),
                  pl.BlockSpec((TA // TQI, TA // TKI, TQI, TKI), lambda s, qt, kt: (qt[s], kt[s], 0, 0)),
                  pl.BlockSpec(bias_d.shape, lambda s, qt, kt: (0, 0, 0)),
                  pl.BlockSpec(bias_e.shape, lambda s, qt, kt: (0, 0, 0))],
        out_specs=pl.BlockSpec((TA, ATTN_WIDTH), lambda s, qt, kt: (qt[s], 0)),
        scratch_shapes=[pltpu.VMEM((TA, TA), F32),
                        pltpu.VMEM((N_HEADS, TA, LANES), F32),
                        pltpu.VMEM((N_HEADS, TA, LANES), F32),
                        pltpu.VMEM((N_HEADS // 2, TA, LANES), F32)],
    )
    return pl.pallas_call(
        _attn_kernel,
        grid_spec=grid_spec,
        out_shape=jax.ShapeDtypeStruct((S, ATTN_WIDTH), BF16),
        compiler_params=_params(("arbitrary",)),
        name="attn",
    )(qtab, ktab, q, k, v, mask4, bias_d, bias_e)


def _relative_bias_blocks(rel_bias):
    dist = jnp.arange(2 * SUB, dtype=jnp.int32)
    max_exact = N_BUCKETS // 2
    dist_f = jnp.maximum(dist, 1).astype(F32)
    large = max_exact + (jnp.log(dist_f / max_exact) / math.log(MAX_DISTANCE / max_exact)
                         * (N_BUCKETS - max_exact)).astype(jnp.int32)
    bucket = jnp.where(dist < max_exact, dist, jnp.minimum(large, N_BUCKETS - 1))
    table = (rel_bias[bucket] - rel_bias[N_BUCKETS - 1]).astype(F32)
    i = jnp.arange(SUB)[:, None]
    j = jnp.arange(SUB)[None, :]
    diag = jnp.transpose(table[jnp.clip(i - j, 0, 2 * SUB - 1)], (2, 0, 1))
    sub = jnp.transpose(table[SUB + i - j], (2, 0, 1))
    return diag, sub


def _pack_bf16_pair(a, b):
    def rnd(x):
        bits = lax.bitcast_convert_type(x, jnp.uint32)
        return bits + jnp.uint32(0x7FFF) + ((bits >> 16) & jnp.uint32(1))
    return (rnd(a) >> 16) | (rnd(b) & jnp.uint32(0xFFFF0000))


def _unpack_bf16_pair(p):
    lo = lax.bitcast_convert_type(p << 16, F32)
    hi = lax.bitcast_convert_type(p & jnp.uint32(0xFFFF0000), F32)
    return lo.astype(BF16), hi.astype(BF16)


def _post_kernel(ya_ref, sga_ref, pc_ref, x_ref, woa_ref, wout_ref, g_ref, b_ref, wr_ref, br_ref,
                 x1_ref, x1p_ref, ridx_ref, rgate_ref):
    y_attn = _dot(ya_ref[...], woa_ref[...])
    merged = sga_ref[...] * y_attn + pc_ref[...]
    mix = _dot(merged.astype(BF16), wout_ref[...])
    x1 = _layer_norm(DEEPNORM_ALPHA * x_ref[...] + mix, g_ref[...], b_ref[...])
    x1_ref[...] = x1
    half = D_MODEL // 2
    x1p_ref[...] = _pack_bf16_pair(x1[:, :half], x1[:, half:])

    logits = jnp.dot(x1, wr_ref[...], preferred_element_type=F32, precision=lax.Precision.HIGHEST) + br_ref[...]
    lane = lax.broadcasted_iota(jnp.int32, logits.shape, 1).astype(F32)
    cur = logits
    vals, idxs = [], []
    for _ in range(TOP_K_EXPERTS):
        m = jnp.max(cur, axis=1, keepdims=True)
        ix = jnp.min(jnp.where(cur == m, lane, float(LANES)), axis=1, keepdims=True)
        vals.append(m)
        idxs.append(ix)
        cur = jnp.where(lane == ix, -jnp.inf, cur)
    exps = [jnp.exp(v - vals[0]) for v in vals]
    denom = exps[0]
    for e in exps[1:]:
        denom = denom + e
    ridx = jnp.zeros_like(logits)
    rgate = jnp.zeros_like(logits)
    for j in range(TOP_K_EXPERTS):
        ridx = jnp.where(lane == float(j), idxs[j], ridx)
        rgate = jnp.where(lane == float(j), exps[j] / denom, rgate)
    ridx_ref[...] = ridx.astype(jnp.int32)
    rgate_ref[...] = rgate


def _post(ya, sg, pc, x, woa, wout, g, b, wr, br):
    S = x.shape[0]
    full = lambda a: pl.BlockSpec(a.shape, lambda i: (0, 0))
    row = lambda n: pl.BlockSpec((TM, n), lambda i: (i, 0))
    return pl.pallas_call(
        _post_kernel,
        grid=(S // TM,),
        in_specs=[row(ATTN_WIDTH), row(D_MODEL), row(D_MODEL), row(D_MODEL)] + [full(a) for a in (woa, wout, g, b, wr, br)],
        out_specs=[row(D_MODEL), row(D_MODEL // 2), row(LANES), row(LANES)],
        out_shape=[jax.ShapeDtypeStruct((S, D_MODEL), F32), jax.ShapeDtypeStruct((S, D_MODEL // 2), jnp.uint32),
                   jax.ShapeDtypeStruct((S, LANES), jnp.int32), jax.ShapeDtypeStruct((S, LANES), F32)],
        compiler_params=_params(("parallel",)),
        name="post",
    )(ya, sg, pc, x, woa, wout, g, b, wr, br)


def _expert_kernel(te_ref, nv_ref, nused_ref, tok_ref, tokn_ref, dst_ref, x_hbm, wgu_ref, bgu_ref, wd_ref, bd_ref,
                   ys_hbm, xs_buf, y_buf, wgu_sc, wd_sc, gsem, ssem):
    t = pl.program_id(0)
    nused = nused_ref[0]
    cur = t % 2
    nxt = 1 - cur

    def start_gather(idx_ref, slot):
        def body(r, carry):
            pltpu.make_async_copy(x_hbm.at[idx_ref[0, 0, r]], xs_buf.at[slot, pl.ds(r, 1)], gsem.at[slot]).start()
            return carry
        lax.fori_loop(0, TME, body, 0)

    def wait_gather(slot):
        pltpu.make_async_copy(xs_buf.at[slot], xs_buf.at[slot], gsem.at[slot]).wait()

    def wait_scatter(slot, n):
        pltpu.make_async_copy(ys_hbm.at[pl.ds(0, n)], ys_hbm.at[pl.ds(0, n)], ssem.at[slot]).wait()

    @pl.when(t == 0)
    def _():
        start_gather(tok_ref, 0)

    @pl.when(t + 1 < nused)
    def _():
        start_gather(tokn_ref, nxt)

    e = te_ref[t]
    e_prev = te_ref[jnp.maximum(t - 1, 0)]

    @pl.when(jnp.logical_or(t == 0, e != e_prev))
    def _():
        wgu_sc[...] = wgu_ref[0].astype(BF16)
        wd_sc[...] = wd_ref[0].astype(BF16)

    @pl.when(t < nused)
    def _():
        wait_gather(cur)

        @pl.when(t >= 2)
        def _():
            wait_scatter(cur, nv_ref[jnp.maximum(t - 2, 0)])

        half = D_MODEL // 2
        lo, hi = _unpack_bf16_pair(xs_buf[cur])
        gu = _dot(lo, wgu_sc[0:half, :]) + _dot(hi, wgu_sc[half:, :]) + bgu_ref[0]
        g = jnp.minimum(gu[:, :D_EXPERT], SWIGLU_LIMIT)
        u = jnp.clip(gu[:, D_EXPERT:], -SWIGLU_LIMIT, SWIGLU_LIMIT)
        act = (u + 1.0) * (g * _sigmoid(SWIGLU_ALPHA * g))
        y_buf[cur] = _dot(act.astype(BF16), wd_sc[...]) + bd_ref[0]

        def scatter(r, carry):
            pltpu.make_async_copy(y_buf.at[cur, pl.ds(r, 1)], ys_hbm.at[dst_ref[0, 0, r]], ssem.at[cur]).start()
            return carry
        lax.fori_loop(0, nv_ref[t], scatter, 0)

        @pl.when(t == nused - 1)
        def _():
            @pl.when(t >= 1)
            def _():
                wait_scatter(nxt, nv_ref[jnp.maximum(t - 1, 0)])
            wait_scatter(cur, nv_ref[t])


def _experts(tile_e, tile_nv, nused, row_token, row_dst, x1p, wgu, bgu, wd, bd, n_tiles, n_dst):
    S = x1p.shape[0]
    idx = lambda a: a.reshape(n_tiles, 1, TME)
    smem_tile = lambda f: pl.BlockSpec((1, 1, TME), f, memory_space=pltpu.SMEM)
    grid_spec = pltpu.PrefetchScalarGridSpec(
        num_scalar_prefetch=3,
        grid=(n_tiles,),
        in_specs=[smem_tile(lambda t, te, nv, nu: (t, 0, 0)),
                  smem_tile(lambda t, te, nv, nu: (jnp.minimum(t + 1, n_tiles - 1), 0, 0)),
                  smem_tile(lambda t, te, nv, nu: (t, 0, 0)),
                  pl.BlockSpec(memory_space=pl.ANY),
                  pl.BlockSpec((1, D_MODEL, 2 * D_EXPERT), lambda t, te, nv, nu: (te[t], 0, 0)),
                  pl.BlockSpec((1, 1, 2 * D_EXPERT), lambda t, te, nv, nu: (te[t], 0, 0)),
                  pl.BlockSpec((1, D_EXPERT, D_MODEL), lambda t, te, nv, nu: (te[t], 0, 0)),
                  pl.BlockSpec((1, 1, D_MODEL), lambda t, te, nv, nu: (te[t], 0, 0))],
        out_specs=pl.BlockSpec(memory_space=pl.ANY),
        scratch_shapes=[pltpu.VMEM((2, TME, D_MODEL // 2), jnp.uint32), pltpu.VMEM((2, TME, D_MODEL), F32),
                        pltpu.VMEM((D_MODEL, 2 * D_EXPERT), BF16), pltpu.VMEM((D_EXPERT, D_MODEL), BF16),
                        pltpu.SemaphoreType.DMA((2,)), pltpu.SemaphoreType.DMA((2,))],
    )
    return pl.pallas_call(
        _expert_kernel,
        grid_spec=grid_spec,
        out_shape=jax.ShapeDtypeStruct((n_dst, 1, D_MODEL), F32),
        compiler_params=_params(("arbitrary",)),
        name="moe_experts",
    )(tile_e, tile_nv, nused, idx(row_token), idx(row_token), idx(row_dst), x1p.reshape(S, 1, D_MODEL // 2),
      wgu, bgu, wd, bd)


def _routing_tables(ridx, S):
    n_flat = S * TOP_K_EXPERTS
    n_tiles = n_flat // TME + N_EXPERTS
    flat_e = ridx.reshape(n_flat)
    order = jnp.argsort(flat_e, stable=True).astype(jnp.int32)
    counts = jnp.sum(flat_e[:, None] == jnp.arange(N_EXPERTS, dtype=jnp.int32)[None, :], axis=0, dtype=jnp.int32)
    padded = ((counts + TME - 1) // TME) * TME
    pad_end = jnp.cumsum(padded)
    pad_start = pad_end - padded
    grp_start = jnp.cumsum(counts) - counts
    nused = (pad_end[-1] // TME).astype(jnp.int32).reshape(1)
    tile_row0 = jnp.arange(n_tiles, dtype=jnp.int32) * TME
    tile_e = jnp.minimum(jnp.sum(tile_row0[:, None] >= pad_end[None, :], axis=1), N_EXPERTS - 1).astype(jnp.int32)
    rows = jnp.arange(n_tiles * TME, dtype=jnp.int32)
    row_e = jnp.repeat(tile_e, TME)
    rank = rows - pad_start[row_e]
    valid = jnp.logical_and(rank < counts[row_e], rows < pad_end[-1])
    src_flat = order[jnp.clip(grp_start[row_e] + rank, 0, n_flat - 1)]
    row_token = jnp.where(valid, src_flat // TOP_K_EXPERTS, 0).astype(jnp.int32)
    row_dst = jnp.where(valid, (src_flat % TOP_K_EXPERTS) * S + src_flat // TOP_K_EXPERTS, n_flat).astype(jnp.int32)
    tile_nv = jnp.sum(valid.reshape(n_tiles, TME), axis=1, dtype=jnp.int32)
    return tile_e, tile_nv, nused, row_token, row_dst, n_tiles


def _final_kernel(x1_ref, y0_ref, y1_ref, y2_ref, y3_ref, rg_ref, p_ref, wpg_ref, wpp_ref, g_ref, b_ref, o_ref):
    h = DEEPNORM_ALPHA * x1_ref[...]
    rg = rg_ref[...]
    for j, y_ref in enumerate((y0_ref, y1_ref, y2_ref, y3_ref)):
        h = h + rg[:, j:j + 1] * y_ref[:, 0, :]
    ple = _sigmoid(_dot(h.astype(BF16), wpg_ref[...])) * _dot(p_ref[...].astype(BF16), wpp_ref[...])
    o_ref[...] = _layer_norm(h + ple, g_ref[...], b_ref[...])


def _final(x1, ys, rgate, p, wpg, wpp, g, b):
    S = x1.shape[0]
    full = lambda a: pl.BlockSpec(a.shape, lambda i: (0, 0))
    row = lambda n: pl.BlockSpec((TM, n), lambda i: (i, 0))
    return pl.pallas_call(
        _final_kernel,
        grid=(S // TM,),
        in_specs=[row(D_MODEL)]
        + [pl.BlockSpec((TM, 1, D_MODEL), lambda i, j=j: (j * (S // TM) + i, 0, 0)) for j in range(TOP_K_EXPERTS)]
        + [row(LANES), row(PLE_DIM), full(wpg), full(wpp), full(g), full(b)],
        out_specs=row(D_MODEL),
        out_shape=jax.ShapeDtypeStruct((S, D_MODEL), F32),
        compiler_params=_params(("parallel",)),
        name="final",
    )(x1, ys, ys, ys, ys, rgate, p, wpg, wpp, g, b)


def _layer(x, p, w_in, b_in, w_o_attn, w_dw, b_dw, conv_ln_g, conv_ln_b, w_o_conv, w_out, ln1_g, ln1_b,
           w_router, b_router, w_gate_up, b_gate_up, w_down, b_down, w_ple_gate, w_ple_proj, ln2_g, ln2_b,
           rel_bias):
    S = x.shape[0]
    assert S % TM == 0 and S % TA == 0 and S % TKI == 0 and (S * TOP_K_EXPERTS) % TME == 0
    top_k = min(TOPK_MAX, S // 4)
    row2 = lambda a: a.reshape(1, -1).astype(F32)

    o_q, o_qi, o_ki, o_cv, o_g = 0, 3 * ATTN_WIDTH, 3 * ATTN_WIDTH + IDX_HEADS * IDX_DIM, \
        3 * ATTN_WIDTH + IDX_HEADS * IDX_DIM + IDX_DIM + IDX_HEADS, \
        3 * ATTN_WIDTH + IDX_HEADS * IDX_DIM + IDX_DIM + IDX_HEADS + 2 * CONV_CH
    kw_pad = LANES - (IDX_DIM + IDX_HEADS)
    wkw = jnp.pad(w_in[:, o_ki:o_cv], ((0, 0), (0, kw_pad)))
    bkw = jnp.pad(b_in[o_ki:o_cv], (0, kw_pad))
    q, k, v, qi, kw, u, sg = _proj(
        x, w_in[:, o_q:o_qi].astype(BF16), row2(b_in[o_q:o_qi]),
        w_in[:, o_qi:o_ki].astype(BF16), row2(b_in[o_qi:o_ki]),
        wkw.astype(BF16), row2(bkw),
        w_in[:, o_cv:o_g].astype(BF16), row2(b_in[o_cv:o_g]),
        w_in[:, o_g:].astype(BF16), row2(b_in[o_g:]))

    part_conv = _conv(u, sg, w_dw, row2(b_dw), row2(conv_ln_g), row2(conv_ln_b), w_o_conv.astype(BF16))

    ki = kw[:, :IDX_DIM].astype(BF16)
    mask4 = _index_mask(qi, kw, jnp.concatenate([ki, ki], axis=1), top_k)
    bias_d, bias_e = _relative_bias_blocks(rel_bias)
    y_attn = _attention(q, k, v, mask4, bias_d, bias_e)

    wr = jnp.pad(w_router, ((0, 0), (0, LANES - N_EXPERTS)))
    br = jnp.pad(b_router, (0, LANES - N_EXPERTS), constant_values=-jnp.inf)
    x1, x1p, ridx, rgate = _post(y_attn, sg, part_conv, x, w_o_attn.astype(BF16), w_out.astype(BF16),
                                 row2(ln1_g), row2(ln1_b), wr, row2(br))

    tile_e, tile_nv, nused, row_token, row_dst, n_tiles = _routing_tables(ridx[:, :TOP_K_EXPERTS], S)
    ys = _experts(tile_e, tile_nv, nused, row_token, row_dst, x1p, w_gate_up, b_gate_up.reshape(N_EXPERTS, 1, -1),
                  w_down, b_down.reshape(N_EXPERTS, 1, -1), n_tiles, S * TOP_K_EXPERTS)
    return _final(x1, ys, rgate, p, w_ple_gate.astype(BF16), w_ple_proj.astype(BF16), row2(ln2_g), row2(ln2_b))


def kernel(x, p, w_in, b_in, w_o_attn, w_dw, b_dw, conv_ln_g, conv_ln_b, w_o_conv, w_out, ln1_g, ln1_b, w_router, b_router, w_gate_up, b_gate_up, w_down, b_down, w_ple_gate, w_ple_proj, ln2_g, ln2_b, rel_bias):
    assert x.shape[0] == 1 and p.shape[0] == DEPTH
    out = _layer(x[0], p[0, 0], w_in[0], b_in[0], w_o_attn[0], w_dw[0], b_dw[0], conv_ln_g[0], conv_ln_b[0],
                 w_o_conv[0], w_out[0], ln1_g[0], ln1_b[0], w_router[0], b_router[0], w_gate_up[0], b_gate_up[0],
                 w_down[0], b_down[0], w_ple_gate[0], w_ple_proj[0], ln2_g[0], ln2_b[0], rel_bias)
    return out[None]
```

```python
import functools
import math

import jax
import jax.numpy as jnp
import numpy as np
from jax import lax
from jax.experimental import pallas as pl
from jax.experimental.pallas import tpu as pltpu

F32 = jnp.float32
BF16 = jnp.bfloat16

D_MODEL = 1024
N_HEADS = 8
HEAD_DIM = 64
ATTN_WIDTH = N_HEADS * HEAD_DIM
ATTN_SCALE = HEAD_DIM ** -0.5
IDX_HEADS = 8
IDX_DIM = 64
IDX_SCALE = (IDX_HEADS ** -0.5) * (IDX_DIM ** -0.5)
TOPK_MAX = 256
CONV_CH = 512
CONV_WIDTH = 31
N_BUCKETS = 32
MAX_DISTANCE = 128
N_EXPERTS = 32
TOP_K_EXPERTS = 4
D_EXPERT = 1024
SWIGLU_LIMIT = 7.0
SWIGLU_ALPHA = 1.702
PLE_DIM = 256
LN_EPS = 1e-5
DEPTH = 1
DEEPNORM_ALPHA = (2 * DEPTH) ** 0.25

LANES = 128
NEG_BIG = -1e30
VMEM_LIMIT = 56 * 1024 * 1024

TM = 512
TQI = 256
TKI = 512
TA = 512
SUB = 128
TME = 256
DMA_UNROLL = 8
CONV_HALO = 32


def _params(sem, vmem=VMEM_LIMIT):
    return pltpu.CompilerParams(dimension_semantics=sem, vmem_limit_bytes=vmem)


def _sigmoid(x):
    return 1.0 / (1.0 + jnp.exp(-x))


def _layer_norm(x, g, b):
    mu = jnp.mean(x, axis=-1, keepdims=True)
    xc = x - mu
    var = jnp.mean(xc * xc, axis=-1, keepdims=True)
    return xc * lax.rsqrt(var + LN_EPS) * g + b


def _dot(a, b):
    return jnp.dot(a, b, preferred_element_type=F32)


def _dot_nt(a, b):
    return lax.dot_general(a, b, (((1,), (1,)), ((), ())), preferred_element_type=F32)


def _proj_kernel(x_ref, wqkv_ref, bqkv_ref, wqi_ref, bqi_ref, wkw_ref, bkw_ref, wcv_ref, bcv_ref,
                 wg_ref, bg_ref, q_ref, k_ref, v_ref, qi_ref, kw_ref, u_ref, sg_ref):
    xb = x_ref[...].astype(BF16)
    qkv = _dot(xb, wqkv_ref[...]) + bqkv_ref[...]
    q_ref[...] = (qkv[:, :ATTN_WIDTH] * ATTN_SCALE).astype(BF16)
    k_ref[...] = qkv[:, ATTN_WIDTH:2 * ATTN_WIDTH].astype(BF16)
    v_ref[...] = qkv[:, 2 * ATTN_WIDTH:].astype(BF16)
    qi_ref[...] = (_dot(xb, wqi_ref[...]) + bqi_ref[...]).astype(BF16)
    kw = _dot(xb, wkw_ref[...]) + bkw_ref[...]
    lane = lax.broadcasted_iota(jnp.int32, kw.shape, 1)
    kw_ref[...] = jnp.where(lane >= IDX_DIM, kw * IDX_SCALE, kw)
    cv = _dot(xb, wcv_ref[...]) + bcv_ref[...]
    u_ref[...] = cv[:, :CONV_CH] * _sigmoid(cv[:, CONV_CH:])
    sg_ref[...] = _sigmoid(_dot(xb, wg_ref[...]) + bg_ref[...])


def _proj(x, wqkv, bqkv, wqi, bqi, wkw, bkw, wcv, bcv, wg, bg):
    S = x.shape[0]
    full = lambda a: pl.BlockSpec(a.shape, lambda i: (0, 0))
    row = lambda n: pl.BlockSpec((TM, n), lambda i: (i, 0))
    outs = [(ATTN_WIDTH, BF16)] * 3 + [(IDX_HEADS * IDX_DIM, BF16), (LANES, F32), (CONV_CH, F32),
                                        (2 * D_MODEL, F32)]
    return pl.pallas_call(
        _proj_kernel,
        grid=(S // TM,),
        in_specs=[row(D_MODEL)] + [full(a) for a in (wqkv, bqkv, wqi, bqi, wkw, bkw, wcv, bcv, wg, bg)],
        out_specs=[row(n) for n, _ in outs],
        out_shape=[jax.ShapeDtypeStruct((S, n), dt) for n, dt in outs],
        compiler_params=_params(("parallel",)),
        name="proj",
    )(x, wqkv, bqkv, wqi, bqi, wkw, bkw, wcv, bcv, wg, bg)


def _conv_kernel(u_ref, wdw_ref, bdw_ref, lng_ref, lnb_ref, wo_ref, sgc_ref, o_ref, buf_ref):
    @pl.when(pl.program_id(0) == 0)
    def _():
        buf_ref[0:CONV_HALO, :] = jnp.zeros((CONV_HALO, CONV_CH), F32)

    buf_ref[CONV_HALO:CONV_HALO + TM, :] = u_ref[...]
    base = CONV_HALO - (CONV_WIDTH - 1)
    acc = jnp.zeros((TM, CONV_CH), F32) + bdw_ref[...]
    for j in range(CONV_WIDTH):
        acc = acc + wdw_ref[j:j + 1, :] * buf_ref[base + j:base + j + TM, :]
    buf_ref[0:CONV_HALO, :] = buf_ref[TM:TM + CONV_HALO, :]
    y = _layer_norm(acc, lng_ref[...], lnb_ref[...])
    y = y * _sigmoid(y)
    o_ref[...] = sgc_ref[...] * _dot(y.astype(BF16), wo_ref[...])


def _conv(u, sg, wdw, bdw, lng, lnb, wo):
    S = u.shape[0]
    full = lambda a: pl.BlockSpec(a.shape, lambda i: (0, 0))
    return pl.pallas_call(
        _conv_kernel,
        grid=(S // TM,),
        in_specs=[pl.BlockSpec((TM, CONV_CH), lambda i: (i, 0)), full(wdw), full(bdw), full(lng), full(lnb),
                  full(wo), pl.BlockSpec((TM, D_MODEL), lambda i: (i, 1))],
        out_specs=pl.BlockSpec((TM, D_MODEL), lambda i: (i, 0)),
        out_shape=jax.ShapeDtypeStruct((S, D_MODEL), F32),
        scratch_shapes=[pltpu.VMEM((TM + CONV_HALO, CONV_CH), F32)],
        compiler_params=_params(("arbitrary",)),
        name="conv",
    )(u, wdw, bdw, lng, lnb, wo, sg)


def _float_key(f):
    b = lax.bitcast_convert_type(f, jnp.int32)
    return b ^ ((b >> 31) & jnp.int32(0x7FFFFFFF))


def _key_float(k):
    b = k ^ ((k >> 31) & jnp.int32(0x7FFFFFFF))
    return lax.bitcast_convert_type(b, F32)


def _index_kernel(qi_ref, wt_ref, ki_ref, mask_ref, sc_ref, qh_ref, *, top_k):
    i = pl.program_id(0)
    nkt = (i * TQI + TQI + TKI - 1) // TKI
    ntile = sc_ref.shape[0]
    q_g = i * TQI + lax.broadcasted_iota(jnp.int32, (1, TQI), 1)
    lane = lax.broadcasted_iota(jnp.int32, (TQI, LANES), 1)

    for h in range(IDX_HEADS):
        qp = qi_ref[:, (h // 2) * LANES:(h // 2 + 1) * LANES]
        keep = (lane < IDX_DIM) if h % 2 == 0 else (lane >= IDX_DIM)
        qh_ref[h] = jnp.where(keep, qp, jnp.zeros_like(qp))

    def score_tile(kt, causal):
        kb = ki_ref[pl.ds(pl.multiple_of(kt * TKI, TKI), TKI), :]
        acc = jnp.zeros((TKI, TQI), F32)
        for h in range(IDX_HEADS):
            acc = acc + wt_ref[h:h + 1, :] * jnp.maximum(_dot_nt(kb, qh_ref[h]), 0.0)
        if causal:
            key_g = kt * TKI + lax.broadcasted_iota(jnp.int32, (TKI, TQI), 0)
            ok = key_g <= q_g
            lo_src = jnp.where(ok, acc, jnp.inf)
            acc = jnp.where(ok, acc, -jnp.inf)
        else:
            lo_src = acc
        sc_ref[kt] = acc
        return jnp.max(acc, axis=0, keepdims=True), jnp.min(lo_src, axis=0, keepdims=True)

    def score_body(kt, carry):
        mx, mn = carry
        tmx, tmn = score_tile(kt, False)
        return jnp.maximum(mx, tmx), jnp.minimum(mn, tmn)

    mx0 = jnp.full((1, TQI), -jnp.inf, F32)
    mn0 = jnp.full((1, TQI), jnp.inf, F32)
    mx, mn = lax.fori_loop(0, nkt - 1, score_body, (mx0, mn0))
    tmx, tmn = score_tile(nkt - 1, True)
    mx = jnp.maximum(mx, tmx)
    mn = jnp.minimum(mn, tmn)
    kf = float(top_k)
    sub = 8

    def count(pivot, strict):
        def body(kt, cnt):
            for r in range(TKI // sub):
                blk = sc_ref[kt, r * sub:(r + 1) * sub, :]
                hit = (blk > pivot) if strict else (blk >= pivot)
                cnt = cnt + jnp.where(hit, 1.0, 0.0)
            return cnt

        cnt = lax.fori_loop(0, nkt, body, jnp.zeros((sub, TQI), F32))
        return jnp.sum(cnt, axis=0, keepdims=True)

    n_causal = (q_g + 1).astype(F32)
    all_sel = n_causal <= kf
    state0 = dict(
        it=jnp.int32(0),
        lo=mn, hi=_key_float(_float_key(mx) + 1),
        glo=jnp.log(jnp.maximum(n_causal, kf + 1.0) / kf), ghi=jnp.full((1, TQI), math.log(0.5 / kf), F32),
        thr=jnp.where(all_sel, NEG_BIG, 0.0).astype(F32),
        done=all_sel.astype(F32), tie=jnp.zeros((1, TQI), F32), side=jnp.zeros((1, TQI), F32),
    )

    def cond(st):
        return jnp.logical_and(st["it"] < 100, jnp.min(st["done"]) < 0.5)

    def step(st):
        it, lo, hi, glo, ghi = st["it"], st["lo"], st["hi"], st["glo"], st["ghi"]
        lo_k, hi_k = _float_key(lo), _float_key(hi)
        probe = jnp.logical_and(st["done"] < 0.5, lo_k + 1 < hi_k)
        tie_now = jnp.logical_and(st["done"] < 0.5, lo_k + 1 >= hi_k)
        frac = jnp.where(it < 24, glo / (glo - ghi), 0.5)
        pk = _float_key(lo + (hi - lo) * frac)
        pk = jnp.where(it < 64, pk, (lo_k >> 1) + (hi_k >> 1) + (lo_k & hi_k & 1))
        pk = jnp.where(probe, jnp.minimum(jnp.maximum(pk, lo_k + 1), hi_k - 1), lo_k)
        pf = _key_float(pk)
        c = count(pf, False)
        hit = jnp.logical_and(probe, c == kf)
        up = jnp.logical_and(probe, c > kf)
        dn = jnp.logical_and(probe, c < kf)
        g = jnp.log(jnp.maximum(c, 0.5) / kf)
        return dict(
            it=it + 1,
            lo=jnp.where(up, pf, lo), hi=jnp.where(dn, pf, hi),
            glo=jnp.where(up, g, jnp.where(jnp.logical_and(dn, st["side"] < -0.5), 0.5 * glo, glo)),
            ghi=jnp.where(dn, g, jnp.where(jnp.logical_and(up, st["side"] > 0.5), 0.5 * ghi, ghi)),
            thr=jnp.where(hit, pf, jnp.where(tie_now, lo, st["thr"])),
            done=jnp.where(jnp.logical_or(hit, tie_now), 1.0, st["done"]),
            tie=jnp.where(tie_now, 1.0, st["tie"]),
            side=jnp.where(up, 1.0, jnp.where(dn, -1.0, st["side"])),
        )

    st = lax.while_loop(cond, step, state0)
    thr = st["thr"]
    tie = st["tie"]
    any_tie = jnp.max(tie) > 0.5

    def emit(kt, sel):
        mask_ref[0, kt] = jnp.where(sel, 0.0, NEG_BIG).T.astype(mask_ref.dtype)

    @pl.when(jnp.logical_not(any_tie))
    def _():
        def body(kt, carry):
            emit(kt, sc_ref[kt] >= thr)
            return carry
        lax.fori_loop(0, nkt, body, 0)

    @pl.when(any_tie)
    def _():
        need = jnp.where(tie > 0.5, kf - count(thr, True), float(2 * ntile * TKI))
        r = lax.broadcasted_iota(jnp.int32, (TKI, TKI), 0)
        c = lax.broadcasted_iota(jnp.int32, (TKI, TKI), 1)
        prefix = jnp.where(c <= r, 1.0, 0.0).astype(BF16)

        def body(kt, seen):
            s = sc_ref[kt]
            eq = s == thr
            rank = seen + _dot(prefix, jnp.where(eq, 1.0, 0.0).astype(BF16))
            emit(kt, jnp.logical_or(s > thr, jnp.logical_and(eq, rank <= need)))
            return seen + jnp.sum(jnp.where(eq, 1.0, 0.0), axis=0, keepdims=True)
        lax.fori_loop(0, nkt, body, jnp.zeros((1, TQI), F32))

    def fill(kt, carry):
        mask_ref[0, kt] = jnp.full((TQI, TKI), NEG_BIG, mask_ref.dtype)
        return carry
    lax.fori_loop(nkt, ntile, fill, 0)


def _index_mask(qi, wt, ki2, top_k):
    S = qi.shape[0]
    nq, nk = S // TQI, S // TKI
    return pl.pallas_call(
        functools.partial(_index_kernel, top_k=top_k),
        grid=(nq,),
        in_specs=[pl.BlockSpec((TQI, IDX_HEADS * IDX_DIM), lambda i: (i, 0)),
                  pl.BlockSpec((IDX_HEADS, TQI), lambda i: (0, i)),
                  pl.BlockSpec((S, LANES), lambda i: (0, 0))],
        out_specs=pl.BlockSpec((1, nk, TQI, TKI), lambda i: (i, 0, 0, 0)),
        out_shape=jax.ShapeDtypeStruct((nq, nk, TQI, TKI), BF16),
        scratch_shapes=[pltpu.VMEM((nk, TKI, TQI), F32),
                        pltpu.VMEM((IDX_HEADS, TQI, LANES), BF16)],
        compiler_params=_params(("parallel",)),
        name="index_mask",
    )(qi, wt, ki2)


def _attn_kernel(qt_ref, kt_ref, q_ref, k_ref, v_ref, m_ref, bd_ref, be_ref, o_ref, m_sc, l_sc, acc_sc):
    step = pl.program_id(0)
    qi = qt_ref[step]
    ki = kt_ref[step]
    nsub = TA // SUB

    @pl.when(ki == 0)
    def _():
        m_sc[...] = jnp.full(m_sc.shape, NEG_BIG, F32)
        l_sc[...] = jnp.zeros(l_sc.shape, F32)
        acc_sc[...] = jnp.zeros(acc_sc.shape, F32)

    lane = lax.broadcasted_iota(jnp.int32, (TA, LANES), 1)
    first = lane < HEAD_DIM

    def bias_tile(h, diagonal):
        zero = jnp.zeros((SUB, SUB), F32)
        rows = []
        for a in range(nsub):
            if diagonal:
                blks = [bd_ref[h] if b == a else (be_ref[h] if b == a - 1 else zero) for b in range(nsub)]
            else:
                blks = [be_ref[h] if (a == 0 and b == nsub - 1) else zero for b in range(nsub)]
            rows.append(jnp.concatenate(blks, axis=1))
        return jnp.concatenate(rows, axis=0)

    def update(near, diagonal):
        maskf = jnp.concatenate([m_ref[a, 0] for a in range(TA // TQI)], axis=0).astype(F32)
        for p in range(N_HEADS // 2):
            cols = slice(p * LANES, (p + 1) * LANES)
            qp = q_ref[:, cols]
            kp = k_ref[:, cols]
            vp = v_ref[:, cols]
            upd = []
            for hh in range(2):
                h = 2 * p + hh
                qh = jnp.where(first if hh == 0 else jnp.logical_not(first), qp, jnp.zeros_like(qp))
                s = _dot_nt(qh, kp) + maskf
                if near:
                    s = s + bias_tile(h, diagonal)
                m_prev = m_sc[h]
                m_next = jnp.maximum(m_prev, jnp.max(s, axis=1, keepdims=True))
                alpha = jnp.exp(m_prev - m_next)
                pexp = jnp.exp(s - m_next[:, :1])
                l_sc[h] = alpha * l_sc[h] + jnp.sum(pexp, axis=1, keepdims=True)
                m_sc[h] = m_next
                upd.append((alpha, _dot(pexp.astype(BF16), vp)))
            acc = acc_sc[p]
            acc_sc[p] = jnp.where(first, upd[0][0] * acc + upd[0][1], upd[1][0] * acc + upd[1][1])

    @pl.when(ki < qi - 1)
    def _():
        update(False, False)

    @pl.when(ki == qi - 1)
    def _():
        update(True, False)

    @pl.when(ki == qi)
    def _():
        update(True, True)
        for p in range(N_HEADS // 2):
            inv = jnp.where(first, 1.0 / l_sc[2 * p], 1.0 / l_sc[2 * p + 1])
            o_ref[:, p * LANES:(p + 1) * LANES] = (acc_sc[p] * inv).astype(o_ref.dtype)


def _attention(q, k, v, mask4, bias_d, bias_e):
    S = q.shape[0]
    nb = S // TA
    pairs = [(a, b) for a in range(nb) for b in range(a + 1)]
    qtab = jnp.asarray(np.array([a for a, _ in pairs], np.int32))
    ktab = jnp.asarray(np.array([b for _, b in pairs], np.int32))
    grid_spec = pltpu.PrefetchScalarGridSpec(
        num_scalar_prefetch=2,
        grid=(len(pairs),),
        in_specs=[pl.BlockSpec((TA, ATTN_WIDTH), lambda s, qt, kt: (qt[s], 0)),
                  pl.BlockSpec((TA, ATTN_WIDTH), lambda s, qt, kt: (kt[s], 0)),
                  pl.BlockSpec((TA, ATTN_WIDTH), lambda s, qt, kt: (kt[s], 0)),
                  pl.BlockSpec((TA // TQI, TA // TKI, TQI, TKI), lambda s, qt, kt: (qt[s], kt[s], 0, 0)),
                  pl.BlockSpec(bias_d.shape, lambda s, qt, kt: (0, 0, 0)),
                  pl.BlockSpec(bias_e.shape, lambda s, qt, kt: (0, 0, 0))],
        out_specs=pl.BlockSpec((TA, ATTN_WIDTH), lambda s, qt, kt: (qt[s], 0)),
        scratch_shapes=[pltpu.VMEM((N_HEADS, TA, LANES), F32),
                        pltpu.VMEM((N_HEADS, TA, LANES), F32),
                        pltpu.VMEM((N_HEADS // 2, TA, LANES), F32)],
    )
    return pl.pallas_call(
        _attn_kernel,
        grid_spec=grid_spec,
        out_shape=jax.ShapeDtypeStruct((S, ATTN_WIDTH), BF16),
        compiler_params=_params(("arbitrary",)),
        name="attn",
    )(qtab, ktab, q, k, v, mask4, bias_d, bias_e)


def _relative_bias_blocks(rel_bias):
    dist = jnp.arange(2 * SUB, dtype=jnp.int32)
    max_exact = N_BUCKETS // 2
    dist_f = jnp.maximum(dist, 1).astype(F32)
    large = max_exact + (jnp.log(dist_f / max_exact) / math.log(MAX_DISTANCE / max_exact)
                         * (N_BUCKETS - max_exact)).astype(jnp.int32)
    bucket = jnp.where(dist < max_exact, dist, jnp.minimum(large, N_BUCKETS - 1))
    table = (rel_bias[bucket] - rel_bias[N_BUCKETS - 1]).astype(F32)
    i = jnp.arange(SUB)[:, None]
    j = jnp.arange(SUB)[None, :]
    diag = jnp.transpose(table[jnp.clip(i - j, 0, 2 * SUB - 1)], (2, 0, 1))
    sub = jnp.transpose(table[SUB + i - j], (2, 0, 1))
    return diag, sub


def _pack_bf16_pair(a, b):
    def rnd(x):
        bits = lax.bitcast_convert_type(x, jnp.uint32)
        return bits + jnp.uint32(0x7FFF) + ((bits >> 16) & jnp.uint32(1))
    return (rnd(a) >> 16) | (rnd(b) & jnp.uint32(0xFFFF0000))


def _unpack_bf16_pair(p):
    lo = lax.bitcast_convert_type(p << 16, F32)
    hi = lax.bitcast_convert_type(p & jnp.uint32(0xFFFF0000), F32)
    return lo.astype(BF16), hi.astype(BF16)


def _post_kernel(ya_ref, sga_ref, pc_ref, x_ref, woa_ref, wout_ref, g_ref, b_ref, wr_ref, br_ref,
                 x1_ref, x1p_ref, ridx_ref, rgate_ref):
    y_attn = _dot(ya_ref[...], woa_ref[...])
    merged = sga_ref[...] * y_attn + pc_ref[...]
    mix = _dot(merged.astype(BF16), wout_ref[...])
    x1 = _layer_norm(DEEPNORM_ALPHA * x_ref[...] + mix, g_ref[...], b_ref[...])
    x1_ref[...] = x1
    half = D_MODEL // 2
    x1p_ref[...] = _pack_bf16_pair(x1[:, :half], x1[:, half:])

    logits = jnp.dot(x1, wr_ref[...], preferred_element_type=F32, precision=lax.Precision.HIGHEST) + br_ref[...]
    lane = lax.broadcasted_iota(jnp.int32, logits.shape, 1).astype(F32)
    cur = logits
    vals, idxs = [], []
    for _ in range(TOP_K_EXPERTS):
        m = jnp.max(cur, axis=1, keepdims=True)
        ix = jnp.min(jnp.where(cur == m, lane, float(LANES)), axis=1, keepdims=True)
        vals.append(m)
        idxs.append(ix)
        cur = jnp.where(lane == ix, -jnp.inf, cur)
    exps = [jnp.exp(v - vals[0]) for v in vals]
    denom = exps[0]
    for e in exps[1:]:
        denom = denom + e
    ridx = jnp.zeros_like(logits)
    rgate = jnp.zeros_like(logits)
    for j in range(TOP_K_EXPERTS):
        ridx = jnp.where(lane == float(j), idxs[j], ridx)
        rgate = jnp.where(lane == float(j), exps[j] / denom, rgate)
    ridx_ref[...] = ridx.astype(jnp.int32)
    rgate_ref[...] = rgate


def _post(ya, sg, pc, x, woa, wout, g, b, wr, br):
    S = x.shape[0]
    full = lambda a: pl.BlockSpec(a.shape, lambda i: (0, 0))
    row = lambda n: pl.BlockSpec((TM, n), lambda i: (i, 0))
    return pl.pallas_call(
        _post_kernel,
        grid=(S // TM,),
        in_specs=[row(ATTN_WIDTH), row(D_MODEL), row(D_MODEL), row(D_MODEL)] + [full(a) for a in (woa, wout, g, b, wr, br)],
        out_specs=[row(D_MODEL), row(D_MODEL // 2), row(LANES), row(LANES)],
        out_shape=[jax.ShapeDtypeStruct((S, D_MODEL), F32), jax.ShapeDtypeStruct((S, D_MODEL // 2), jnp.uint32),
                   jax.ShapeDtypeStruct((S, LANES), jnp.int32), jax.ShapeDtypeStruct((S, LANES), F32)],
        compiler_params=_params(("parallel",)),
        name="post",
    )(ya, sg, pc, x, woa, wout, g, b, wr, br)


def _expert_kernel(te_ref, nv_ref, nused_ref, tok_ref, tokn_ref, dst_ref, x_hbm, wgu_ref, bgu_ref, wd_ref, bd_ref,
                   ys_hbm, xs_buf, y_buf, wgu_sc, wd_sc, gsem, ssem):
    t = pl.program_id(0)
    nused = nused_ref[0]
    cur = t % 2
    nxt = 1 - cur

    def start_gather(idx_ref, slot):
        def body(r, carry):
            pltpu.make_async_copy(x_hbm.at[idx_ref[0, 0, r]], xs_buf.at[slot, pl.ds(r, 1)], gsem.at[slot]).start()
            return carry
        lax.fori_loop(0, TME, body, 0, unroll=DMA_UNROLL)

    def wait_gather(slot):
        pltpu.make_async_copy(xs_buf.at[slot], xs_buf.at[slot], gsem.at[slot]).wait()

    def wait_scatter(slot, n):
        pltpu.make_async_copy(ys_hbm.at[pl.ds(0, n)], ys_hbm.at[pl.ds(0, n)], ssem.at[slot]).wait()

    @pl.when(t == 0)
    def _():
        start_gather(tok_ref, 0)

    @pl.when(t + 1 < nused)
    def _():
        start_gather(tokn_ref, nxt)

    e = te_ref[t]
    e_prev = te_ref[jnp.maximum(t - 1, 0)]

    @pl.when(jnp.logical_or(t == 0, e != e_prev))
    def _():
        wgu_sc[...] = wgu_ref[0].astype(BF16)
        wd_sc[...] = wd_ref[0].astype(BF16)

    @pl.when(t < nused)
    def _():
        wait_gather(cur)

        @pl.when(t >= 2)
        def _():
            wait_scatter(cur, nv_ref[jnp.maximum(t - 2, 0)])

        half = D_MODEL // 2
        lo, hi = _unpack_bf16_pair(xs_buf[cur])
        gu = _dot(lo, wgu_sc[0:half, :]) + _dot(hi, wgu_sc[half:, :]) + bgu_ref[0]
        g = jnp.minimum(gu[:, :D_EXPERT], SWIGLU_LIMIT)
        u = jnp.clip(gu[:, D_EXPERT:], -SWIGLU_LIMIT, SWIGLU_LIMIT)
        act = (u + 1.0) * (g * _sigmoid(SWIGLU_ALPHA * g))
        y_buf[cur] = _dot(act.astype(BF16), wd_sc[...]) + bd_ref[0]

        def scatter(r, carry):
            pltpu.make_async_copy(y_buf.at[cur, pl.ds(r, 1)], ys_hbm.at[dst_ref[0, 0, r]], ssem.at[cur]).start()
            return carry

        def scatter_group(c, carry):
            for j in range(DMA_UNROLL):
                scatter(c * DMA_UNROLL + j, carry)
            return carry
        n_groups = nv_ref[t] // DMA_UNROLL
        lax.fori_loop(0, n_groups, scatter_group, 0)
        lax.fori_loop(n_groups * DMA_UNROLL, nv_ref[t], scatter, 0)

        @pl.when(t == nused - 1)
        def _():
            @pl.when(t >= 1)
            def _():
                wait_scatter(nxt, nv_ref[jnp.maximum(t - 1, 0)])
            wait_scatter(cur, nv_ref[t])


def _experts(tile_e, tile_nv, nused, row_token, row_dst, x1p, wgu, bgu, wd, bd, n_tiles, n_dst):
    S = x1p.shape[0]
    idx = lambda a: a.reshape(n_tiles, 1, TME)
    smem_tile = lambda f: pl.BlockSpec((1, 1, TME), f, memory_space=pltpu.SMEM)
    grid_spec = pltpu.PrefetchScalarGridSpec(
        num_scalar_prefetch=3,
        grid=(n_tiles,),
        in_specs=[smem_tile(lambda t, te, nv, nu: (t, 0, 0)),
                  smem_tile(lambda t, te, nv, nu: (jnp.minimum(t + 1, n_tiles - 1), 0, 0)),
                  smem_tile(lambda t, te, nv, nu: (t, 0, 0)),
                  pl.BlockSpec(memory_space=pl.ANY),
                  pl.BlockSpec((1, D_MODEL, 2 * D_EXPERT), lambda t, te, nv, nu: (te[t], 0, 0)),
                  pl.BlockSpec((1, 1, 2 * D_EXPERT), lambda t, te, nv, nu: (te[t], 0, 0)),
                  pl.BlockSpec((1, D_EXPERT, D_MODEL), lambda t, te, nv, nu: (te[t], 0, 0)),
                  pl.BlockSpec((1, 1, D_MODEL), lambda t, te, nv, nu: (te[t], 0, 0))],
        out_specs=pl.BlockSpec(memory_space=pl.ANY),
        scratch_shapes=[pltpu.VMEM((2, TME, D_MODEL // 2), jnp.uint32), pltpu.VMEM((2, TME, D_MODEL), F32),
                        pltpu.VMEM((D_MODEL, 2 * D_EXPERT), BF16), pltpu.VMEM((D_EXPERT, D_MODEL), BF16),
                        pltpu.SemaphoreType.DMA((2,)), pltpu.SemaphoreType.DMA((2,))],
    )
    return pl.pallas_call(
        _expert_kernel,
        grid_spec=grid_spec,
        out_shape=jax.ShapeDtypeStruct((n_dst, 1, D_MODEL), F32),
        compiler_params=_params(("arbitrary",)),
        name="moe_experts",
    )(tile_e, tile_nv, nused, idx(row_token), idx(row_token), idx(row_dst), x1p.reshape(S, 1, D_MODEL // 2),
      wgu, bgu, wd, bd)


def _routing_tables(ridx, S):
    n_flat = S * TOP_K_EXPERTS
    n_tiles = n_flat // TME + N_EXPERTS
    flat_e = ridx.reshape(n_flat)
    order = jnp.argsort(flat_e, stable=True).astype(jnp.int32)
    counts = jnp.sum(flat_e[:, None] == jnp.arange(N_EXPERTS, dtype=jnp.int32)[None, :], axis=0, dtype=jnp.int32)
    padded = ((counts + TME - 1) // TME) * TME
    pad_end = jnp.cumsum(padded)
    pad_start = pad_end - padded
    grp_start = jnp.cumsum(counts) - counts
    nused = (pad_end[-1] // TME).astype(jnp.int32).reshape(1)
    tile_row0 = jnp.arange(n_tiles, dtype=jnp.int32) * TME
    tile_e = jnp.minimum(jnp.sum(tile_row0[:, None] >= pad_end[None, :], axis=1), N_EXPERTS - 1).astype(jnp.int32)
    rows = jnp.arange(n_tiles * TME, dtype=jnp.int32)
    row_e = jnp.repeat(tile_e, TME)
    rank = rows - pad_start[row_e]
    valid = jnp.logical_and(rank < counts[row_e], rows < pad_end[-1])
    src_flat = order[jnp.clip(grp_start[row_e] + rank, 0, n_flat - 1)]
    row_token = jnp.where(valid, src_flat // TOP_K_EXPERTS, 0).astype(jnp.int32)
    row_dst = jnp.where(valid, (src_flat % TOP_K_EXPERTS) * S + src_flat // TOP_K_EXPERTS, n_flat).astype(jnp.int32)
    tile_nv = jnp.sum(valid.reshape(n_tiles, TME), axis=1, dtype=jnp.int32)
    return tile_e, tile_nv, nused, row_token, row_dst, n_tiles


def _final_kernel(x1_ref, y0_ref, y1_ref, y2_ref, y3_ref, rg_ref, p_ref, wpg_ref, wpp_ref, g_ref, b_ref, o_ref):
    h = DEEPNORM_ALPHA * x1_ref[...]
    rg = rg_ref[...]
    for j, y_ref in enumerate((y0_ref, y1_ref, y2_ref, y3_ref)):
        h = h + rg[:, j:j + 1] * y_ref[:, 0, :]
    ple = _sigmoid(_dot(h.astype(BF16), wpg_ref[...])) * _dot(p_ref[...].astype(BF16), wpp_ref[...])
    o_ref[...] = _layer_norm(h + ple, g_ref[...], b_ref[...])


def _final(x1, ys, rgate, p, wpg, wpp, g, b):
    S = x1.shape[0]
    full = lambda a: pl.BlockSpec(a.shape, lambda i: (0, 0))
    row = lambda n: pl.BlockSpec((TM, n), lambda i: (i, 0))
    return pl.pallas_call(
        _final_kernel,
        grid=(S // TM,),
        in_specs=[row(D_MODEL)]
        + [pl.BlockSpec((TM, 1, D_MODEL), lambda i, j=j: (j * (S // TM) + i, 0, 0)) for j in range(TOP_K_EXPERTS)]
        + [row(LANES), row(PLE_DIM), full(wpg), full(wpp), full(g), full(b)],
        out_specs=row(D_MODEL),
        out_shape=jax.ShapeDtypeStruct((S, D_MODEL), F32),
        compiler_params=_params(("parallel",)),
        name="final",
    )(x1, ys, ys, ys, ys, rgate, p, wpg, wpp, g, b)


def _layer(x, p, w_in, b_in, w_o_attn, w_dw, b_dw, conv_ln_g, conv_ln_b, w_o_conv, w_out, ln1_g, ln1_b,
           w_router, b_router, w_gate_up, b_gate_up, w_down, b_down, w_ple_gate, w_ple_proj, ln2_g, ln2_b,
           rel_bias):
    S = x.shape[0]
    assert S % TM == 0 and S % TA == 0 and S % TKI == 0 and (S * TOP_K_EXPERTS) % TME == 0
    top_k = min(TOPK_MAX, S // 4)
    row2 = lambda a: a.reshape(1, -1).astype(F32)

    o_q, o_qi, o_ki, o_cv, o_g = 0, 3 * ATTN_WIDTH, 3 * ATTN_WIDTH + IDX_HEADS * IDX_DIM, \
        3 * ATTN_WIDTH + IDX_HEADS * IDX_DIM + IDX_DIM + IDX_HEADS, \
        3 * ATTN_WIDTH + IDX_HEADS * IDX_DIM + IDX_DIM + IDX_HEADS + 2 * CONV_CH
    kw_pad = LANES - (IDX_DIM + IDX_HEADS)
    wkw = jnp.pad(w_in[:, o_ki:o_cv], ((0, 0), (0, kw_pad)))
    bkw = jnp.pad(b_in[o_ki:o_cv], (0, kw_pad))
    q, k, v, qi, kw, u, sg = _proj(
        x, w_in[:, o_q:o_qi].astype(BF16), row2(b_in[o_q:o_qi]),
        w_in[:, o_qi:o_ki].astype(BF16), row2(b_in[o_qi:o_ki]),
        wkw.astype(BF16), row2(bkw),
        w_in[:, o_cv:o_g].astype(BF16), row2(b_in[o_cv:o_g]),
        w_in[:, o_g:].astype(BF16), row2(b_in[o_g:]))

    part_conv = _conv(u, sg, w_dw, row2(b_dw), row2(conv_ln_g), row2(conv_ln_b), w_o_conv.astype(BF16))

    ki = kw[:, :IDX_DIM].astype(BF16)
    mask4 = _index_mask(qi, kw[:, IDX_DIM:IDX_DIM + IDX_HEADS].T, jnp.concatenate([ki, ki], axis=1), top_k)
    bias_d, bias_e = _relative_bias_blocks(rel_bias)
    y_attn = _attention(q, k, v, mask4, bias_d, bias_e)

    wr = jnp.pad(w_router, ((0, 0), (0, LANES - N_EXPERTS)))
    br = jnp.pad(b_router, (0, LANES - N_EXPERTS), constant_values=-jnp.inf)
    x1, x1p, ridx, rgate = _post(y_attn, sg, part_conv, x, w_o_attn.astype(BF16), w_out.astype(BF16),
                                 row2(ln1_g), row2(ln1_b), wr, row2(br))

    tile_e, tile_nv, nused, row_token, row_dst, n_tiles = _routing_tables(ridx[:, :TOP_K_EXPERTS], S)
    ys = _experts(tile_e, tile_nv, nused, row_token, row_dst, x1p, w_gate_up, b_gate_up.reshape(N_EXPERTS, 1, -1),
                  w_down, b_down.reshape(N_EXPERTS, 1, -1), n_tiles, S * TOP_K_EXPERTS)
    return _final(x1, ys, rgate, p, w_ple_gate.astype(BF16), w_ple_proj.astype(BF16), row2(ln2_g), row2(ln2_b))


def kernel(x, p, w_in, b_in, w_o_attn, w_dw, b_dw, conv_ln_g, conv_ln_b, w_o_conv, w_out, ln1_g, ln1_b, w_router, b_router, w_gate_up, b_gate_up, w_down, b_down, w_ple_gate, w_ple_proj, ln2_g, ln2_b, rel_bias):
    assert x.shape[0] == 1 and p.shape[0] == DEPTH
    out = _layer(x[0], p[0, 0], w_in[0], b_in[0], w_o_attn[0], w_dw[0], b_dw[0], conv_ln_g[0], conv_ln_b[0],
                 w_o_conv[0], w_out[0], ln1_g[0], ln1_b[0], w_router[0], b_router[0], w_gate_up[0], b_gate_up[0],
                 w_down[0], b_down[0], w_ple_gate[0], w_ple_proj[0], ln2_g[0], ln2_b[0], rel_bias)
    return out[None]
```

```python
import functools
import math

import jax
import jax.numpy as jnp
import numpy as np
from jax import lax
from jax.experimental import pallas as pl
from jax.experimental.pallas import tpu as pltpu

F32 = jnp.float32
BF16 = jnp.bfloat16

D_MODEL = 1024
N_HEADS = 8
HEAD_DIM = 64
ATTN_WIDTH = N_HEADS * HEAD_DIM
ATTN_SCALE = HEAD_DIM ** -0.5
IDX_HEADS = 8
IDX_DIM = 64
IDX_SCALE = (IDX_HEADS ** -0.5) * (IDX_DIM ** -0.5)
TOPK_MAX = 256
CONV_CH = 512
CONV_WIDTH = 31
N_BUCKETS = 32
MAX_DISTANCE = 128
N_EXPERTS = 32
TOP_K_EXPERTS = 4
D_EXPERT = 1024
SWIGLU_LIMIT = 7.0
SWIGLU_ALPHA = 1.702
PLE_DIM = 256
LN_EPS = 1e-5
DEPTH = 1
DEEPNORM_ALPHA = (2 * DEPTH) ** 0.25

LANES = 128
NEG_BIG = -1e30
F32_TINY = float(np.finfo(np.float32).tiny)
VMEM_LIMIT = 56 * 1024 * 1024

TM = 512
TQI = 256
TKI = 512
TA = 512
SUB = 128
TME = 256
DMA_UNROLL = 8
CONV_HALO = 32


def _params(sem, vmem=VMEM_LIMIT):
    return pltpu.CompilerParams(dimension_semantics=sem, vmem_limit_bytes=vmem)


def _sigmoid(x):
    return 1.0 / (1.0 + jnp.exp(-x))


def _layer_norm(x, g, b):
    mu = jnp.mean(x, axis=-1, keepdims=True)
    xc = x - mu
    var = jnp.mean(xc * xc, axis=-1, keepdims=True)
    return xc * lax.rsqrt(var + LN_EPS) * g + b


def _dot(a, b):
    return jnp.dot(a, b, preferred_element_type=F32)


def _dot_nt(a, b):
    return lax.dot_general(a, b, (((1,), (1,)), ((), ())), preferred_element_type=F32)


def _proj_kernel(x_ref, wqkv_ref, bqkv_ref, wqi_ref, bqi_ref, wkw_ref, bkw_ref, wcv_ref, bcv_ref,
                 wg_ref, bg_ref, q_ref, k_ref, v_ref, qi_ref, kw_ref, u_ref, sg_ref):
    xb = x_ref[...].astype(BF16)
    qkv = _dot(xb, wqkv_ref[...]) + bqkv_ref[...]
    q_ref[...] = (qkv[:, :ATTN_WIDTH] * ATTN_SCALE).astype(BF16)
    k_ref[...] = qkv[:, ATTN_WIDTH:2 * ATTN_WIDTH].astype(BF16)
    v_ref[...] = qkv[:, 2 * ATTN_WIDTH:].astype(BF16)
    qi_ref[...] = (_dot(xb, wqi_ref[...]) + bqi_ref[...]).astype(BF16)
    kw = _dot(xb, wkw_ref[...]) + bkw_ref[...]
    lane = lax.broadcasted_iota(jnp.int32, kw.shape, 1)
    kw_ref[...] = jnp.where(lane >= IDX_DIM, kw * IDX_SCALE, kw)
    cv = _dot(xb, wcv_ref[...]) + bcv_ref[...]
    u_ref[...] = cv[:, :CONV_CH] * _sigmoid(cv[:, CONV_CH:])
    sg_ref[...] = _sigmoid(_dot(xb, wg_ref[...]) + bg_ref[...])


def _proj(x, wqkv, bqkv, wqi, bqi, wkw, bkw, wcv, bcv, wg, bg):
    S = x.shape[0]
    full = lambda a: pl.BlockSpec(a.shape, lambda i: (0, 0))
    row = lambda n: pl.BlockSpec((TM, n), lambda i: (i, 0))
    outs = [(ATTN_WIDTH, BF16)] * 3 + [(IDX_HEADS * IDX_DIM, BF16), (LANES, F32), (CONV_CH, F32),
                                        (2 * D_MODEL, F32)]
    return pl.pallas_call(
        _proj_kernel,
        grid=(S // TM,),
        in_specs=[row(D_MODEL)] + [full(a) for a in (wqkv, bqkv, wqi, bqi, wkw, bkw, wcv, bcv, wg, bg)],
        out_specs=[row(n) for n, _ in outs],
        out_shape=[jax.ShapeDtypeStruct((S, n), dt) for n, dt in outs],
        compiler_params=_params(("parallel",)),
        name="proj",
    )(x, wqkv, bqkv, wqi, bqi, wkw, bkw, wcv, bcv, wg, bg)


def _conv_kernel(u_ref, wdw_ref, bdw_ref, lng_ref, lnb_ref, wo_ref, sgc_ref, o_ref, buf_ref):
    @pl.when(pl.program_id(0) == 0)
    def _():
        buf_ref[0:CONV_HALO, :] = jnp.zeros((CONV_HALO, CONV_CH), F32)

    buf_ref[CONV_HALO:CONV_HALO + TM, :] = u_ref[...]
    base = CONV_HALO - (CONV_WIDTH - 1)
    acc = jnp.zeros((TM, CONV_CH), F32) + bdw_ref[...]
    for j in range(CONV_WIDTH):
        acc = acc + wdw_ref[j:j + 1, :] * buf_ref[base + j:base + j + TM, :]
    buf_ref[0:CONV_HALO, :] = buf_ref[TM:TM + CONV_HALO, :]
    y = _layer_norm(acc, lng_ref[...], lnb_ref[...])
    y = y * _sigmoid(y)
    o_ref[...] = sgc_ref[...] * _dot(y.astype(BF16), wo_ref[...])


def _conv(u, sg, wdw, bdw, lng, lnb, wo):
    S = u.shape[0]
    full = lambda a: pl.BlockSpec(a.shape, lambda i: (0, 0))
    return pl.pallas_call(
        _conv_kernel,
        grid=(S // TM,),
        in_specs=[pl.BlockSpec((TM, CONV_CH), lambda i: (i, 0)), full(wdw), full(bdw), full(lng), full(lnb),
                  full(wo), pl.BlockSpec((TM, D_MODEL), lambda i: (i, 1))],
        out_specs=pl.BlockSpec((TM, D_MODEL), lambda i: (i, 0)),
        out_shape=jax.ShapeDtypeStruct((S, D_MODEL), F32),
        scratch_shapes=[pltpu.VMEM((TM + CONV_HALO, CONV_CH), F32)],
        compiler_params=_params(("arbitrary",)),
        name="conv",
    )(u, wdw, bdw, lng, lnb, wo, sg)


def _float_key(f):
    b = lax.bitcast_convert_type(f, jnp.int32)
    return b ^ ((b >> 31) & jnp.int32(0x7FFFFFFF))


def _key_float(k):
    b = k ^ ((k >> 31) & jnp.int32(0x7FFFFFFF))
    return lax.bitcast_convert_type(b, F32)


def _index_kernel(qi_ref, wt_ref, ki_ref, mask_ref, sc_ref, qh_ref, *, top_k):
    i = pl.program_id(0)
    nkt = (i * TQI + TQI + TKI - 1) // TKI
    ntile = sc_ref.shape[0]
    q_g = i * TQI + lax.broadcasted_iota(jnp.int32, (1, TQI), 1)
    lane = lax.broadcasted_iota(jnp.int32, (TQI, LANES), 1)

    for h in range(IDX_HEADS):
        qp = qi_ref[:, (h // 2) * LANES:(h // 2 + 1) * LANES]
        keep = (lane < IDX_DIM) if h % 2 == 0 else (lane >= IDX_DIM)
        qh_ref[h] = jnp.where(keep, qp, jnp.zeros_like(qp))

    def score_tile(kt, causal):
        kb = ki_ref[pl.ds(pl.multiple_of(kt * TKI, TKI), TKI), :]
        acc = jnp.zeros((TKI, TQI), F32)
        for h in range(IDX_HEADS):
            acc = acc + wt_ref[h:h + 1, :] * jnp.maximum(_dot_nt(kb, qh_ref[h]), 0.0)
        if causal:
            key_g = kt * TKI + lax.broadcasted_iota(jnp.int32, (TKI, TQI), 0)
            ok = key_g <= q_g
            lo_src = jnp.where(ok, acc, jnp.inf)
            acc = jnp.where(ok, acc, -jnp.inf)
        else:
            lo_src = acc
        sc_ref[kt] = acc
        return jnp.max(acc, axis=0, keepdims=True), jnp.min(lo_src, axis=0, keepdims=True)

    def score_body(kt, carry):
        mx, mn = carry
        tmx, tmn = score_tile(kt, False)
        return jnp.maximum(mx, tmx), jnp.minimum(mn, tmn)

    mx0 = jnp.full((1, TQI), -jnp.inf, F32)
    mn0 = jnp.full((1, TQI), jnp.inf, F32)
    mx, mn = lax.fori_loop(0, nkt - 1, score_body, (mx0, mn0))
    tmx, tmn = score_tile(nkt - 1, True)
    mx = jnp.maximum(mx, tmx)
    mn = jnp.minimum(mn, tmn)
    kf = float(top_k)
    sub = 8

    def count(pivot, strict):
        def body(kt, cnt):
            for r in range(TKI // sub):
                blk = sc_ref[kt, r * sub:(r + 1) * sub, :]
                hit = (blk > pivot) if strict else (blk >= pivot)
                cnt = cnt + jnp.where(hit, 1.0, 0.0)
            return cnt

        cnt = lax.fori_loop(0, nkt, body, jnp.zeros((sub, TQI), F32))
        return jnp.sum(cnt, axis=0, keepdims=True)

    n_causal = (q_g + 1).astype(F32)
    all_sel = n_causal <= kf
    state0 = dict(
        it=jnp.int32(0),
        lo=mn, hi=_key_float(_float_key(mx) + 1),
        glo=jnp.log(jnp.maximum(n_causal, kf + 1.0) / kf), ghi=jnp.full((1, TQI), math.log(0.5 / kf), F32),
        thr=jnp.where(all_sel, NEG_BIG, 0.0).astype(F32),
        done=all_sel.astype(F32), tie=jnp.zeros((1, TQI), F32), side=jnp.zeros((1, TQI), F32),
    )

    def cond(st):
        return jnp.logical_and(st["it"] < 100, jnp.min(st["done"]) < 0.5)

    def step(st):
        it, lo, hi, glo, ghi = st["it"], st["lo"], st["hi"], st["glo"], st["ghi"]
        lo_k, hi_k = _float_key(lo), _float_key(hi)
        above_lo = jnp.where(jnp.abs(lo) < F32_TINY, F32_TINY, _key_float(lo_k + 1))
        below_hi = jnp.where(hi == F32_TINY, 0.0, _key_float(hi_k - 1))
        probe = jnp.logical_and(st["done"] < 0.5, above_lo < hi)
        tie_now = jnp.logical_and(st["done"] < 0.5, above_lo >= hi)
        frac = jnp.where(it < 24, glo / (glo - ghi), 0.5)
        pf = lo + (hi - lo) * frac
        pf = jnp.where(it < 64, pf, _key_float((lo_k >> 1) + (hi_k >> 1) + (lo_k & hi_k & 1)))
        pf = jnp.where(it == 0, 0.0, jnp.where(it == 1, F32_TINY, pf))
        pf = jnp.where(probe, jnp.minimum(jnp.maximum(pf, above_lo), below_hi), lo)
        c = count(pf, False)
        hit = jnp.logical_and(probe, c == kf)
        up = jnp.logical_and(probe, c > kf)
        dn = jnp.logical_and(probe, c < kf)
        g = jnp.log(jnp.maximum(c, 0.5) / kf)
        return dict(
            it=it + 1,
            lo=jnp.where(up, pf, lo), hi=jnp.where(dn, pf, hi),
            glo=jnp.where(up, g, jnp.where(jnp.logical_and(dn, st["side"] < -0.5), 0.5 * glo, glo)),
            ghi=jnp.where(dn, g, jnp.where(jnp.logical_and(up, st["side"] > 0.5), 0.5 * ghi, ghi)),
            thr=jnp.where(hit, pf, jnp.where(tie_now, lo, st["thr"])),
            done=jnp.where(jnp.logical_or(hit, tie_now), 1.0, st["done"]),
            tie=jnp.where(tie_now, 1.0, st["tie"]),
            side=jnp.where(up, 1.0, jnp.where(dn, -1.0, st["side"])),
        )

    st = lax.while_loop(cond, step, state0)
    thr = st["thr"]
    tie = st["tie"]
    any_tie = jnp.max(tie) > 0.5

    def emit(kt, sel):
        mask_ref[0, kt] = jnp.where(sel, 0.0, NEG_BIG).T.astype(mask_ref.dtype)

    @pl.when(jnp.logical_not(any_tie))
    def _():
        def body(kt, carry):
            emit(kt, sc_ref[kt] >= thr)
            return carry
        lax.fori_loop(0, nkt, body, 0)

    @pl.when(any_tie)
    def _():
        need = jnp.where(tie > 0.5, kf - count(thr, True), float(2 * ntile * TKI))
        r = lax.broadcasted_iota(jnp.int32, (TKI, TKI), 0)
        c = lax.broadcasted_iota(jnp.int32, (TKI, TKI), 1)
        prefix = jnp.where(c <= r, 1.0, 0.0).astype(BF16)

        def body(kt, seen):
            s = sc_ref[kt]
            eq = s == thr
            rank = seen + _dot(prefix, jnp.where(eq, 1.0, 0.0).astype(BF16))
            emit(kt, jnp.logical_or(s > thr, jnp.logical_and(eq, rank <= need)))
            return seen + jnp.sum(jnp.where(eq, 1.0, 0.0), axis=0, keepdims=True)
        lax.fori_loop(0, nkt, body, jnp.zeros((1, TQI), F32))

    def fill(kt, carry):
        mask_ref[0, kt] = jnp.full((TQI, TKI), NEG_BIG, mask_ref.dtype)
        return carry
    lax.fori_loop(nkt, ntile, fill, 0)


def _index_mask(qi, wt, ki2, top_k):
    S = qi.shape[0]
    nq, nk = S // TQI, S // TKI
    return pl.pallas_call(
        functools.partial(_index_kernel, top_k=top_k),
        grid=(nq,),
        in_specs=[pl.BlockSpec((TQI, IDX_HEADS * IDX_DIM), lambda i: (i, 0)),
                  pl.BlockSpec((IDX_HEADS, TQI), lambda i: (0, i)),
                  pl.BlockSpec((S, LANES), lambda i: (0, 0))],
        out_specs=pl.BlockSpec((1, nk, TQI, TKI), lambda i: (i, 0, 0, 0)),
        out_shape=jax.ShapeDtypeStruct((nq, nk, TQI, TKI), BF16),
        scratch_shapes=[pltpu.VMEM((nk, TKI, TQI), F32),
                        pltpu.VMEM((IDX_HEADS, TQI, LANES), BF16)],
        compiler_params=_params(("parallel",)),
        name="index_mask",
    )(qi, wt, ki2)


def _attn_kernel(qt_ref, kt_ref, q_ref, k_ref, v_ref, m_ref, bd_ref, be_ref, o_ref, m_sc, acc_sc):
    step = pl.program_id(0)
    qi = qt_ref[step]
    ki = kt_ref[step]
    nsub = TA // SUB

    @pl.when(ki == 0)
    def _():
        m_sc[...] = jnp.full(m_sc.shape, NEG_BIG, F32)
        acc_sc[...] = jnp.zeros(acc_sc.shape, F32)

    lane = lax.broadcasted_iota(jnp.int32, (TA, LANES), 1)
    first = lane < HEAD_DIM

    def bias_tile(h, diagonal):
        zero = jnp.zeros((SUB, SUB), F32)
        rows = []
        for a in range(nsub):
            if diagonal:
                blks = [bd_ref[h] if b == a else (be_ref[h] if b == a - 1 else zero) for b in range(nsub)]
            else:
                blks = [be_ref[h] if (a == 0 and b == nsub - 1) else zero for b in range(nsub)]
            rows.append(jnp.concatenate(blks, axis=1))
        return jnp.concatenate(rows, axis=0)

    def update(near, diagonal):
        maskf = jnp.concatenate([m_ref[a, 0] for a in range(TA // TQI)], axis=0).astype(F32)
        for p in range(N_HEADS // 2):
            cols = slice(p * LANES, (p + 1) * LANES)
            qp = q_ref[:, cols]
            kp = k_ref[:, cols]
            vp = v_ref[:, cols]
            for hh in range(2):
                h = 2 * p + hh
                mine = first if hh == 0 else jnp.logical_not(first)
                qh = jnp.where(mine, qp, jnp.zeros_like(qp))
                vh = jnp.where(mine, vp, jnp.ones_like(vp))
                s = _dot_nt(qh, kp) + maskf
                if near:
                    s = s + bias_tile(h, diagonal)
                m_prev = m_sc[h]
                m_next = jnp.maximum(m_prev, jnp.max(s, axis=1, keepdims=True))
                pexp = jnp.exp(s - jnp.concatenate([m_next] * (TA // LANES), axis=1)).astype(BF16)
                acc_sc[h] = jnp.exp(m_prev - m_next) * acc_sc[h] + _dot(pexp, vh)
                m_sc[h] = m_next

    @pl.when(ki < qi - 1)
    def _():
        update(False, False)

    @pl.when(ki == qi - 1)
    def _():
        update(True, False)

    @pl.when(ki == qi)
    def _():
        update(True, True)
        for p in range(N_HEADS // 2):
            a0, a1 = acc_sc[2 * p], acc_sc[2 * p + 1]
            d0 = pltpu.roll(a0, HEAD_DIM, axis=1)
            d1 = pltpu.roll(a1, HEAD_DIM, axis=1)
            o_ref[:, p * LANES:(p + 1) * LANES] = jnp.where(first, a0 / d0, a1 / d1).astype(o_ref.dtype)


def _attention(q, k, v, mask4, bias_d, bias_e):
    S = q.shape[0]
    nb = S // TA
    pairs = [(a, b) for a in range(nb) for b in range(a + 1)]
    qtab = jnp.asarray(np.array([a for a, _ in pairs], np.int32))
    ktab = jnp.asarray(np.array([b for _, b in pairs], np.int32))
    grid_spec = pltpu.PrefetchScalarGridSpec(
        num_scalar_prefetch=2,
        grid=(len(pairs),),
        in_specs=[pl.BlockSpec((TA, ATTN_WIDTH), lambda s, qt, kt: (qt[s], 0)),
                  pl.BlockSpec((TA, ATTN_WIDTH), lambda s, qt, kt: (kt[s], 0)),
                  pl.BlockSpec((TA, ATTN_WIDTH), lambda s, qt, kt: (kt[s], 0)),
                  pl.BlockSpec((TA // TQI, TA // TKI, TQI, TKI), lambda s, qt, kt: (qt[s], kt[s], 0, 0)),
                  pl.BlockSpec(bias_d.shape, lambda s, qt, kt: (0, 0, 0)),
                  pl.BlockSpec(bias_e.shape, lambda s, qt, kt: (0, 0, 0))],
        out_specs=pl.BlockSpec((TA, ATTN_WIDTH), lambda s, qt, kt: (qt[s], 0)),
        scratch_shapes=[pltpu.VMEM((N_HEADS, TA, LANES), F32),
                        pltpu.VMEM((N_HEADS, TA, LANES), F32)],
    )
    return pl.pallas_call(
        _attn_kernel,
        grid_spec=grid_spec,
        out_shape=jax.ShapeDtypeStruct((S, ATTN_WIDTH), BF16),
        compiler_params=_params(("arbitrary",)),
        name="attn",
    )(qtab, ktab, q, k, v, mask4, bias_d, bias_e)


def _relative_bias_blocks(rel_bias):
    dist = jnp.arange(2 * SUB, dtype=jnp.int32)
    max_exact = N_BUCKETS // 2
    dist_f = jnp.maximum(dist, 1).astype(F32)
    large = max_exact + (jnp.log(dist_f / max_exact) / math.log(MAX_DISTANCE / max_exact)
                         * (N_BUCKETS - max_exact)).astype(jnp.int32)
    bucket = jnp.where(dist < max_exact, dist, jnp.minimum(large, N_BUCKETS - 1))
    table = (rel_bias[bucket] - rel_bias[N_BUCKETS - 1]).astype(F32)
    i = jnp.arange(SUB)[:, None]
    j = jnp.arange(SUB)[None, :]
    diag = jnp.transpose(table[jnp.clip(i - j, 0, 2 * SUB - 1)], (2, 0, 1))
    sub = jnp.transpose(table[SUB + i - j], (2, 0, 1))
    return diag, sub


def _pack_bf16_pair(a, b):
    def rnd(x):
        bits = lax.bitcast_convert_type(x, jnp.uint32)
        return bits + jnp.uint32(0x7FFF) + ((bits >> 16) & jnp.uint32(1))
    return (rnd(a) >> 16) | (rnd(b) & jnp.uint32(0xFFFF0000))


def _unpack_bf16_pair(p):
    lo = lax.bitcast_convert_type(p << 16, F32)
    hi = lax.bitcast_convert_type(p & jnp.uint32(0xFFFF0000), F32)
    return lo.astype(BF16), hi.astype(BF16)


def _post_kernel(ya_ref, sga_ref, pc_ref, x_ref, woa_ref, wout_ref, g_ref, b_ref, wr_ref, br_ref,
                 x1_ref, x1p_ref, ridx_ref, rgate_ref):
    y_attn = _dot(ya_ref[...], woa_ref[...])
    merged = sga_ref[...] * y_attn + pc_ref[...]
    mix = _dot(merged.astype(BF16), wout_ref[...])
    x1 = _layer_norm(DEEPNORM_ALPHA * x_ref[...] + mix, g_ref[...], b_ref[...])
    x1_ref[...] = x1
    half = D_MODEL // 2
    x1p_ref[...] = _pack_bf16_pair(x1[:, :half], x1[:, half:])

    logits = jnp.dot(x1, wr_ref[...], preferred_element_type=F32, precision=lax.Precision.HIGHEST) + br_ref[...]
    lane = lax.broadcasted_iota(jnp.int32, logits.shape, 1).astype(F32)
    cur = logits
    vals, idxs = [], []
    for _ in range(TOP_K_EXPERTS):
        m = jnp.max(cur, axis=1, keepdims=True)
        ix = jnp.min(jnp.where(cur == m, lane, float(LANES)), axis=1, keepdims=True)
        vals.append(m)
        idxs.append(ix)
        cur = jnp.where(lane == ix, -jnp.inf, cur)
    exps = [jnp.exp(v - vals[0]) for v in vals]
    denom = exps[0]
    for e in exps[1:]:
        denom = denom + e
    ridx = jnp.zeros_like(logits)
    rgate = jnp.zeros_like(logits)
    for j in range(TOP_K_EXPERTS):
        ridx = jnp.where(lane == float(j), idxs[j], ridx)
        rgate = jnp.where(lane == float(j), exps[j] / denom, rgate)
    ridx_ref[...] = ridx.astype(jnp.int32)
    rgate_ref[...] = rgate


def _post(ya, sg, pc, x, woa, wout, g, b, wr, br):
    S = x.shape[0]
    full = lambda a: pl.BlockSpec(a.shape, lambda i: (0, 0))
    row = lambda n: pl.BlockSpec((TM, n), lambda i: (i, 0))
    return pl.pallas_call(
        _post_kernel,
        grid=(S // TM,),
        in_specs=[row(ATTN_WIDTH), row(D_MODEL), row(D_MODEL), row(D_MODEL)] + [full(a) for a in (woa, wout, g, b, wr, br)],
        out_specs=[row(D_MODEL), row(D_MODEL // 2), row(LANES), row(LANES)],
        out_shape=[jax.ShapeDtypeStruct((S, D_MODEL), F32), jax.ShapeDtypeStruct((S, D_MODEL // 2), jnp.uint32),
                   jax.ShapeDtypeStruct((S, LANES), jnp.int32), jax.ShapeDtypeStruct((S, LANES), F32)],
        compiler_params=_params(("parallel",)),
        name="post",
    )(ya, sg, pc, x, woa, wout, g, b, wr, br)


def _expert_kernel(te_ref, nv_ref, nused_ref, tok_ref, tokn_ref, dst_ref, x_hbm, wgu_ref, bgu_ref, wd_ref, bd_ref,
                   ys_hbm, xs_buf, y_buf, wgu_sc, wd_sc, gsem, ssem):
    t = pl.program_id(0)
    nused = nused_ref[0]
    cur = t % 2
    nxt = 1 - cur

    def start_gather(idx_ref, slot):
        def body(r, carry):
            pltpu.make_async_copy(x_hbm.at[idx_ref[0, 0, r]], xs_buf.at[slot, pl.ds(r, 1)], gsem.at[slot]).start()
            return carry
        lax.fori_loop(0, TME, body, 0, unroll=DMA_UNROLL)

    def wait_gather(slot):
        pltpu.make_async_copy(xs_buf.at[slot], xs_buf.at[slot], gsem.at[slot]).wait()

    def wait_scatter(slot, n):
        pltpu.make_async_copy(ys_hbm.at[pl.ds(0, n)], ys_hbm.at[pl.ds(0, n)], ssem.at[slot]).wait()

    @pl.when(t == 0)
    def _():
        start_gather(tok_ref, 0)

    @pl.when(t + 1 < nused)
    def _():
        start_gather(tokn_ref, nxt)

    e = te_ref[t]
    e_prev = te_ref[jnp.maximum(t - 1, 0)]

    @pl.when(jnp.logical_or(t == 0, e != e_prev))
    def _():
        wgu_sc[...] = wgu_ref[0].astype(BF16)
        wd_sc[...] = wd_ref[0].astype(BF16)

    @pl.when(t < nused)
    def _():
        wait_gather(cur)

        @pl.when(t >= 2)
        def _():
            wait_scatter(cur, nv_ref[jnp.maximum(t - 2, 0)])

        half = D_MODEL // 2
        lo, hi = _unpack_bf16_pair(xs_buf[cur])
        gu = _dot(lo, wgu_sc[0:half, :]) + _dot(hi, wgu_sc[half:, :]) + bgu_ref[0]
        g = jnp.minimum(gu[:, :D_EXPERT], SWIGLU_LIMIT)
        u = jnp.clip(gu[:, D_EXPERT:], -SWIGLU_LIMIT, SWIGLU_LIMIT)
        act = (u + 1.0) * (g * _sigmoid(SWIGLU_ALPHA * g))
        y_buf[cur] = _dot(act.astype(BF16), wd_sc[...]) + bd_ref[0]

        def scatter(r, carry):
            pltpu.make_async_copy(y_buf.at[cur, pl.ds(r, 1)], ys_hbm.at[dst_ref[0, 0, r]], ssem.at[cur]).start()
            return carry

        def scatter_group(c, carry):
            for j in range(DMA_UNROLL):
                scatter(c * DMA_UNROLL + j, carry)
            return carry
        n_groups = nv_ref[t] // DMA_UNROLL
        lax.fori_loop(0, n_groups, scatter_group, 0)
        lax.fori_loop(n_groups * DMA_UNROLL, nv_ref[t], scatter, 0)

        @pl.when(t == nused - 1)
        def _():
            @pl.when(t >= 1)
            def _():
                wait_scatter(nxt, nv_ref[jnp.maximum(t - 1, 0)])
            wait_scatter(cur, nv_ref[t])


def _experts(tile_e, tile_nv, nused, row_token, row_dst, x1p, wgu, bgu, wd, bd, n_tiles, n_dst):
    S = x1p.shape[0]
    idx = lambda a: a.reshape(n_tiles, 1, TME)
    smem_tile = lambda f: pl.BlockSpec((1, 1, TME), f, memory_space=pltpu.SMEM)
    grid_spec = pltpu.PrefetchScalarGridSpec(
        num_scalar_prefetch=3,
        grid=(n_tiles,),
        in_specs=[smem_tile(lambda t, te, nv, nu: (t, 0, 0)),
                  smem_tile(lambda t, te, nv, nu: (jnp.minimum(t + 1, n_tiles - 1), 0, 0)),
                  smem_tile(lambda t, te, nv, nu: (t, 0, 0)),
                  pl.BlockSpec(memory_space=pl.ANY),
                  pl.BlockSpec((1, D_MODEL, 2 * D_EXPERT), lambda t, te, nv, nu: (te[t], 0, 0)),
                  pl.BlockSpec((1, 1, 2 * D_EXPERT), lambda t, te, nv, nu: (te[t], 0, 0)),
                  pl.BlockSpec((1, D_EXPERT, D_MODEL), lambda t, te, nv, nu: (te[t], 0, 0)),
                  pl.BlockSpec((1, 1, D_MODEL), lambda t, te, nv, nu: (te[t], 0, 0))],
        out_specs=pl.BlockSpec(memory_space=pl.ANY),
        scratch_shapes=[pltpu.VMEM((2, TME, D_MODEL // 2), jnp.uint32), pltpu.VMEM((2, TME, D_MODEL), F32),
                        pltpu.VMEM((D_MODEL, 2 * D_EXPERT), BF16), pltpu.VMEM((D_EXPERT, D_MODEL), BF16),
                        pltpu.SemaphoreType.DMA((2,)), pltpu.SemaphoreType.DMA((2,))],
    )
    return pl.pallas_call(
        _expert_kernel,
        grid_spec=grid_spec,
        out_shape=jax.ShapeDtypeStruct((n_dst, 1, D_MODEL), F32),
        compiler_params=_params(("arbitrary",)),
        name="moe_experts",
    )(tile_e, tile_nv, nused, idx(row_token), idx(row_token), idx(row_dst), x1p.reshape(S, 1, D_MODEL // 2),
      wgu, bgu, wd, bd)


def _routing_tables(ridx, S):
    n_flat = S * TOP_K_EXPERTS
    n_tiles = n_flat // TME + N_EXPERTS
    flat_e = ridx.reshape(n_flat)
    order = jnp.argsort(flat_e, stable=True).astype(jnp.int32)
    counts = jnp.sum(flat_e[:, None] == jnp.arange(N_EXPERTS, dtype=jnp.int32)[None, :], axis=0, dtype=jnp.int32)
    padded = ((counts + TME - 1) // TME) * TME
    pad_end = jnp.cumsum(padded)
    pad_start = pad_end - padded
    grp_start = jnp.cumsum(counts) - counts
    nused = (pad_end[-1] // TME).astype(jnp.int32).reshape(1)
    tile_row0 = jnp.arange(n_tiles, dtype=jnp.int32) * TME
    tile_e = jnp.minimum(jnp.sum(tile_row0[:, None] >= pad_end[None, :], axis=1), N_EXPERTS - 1).astype(jnp.int32)
    rows = jnp.arange(n_tiles * TME, dtype=jnp.int32)
    row_e = jnp.repeat(tile_e, TME)
    rank = rows - pad_start[row_e]
    valid = jnp.logical_and(rank < counts[row_e], rows < pad_end[-1])
    src_flat = order[jnp.clip(grp_start[row_e] + rank, 0, n_flat - 1)]
    row_token = jnp.where(valid, src_flat // TOP_K_EXPERTS, 0).astype(jnp.int32)
    row_dst = jnp.where(valid, (src_flat % TOP_K_EXPERTS) * S + src_flat // TOP_K_EXPERTS, n_flat).astype(jnp.int32)
    tile_nv = jnp.sum(valid.reshape(n_tiles, TME), axis=1, dtype=jnp.int32)
    return tile_e, tile_nv, nused, row_token, row_dst, n_tiles


def _final_kernel(x1_ref, y0_ref, y1_ref, y2_ref, y3_ref, rg_ref, p_ref, wpg_ref, wpp_ref, g_ref, b_ref, o_ref):
    h = DEEPNORM_ALPHA * x1_ref[...]
    rg = rg_ref[...]
    for j, y_ref in enumerate((y0_ref, y1_ref, y2_ref, y3_ref)):
        h = h + rg[:, j:j + 1] * y_ref[:, 0, :]
    ple = _sigmoid(_dot(h.astype(BF16), wpg_ref[...])) * _dot(p_ref[...].astype(BF16), wpp_ref[...])
    o_ref[...] = _layer_norm(h + ple, g_ref[...], b_ref[...])


def _final(x1, ys, rgate, p, wpg, wpp, g, b):
    S = x1.shape[0]
    full = lambda a: pl.BlockSpec(a.shape, lambda i: (0, 0))
    row = lambda n: pl.BlockSpec((TM, n), lambda i: (i, 0))
    return pl.pallas_call(
        _final_kernel,
        grid=(S // TM,),
        in_specs=[row(D_MODEL)]
        + [pl.BlockSpec((TM, 1, D_MODEL), lambda i, j=j: (j * (S // TM) + i, 0, 0)) for j in range(TOP_K_EXPERTS)]
        + [row(LANES), row(PLE_DIM), full(wpg), full(wpp), full(g), full(b)],
        out_specs=row(D_MODEL),
        out_shape=jax.ShapeDtypeStruct((S, D_MODEL), F32),
        compiler_params=_params(("parallel",)),
        name="final",
    )(x1, ys, ys, ys, ys, rgate, p, wpg, wpp, g, b)


def _layer(x, p, w_in, b_in, w_o_attn, w_dw, b_dw, conv_ln_g, conv_ln_b, w_o_conv, w_out, ln1_g, ln1_b,
           w_router, b_router, w_gate_up, b_gate_up, w_down, b_down, w_ple_gate, w_ple_proj, ln2_g, ln2_b,
           rel_bias):
    S = x.shape[0]
    assert S % TM == 0 and S % TA == 0 and S % TKI == 0 and (S * TOP_K_EXPERTS) % TME == 0
    top_k = min(TOPK_MAX, S // 4)
    row2 = lambda a: a.reshape(1, -1).astype(F32)

    o_q, o_qi, o_ki, o_cv, o_g = 0, 3 * ATTN_WIDTH, 3 * ATTN_WIDTH + IDX_HEADS * IDX_DIM, \
        3 * ATTN_WIDTH + IDX_HEADS * IDX_DIM + IDX_DIM + IDX_HEADS, \
        3 * ATTN_WIDTH + IDX_HEADS * IDX_DIM + IDX_DIM + IDX_HEADS + 2 * CONV_CH
    kw_pad = LANES - (IDX_DIM + IDX_HEADS)
    wkw = jnp.pad(w_in[:, o_ki:o_cv], ((0, 0), (0, kw_pad)))
    bkw = jnp.pad(b_in[o_ki:o_cv], (0, kw_pad))
    q, k, v, qi, kw, u, sg = _proj(
        x, w_in[:, o_q:o_qi].astype(BF16), row2(b_in[o_q:o_qi]),
        w_in[:, o_qi:o_ki].astype(BF16), row2(b_in[o_qi:o_ki]),
        wkw.astype(BF16), row2(bkw),
        w_in[:, o_cv:o_g].astype(BF16), row2(b_in[o_cv:o_g]),
        w_in[:, o_g:].astype(BF16), row2(b_in[o_g:]))

    part_conv = _conv(u, sg, w_dw, row2(b_dw), row2(conv_ln_g), row2(conv_ln_b), w_o_conv.astype(BF16))

    ki = kw[:, :IDX_DIM].astype(BF16)
    mask4 = _index_mask(qi, kw[:, IDX_DIM:IDX_DIM + IDX_HEADS].T, jnp.concatenate([ki, ki], axis=1), top_k)
    bias_d, bias_e = _relative_bias_blocks(rel_bias)
    y_attn = _attention(q, k, v, mask4, bias_d, bias_e)

    wr = jnp.pad(w_router, ((0, 0), (0, LANES - N_EXPERTS)))
    br = jnp.pad(b_router, (0, LANES - N_EXPERTS), constant_values=-jnp.inf)
    x1, x1p, ridx, rgate = _post(y_attn, sg, part_conv, x, w_o_attn.astype(BF16), w_out.astype(BF16),
                                 row2(ln1_g), row2(ln1_b), wr, row2(br))

    tile_e, tile_nv, nused, row_token, row_dst, n_tiles = _routing_tables(ridx[:, :TOP_K_EXPERTS], S)
    ys = _experts(tile_e, tile_nv, nused, row_token, row_dst, x1p, w_gate_up, b_gate_up.reshape(N_EXPERTS, 1, -1),
                  w_down, b_down.reshape(N_EXPERTS, 1, -1), n_tiles, S * TOP_K_EXPERTS)
    return _final(x1, ys, rgate, p, w_ple_gate.astype(BF16), w_ple_proj.astype(BF16), row2(ln2_g), row2(ln2_b))


def kernel(x, p, w_in, b_in, w_o_attn, w_dw, b_dw, conv_ln_g, conv_ln_b, w_o_conv, w_out, ln1_g, ln1_b, w_router, b_router, w_gate_up, b_gate_up, w_down, b_down, w_ple_gate, w_ple_proj, ln2_g, ln2_b, rel_bias):
    assert x.shape[0] == 1 and p.shape[0] == DEPTH
    out = _layer(x[0], p[0, 0], w_in[0], b_in[0], w_o_attn[0], w_dw[0], b_dw[0], conv_ln_g[0], conv_ln_b[0],
                 w_o_conv[0], w_out[0], ln1_g[0], ln1_b[0], w_router[0], b_router[0], w_gate_up[0], b_gate_up[0],
                 w_down[0], b_down[0], w_ple_gate[0], w_ple_proj[0], ln2_g[0], ln2_b[0], rel_bias)
    return out[None]
```

```python
import functools
import math

import jax
import jax.numpy as jnp
import numpy as np
from jax import lax
from jax.experimental import pallas as pl
from jax.experimental.pallas import tpu as pltpu

F32 = jnp.float32
BF16 = jnp.bfloat16

D_MODEL = 1024
N_HEADS = 8
HEAD_DIM = 64
ATTN_WIDTH = N_HEADS * HEAD_DIM
ATTN_SCALE = HEAD_DIM ** -0.5
LOG2E = math.log2(math.e)
IDX_HEADS = 8
IDX_DIM = 64
IDX_SCALE = (IDX_HEADS ** -0.5) * (IDX_DIM ** -0.5)
TOPK_MAX = 256
CONV_CH = 512
CONV_WIDTH = 31
N_BUCKETS = 32
MAX_DISTANCE = 128
N_EXPERTS = 32
TOP_K_EXPERTS = 4
D_EXPERT = 1024
SWIGLU_LIMIT = 7.0
SWIGLU_ALPHA = 1.702
PLE_DIM = 256
LN_EPS = 1e-5
DEPTH = 1
DEEPNORM_ALPHA = (2 * DEPTH) ** 0.25

LANES = 128
NEG_BIG = -1e30
F32_TINY = float(np.finfo(np.float32).tiny)
VMEM_LIMIT = 56 * 1024 * 1024

TM = 512
TQI = 256
TKI = 512
TA = 512
SUB = 128
TME = 256
DMA_UNROLL = 8
CONV_HALO = 32


def _params(sem, vmem=VMEM_LIMIT):
    return pltpu.CompilerParams(dimension_semantics=sem, vmem_limit_bytes=vmem)


def _sigmoid(x):
    return 1.0 / (1.0 + jnp.exp(-x))


def _layer_norm(x, g, b):
    mu = jnp.mean(x, axis=-1, keepdims=True)
    xc = x - mu
    var = jnp.mean(xc * xc, axis=-1, keepdims=True)
    return xc * lax.rsqrt(var + LN_EPS) * g + b


def _dot(a, b):
    return jnp.dot(a, b, preferred_element_type=F32)


def _dot_nt(a, b):
    return lax.dot_general(a, b, (((1,), (1,)), ((), ())), preferred_element_type=F32)


def _proj_kernel(x_ref, wqkv_ref, bqkv_ref, wqi_ref, bqi_ref, wkw_ref, bkw_ref, wcv_ref, bcv_ref,
                 wg_ref, bg_ref, q_ref, k_ref, v_ref, qi_ref, kw_ref, u_ref, sg_ref):
    xb = x_ref[...].astype(BF16)
    qkv = _dot(xb, wqkv_ref[...]) + bqkv_ref[...]
    q_ref[...] = (qkv[:, :ATTN_WIDTH] * (ATTN_SCALE * LOG2E)).astype(BF16)
    k_ref[...] = qkv[:, ATTN_WIDTH:2 * ATTN_WIDTH].astype(BF16)
    v_ref[...] = qkv[:, 2 * ATTN_WIDTH:].astype(BF16)
    qi_ref[...] = (_dot(xb, wqi_ref[...]) + bqi_ref[...]).astype(BF16)
    kw = _dot(xb, wkw_ref[...]) + bkw_ref[...]
    lane = lax.broadcasted_iota(jnp.int32, kw.shape, 1)
    kw_ref[...] = jnp.where(lane >= IDX_DIM, kw * IDX_SCALE, kw)
    cv = _dot(xb, wcv_ref[...]) + bcv_ref[...]
    u_ref[...] = cv[:, :CONV_CH] * _sigmoid(cv[:, CONV_CH:])
    sg_ref[...] = _sigmoid(_dot(xb, wg_ref[...]) + bg_ref[...])


def _proj(x, wqkv, bqkv, wqi, bqi, wkw, bkw, wcv, bcv, wg, bg):
    S = x.shape[0]
    full = lambda a: pl.BlockSpec(a.shape, lambda i: (0, 0))
    row = lambda n: pl.BlockSpec((TM, n), lambda i: (i, 0))
    outs = [(ATTN_WIDTH, BF16)] * 3 + [(IDX_HEADS * IDX_DIM, BF16), (LANES, F32), (CONV_CH, F32),
                                        (2 * D_MODEL, F32)]
    return pl.pallas_call(
        _proj_kernel,
        grid=(S // TM,),
        in_specs=[row(D_MODEL)] + [full(a) for a in (wqkv, bqkv, wqi, bqi, wkw, bkw, wcv, bcv, wg, bg)],
        out_specs=[row(n) for n, _ in outs],
        out_shape=[jax.ShapeDtypeStruct((S, n), dt) for n, dt in outs],
        compiler_params=_params(("parallel",)),
        name="proj",
    )(x, wqkv, bqkv, wqi, bqi, wkw, bkw, wcv, bcv, wg, bg)


def _conv_kernel(u_ref, wdw_ref, bdw_ref, lng_ref, lnb_ref, wo_ref, sgc_ref, o_ref, buf_ref):
    @pl.when(pl.program_id(0) == 0)
    def _():
        buf_ref[0:CONV_HALO, :] = jnp.zeros((CONV_HALO, CONV_CH), F32)

    buf_ref[CONV_HALO:CONV_HALO + TM, :] = u_ref[...]
    base = CONV_HALO - (CONV_WIDTH - 1)
    acc = jnp.zeros((TM, CONV_CH), F32) + bdw_ref[...]
    for j in range(CONV_WIDTH):
        acc = acc + wdw_ref[j:j + 1, :] * buf_ref[base + j:base + j + TM, :]
    buf_ref[0:CONV_HALO, :] = buf_ref[TM:TM + CONV_HALO, :]
    y = _layer_norm(acc, lng_ref[...], lnb_ref[...])
    y = y * _sigmoid(y)
    o_ref[...] = sgc_ref[...] * _dot(y.astype(BF16), wo_ref[...])


def _conv(u, sg, wdw, bdw, lng, lnb, wo):
    S = u.shape[0]
    full = lambda a: pl.BlockSpec(a.shape, lambda i: (0, 0))
    return pl.pallas_call(
        _conv_kernel,
        grid=(S // TM,),
        in_specs=[pl.BlockSpec((TM, CONV_CH), lambda i: (i, 0)), full(wdw), full(bdw), full(lng), full(lnb),
                  full(wo), pl.BlockSpec((TM, D_MODEL), lambda i: (i, 1))],
        out_specs=pl.BlockSpec((TM, D_MODEL), lambda i: (i, 0)),
        out_shape=jax.ShapeDtypeStruct((S, D_MODEL), F32),
        scratch_shapes=[pltpu.VMEM((TM + CONV_HALO, CONV_CH), F32)],
        compiler_params=_params(("arbitrary",)),
        name="conv",
    )(u, wdw, bdw, lng, lnb, wo, sg)


def _float_key(f):
    b = lax.bitcast_convert_type(f, jnp.int32)
    return b ^ ((b >> 31) & jnp.int32(0x7FFFFFFF))


def _key_float(k):
    b = k ^ ((k >> 31) & jnp.int32(0x7FFFFFFF))
    return lax.bitcast_convert_type(b, F32)


def _index_kernel(qi_ref, wt_ref, ki_ref, mask_ref, sc_ref, qh_ref, *, top_k):
    i = pl.program_id(0)
    nkt = (i * TQI + TQI + TKI - 1) // TKI
    ntile = sc_ref.shape[0]
    q_g = i * TQI + lax.broadcasted_iota(jnp.int32, (1, TQI), 1)
    lane = lax.broadcasted_iota(jnp.int32, (TQI, LANES), 1)

    for h in range(IDX_HEADS):
        qp = qi_ref[:, (h // 2) * LANES:(h // 2 + 1) * LANES]
        keep = (lane < IDX_DIM) if h % 2 == 0 else (lane >= IDX_DIM)
        qh_ref[h] = jnp.where(keep, qp, jnp.zeros_like(qp))

    def score_tile(kt, causal):
        kb = ki_ref[pl.ds(pl.multiple_of(kt * TKI, TKI), TKI), :]
        acc = jnp.zeros((TKI, TQI), F32)
        for h in range(IDX_HEADS):
            acc = acc + wt_ref[h:h + 1, :] * jnp.maximum(_dot_nt(kb, qh_ref[h]), 0.0)
        if causal:
            key_g = kt * TKI + lax.broadcasted_iota(jnp.int32, (TKI, TQI), 0)
            ok = key_g <= q_g
            lo_src = jnp.where(ok, acc, jnp.inf)
            acc = jnp.where(ok, acc, -jnp.inf)
        else:
            lo_src = acc
        sc_ref[kt] = acc
        return jnp.max(acc, axis=0, keepdims=True), jnp.min(lo_src, axis=0, keepdims=True)

    def score_body(kt, carry):
        mx, mn = carry
        tmx, tmn = score_tile(kt, False)
        return jnp.maximum(mx, tmx), jnp.minimum(mn, tmn)

    mx0 = jnp.full((1, TQI), -jnp.inf, F32)
    mn0 = jnp.full((1, TQI), jnp.inf, F32)
    mx, mn = lax.fori_loop(0, nkt - 1, score_body, (mx0, mn0))
    tmx, tmn = score_tile(nkt - 1, True)
    mx = jnp.maximum(mx, tmx)
    mn = jnp.minimum(mn, tmn)
    kf = float(top_k)
    sub = 8

    def count(pivot, strict):
        def tile(kt, cnt):
            for r in range(TKI // sub):
                blk = sc_ref[kt, r * sub:(r + 1) * sub, :]
                hit = (blk > pivot) if strict else (blk >= pivot)
                cnt = cnt + jnp.where(hit, 1.0, 0.0)
            return cnt

        def pair(j, cnts):
            return tile(2 * j, cnts[0]), tile(2 * j + 1, cnts[1])

        zero = jnp.zeros((sub, TQI), F32)
        c0, c1 = lax.fori_loop(0, nkt // 2, pair, (zero, zero))
        c0 = lax.fori_loop(2 * (nkt // 2), nkt, tile, c0)
        return jnp.sum(c0 + c1, axis=0, keepdims=True)

    n_causal = (q_g + 1).astype(F32)
    all_sel = n_causal <= kf
    state0 = dict(
        it=jnp.int32(0),
        lo=mn, hi=_key_float(_float_key(mx) + 1),
        glo=jnp.log(jnp.maximum(n_causal, kf + 1.0) / kf), ghi=jnp.full((1, TQI), math.log(0.5 / kf), F32),
        thr=jnp.where(all_sel, NEG_BIG, 0.0).astype(F32),
        done=all_sel.astype(F32), tie=jnp.zeros((1, TQI), F32), side=jnp.zeros((1, TQI), F32),
    )

    def cond(st):
        return jnp.logical_and(st["it"] < 100, jnp.min(st["done"]) < 0.5)

    def step(st):
        it, lo, hi, glo, ghi = st["it"], st["lo"], st["hi"], st["glo"], st["ghi"]
        lo_k, hi_k = _float_key(lo), _float_key(hi)
        above_lo = jnp.where(jnp.abs(lo) < F32_TINY, F32_TINY, _key_float(lo_k + 1))
        below_hi = jnp.where(hi == F32_TINY, 0.0, _key_float(hi_k - 1))
        probe = jnp.logical_and(st["done"] < 0.5, above_lo < hi)
        tie_now = jnp.logical_and(st["done"] < 0.5, above_lo >= hi)
        frac = jnp.where(it < 24, glo / (glo - ghi), 0.5)
        pf = lo + (hi - lo) * frac
        pf = jnp.where(it < 64, pf, _key_float((lo_k >> 1) + (hi_k >> 1) + (lo_k & hi_k & 1)))
        pf = jnp.where(it == 0, 0.0, jnp.where(it == 1, F32_TINY, pf))
        pf = jnp.where(probe, jnp.minimum(jnp.maximum(pf, above_lo), below_hi), lo)
        c = count(pf, False)
        hit = jnp.logical_and(probe, c == kf)
        up = jnp.logical_and(probe, c > kf)
        dn = jnp.logical_and(probe, c < kf)
        g = jnp.log(jnp.maximum(c, 0.5) / kf)
        return dict(
            it=it + 1,
            lo=jnp.where(up, pf, lo), hi=jnp.where(dn, pf, hi),
            glo=jnp.where(up, g, jnp.where(jnp.logical_and(dn, st["side"] < -0.5), 0.5 * glo, glo)),
            ghi=jnp.where(dn, g, jnp.where(jnp.logical_and(up, st["side"] > 0.5), 0.5 * ghi, ghi)),
            thr=jnp.where(hit, pf, jnp.where(tie_now, lo, st["thr"])),
            done=jnp.where(jnp.logical_or(hit, tie_now), 1.0, st["done"]),
            tie=jnp.where(tie_now, 1.0, st["tie"]),
            side=jnp.where(up, 1.0, jnp.where(dn, -1.0, st["side"])),
        )

    st = lax.while_loop(cond, step, state0)
    thr = st["thr"]
    tie = st["tie"]
    any_tie = jnp.max(tie) > 0.5

    def emit(kt, sel):
        mask_ref[0, kt] = jnp.where(sel, 0.0, NEG_BIG).T.astype(mask_ref.dtype)

    @pl.when(jnp.logical_not(any_tie))
    def _():
        def body(kt, carry):
            emit(kt, sc_ref[kt] >= thr)
            return carry
        lax.fori_loop(0, nkt, body, 0)

    @pl.when(any_tie)
    def _():
        need = jnp.where(tie > 0.5, kf - count(thr, True), float(2 * ntile * TKI))
        r = lax.broadcasted_iota(jnp.int32, (TKI, TKI), 0)
        c = lax.broadcasted_iota(jnp.int32, (TKI, TKI), 1)
        prefix = jnp.where(c <= r, 1.0, 0.0).astype(BF16)

        def body(kt, seen):
            s = sc_ref[kt]
            eq = s == thr
            rank = seen + _dot(prefix, jnp.where(eq, 1.0, 0.0).astype(BF16))
            emit(kt, jnp.logical_or(s > thr, jnp.logical_and(eq, rank <= need)))
            return seen + jnp.sum(jnp.where(eq, 1.0, 0.0), axis=0, keepdims=True)
        lax.fori_loop(0, nkt, body, jnp.zeros((1, TQI), F32))

    def fill(kt, carry):
        mask_ref[0, kt] = jnp.full((TQI, TKI), NEG_BIG, mask_ref.dtype)
        return carry
    lax.fori_loop(nkt, ntile, fill, 0)


def _index_mask(qi, wt, ki2, top_k):
    S = qi.shape[0]
    nq, nk = S // TQI, S // TKI
    return pl.pallas_call(
        functools.partial(_index_kernel, top_k=top_k),
        grid=(nq,),
        in_specs=[pl.BlockSpec((TQI, IDX_HEADS * IDX_DIM), lambda i: (i, 0)),
                  pl.BlockSpec((IDX_HEADS, TQI), lambda i: (0, i)),
                  pl.BlockSpec((S, LANES), lambda i: (0, 0))],
        out_specs=pl.BlockSpec((1, nk, TQI, TKI), lambda i: (i, 0, 0, 0)),
        out_shape=jax.ShapeDtypeStruct((nq, nk, TQI, TKI), BF16),
        scratch_shapes=[pltpu.VMEM((nk, TKI, TQI), F32),
                        pltpu.VMEM((IDX_HEADS, TQI, LANES), BF16)],
        compiler_params=_params(("parallel",)),
        name="index_mask",
    )(qi, wt, ki2)


def _attn_kernel(qt_ref, kt_ref, q_ref, k_ref, v_ref, m_ref, bd_ref, be_ref, o_ref, m_sc, acc_sc):
    step = pl.program_id(0)
    qi = qt_ref[step]
    ki = kt_ref[step]
    nsub = TA // SUB

    @pl.when(ki == 0)
    def _():
        m_sc[...] = jnp.full(m_sc.shape, NEG_BIG, F32)
        acc_sc[...] = jnp.zeros(acc_sc.shape, F32)

    lane = lax.broadcasted_iota(jnp.int32, (TA, LANES), 1)
    first = lane < HEAD_DIM

    def bias_tile(h, diagonal):
        zero = jnp.zeros((SUB, SUB), F32)
        rows = []
        for a in range(nsub):
            if diagonal:
                blks = [bd_ref[h] if b == a else (be_ref[h] if b == a - 1 else zero) for b in range(nsub)]
            else:
                blks = [be_ref[h] if (a == 0 and b == nsub - 1) else zero for b in range(nsub)]
            rows.append(jnp.concatenate(blks, axis=1))
        return jnp.concatenate(rows, axis=0)

    def update(near, diagonal):
        maskf = jnp.concatenate([m_ref[a, 0] for a in range(TA // TQI)], axis=0).astype(F32)
        for p in range(N_HEADS // 2):
            cols = slice(p * LANES, (p + 1) * LANES)
            qp = q_ref[:, cols]
            kp = k_ref[:, cols]
            vp = v_ref[:, cols]
            for hh in range(2):
                h = 2 * p + hh
                mine = first if hh == 0 else jnp.logical_not(first)
                qh = jnp.where(mine, qp, jnp.zeros_like(qp))
                vh = jnp.where(mine, vp, jnp.ones_like(vp))
                s = _dot_nt(qh, kp) + maskf
                if near:
                    s = s + bias_tile(h, diagonal)
                m_prev = m_sc[h]
                m_next = jnp.maximum(m_prev, jnp.max(s, axis=1, keepdims=True))
                pexp = jnp.exp2(s - jnp.concatenate([m_next] * (TA // LANES), axis=1)).astype(BF16)
                acc_sc[h] = jnp.exp2(m_prev - m_next) * acc_sc[h] + _dot(pexp, vh)
                m_sc[h] = m_next

    @pl.when(ki < qi - 1)
    def _():
        update(False, False)

    @pl.when(ki == qi - 1)
    def _():
        update(True, False)

    @pl.when(ki == qi)
    def _():
        update(True, True)
        for p in range(N_HEADS // 2):
            a0, a1 = acc_sc[2 * p], acc_sc[2 * p + 1]
            d0 = pltpu.roll(a0, HEAD_DIM, axis=1)
            d1 = pltpu.roll(a1, HEAD_DIM, axis=1)
            o_ref[:, p * LANES:(p + 1) * LANES] = jnp.where(first, a0 / d0, a1 / d1).astype(o_ref.dtype)


def _attention(q, k, v, mask4, bias_d, bias_e):
    S = q.shape[0]
    nb = S // TA
    pairs = [(a, b) for a in range(nb) for b in range(a + 1)]
    qtab = jnp.asarray(np.array([a for a, _ in pairs], np.int32))
    ktab = jnp.asarray(np.array([b for _, b in pairs], np.int32))
    grid_spec = pltpu.PrefetchScalarGridSpec(
        num_scalar_prefetch=2,
        grid=(len(pairs),),
        in_specs=[pl.BlockSpec((TA, ATTN_WIDTH), lambda s, qt, kt: (qt[s], 0)),
                  pl.BlockSpec((TA, ATTN_WIDTH), lambda s, qt, kt: (kt[s], 0)),
                  pl.BlockSpec((TA, ATTN_WIDTH), lambda s, qt, kt: (kt[s], 0)),
                  pl.BlockSpec((TA // TQI, TA // TKI, TQI, TKI), lambda s, qt, kt: (qt[s], kt[s], 0, 0)),
                  pl.BlockSpec(bias_d.shape, lambda s, qt, kt: (0, 0, 0)),
                  pl.BlockSpec(bias_e.shape, lambda s, qt, kt: (0, 0, 0))],
        out_specs=pl.BlockSpec((TA, ATTN_WIDTH), lambda s, qt, kt: (qt[s], 0)),
        scratch_shapes=[pltpu.VMEM((N_HEADS, TA, LANES), F32),
                        pltpu.VMEM((N_HEADS, TA, LANES), F32)],
    )
    return pl.pallas_call(
        _attn_kernel,
        grid_spec=grid_spec,
        out_shape=jax.ShapeDtypeStruct((S, ATTN_WIDTH), BF16),
        compiler_params=_params(("arbitrary",)),
        name="attn",
    )(qtab, ktab, q, k, v, mask4, bias_d, bias_e)


def _relative_bias_blocks(rel_bias):
    dist = jnp.arange(2 * SUB, dtype=jnp.int32)
    max_exact = N_BUCKETS // 2
    dist_f = jnp.maximum(dist, 1).astype(F32)
    large = max_exact + (jnp.log(dist_f / max_exact) / math.log(MAX_DISTANCE / max_exact)
                         * (N_BUCKETS - max_exact)).astype(jnp.int32)
    bucket = jnp.where(dist < max_exact, dist, jnp.minimum(large, N_BUCKETS - 1))
    table = ((rel_bias[bucket] - rel_bias[N_BUCKETS - 1]) * LOG2E).astype(F32)
    i = jnp.arange(SUB)[:, None]
    j = jnp.arange(SUB)[None, :]
    diag = jnp.transpose(table[jnp.clip(i - j, 0, 2 * SUB - 1)], (2, 0, 1))
    sub = jnp.transpose(table[SUB + i - j], (2, 0, 1))
    return diag, sub


def _pack_bf16_pair(a, b):
    def rnd(x):
        bits = lax.bitcast_convert_type(x, jnp.uint32)
        return bits + jnp.uint32(0x7FFF) + ((bits >> 16) & jnp.uint32(1))
    return (rnd(a) >> 16) | (rnd(b) & jnp.uint32(0xFFFF0000))


def _unpack_bf16_pair(p):
    lo = lax.bitcast_convert_type(p << 16, F32)
    hi = lax.bitcast_convert_type(p & jnp.uint32(0xFFFF0000), F32)
    return lo.astype(BF16), hi.astype(BF16)


def _post_kernel(ya_ref, sga_ref, pc_ref, x_ref, woa_ref, wout_ref, g_ref, b_ref, wr_ref, br_ref,
                 x1_ref, x1p_ref, ridx_ref, rgate_ref):
    y_attn = _dot(ya_ref[...], woa_ref[...])
    merged = sga_ref[...] * y_attn + pc_ref[...]
    mix = _dot(merged.astype(BF16), wout_ref[...])
    x1 = _layer_norm(DEEPNORM_ALPHA * x_ref[...] + mix, g_ref[...], b_ref[...])
    x1_ref[...] = x1
    half = D_MODEL // 2
    x1p_ref[:, 0, :] = _pack_bf16_pair(x1[:, :half], x1[:, half:])

    logits = jnp.dot(x1, wr_ref[...], preferred_element_type=F32, precision=lax.Precision.HIGHEST) + br_ref[...]
    lane = lax.broadcasted_iota(jnp.int32, logits.shape, 1).astype(F32)
    cur = logits
    vals, idxs = [], []
    for _ in range(TOP_K_EXPERTS):
        m = jnp.max(cur, axis=1, keepdims=True)
        ix = jnp.min(jnp.where(cur == m, lane, float(LANES)), axis=1, keepdims=True)
        vals.append(m)
        idxs.append(ix)
        cur = jnp.where(lane == ix, -jnp.inf, cur)
    exps = [jnp.exp(v - vals[0]) for v in vals]
    denom = exps[0]
    for e in exps[1:]:
        denom = denom + e
    ridx = jnp.zeros_like(logits)
    rgate = jnp.zeros_like(logits)
    for j in range(TOP_K_EXPERTS):
        ridx = jnp.where(lane == float(j), idxs[j], ridx)
        rgate = jnp.where(lane == float(j), exps[j] / denom, rgate)
    ridx_ref[...] = ridx.astype(jnp.int32)
    rgate_ref[...] = rgate


def _post(ya, sg, pc, x, woa, wout, g, b, wr, br):
    S = x.shape[0]
    full = lambda a: pl.BlockSpec(a.shape, lambda i: (0, 0))
    row = lambda n: pl.BlockSpec((TM, n), lambda i: (i, 0))
    return pl.pallas_call(
        _post_kernel,
        grid=(S // TM,),
        in_specs=[row(ATTN_WIDTH), row(D_MODEL), row(D_MODEL), row(D_MODEL)] + [full(a) for a in (woa, wout, g, b, wr, br)],
        out_specs=[row(D_MODEL), pl.BlockSpec((TM, 1, D_MODEL // 2), lambda i: (i, 0, 0)), row(LANES), row(LANES)],
        out_shape=[jax.ShapeDtypeStruct((S, D_MODEL), F32), jax.ShapeDtypeStruct((S, 1, D_MODEL // 2), jnp.uint32),
                   jax.ShapeDtypeStruct((S, LANES), jnp.int32), jax.ShapeDtypeStruct((S, LANES), F32)],
        compiler_params=_params(("parallel",)),
        name="post",
    )(ya, sg, pc, x, woa, wout, g, b, wr, br)


def _expert_kernel(te_ref, nv_ref, nused_ref, tok_ref, tokn_ref, dst_ref, x_hbm, wgu_ref, bgu_ref, wd_ref, bd_ref,
                   ys_hbm, xs_buf, y_buf, wgu_sc, wd_sc, gsem, ssem):
    t = pl.program_id(0)
    nused = nused_ref[0]
    cur = t % 2
    nxt = 1 - cur

    def start_gather(idx_ref, slot):
        def body(r, carry):
            pltpu.make_async_copy(x_hbm.at[idx_ref[0, 0, r]], xs_buf.at[slot, pl.ds(r, 1)], gsem.at[slot]).start()
            return carry
        lax.fori_loop(0, TME, body, 0, unroll=DMA_UNROLL)

    def wait_gather(slot):
        pltpu.make_async_copy(xs_buf.at[slot], xs_buf.at[slot], gsem.at[slot]).wait()

    def wait_scatter(slot, n):
        pltpu.make_async_copy(ys_hbm.at[pl.ds(0, n)], ys_hbm.at[pl.ds(0, n)], ssem.at[slot]).wait()

    @pl.when(t == 0)
    def _():
        start_gather(tok_ref, 0)

    @pl.when(t + 1 < nused)
    def _():
        start_gather(tokn_ref, nxt)

    e = te_ref[t]
    e_prev = te_ref[jnp.maximum(t - 1, 0)]

    @pl.when(jnp.logical_or(t == 0, e != e_prev))
    def _():
        wgu_sc[...] = wgu_ref[0].astype(BF16)
        wd_sc[...] = wd_ref[0].astype(BF16)

    @pl.when(t < nused)
    def _():
        wait_gather(cur)

        @pl.when(t >= 2)
        def _():
            wait_scatter(cur, nv_ref[jnp.maximum(t - 2, 0)])

        half = D_MODEL // 2
        lo, hi = _unpack_bf16_pair(xs_buf[cur])
        gu = _dot(lo, wgu_sc[0:half, :]) + _dot(hi, wgu_sc[half:, :]) + bgu_ref[0]
        g = jnp.minimum(gu[:, :D_EXPERT], SWIGLU_LIMIT)
        u = jnp.clip(gu[:, D_EXPERT:], -SWIGLU_LIMIT, SWIGLU_LIMIT)
        act = (u + 1.0) * (g * _sigmoid(SWIGLU_ALPHA * g))
        y_buf[cur] = _dot(act.astype(BF16), wd_sc[...]) + bd_ref[0]

        def scatter(r, carry):
            pltpu.make_async_copy(y_buf.at[cur, pl.ds(r, 1)], ys_hbm.at[dst_ref[0, 0, r]], ssem.at[cur]).start()
            return carry

        def scatter_group(c, carry):
            for j in range(DMA_UNROLL):
                scatter(c * DMA_UNROLL + j, carry)
            return carry
        n_groups = nv_ref[t] // DMA_UNROLL
        lax.fori_loop(0, n_groups, scatter_group, 0)
        lax.fori_loop(n_groups * DMA_UNROLL, nv_ref[t], scatter, 0)

        @pl.when(t == nused - 1)
        def _():
            @pl.when(t >= 1)
            def _():
                wait_scatter(nxt, nv_ref[jnp.maximum(t - 1, 0)])
            wait_scatter(cur, nv_ref[t])


def _experts(tile_e, tile_nv, nused, row_token, row_dst, x1p, wgu, bgu, wd, bd, n_tiles, n_dst):
    idx = lambda a: a.reshape(n_tiles, 1, TME)
    smem_tile = lambda f: pl.BlockSpec((1, 1, TME), f, memory_space=pltpu.SMEM)
    grid_spec = pltpu.PrefetchScalarGridSpec(
        num_scalar_prefetch=3,
        grid=(n_tiles,),
        in_specs=[smem_tile(lambda t, te, nv, nu: (t, 0, 0)),
                  smem_tile(lambda t, te, nv, nu: (jnp.minimum(t + 1, n_tiles - 1), 0, 0)),
                  smem_tile(lambda t, te, nv, nu: (t, 0, 0)),
                  pl.BlockSpec(memory_space=pl.ANY),
                  pl.BlockSpec((1, D_MODEL, 2 * D_EXPERT), lambda t, te, nv, nu: (te[t], 0, 0)),
                  pl.BlockSpec((1, 1, 2 * D_EXPERT), lambda t, te, nv, nu: (te[t], 0, 0)),
                  pl.BlockSpec((1, D_EXPERT, D_MODEL), lambda t, te, nv, nu: (te[t], 0, 0)),
                  pl.BlockSpec((1, 1, D_MODEL), lambda t, te, nv, nu: (te[t], 0, 0))],
        out_specs=pl.BlockSpec(memory_space=pl.ANY),
        scratch_shapes=[pltpu.VMEM((2, TME, D_MODEL // 2), jnp.uint32), pltpu.VMEM((2, TME, D_MODEL), F32),
                        pltpu.VMEM((D_MODEL, 2 * D_EXPERT), BF16), pltpu.VMEM((D_EXPERT, D_MODEL), BF16),
                        pltpu.SemaphoreType.DMA((2,)), pltpu.SemaphoreType.DMA((2,))],
    )
    return pl.pallas_call(
        _expert_kernel,
        grid_spec=grid_spec,
        out_shape=jax.ShapeDtypeStruct((n_dst, 1, D_MODEL), F32),
        compiler_params=_params(("arbitrary",)),
        name="moe_experts",
    )(tile_e, tile_nv, nused, idx(row_token), idx(row_token), idx(row_dst), x1p, wgu, bgu, wd, bd)


def _routing_tables(ridx, S):
    n_flat = S * TOP_K_EXPERTS
    n_tiles = n_flat // TME + N_EXPERTS
    flat_e = ridx.reshape(n_flat)
    order = jnp.argsort(flat_e, stable=True).astype(jnp.int32)
    counts = jnp.sum(flat_e[:, None] == jnp.arange(N_EXPERTS, dtype=jnp.int32)[None, :], axis=0, dtype=jnp.int32)
    padded = ((counts + TME - 1) // TME) * TME
    pad_end = jnp.cumsum(padded)
    pad_start = pad_end - padded
    grp_start = jnp.cumsum(counts) - counts
    nused = (pad_end[-1] // TME).astype(jnp.int32).reshape(1)
    tile_row0 = jnp.arange(n_tiles, dtype=jnp.int32) * TME
    tile_e = jnp.minimum(jnp.sum(tile_row0[:, None] >= pad_end[None, :], axis=1), N_EXPERTS - 1).astype(jnp.int32)
    tile_rank0 = tile_row0 - pad_start[tile_e]
    tile_nv = jnp.where(tile_row0 < pad_end[-1], jnp.clip(counts[tile_e] - tile_rank0, 0, TME), 0).astype(jnp.int32)
    within = jnp.arange(TME, dtype=jnp.int32)[None, :]
    valid = within < tile_nv[:, None]
    src_flat = order[jnp.clip((grp_start[tile_e] + tile_rank0)[:, None] + within, 0, n_flat - 1)]
    row_token = jnp.where(valid, src_flat // TOP_K_EXPERTS, 0).astype(jnp.int32)
    row_dst = jnp.where(valid, (src_flat % TOP_K_EXPERTS) * S + src_flat // TOP_K_EXPERTS, 0).astype(jnp.int32)
    return tile_e, tile_nv, nused, row_token, row_dst, n_tiles


def _final_kernel(x1_ref, y0_ref, y1_ref, y2_ref, y3_ref, rg_ref, p_ref, wpg_ref, wpp_ref, g_ref, b_ref, o_ref):
    h = DEEPNORM_ALPHA * x1_ref[...]
    rg = rg_ref[...]
    for j, y_ref in enumerate((y0_ref, y1_ref, y2_ref, y3_ref)):
        h = h + rg[:, j:j + 1] * y_ref[:, 0, :]
    ple = _sigmoid(_dot(h.astype(BF16), wpg_ref[...])) * _dot(p_ref[...].astype(BF16), wpp_ref[...])
    o_ref[...] = _layer_norm(h + ple, g_ref[...], b_ref[...])


def _final(x1, ys, rgate, p, wpg, wpp, g, b):
    S = x1.shape[0]
    full = lambda a: pl.BlockSpec(a.shape, lambda i: (0, 0))
    row = lambda n: pl.BlockSpec((TM, n), lambda i: (i, 0))
    return pl.pallas_call(
        _final_kernel,
        grid=(S // TM,),
        in_specs=[row(D_MODEL)]
        + [pl.BlockSpec((TM, 1, D_MODEL), lambda i, j=j: (j * (S // TM) + i, 0, 0)) for j in range(TOP_K_EXPERTS)]
        + [row(LANES), row(PLE_DIM), full(wpg), full(wpp), full(g), full(b)],
        out_specs=row(D_MODEL),
        out_shape=jax.ShapeDtypeStruct((S, D_MODEL), F32),
        compiler_params=_params(("parallel",)),
        name="final",
    )(x1, ys, ys, ys, ys, rgate, p, wpg, wpp, g, b)


def _layer(x, p, w_in, b_in, w_o_attn, w_dw, b_dw, conv_ln_g, conv_ln_b, w_o_conv, w_out, ln1_g, ln1_b,
           w_router, b_router, w_gate_up, b_gate_up, w_down, b_down, w_ple_gate, w_ple_proj, ln2_g, ln2_b,
           rel_bias):
    S = x.shape[0]
    assert S % TM == 0 and S % TA == 0 and S % TKI == 0 and (S * TOP_K_EXPERTS) % TME == 0
    top_k = min(TOPK_MAX, S // 4)
    row2 = lambda a: a.reshape(1, -1).astype(F32)

    o_q, o_qi, o_ki, o_cv, o_g = 0, 3 * ATTN_WIDTH, 3 * ATTN_WIDTH + IDX_HEADS * IDX_DIM, \
        3 * ATTN_WIDTH + IDX_HEADS * IDX_DIM + IDX_DIM + IDX_HEADS, \
        3 * ATTN_WIDTH + IDX_HEADS * IDX_DIM + IDX_DIM + IDX_HEADS + 2 * CONV_CH
    kw_pad = LANES - (IDX_DIM + IDX_HEADS)
    wkw = jnp.pad(w_in[:, o_ki:o_cv], ((0, 0), (0, kw_pad)))
    bkw = jnp.pad(b_in[o_ki:o_cv], (0, kw_pad))
    q, k, v, qi, kw, u, sg = _proj(
        x, w_in[:, o_q:o_qi].astype(BF16), row2(b_in[o_q:o_qi]),
        w_in[:, o_qi:o_ki].astype(BF16), row2(b_in[o_qi:o_ki]),
        wkw.astype(BF16), row2(bkw),
        w_in[:, o_cv:o_g].astype(BF16), row2(b_in[o_cv:o_g]),
        w_in[:, o_g:].astype(BF16), row2(b_in[o_g:]))

    part_conv = _conv(u, sg, w_dw, row2(b_dw), row2(conv_ln_g), row2(conv_ln_b), w_o_conv.astype(BF16))

    ki = kw[:, :IDX_DIM].astype(BF16)
    mask4 = _index_mask(qi, kw[:, IDX_DIM:IDX_DIM + IDX_HEADS].T, jnp.concatenate([ki, ki], axis=1), top_k)
    bias_d, bias_e = _relative_bias_blocks(rel_bias)
    y_attn = _attention(q, k, v, mask4, bias_d, bias_e)

    wr = jnp.pad(w_router, ((0, 0), (0, LANES - N_EXPERTS)))
    br = jnp.pad(b_router, (0, LANES - N_EXPERTS), constant_values=-jnp.inf)
    x1, x1p, ridx, rgate = _post(y_attn, sg, part_conv, x, w_o_attn.astype(BF16), w_out.astype(BF16),
                                 row2(ln1_g), row2(ln1_b), wr, row2(br))

    tile_e, tile_nv, nused, row_token, row_dst, n_tiles = _routing_tables(ridx[:, :TOP_K_EXPERTS], S)
    ys = _experts(tile_e, tile_nv, nused, row_token, row_dst, x1p, w_gate_up, b_gate_up.reshape(N_EXPERTS, 1, -1),
                  w_down, b_down.reshape(N_EXPERTS, 1, -1), n_tiles, S * TOP_K_EXPERTS)
    return _final(x1, ys, rgate, p, w_ple_gate.astype(BF16), w_ple_proj.astype(BF16), row2(ln2_g), row2(ln2_b))


def kernel(x, p, w_in, b_in, w_o_attn, w_dw, b_dw, conv_ln_g, conv_ln_b, w_o_conv, w_out, ln1_g, ln1_b, w_router, b_router, w_gate_up, b_gate_up, w_down, b_down, w_ple_gate, w_ple_proj, ln2_g, ln2_b, rel_bias):
    assert x.shape[0] == 1 and p.shape[0] == DEPTH
    out = _layer(x[0], p[0, 0], w_in[0], b_in[0], w_o_attn[0], w_dw[0], b_dw[0], conv_ln_g[0], conv_ln_b[0],
                 w_o_conv[0], w_out[0], ln1_g[0], ln1_b[0], w_router[0], b_router[0], w_gate_up[0], b_gate_up[0],
                 w_down[0], b_down[0], w_ple_gate[0], w_ple_proj[0], ln2_g[0], ln2_b[0], rel_bias)
    return out[None]
```

```python
import functools
import math

import jax
import jax.numpy as jnp
import numpy as np
from jax import lax
from jax.experimental import pallas as pl
from jax.experimental.pallas import tpu as pltpu

F32 = jnp.float32
BF16 = jnp.bfloat16

D_MODEL = 1024
N_HEADS = 8
HEAD_DIM = 64
ATTN_WIDTH = N_HEADS * HEAD_DIM
ATTN_SCALE = HEAD_DIM ** -0.5
LOG2E = math.log2(math.e)
IDX_HEADS = 8
IDX_DIM = 64
IDX_SCALE = (IDX_HEADS ** -0.5) * (IDX_DIM ** -0.5)
TOPK_MAX = 256
CONV_CH = 512
CONV_WIDTH = 31
N_BUCKETS = 32
MAX_DISTANCE = 128
N_EXPERTS = 32
TOP_K_EXPERTS = 4
D_EXPERT = 1024
SWIGLU_LIMIT = 7.0
SWIGLU_ALPHA = 1.702
PLE_DIM = 256
LN_EPS = 1e-5
DEPTH = 1
DEEPNORM_ALPHA = (2 * DEPTH) ** 0.25

LANES = 128
NEG_BIG = -1e30
F32_TINY = float(np.finfo(np.float32).tiny)
VMEM_LIMIT = 56 * 1024 * 1024

TM = 512
TQI = 256
TKI = 512
TA = 512
SUB = 128
TME = 256
DMA_UNROLL = 8
CONV_HALO = 32


def _params(sem, vmem=VMEM_LIMIT):
    return pltpu.CompilerParams(dimension_semantics=sem, vmem_limit_bytes=vmem)


def _sigmoid(x):
    return 1.0 / (1.0 + jnp.exp(-x))


def _layer_norm(x, g, b):
    mu = jnp.mean(x, axis=-1, keepdims=True)
    xc = x - mu
    var = jnp.mean(xc * xc, axis=-1, keepdims=True)
    return xc * lax.rsqrt(var + LN_EPS) * g + b


def _dot(a, b):
    return jnp.dot(a, b, preferred_element_type=F32)


def _dot_nt(a, b):
    return lax.dot_general(a, b, (((1,), (1,)), ((), ())), preferred_element_type=F32)


def _proj_kernel(x_ref, wqkv_ref, bqkv_ref, wqi_ref, bqi_ref, wkw_ref, bkw_ref, wcv_ref, bcv_ref,
                 wg_ref, bg_ref, q_ref, k_ref, v_ref, qi_ref, kw_ref, u_ref, sg_ref):
    xb = x_ref[...].astype(BF16)
    qkv = _dot(xb, wqkv_ref[...]) + bqkv_ref[...]
    q_ref[...] = (qkv[:, :ATTN_WIDTH] * (ATTN_SCALE * LOG2E)).astype(BF16)
    k_ref[...] = qkv[:, ATTN_WIDTH:2 * ATTN_WIDTH].astype(BF16)
    v_ref[...] = qkv[:, 2 * ATTN_WIDTH:].astype(BF16)
    qi_ref[...] = (_dot(xb, wqi_ref[...]) + bqi_ref[...]).astype(BF16)
    kw = _dot(xb, wkw_ref[...]) + bkw_ref[...]
    lane = lax.broadcasted_iota(jnp.int32, kw.shape, 1)
    kw_ref[...] = jnp.where(lane >= IDX_DIM, kw * IDX_SCALE, kw)
    cv = _dot(xb, wcv_ref[...]) + bcv_ref[...]
    u_ref[...] = cv[:, :CONV_CH] * _sigmoid(cv[:, CONV_CH:])
    sg_ref[...] = _sigmoid(_dot(xb, wg_ref[...]) + bg_ref[...])


def _proj(x, wqkv, bqkv, wqi, bqi, wkw, bkw, wcv, bcv, wg, bg):
    S = x.shape[0]
    full = lambda a: pl.BlockSpec(a.shape, lambda i: (0, 0))
    row = lambda n: pl.BlockSpec((TM, n), lambda i: (i, 0))
    outs = [(ATTN_WIDTH, BF16)] * 3 + [(IDX_HEADS * IDX_DIM, BF16), (LANES, F32), (CONV_CH, F32),
                                        (2 * D_MODEL, F32)]
    return pl.pallas_call(
        _proj_kernel,
        grid=(S // TM,),
        in_specs=[row(D_MODEL)] + [full(a) for a in (wqkv, bqkv, wqi, bqi, wkw, bkw, wcv, bcv, wg, bg)],
        out_specs=[row(n) for n, _ in outs],
        out_shape=[jax.ShapeDtypeStruct((S, n), dt) for n, dt in outs],
        compiler_params=_params(("parallel",)),
        name="proj",
    )(x, wqkv, bqkv, wqi, bqi, wkw, bkw, wcv, bcv, wg, bg)


def _conv_kernel(u_ref, wdw_ref, bdw_ref, lng_ref, lnb_ref, wo_ref, sgc_ref, o_ref, buf_ref):
    @pl.when(pl.program_id(0) == 0)
    def _():
        buf_ref[0:CONV_HALO, :] = jnp.zeros((CONV_HALO, CONV_CH), F32)

    buf_ref[CONV_HALO:CONV_HALO + TM, :] = u_ref[...]
    base = CONV_HALO - (CONV_WIDTH - 1)
    acc = jnp.zeros((TM, CONV_CH), F32) + bdw_ref[...]
    sub = 8
    for r in range(sub):
        taps = [j for j in range(CONV_WIDTH) if (base + j) % sub == r]
        span = max(base + j - r for j in taps) + TM
        shifted = buf_ref[r:r + span, :]
        for j in taps:
            off = base + j - r
            acc = acc + wdw_ref[j:j + 1, :] * shifted[off:off + TM, :]
    buf_ref[0:CONV_HALO, :] = buf_ref[TM:TM + CONV_HALO, :]
    y = _layer_norm(acc, lng_ref[...], lnb_ref[...])
    y = y * _sigmoid(y)
    o_ref[...] = sgc_ref[...] * _dot(y.astype(BF16), wo_ref[...])


def _conv(u, sg, wdw, bdw, lng, lnb, wo):
    S = u.shape[0]
    full = lambda a: pl.BlockSpec(a.shape, lambda i: (0, 0))
    return pl.pallas_call(
        _conv_kernel,
        grid=(S // TM,),
        in_specs=[pl.BlockSpec((TM, CONV_CH), lambda i: (i, 0)), full(wdw), full(bdw), full(lng), full(lnb),
                  full(wo), pl.BlockSpec((TM, D_MODEL), lambda i: (i, 1))],
        out_specs=pl.BlockSpec((TM, D_MODEL), lambda i: (i, 0)),
        out_shape=jax.ShapeDtypeStruct((S, D_MODEL), F32),
        scratch_shapes=[pltpu.VMEM((TM + CONV_HALO, CONV_CH), F32)],
        compiler_params=_params(("arbitrary",)),
        name="conv",
    )(u, wdw, bdw, lng, lnb, wo, sg)


def _float_key(f):
    b = lax.bitcast_convert_type(f, jnp.int32)
    return b ^ ((b >> 31) & jnp.int32(0x7FFFFFFF))


def _key_float(k):
    b = k ^ ((k >> 31) & jnp.int32(0x7FFFFFFF))
    return lax.bitcast_convert_type(b, F32)


def _index_kernel(qi_ref, wt_ref, ki_ref, mask_ref, sc_ref, qh_ref, *, top_k):
    i = pl.program_id(0)
    nkt = (i * TQI + TQI + TKI - 1) // TKI
    ntile = sc_ref.shape[0]
    q_g = i * TQI + lax.broadcasted_iota(jnp.int32, (1, TQI), 1)
    lane = lax.broadcasted_iota(jnp.int32, (TQI, LANES), 1)

    for h in range(IDX_HEADS):
        qp = qi_ref[:, (h // 2) * LANES:(h // 2 + 1) * LANES]
        keep = (lane < IDX_DIM) if h % 2 == 0 else (lane >= IDX_DIM)
        qh_ref[h] = jnp.where(keep, qp, jnp.zeros_like(qp))

    def score_tile(kt, causal):
        kb = ki_ref[pl.ds(pl.multiple_of(kt * TKI, TKI), TKI), :]
        acc = jnp.zeros((TKI, TQI), F32)
        for h in range(IDX_HEADS):
            acc = acc + wt_ref[h:h + 1, :] * jnp.maximum(_dot_nt(kb, qh_ref[h]), 0.0)
        if causal:
            key_g = kt * TKI + lax.broadcasted_iota(jnp.int32, (TKI, TQI), 0)
            ok = key_g <= q_g
            lo_src = jnp.where(ok, acc, jnp.inf)
            acc = jnp.where(ok, acc, -jnp.inf)
        else:
            lo_src = acc
        sc_ref[kt] = acc
        return jnp.max(acc, axis=0, keepdims=True), jnp.min(lo_src, axis=0, keepdims=True)

    def score_body(kt, carry):
        mx, mn = carry
        tmx, tmn = score_tile(kt, False)
        return jnp.maximum(mx, tmx), jnp.minimum(mn, tmn)

    mx0 = jnp.full((1, TQI), -jnp.inf, F32)
    mn0 = jnp.full((1, TQI), jnp.inf, F32)
    mx, mn = lax.fori_loop(0, nkt - 1, score_body, (mx0, mn0))
    tmx, tmn = score_tile(nkt - 1, True)
    mx = jnp.maximum(mx, tmx)
    mn = jnp.minimum(mn, tmn)
    kf = float(top_k)
    sub = 8

    def count(pivot, strict):
        def tile(kt, cnt):
            for r in range(TKI // sub):
                blk = sc_ref[kt, r * sub:(r + 1) * sub, :]
                hit = (blk > pivot) if strict else (blk >= pivot)
                cnt = cnt + jnp.where(hit, 1.0, 0.0)
            return cnt

        def pair(j, cnts):
            return tile(2 * j, cnts[0]), tile(2 * j + 1, cnts[1])

        zero = jnp.zeros((sub, TQI), F32)
        c0, c1 = lax.fori_loop(0, nkt // 2, pair, (zero, zero))
        c0 = lax.fori_loop(2 * (nkt // 2), nkt, tile, c0)
        return jnp.sum(c0 + c1, axis=0, keepdims=True)

    n_causal = (q_g + 1).astype(F32)
    all_sel = n_causal <= kf
    state0 = dict(
        it=jnp.int32(0),
        lo=mn, hi=_key_float(_float_key(mx) + 1),
        glo=jnp.log(jnp.maximum(n_causal, kf + 1.0) / kf), ghi=jnp.full((1, TQI), math.log(0.5 / kf), F32),
        thr=jnp.where(all_sel, NEG_BIG, 0.0).astype(F32),
        done=all_sel.astype(F32), tie=jnp.zeros((1, TQI), F32), side=jnp.zeros((1, TQI), F32),
    )

    def cond(st):
        return jnp.logical_and(st["it"] < 100, jnp.min(st["done"]) < 0.5)

    def step(st):
        it, lo, hi, glo, ghi = st["it"], st["lo"], st["hi"], st["glo"], st["ghi"]
        lo_k, hi_k = _float_key(lo), _float_key(hi)
        above_lo = jnp.where(jnp.abs(lo) < F32_TINY, F32_TINY, _key_float(lo_k + 1))
        below_hi = jnp.where(hi == F32_TINY, 0.0, _key_float(hi_k - 1))
        probe = jnp.logical_and(st["done"] < 0.5, above_lo < hi)
        tie_now = jnp.logical_and(st["done"] < 0.5, above_lo >= hi)
        frac = jnp.where(it < 24, glo / (glo - ghi), 0.5)
        pf = lo + (hi - lo) * frac
        pf = jnp.where(it < 64, pf, _key_float((lo_k >> 1) + (hi_k >> 1) + (lo_k & hi_k & 1)))
        pf = jnp.where(it == 0, 0.0, jnp.where(it == 1, F32_TINY, pf))
        pf = jnp.where(probe, jnp.minimum(jnp.maximum(pf, above_lo), below_hi), lo)
        c = count(pf, False)
        hit = jnp.logical_and(probe, c == kf)
        up = jnp.logical_and(probe, c > kf)
        dn = jnp.logical_and(probe, c < kf)
        g = jnp.log(jnp.maximum(c, 0.5) / kf)
        return dict(
            it=it + 1,
            lo=jnp.where(up, pf, lo), hi=jnp.where(dn, pf, hi),
            glo=jnp.where(up, g, jnp.where(jnp.logical_and(dn, st["side"] < -0.5), 0.5 * glo, glo)),
            ghi=jnp.where(dn, g, jnp.where(jnp.logical_and(up, st["side"] > 0.5), 0.5 * ghi, ghi)),
            thr=jnp.where(hit, pf, jnp.where(tie_now, lo, st["thr"])),
            done=jnp.where(jnp.logical_or(hit, tie_now), 1.0, st["done"]),
            tie=jnp.where(tie_now, 1.0, st["tie"]),
            side=jnp.where(up, 1.0, jnp.where(dn, -1.0, st["side"])),
        )

    st = lax.while_loop(cond, step, state0)
    thr = st["thr"]
    tie = st["tie"]
    any_tie = jnp.max(tie) > 0.5

    def emit(kt, sel):
        mask_ref[0, kt] = jnp.where(sel, 0.0, NEG_BIG).astype(mask_ref.dtype).T

    @pl.when(jnp.logical_not(any_tie))
    def _():
        def body(kt, carry):
            emit(kt, sc_ref[kt] >= thr)
            return carry
        lax.fori_loop(0, nkt, body, 0)

    @pl.when(any_tie)
    def _():
        need = jnp.where(tie > 0.5, kf - count(thr, True), float(2 * ntile * TKI))
        r = lax.broadcasted_iota(jnp.int32, (TKI, TKI), 0)
        c = lax.broadcasted_iota(jnp.int32, (TKI, TKI), 1)
        prefix = jnp.where(c <= r, 1.0, 0.0).astype(BF16)

        def body(kt, seen):
            s = sc_ref[kt]
            eq = s == thr
            rank = seen + _dot(prefix, jnp.where(eq, 1.0, 0.0).astype(BF16))
            emit(kt, jnp.logical_or(s > thr, jnp.logical_and(eq, rank <= need)))
            return seen + jnp.sum(jnp.where(eq, 1.0, 0.0), axis=0, keepdims=True)
        lax.fori_loop(0, nkt, body, jnp.zeros((1, TQI), F32))

    def fill(kt, carry):
        mask_ref[0, kt] = jnp.full((TQI, TKI), NEG_BIG, mask_ref.dtype)
        return carry
    lax.fori_loop(nkt, ntile, fill, 0)


def _index_mask(qi, wt, ki2, top_k):
    S = qi.shape[0]
    nq, nk = S // TQI, S // TKI
    return pl.pallas_call(
        functools.partial(_index_kernel, top_k=top_k),
        grid=(nq,),
        in_specs=[pl.BlockSpec((TQI, IDX_HEADS * IDX_DIM), lambda i: (i, 0)),
                  pl.BlockSpec((IDX_HEADS, TQI), lambda i: (0, i)),
                  pl.BlockSpec((S, LANES), lambda i: (0, 0))],
        out_specs=pl.BlockSpec((1, nk, TQI, TKI), lambda i: (i, 0, 0, 0)),
        out_shape=jax.ShapeDtypeStruct((nq, nk, TQI, TKI), BF16),
        scratch_shapes=[pltpu.VMEM((nk, TKI, TQI), F32),
                        pltpu.VMEM((IDX_HEADS, TQI, LANES), BF16)],
        compiler_params=_params(("parallel",)),
        name="index_mask",
    )(qi, wt, ki2)


def _attn_kernel(qt_ref, kt_ref, q_ref, k_ref, v_ref, m_ref, bd_ref, be_ref, o_ref, *state):
    m_sc, acc_sc = state[:N_HEADS], state[N_HEADS:]
    step = pl.program_id(0)
    qi = qt_ref[step]
    ki = kt_ref[step]
    nsub = TA // SUB

    @pl.when(ki == 0)
    def _():
        for h in range(N_HEADS):
            m_sc[h][...] = jnp.full((TA, LANES), NEG_BIG, F32)
            acc_sc[h][...] = jnp.zeros((TA, LANES), F32)

    lane = lax.broadcasted_iota(jnp.int32, (TA, LANES), 1)
    first = lane < HEAD_DIM

    def bias_tile(h, diagonal):
        zero = jnp.zeros((SUB, SUB), F32)
        rows = []
        for a in range(nsub):
            if diagonal:
                blks = [bd_ref[h] if b == a else (be_ref[h] if b == a - 1 else zero) for b in range(nsub)]
            else:
                blks = [be_ref[h] if (a == 0 and b == nsub - 1) else zero for b in range(nsub)]
            rows.append(jnp.concatenate(blks, axis=1))
        return jnp.concatenate(rows, axis=0)

    def update(near, diagonal):
        maskf = jnp.concatenate([m_ref[a, 0] for a in range(TA // TQI)], axis=0).astype(F32)

        def logits(h):
            cols = slice((h // 2) * LANES, (h // 2 + 1) * LANES)
            qp = q_ref[:, cols]
            mine = first if h % 2 == 0 else jnp.logical_not(first)
            s = _dot_nt(jnp.where(mine, qp, jnp.zeros_like(qp)), k_ref[:, cols]) + maskf
            return s + bias_tile(h, diagonal) if near else s

        s = logits(0)
        for h in range(N_HEADS):
            s_next = logits(h + 1) if h + 1 < N_HEADS else None
            vp = v_ref[:, (h // 2) * LANES:(h // 2 + 1) * LANES]
            mine = first if h % 2 == 0 else jnp.logical_not(first)
            vh = jnp.where(mine, vp, jnp.ones_like(vp))
            m_prev = m_sc[h][...]
            m_next = jnp.maximum(m_prev, jnp.max(s, axis=1, keepdims=True))
            pexp = jnp.exp2(s - jnp.concatenate([m_next] * (TA // LANES), axis=1)).astype(BF16)
            acc_sc[h][...] = jnp.exp2(m_prev - m_next) * acc_sc[h][...] + _dot(pexp, vh)
            m_sc[h][...] = m_next
            s = s_next

    @pl.when(ki < qi - 1)
    def _():
        update(False, False)

    @pl.when(ki == qi - 1)
    def _():
        update(True, False)

    @pl.when(ki == qi)
    def _():
        update(True, True)
        for p in range(N_HEADS // 2):
            a0, a1 = acc_sc[2 * p][...], acc_sc[2 * p + 1][...]
            d0 = pltpu.roll(a0, HEAD_DIM, axis=1)
            d1 = pltpu.roll(a1, HEAD_DIM, axis=1)
            o_ref[:, p * LANES:(p + 1) * LANES] = jnp.where(first, a0 / d0, a1 / d1).astype(o_ref.dtype)


def _attention(q, k, v, mask4, bias_d, bias_e):
    S = q.shape[0]
    nb = S // TA
    pairs = [(a, b) for a in range(nb) for b in range(a + 1)]
    qtab = jnp.asarray(np.array([a for a, _ in pairs], np.int32))
    ktab = jnp.asarray(np.array([b for _, b in pairs], np.int32))
    grid_spec = pltpu.PrefetchScalarGridSpec(
        num_scalar_prefetch=2,
        grid=(len(pairs),),
        in_specs=[pl.BlockSpec((TA, ATTN_WIDTH), lambda s, qt, kt: (qt[s], 0)),
                  pl.BlockSpec((TA, ATTN_WIDTH), lambda s, qt, kt: (kt[s], 0)),
                  pl.BlockSpec((TA, ATTN_WIDTH), lambda s, qt, kt: (kt[s], 0)),
                  pl.BlockSpec((TA // TQI, TA // TKI, TQI, TKI), lambda s, qt, kt: (qt[s], kt[s], 0, 0)),
                  pl.BlockSpec(bias_d.shape, lambda s, qt, kt: (0, 0, 0)),
                  pl.BlockSpec(bias_e.shape, lambda s, qt, kt: (0, 0, 0))],
        out_specs=pl.BlockSpec((TA, ATTN_WIDTH), lambda s, qt, kt: (qt[s], 0)),
        scratch_shapes=[pltpu.VMEM((TA, LANES), F32)] * (2 * N_HEADS),
    )
    return pl.pallas_call(
        _attn_kernel,
        grid_spec=grid_spec,
        out_shape=jax.ShapeDtypeStruct((S, ATTN_WIDTH), BF16),
        compiler_params=_params(("arbitrary",)),
        name="attn",
    )(qtab, ktab, q, k, v, mask4, bias_d, bias_e)


def _relative_bias_blocks(rel_bias):
    dist = jnp.arange(2 * SUB, dtype=jnp.int32)
    max_exact = N_BUCKETS // 2
    dist_f = jnp.maximum(dist, 1).astype(F32)
    large = max_exact + (jnp.log(dist_f / max_exact) / math.log(MAX_DISTANCE / max_exact)
                         * (N_BUCKETS - max_exact)).astype(jnp.int32)
    bucket = jnp.where(dist < max_exact, dist, jnp.minimum(large, N_BUCKETS - 1))
    table = ((rel_bias[bucket] - rel_bias[N_BUCKETS - 1]) * LOG2E).astype(F32)
    i = jnp.arange(SUB)[:, None]
    j = jnp.arange(SUB)[None, :]
    diag = jnp.transpose(table[jnp.clip(i - j, 0, 2 * SUB - 1)], (2, 0, 1))
    sub = jnp.transpose(table[SUB + i - j], (2, 0, 1))
    return diag, sub


def _pack_bf16_pair(a, b):
    def rnd(x):
        bits = lax.bitcast_convert_type(x, jnp.uint32)
        return bits + jnp.uint32(0x7FFF) + ((bits >> 16) & jnp.uint32(1))
    return (rnd(a) >> 16) | (rnd(b) & jnp.uint32(0xFFFF0000))


def _unpack_bf16_pair(p):
    lo = lax.bitcast_convert_type(p << 16, F32)
    hi = lax.bitcast_convert_type(p & jnp.uint32(0xFFFF0000), F32)
    return lo.astype(BF16), hi.astype(BF16)


def _post_kernel(ya_ref, sga_ref, pc_ref, x_ref, woa_ref, wout_ref, g_ref, b_ref, wr_ref, br_ref,
                 x1_ref, x1p_ref, ridx_ref, rgate_ref):
    y_attn = _dot(ya_ref[...], woa_ref[...])
    merged = sga_ref[...] * y_attn + pc_ref[...]
    mix = _dot(merged.astype(BF16), wout_ref[...])
    x1 = _layer_norm(DEEPNORM_ALPHA * x_ref[...] + mix, g_ref[...], b_ref[...])
    x1_ref[...] = x1
    half = D_MODEL // 2
    x1p_ref[:, 0, :] = _pack_bf16_pair(x1[:, :half], x1[:, half:])

    logits = jnp.dot(x1, wr_ref[...], preferred_element_type=F32, precision=lax.Precision.HIGHEST) + br_ref[...]
    lane = lax.broadcasted_iota(jnp.int32, logits.shape, 1).astype(F32)
    cur = logits
    vals, idxs = [], []
    for _ in range(TOP_K_EXPERTS):
        m = jnp.max(cur, axis=1, keepdims=True)
        ix = jnp.min(jnp.where(cur == m, lane, float(LANES)), axis=1, keepdims=True)
        vals.append(m)
        idxs.append(ix)
        cur = jnp.where(lane == ix, -jnp.inf, cur)
    exps = [jnp.exp(v - vals[0]) for v in vals]
    denom = exps[0]
    for e in exps[1:]:
        denom = denom + e
    ridx = jnp.zeros_like(logits)
    rgate = jnp.zeros_like(logits)
    for j in range(TOP_K_EXPERTS):
        ridx = jnp.where(lane == float(j), idxs[j], ridx)
        rgate = jnp.where(lane == float(j), exps[j] / denom, rgate)
    ridx_ref[...] = ridx.astype(jnp.int32)
    rgate_ref[...] = rgate


def _post(ya, sg, pc, x, woa, wout, g, b, wr, br):
    S = x.shape[0]
    full = lambda a: pl.BlockSpec(a.shape, lambda i: (0, 0))
    row = lambda n: pl.BlockSpec((TM, n), lambda i: (i, 0))
    return pl.pallas_call(
        _post_kernel,
        grid=(S // TM,),
        in_specs=[row(ATTN_WIDTH), row(D_MODEL), row(D_MODEL), row(D_MODEL)] + [full(a) for a in (woa, wout, g, b, wr, br)],
        out_specs=[row(D_MODEL), pl.BlockSpec((TM, 1, D_MODEL // 2), lambda i: (i, 0, 0)), row(LANES), row(LANES)],
        out_shape=[jax.ShapeDtypeStruct((S, D_MODEL), F32), jax.ShapeDtypeStruct((S, 1, D_MODEL // 2), jnp.uint32),
                   jax.ShapeDtypeStruct((S, LANES), jnp.int32), jax.ShapeDtypeStruct((S, LANES), F32)],
        compiler_params=_params(("parallel",)),
        name="post",
    )(ya, sg, pc, x, woa, wout, g, b, wr, br)


def _expert_kernel(te_ref, nv_ref, nused_ref, tok_ref, tokn_ref, dst_ref, x_hbm, wgu_ref, bgu_ref, wd_ref, bd_ref,
                   ys_hbm, xs_buf, y_buf, wgu_sc, wd_sc, gsem, ssem):
    t = pl.program_id(0)
    nused = nused_ref[0]
    cur = t % 2
    nxt = 1 - cur

    def start_gather(idx_ref, slot):
        def body(r, carry):
            pltpu.make_async_copy(x_hbm.at[idx_ref[0, 0, r]], xs_buf.at[slot, pl.ds(r, 1)], gsem.at[slot]).start()
            return carry
        lax.fori_loop(0, TME, body, 0, unroll=DMA_UNROLL)

    def wait_gather(slot):
        pltpu.make_async_copy(xs_buf.at[slot], xs_buf.at[slot], gsem.at[slot]).wait()

    def wait_scatter(slot, n):
        pltpu.make_async_copy(ys_hbm.at[pl.ds(0, n)], ys_hbm.at[pl.ds(0, n)], ssem.at[slot]).wait()

    @pl.when(t == 0)
    def _():
        start_gather(tok_ref, 0)

    @pl.when(t + 1 < nused)
    def _():
        start_gather(tokn_ref, nxt)

    e = te_ref[t]
    e_prev = te_ref[jnp.maximum(t - 1, 0)]

    @pl.when(jnp.logical_or(t == 0, e != e_prev))
    def _():
        wgu_sc[...] = wgu_ref[0].astype(BF16)
        wd_sc[...] = wd_ref[0].astype(BF16)

    @pl.when(t < nused)
    def _():
        wait_gather(cur)

        @pl.when(t >= 2)
        def _():
            wait_scatter(cur, nv_ref[jnp.maximum(t - 2, 0)])

        half = D_MODEL // 2
        lo, hi = _unpack_bf16_pair(xs_buf[cur])
        gu = _dot(lo, wgu_sc[0:half, :]) + _dot(hi, wgu_sc[half:, :]) + bgu_ref[0]
        g = jnp.minimum(gu[:, :D_EXPERT], SWIGLU_LIMIT)
        u = jnp.clip(gu[:, D_EXPERT:], -SWIGLU_LIMIT, SWIGLU_LIMIT)
        act = (u + 1.0) * (g * _sigmoid(SWIGLU_ALPHA * g))
        y_buf[cur] = _dot(act.astype(BF16), wd_sc[...]) + bd_ref[0]

        def scatter(r, carry):
            pltpu.make_async_copy(y_buf.at[cur, pl.ds(r, 1)], ys_hbm.at[dst_ref[0, 0, r]], ssem.at[cur]).start()
            return carry

        def scatter_group(c, carry):
            for j in range(DMA_UNROLL):
                scatter(c * DMA_UNROLL + j, carry)
            return carry
        n_groups = nv_ref[t] // DMA_UNROLL
        lax.fori_loop(0, n_groups, scatter_group, 0)
        lax.fori_loop(n_groups * DMA_UNROLL, nv_ref[t], scatter, 0)

        @pl.when(t == nused - 1)
        def _():
            @pl.when(t >= 1)
            def _():
                wait_scatter(nxt, nv_ref[jnp.maximum(t - 1, 0)])
            wait_scatter(cur, nv_ref[t])


def _experts(tile_e, tile_nv, nused, row_token, row_dst, x1p, wgu, bgu, wd, bd, n_tiles, n_dst):
    idx = lambda a: a.reshape(n_tiles, 1, TME)
    smem_tile = lambda f: pl.BlockSpec((1, 1, TME), f, memory_space=pltpu.SMEM)
    grid_spec = pltpu.PrefetchScalarGridSpec(
        num_scalar_prefetch=3,
        grid=(n_tiles,),
        in_specs=[smem_tile(lambda t, te, nv, nu: (t, 0, 0)),
                  smem_tile(lambda t, te, nv, nu: (jnp.minimum(t + 1, n_tiles - 1), 0, 0)),
                  smem_tile(lambda t, te, nv, nu: (t, 0, 0)),
                  pl.BlockSpec(memory_space=pl.ANY),
                  pl.BlockSpec((1, D_MODEL, 2 * D_EXPERT), lambda t, te, nv, nu: (te[t], 0, 0)),
                  pl.BlockSpec((1, 1, 2 * D_EXPERT), lambda t, te, nv, nu: (te[t], 0, 0)),
                  pl.BlockSpec((1, D_EXPERT, D_MODEL), lambda t, te, nv, nu: (te[t], 0, 0)),
                  pl.BlockSpec((1, 1, D_MODEL), lambda t, te, nv, nu: (te[t], 0, 0))],
        out_specs=pl.BlockSpec(memory_space=pl.ANY),
        scratch_shapes=[pltpu.VMEM((2, TME, D_MODEL // 2), jnp.uint32), pltpu.VMEM((2, TME, D_MODEL), F32),
                        pltpu.VMEM((D_MODEL, 2 * D_EXPERT), BF16), pltpu.VMEM((D_EXPERT, D_MODEL), BF16),
                        pltpu.SemaphoreType.DMA((2,)), pltpu.SemaphoreType.DMA((2,))],
    )
    return pl.pallas_call(
        _expert_kernel,
        grid_spec=grid_spec,
        out_shape=jax.ShapeDtypeStruct((n_dst, 1, D_MODEL), F32),
        compiler_params=_params(("arbitrary",)),
        name="moe_experts",
    )(tile_e, tile_nv, nused, idx(row_token), idx(row_token), idx(row_dst), x1p, wgu, bgu, wd, bd)


def _routing_tables(ridx, S):
    n_flat = S * TOP_K_EXPERTS
    n_tiles = n_flat // TME + N_EXPERTS
    flat_e = ridx.reshape(n_flat)
    order = jnp.argsort(flat_e, stable=True).astype(jnp.int32)
    counts = jnp.sum(flat_e[:, None] == jnp.arange(N_EXPERTS, dtype=jnp.int32)[None, :], axis=0, dtype=jnp.int32)
    padded = ((counts + TME - 1) // TME) * TME
    pad_end = jnp.cumsum(padded)
    pad_start = pad_end - padded
    grp_start = jnp.cumsum(counts) - counts
    nused = (pad_end[-1] // TME).astype(jnp.int32).reshape(1)
    tile_row0 = jnp.arange(n_tiles, dtype=jnp.int32) * TME
    tile_e = jnp.minimum(jnp.sum(tile_row0[:, None] >= pad_end[None, :], axis=1), N_EXPERTS - 1).astype(jnp.int32)
    tile_rank0 = tile_row0 - pad_start[tile_e]
    tile_nv = jnp.where(tile_row0 < pad_end[-1], jnp.clip(counts[tile_e] - tile_rank0, 0, TME), 0).astype(jnp.int32)
    within = jnp.arange(TME, dtype=jnp.int32)[None, :]
    valid = within < tile_nv[:, None]
    src_flat = lax.optimization_barrier(order[jnp.clip((grp_start[tile_e] + tile_rank0)[:, None] + within, 0, n_flat - 1)])
    row_token = jnp.where(valid, src_flat // TOP_K_EXPERTS, 0).astype(jnp.int32)
    row_dst = jnp.where(valid, (src_flat % TOP_K_EXPERTS) * S + src_flat // TOP_K_EXPERTS, 0).astype(jnp.int32)
    return tile_e, tile_nv, nused, row_token, row_dst, n_tiles


def _final_kernel(x1_ref, y0_ref, y1_ref, y2_ref, y3_ref, rg_ref, p_ref, wpg_ref, wpp_ref, g_ref, b_ref, o_ref):
    h = DEEPNORM_ALPHA * x1_ref[...]
    rg = rg_ref[...]
    for j, y_ref in enumerate((y0_ref, y1_ref, y2_ref, y3_ref)):
        h = h + rg[:, j:j + 1] * y_ref[:, 0, :]
    ple = _sigmoid(_dot(h.astype(BF16), wpg_ref[...])) * _dot(p_ref[...].astype(BF16), wpp_ref[...])
    o_ref[...] = _layer_norm(h + ple, g_ref[...], b_ref[...])


def _final(x1, ys, rgate, p, wpg, wpp, g, b):
    S = x1.shape[0]
    full = lambda a: pl.BlockSpec(a.shape, lambda i: (0, 0))
    row = lambda n: pl.BlockSpec((TM, n), lambda i: (i, 0))
    return pl.pallas_call(
        _final_kernel,
        grid=(S // TM,),
        in_specs=[row(D_MODEL)]
        + [pl.BlockSpec((TM, 1, D_MODEL), lambda i, j=j: (j * (S // TM) + i, 0, 0)) for j in range(TOP_K_EXPERTS)]
        + [row(LANES), row(PLE_DIM), full(wpg), full(wpp), full(g), full(b)],
        out_specs=row(D_MODEL),
        out_shape=jax.ShapeDtypeStruct((S, D_MODEL), F32),
        compiler_params=_params(("parallel",)),
        name="final",
    )(x1, ys, ys, ys, ys, rgate, p, wpg, wpp, g, b)


def _layer(x, p, w_in, b_in, w_o_attn, w_dw, b_dw, conv_ln_g, conv_ln_b, w_o_conv, w_out, ln1_g, ln1_b,
           w_router, b_router, w_gate_up, b_gate_up, w_down, b_down, w_ple_gate, w_ple_proj, ln2_g, ln2_b,
           rel_bias):
    S = x.shape[0]
    assert S % TM == 0 and S % TA == 0 and S % TKI == 0 and (S * TOP_K_EXPERTS) % TME == 0
    top_k = min(TOPK_MAX, S // 4)
    row2 = lambda a: a.reshape(1, -1).astype(F32)

    o_q, o_qi, o_ki, o_cv, o_g = 0, 3 * ATTN_WIDTH, 3 * ATTN_WIDTH + IDX_HEADS * IDX_DIM, \
        3 * ATTN_WIDTH + IDX_HEADS * IDX_DIM + IDX_DIM + IDX_HEADS, \
        3 * ATTN_WIDTH + IDX_HEADS * IDX_DIM + IDX_DIM + IDX_HEADS + 2 * CONV_CH
    kw_pad = LANES - (IDX_DIM + IDX_HEADS)
    wkw = jnp.pad(w_in[:, o_ki:o_cv], ((0, 0), (0, kw_pad)))
    bkw = jnp.pad(b_in[o_ki:o_cv], (0, kw_pad))
    q, k, v, qi, kw, u, sg = _proj(
        x, w_in[:, o_q:o_qi].astype(BF16), row2(b_in[o_q:o_qi]),
        w_in[:, o_qi:o_ki].astype(BF16), row2(b_in[o_qi:o_ki]),
        wkw.astype(BF16), row2(bkw),
        w_in[:, o_cv:o_g].astype(BF16), row2(b_in[o_cv:o_g]),
        w_in[:, o_g:].astype(BF16), row2(b_in[o_g:]))

    part_conv = _conv(u, sg, w_dw, row2(b_dw), row2(conv_ln_g), row2(conv_ln_b), w_o_conv.astype(BF16))

    ki = kw[:, :IDX_DIM].astype(BF16)
    mask4 = _index_mask(qi, kw[:, IDX_DIM:IDX_DIM + IDX_HEADS].T, jnp.concatenate([ki, ki], axis=1), top_k)
    bias_d, bias_e = _relative_bias_blocks(rel_bias)
    y_attn = _attention(q, k, v, mask4, bias_d, bias_e)

    wr = jnp.pad(w_router, ((0, 0), (0, LANES - N_EXPERTS)))
    br = jnp.pad(b_router, (0, LANES - N_EXPERTS), constant_values=-jnp.inf)
    x1, x1p, ridx, rgate = _post(y_attn, sg, part_conv, x, w_o_attn.astype(BF16), w_out.astype(BF16),
                                 row2(ln1_g), row2(ln1_b), wr, row2(br))

    tile_e, tile_nv, nused, row_token, row_dst, n_tiles = _routing_tables(ridx[:, :TOP_K_EXPERTS], S)
    ys = _experts(tile_e, tile_nv, nused, row_token, row_dst, x1p, w_gate_up, b_gate_up.reshape(N_EXPERTS, 1, -1),
                  w_down, b_down.reshape(N_EXPERTS, 1, -1), n_tiles, S * TOP_K_EXPERTS)
    return _final(x1, ys, rgate, p, w_ple_gate.astype(BF16), w_ple_proj.astype(BF16), row2(ln2_g), row2(ln2_b))


def kernel(x, p, w_in, b_in, w_o_attn, w_dw, b_dw, conv_ln_g, conv_ln_b, w_o_conv, w_out, ln1_g, ln1_b, w_router, b_router, w_gate_up, b_gate_up, w_down, b_down, w_ple_gate, w_ple_proj, ln2_g, ln2_b, rel_bias):
    assert x.shape[0] == 1 and p.shape[0] == DEPTH
    out = _layer(x[0], p[0, 0], w_in[0], b_in[0], w_o_attn[0], w_dw[0], b_dw[0], conv_ln_g[0], conv_ln_b[0],
                 w_o_conv[0], w_out[0], ln1_g[0], ln1_b[0], w_router[0], b_router[0], w_gate_up[0], b_gate_up[0],
                 w_down[0], b_down[0], w_ple_gate[0], w_ple_proj[0], ln2_g[0], ln2_b[0], rel_bias)
    return out[None]
```

```python
import functools
import math

import jax
import jax.numpy as jnp
import numpy as np
from jax import lax
from jax.experimental import pallas as pl
from jax.experimental.pallas import tpu as pltpu

F32 = jnp.float32
BF16 = jnp.bfloat16

D_MODEL = 1024
N_HEADS = 8
HEAD_DIM = 64
ATTN_WIDTH = N_HEADS * HEAD_DIM
ATTN_SCALE = HEAD_DIM ** -0.5
LOG2E = math.log2(math.e)
IDX_HEADS = 8
IDX_DIM = 64
IDX_SCALE = (IDX_HEADS ** -0.5) * (IDX_DIM ** -0.5)
TOPK_MAX = 256
CONV_CH = 512
CONV_WIDTH = 31
N_BUCKETS = 32
MAX_DISTANCE = 128
N_EXPERTS = 32
TOP_K_EXPERTS = 4
D_EXPERT = 1024
SWIGLU_LIMIT = 7.0
SWIGLU_ALPHA = 1.702
PLE_DIM = 256
LN_EPS = 1e-5
DEPTH = 1
DEEPNORM_ALPHA = (2 * DEPTH) ** 0.25

LANES = 128
NEG_BIG = -1e30
F32_TINY = float(np.finfo(np.float32).tiny)
VMEM_LIMIT = 56 * 1024 * 1024

TM = 512
TQI = 256
TKI = 512
EDGE_FIRST = 9
SEARCH_CAP = 128
TA = 512
SUB = 128
TME = 256
DMA_UNROLL = 8
CONV_HALO = 32


def _params(sem, vmem=VMEM_LIMIT):
    return pltpu.CompilerParams(dimension_semantics=sem, vmem_limit_bytes=vmem)


def _sigmoid(x):
    return 1.0 / (1.0 + jnp.exp(-x))


def _layer_norm(x, g, b):
    mu = jnp.mean(x, axis=-1, keepdims=True)
    xc = x - mu
    var = jnp.mean(xc * xc, axis=-1, keepdims=True)
    return xc * lax.rsqrt(var + LN_EPS) * g + b


def _dot(a, b):
    return jnp.dot(a, b, preferred_element_type=F32)


def _dot_nt(a, b):
    return lax.dot_general(a, b, (((1,), (1,)), ((), ())), preferred_element_type=F32)


def _proj_kernel(x_ref, wqkv_ref, bqkv_ref, wqi_ref, bqi_ref, wkw_ref, bkw_ref, wcv_ref, bcv_ref,
                 wg_ref, bg_ref, q_ref, k_ref, v_ref, qi_ref, kw_ref, u_ref, sg_ref):
    xb = x_ref[...].astype(BF16)
    qkv = _dot(xb, wqkv_ref[...]) + bqkv_ref[...]
    q_ref[...] = (qkv[:, :ATTN_WIDTH] * (ATTN_SCALE * LOG2E)).astype(BF16)
    k_ref[...] = qkv[:, ATTN_WIDTH:2 * ATTN_WIDTH].astype(BF16)
    v_ref[...] = qkv[:, 2 * ATTN_WIDTH:].astype(BF16)
    qi_ref[...] = (_dot(xb, wqi_ref[...]) + bqi_ref[...]).astype(BF16)
    kw = _dot(xb, wkw_ref[...]) + bkw_ref[...]
    lane = lax.broadcasted_iota(jnp.int32, kw.shape, 1)
    kw_ref[...] = jnp.where(lane >= IDX_DIM, kw * IDX_SCALE, kw)
    cv = _dot(xb, wcv_ref[...]) + bcv_ref[...]
    u_ref[...] = cv[:, :CONV_CH] * _sigmoid(cv[:, CONV_CH:])
    sg_ref[...] = _sigmoid(_dot(xb, wg_ref[...]) + bg_ref[...])


def _proj(x, wqkv, bqkv, wqi, bqi, wkw, bkw, wcv, bcv, wg, bg):
    S = x.shape[0]
    full = lambda a: pl.BlockSpec(a.shape, lambda i: (0, 0))
    row = lambda n: pl.BlockSpec((TM, n), lambda i: (i, 0))
    outs = [(ATTN_WIDTH, BF16)] * 3 + [(IDX_HEADS * IDX_DIM, BF16), (LANES, F32), (CONV_CH, F32),
                                        (2 * D_MODEL, F32)]
    return pl.pallas_call(
        _proj_kernel,
        grid=(S // TM,),
        in_specs=[row(D_MODEL)] + [full(a) for a in (wqkv, bqkv, wqi, bqi, wkw, bkw, wcv, bcv, wg, bg)],
        out_specs=[row(n) for n, _ in outs],
        out_shape=[jax.ShapeDtypeStruct((S, n), dt) for n, dt in outs],
        compiler_params=_params(("parallel",)),
        name="proj",
    )(x, wqkv, bqkv, wqi, bqi, wkw, bkw, wcv, bcv, wg, bg)


def _conv_kernel(u_ref, wdw_ref, bdw_ref, lng_ref, lnb_ref, wo_ref, sgc_ref, o_ref, buf_ref):
    @pl.when(pl.program_id(0) == 0)
    def _():
        buf_ref[0:CONV_HALO, :] = jnp.zeros((CONV_HALO, CONV_CH), F32)

    buf_ref[CONV_HALO:CONV_HALO + TM, :] = u_ref[...]
    base = CONV_HALO - (CONV_WIDTH - 1)
    acc = jnp.zeros((TM, CONV_CH), F32) + bdw_ref[...]
    sub = 8
    for r in range(sub):
        taps = [j for j in range(CONV_WIDTH) if (base + j) % sub == r]
        span = max(base + j - r for j in taps) + TM
        shifted = buf_ref[r:r + span, :]
        for j in taps:
            off = base + j - r
            acc = acc + wdw_ref[j:j + 1, :] * shifted[off:off + TM, :]
    buf_ref[0:CONV_HALO, :] = buf_ref[TM:TM + CONV_HALO, :]
    y = _layer_norm(acc, lng_ref[...], lnb_ref[...])
    y = y * _sigmoid(y)
    o_ref[...] = sgc_ref[...] * _dot(y.astype(BF16), wo_ref[...])


def _conv(u, sg, wdw, bdw, lng, lnb, wo):
    S = u.shape[0]
    full = lambda a: pl.BlockSpec(a.shape, lambda i: (0, 0))
    return pl.pallas_call(
        _conv_kernel,
        grid=(S // TM,),
        in_specs=[pl.BlockSpec((TM, CONV_CH), lambda i: (i, 0)), full(wdw), full(bdw), full(lng), full(lnb),
                  full(wo), pl.BlockSpec((TM, D_MODEL), lambda i: (i, 1))],
        out_specs=pl.BlockSpec((TM, D_MODEL), lambda i: (i, 0)),
        out_shape=jax.ShapeDtypeStruct((S, D_MODEL), F32),
        scratch_shapes=[pltpu.VMEM((TM + CONV_HALO, CONV_CH), F32)],
        compiler_params=_params(("arbitrary",)),
        name="conv",
    )(u, wdw, bdw, lng, lnb, wo, sg)


def _float_key(f):
    b = lax.bitcast_convert_type(f, jnp.int32)
    return b ^ ((b >> 31) & jnp.int32(0x7FFFFFFF))


def _key_float(k):
    b = k ^ ((k >> 31) & jnp.int32(0x7FFFFFFF))
    return lax.bitcast_convert_type(b, F32)


def _index_kernel(qi_ref, wt_ref, ki_ref, mask_ref, sc_ref, qh_ref, *, top_k):
    i = pl.program_id(0)
    nkt = (i * TQI + TQI + TKI - 1) // TKI
    ntile = sc_ref.shape[0]
    q_g = i * TQI + lax.broadcasted_iota(jnp.int32, (1, TQI), 1)
    lane = lax.broadcasted_iota(jnp.int32, (TQI, LANES), 1)

    for h in range(IDX_HEADS):
        qp = qi_ref[:, (h // 2) * LANES:(h // 2 + 1) * LANES]
        keep = (lane < IDX_DIM) if h % 2 == 0 else (lane >= IDX_DIM)
        qh_ref[h] = jnp.where(keep, qp, jnp.zeros_like(qp))

    def score_tile(kt, causal):
        kb = ki_ref[pl.ds(pl.multiple_of(kt * TKI, TKI), TKI), :]
        acc = jnp.zeros((TKI, TQI), F32)
        for h in range(IDX_HEADS):
            acc = acc + wt_ref[h:h + 1, :] * jnp.maximum(_dot_nt(kb, qh_ref[h]), 0.0)
        if causal:
            key_g = kt * TKI + lax.broadcasted_iota(jnp.int32, (TKI, TQI), 0)
            ok = key_g <= q_g
            lo_src = jnp.where(ok, acc, jnp.inf)
            acc = jnp.where(ok, acc, -jnp.inf)
        else:
            lo_src = acc
        sc_ref[kt] = acc
        return jnp.max(acc, axis=0, keepdims=True), jnp.min(lo_src, axis=0, keepdims=True)

    def score_body(kt, carry):
        mx, mn = carry
        tmx, tmn = score_tile(kt, False)
        return jnp.maximum(mx, tmx), jnp.minimum(mn, tmn)

    mx0 = jnp.full((1, TQI), -jnp.inf, F32)
    mn0 = jnp.full((1, TQI), jnp.inf, F32)
    mx, mn = lax.fori_loop(0, nkt - 1, score_body, (mx0, mn0))
    tmx, tmn = score_tile(nkt - 1, True)
    mx = jnp.maximum(mx, tmx)
    mn = jnp.minimum(mn, tmn)
    kf = float(top_k)
    sub = 8

    def count(pivot, strict):
        def tile(kt, cnt):
            for r in range(TKI // sub):
                blk = sc_ref[kt, r * sub:(r + 1) * sub, :]
                hit = (blk > pivot) if strict else (blk >= pivot)
                cnt = cnt + jnp.where(hit, 1.0, 0.0)
            return cnt

        def pair(j, cnts):
            return tile(2 * j, cnts[0]), tile(2 * j + 1, cnts[1])

        zero = jnp.zeros((sub, TQI), F32)
        c0, c1 = lax.fori_loop(0, nkt // 2, pair, (zero, zero))
        c0 = lax.fori_loop(2 * (nkt // 2), nkt, tile, c0)
        return jnp.sum(c0 + c1, axis=0, keepdims=True)

    n_causal = (q_g + 1).astype(F32)
    all_sel = n_causal <= kf
    zeros = jnp.zeros((1, TQI), F32)
    state0 = dict(
        it=jnp.int32(0),
        lo=mn, hi=_key_float(_float_key(mx) + 1), clo=n_causal, chi=zeros,
        glo=jnp.log(jnp.maximum(n_causal, kf + 1.0) / kf), ghi=jnp.full((1, TQI), math.log(0.5 / kf), F32),
        thr=jnp.where(all_sel, NEG_BIG, 0.0).astype(F32),
        done=all_sel.astype(F32), tie=zeros, side=zeros, forced=zeros, use_forced=zeros,
    )

    def above(x):
        return jnp.where(jnp.abs(x) < F32_TINY, F32_TINY, _key_float(_float_key(x) + 1))

    def bracket_edges(lo, hi):
        def tile(kt, carry):
            a, b = carry
            for r in range(TKI // sub):
                blk = sc_ref[kt, r * sub:(r + 1) * sub, :]
                a = jnp.maximum(a, jnp.where(blk < hi, blk, -jnp.inf))
                b = jnp.minimum(b, jnp.where(blk >= lo, blk, jnp.inf))
            return a, b
        a, b = lax.fori_loop(0, nkt, tile, (jnp.full((sub, TQI), -jnp.inf, F32), jnp.full((sub, TQI), jnp.inf, F32)))
        return jnp.max(a, axis=0, keepdims=True), jnp.min(b, axis=0, keepdims=True)

    def cond(st):
        return jnp.logical_and(st["it"] < SEARCH_CAP, jnp.min(st["done"]) < 0.5)

    def count_step(st):
        it, lo, hi, glo, ghi = st["it"], st["lo"], st["hi"], st["glo"], st["ghi"]
        lo_k, hi_k = _float_key(lo), _float_key(hi)
        above_lo = above(lo)
        below_hi = jnp.where(hi == F32_TINY, 0.0, _key_float(hi_k - 1))
        probe = jnp.logical_and(st["done"] < 0.5, above_lo < hi)
        tie_now = jnp.logical_and(st["done"] < 0.5, above_lo >= hi)
        frac = jnp.where(it < 24, glo / (glo - ghi), 0.5)
        pf = lo + (hi - lo) * frac
        pf = jnp.where(it < 64, pf, _key_float((lo_k >> 1) + (hi_k >> 1) + (lo_k & hi_k & 1)))
        pf = jnp.where(it == 0, 0.0, jnp.where(it == 1, F32_TINY, pf))
        pf = jnp.where(st["use_forced"] > 0.5, st["forced"], pf)
        pf = jnp.where(probe, jnp.minimum(jnp.maximum(pf, above_lo), below_hi), lo)
        c = count(pf, False)
        hit = jnp.logical_and(probe, c == kf)
        up = jnp.logical_and(probe, c > kf)
        dn = jnp.logical_and(probe, c < kf)
        g = jnp.log(jnp.maximum(c, 0.5) / kf)
        return dict(
            it=it + 1,
            lo=jnp.where(up, pf, lo), hi=jnp.where(dn, pf, hi),
            clo=jnp.where(up, c, st["clo"]), chi=jnp.where(dn, c, st["chi"]),
            glo=jnp.where(up, g, jnp.where(jnp.logical_and(dn, st["side"] < -0.5), 0.5 * glo, glo)),
            ghi=jnp.where(dn, g, jnp.where(jnp.logical_and(up, st["side"] > 0.5), 0.5 * ghi, ghi)),
            thr=jnp.where(hit, pf, jnp.where(tie_now, lo, st["thr"])),
            done=jnp.where(jnp.logical_or(hit, tie_now), 1.0, st["done"]),
            tie=jnp.where(tie_now, 1.0, st["tie"]),
            side=jnp.where(up, 1.0, jnp.where(dn, -1.0, st["side"])),
            forced=st["forced"], use_forced=zeros,
        )

    def edge_step(st):
        lo, hi = st["lo"], st["hi"]
        probe = jnp.logical_and(st["done"] < 0.5, above(lo) < hi)
        a, b = bracket_edges(lo, hi)
        one_above = st["chi"] == kf - 1.0
        one_below = st["clo"] == kf + 1.0
        new = dict(st)
        new.update(
            it=st["it"] + 1,
            lo=jnp.where(probe, b, lo), hi=jnp.where(probe, above(a), hi),
            forced=jnp.where(one_above, a, above(b)),
            use_forced=jnp.where(jnp.logical_and(probe, jnp.logical_or(one_above, one_below)), 1.0, 0.0),
        )
        return new

    def step(st):
        it = st["it"]
        is_edge = jnp.logical_and(it >= EDGE_FIRST, (it - EDGE_FIRST) % 3 == 0)
        return lax.cond(is_edge, edge_step, count_step, st)

    st = lax.while_loop(cond, step, state0)
    thr = st["thr"]
    tie = st["tie"]
    any_tie = jnp.max(tie) > 0.5

    def emit(kt, sel):
        mask_ref[0, kt] = jnp.where(sel, 0.0, NEG_BIG).astype(mask_ref.dtype).T

    @pl.when(jnp.logical_not(any_tie))
    def _():
        def body(kt, carry):
            emit(kt, sc_ref[kt] >= thr)
            return carry
        lax.fori_loop(0, nkt, body, 0)

    @pl.when(any_tie)
    def _():
        need = jnp.where(tie > 0.5, kf - count(thr, True), float(2 * ntile * TKI))
        r = lax.broadcasted_iota(jnp.int32, (TKI, TKI), 0)
        c = lax.broadcasted_iota(jnp.int32, (TKI, TKI), 1)
        prefix = jnp.where(c <= r, 1.0, 0.0).astype(BF16)

        def body(kt, seen):
            s = sc_ref[kt]
            eq = s == thr
            rank = seen + _dot(prefix, jnp.where(eq, 1.0, 0.0).astype(BF16))
            emit(kt, jnp.logical_or(s > thr, jnp.logical_and(eq, rank <= need)))
            return seen + jnp.sum(jnp.where(eq, 1.0, 0.0), axis=0, keepdims=True)
        lax.fori_loop(0, nkt, body, jnp.zeros((1, TQI), F32))

    def fill(kt, carry):
        mask_ref[0, kt] = jnp.full((TQI, TKI), NEG_BIG, mask_ref.dtype)
        return carry
    lax.fori_loop(nkt, ntile, fill, 0)


def _index_mask(qi, wt, ki2, top_k):
    S = qi.shape[0]
    nq, nk = S // TQI, S // TKI
    return pl.pallas_call(
        functools.partial(_index_kernel, top_k=top_k),
        grid=(nq,),
        in_specs=[pl.BlockSpec((TQI, IDX_HEADS * IDX_DIM), lambda i: (i, 0)),
                  pl.BlockSpec((IDX_HEADS, TQI), lambda i: (0, i)),
                  pl.BlockSpec((S, LANES), lambda i: (0, 0))],
        out_specs=pl.BlockSpec((1, nk, TQI, TKI), lambda i: (i, 0, 0, 0)),
        out_shape=jax.ShapeDtypeStruct((nq, nk, TQI, TKI), BF16),
        scratch_shapes=[pltpu.VMEM((nk, TKI, TQI), F32),
                        pltpu.VMEM((IDX_HEADS, TQI, LANES), BF16)],
        compiler_params=_params(("parallel",)),
        name="index_mask",
    )(qi, wt, ki2)


def _attn_kernel(qt_ref, kt_ref, q_ref, k_ref, v_ref, m_ref, bd_ref, be_ref, o_ref, *state):
    m_sc, acc_sc = state[:N_HEADS], state[N_HEADS:]
    step = pl.program_id(0)
    qi = qt_ref[step]
    ki = kt_ref[step]
    nsub = TA // SUB

    @pl.when(ki == 0)
    def _():
        for h in range(N_HEADS):
            m_sc[h][...] = jnp.full((TA, LANES), NEG_BIG, F32)
            acc_sc[h][...] = jnp.zeros((TA, LANES), F32)

    lane = lax.broadcasted_iota(jnp.int32, (TA, LANES), 1)
    first = lane < HEAD_DIM

    def bias_tile(h, diagonal):
        zero = jnp.zeros((SUB, SUB), F32)
        rows = []
        for a in range(nsub):
            if diagonal:
                blks = [bd_ref[h] if b == a else (be_ref[h] if b == a - 1 else zero) for b in range(nsub)]
            else:
                blks = [be_ref[h] if (a == 0 and b == nsub - 1) else zero for b in range(nsub)]
            rows.append(jnp.concatenate(blks, axis=1))
        return jnp.concatenate(rows, axis=0)

    def update(near, diagonal):
        maskf = jnp.concatenate([m_ref[a, 0] for a in range(TA // TQI)], axis=0).astype(F32)

        def logits(h):
            cols = slice((h // 2) * LANES, (h // 2 + 1) * LANES)
            qp = q_ref[:, cols]
            mine = first if h % 2 == 0 else jnp.logical_not(first)
            s = _dot_nt(jnp.where(mine, qp, jnp.zeros_like(qp)), k_ref[:, cols]) + maskf
            return s + bias_tile(h, diagonal) if near else s

        s = logits(0)
        for h in range(N_HEADS):
            s_next = logits(h + 1) if h + 1 < N_HEADS else None
            vp = v_ref[:, (h // 2) * LANES:(h // 2 + 1) * LANES]
            mine = first if h % 2 == 0 else jnp.logical_not(first)
            vh = jnp.where(mine, vp, jnp.ones_like(vp))
            m_prev = m_sc[h][...]
            m_next = jnp.maximum(m_prev, jnp.max(s, axis=1, keepdims=True))
            pexp = jnp.exp2(s - jnp.concatenate([m_next] * (TA // LANES), axis=1)).astype(BF16)
            acc_sc[h][...] = jnp.exp2(m_prev - m_next) * acc_sc[h][...] + _dot(pexp, vh)
            m_sc[h][...] = m_next
            s = s_next

    @pl.when(ki < qi - 1)
    def _():
        update(False, False)

    @pl.when(ki == qi - 1)
    def _():
        update(True, False)

    @pl.when(ki == qi)
    def _():
        update(True, True)
        for p in range(N_HEADS // 2):
            a0, a1 = acc_sc[2 * p][...], acc_sc[2 * p + 1][...]
            d0 = pltpu.roll(a0, HEAD_DIM, axis=1)
            d1 = pltpu.roll(a1, HEAD_DIM, axis=1)
            o_ref[:, p * LANES:(p + 1) * LANES] = jnp.where(first, a0 / d0, a1 / d1).astype(o_ref.dtype)


def _attention(q, k, v, mask4, bias_d, bias_e):
    S = q.shape[0]
    nb = S // TA
    pairs = [(a, b) for a in range(nb) for b in range(a + 1)]
    qtab = jnp.asarray(np.array([a for a, _ in pairs], np.int32))
    ktab = jnp.asarray(np.array([b for _, b in pairs], np.int32))
    grid_spec = pltpu.PrefetchScalarGridSpec(
        num_scalar_prefetch=2,
        grid=(len(pairs),),
        in_specs=[pl.BlockSpec((TA, ATTN_WIDTH), lambda s, qt, kt: (qt[s], 0)),
                  pl.BlockSpec((TA, ATTN_WIDTH), lambda s, qt, kt: (kt[s], 0)),
                  pl.BlockSpec((TA, ATTN_WIDTH), lambda s, qt, kt: (kt[s], 0)),
                  pl.BlockSpec((TA // TQI, TA // TKI, TQI, TKI), lambda s, qt, kt: (qt[s], kt[s], 0, 0)),
                  pl.BlockSpec(bias_d.shape, lambda s, qt, kt: (0, 0, 0)),
                  pl.BlockSpec(bias_e.shape, lambda s, qt, kt: (0, 0, 0))],
        out_specs=pl.BlockSpec((TA, ATTN_WIDTH), lambda s, qt, kt: (qt[s], 0)),
        scratch_shapes=[pltpu.VMEM((TA, LANES), F32)] * (2 * N_HEADS),
    )
    return pl.pallas_call(
        _attn_kernel,
        grid_spec=grid_spec,
        out_shape=jax.ShapeDtypeStruct((S, ATTN_WIDTH), BF16),
        compiler_params=_params(("arbitrary",)),
        name="attn",
    )(qtab, ktab, q, k, v, mask4, bias_d, bias_e)


def _relative_bias_blocks(rel_bias):
    dist = jnp.arange(2 * SUB, dtype=jnp.int32)
    max_exact = N_BUCKETS // 2
    dist_f = jnp.maximum(dist, 1).astype(F32)
    large = max_exact + (jnp.log(dist_f / max_exact) / math.log(MAX_DISTANCE / max_exact)
                         * (N_BUCKETS - max_exact)).astype(jnp.int32)
    bucket = jnp.where(dist < max_exact, dist, jnp.minimum(large, N_BUCKETS - 1))
    table = ((rel_bias[bucket] - rel_bias[N_BUCKETS - 1]) * LOG2E).astype(F32)
    i = jnp.arange(SUB)[:, None]
    j = jnp.arange(SUB)[None, :]
    diag = jnp.transpose(table[jnp.clip(i - j, 0, 2 * SUB - 1)], (2, 0, 1))
    sub = jnp.transpose(table[SUB + i - j], (2, 0, 1))
    return diag, sub


def _pack_bf16_pair(a, b):
    def rnd(x):
        bits = lax.bitcast_convert_type(x, jnp.uint32)
        return bits + jnp.uint32(0x7FFF) + ((bits >> 16) & jnp.uint32(1))
    return (rnd(a) >> 16) | (rnd(b) & jnp.uint32(0xFFFF0000))


def _unpack_bf16_pair(p):
    lo = lax.bitcast_convert_type(p << 16, F32)
    hi = lax.bitcast_convert_type(p & jnp.uint32(0xFFFF0000), F32)
    return lo.astype(BF16), hi.astype(BF16)


def _post_kernel(ya_ref, sga_ref, pc_ref, x_ref, woa_ref, wout_ref, g_ref, b_ref, wr_ref, br_ref,
                 x1_ref, x1p_ref, ridx_ref, rgate_ref):
    y_attn = _dot(ya_ref[...], woa_ref[...])
    merged = sga_ref[...] * y_attn + pc_ref[...]
    mix = _dot(merged.astype(BF16), wout_ref[...])
    x1 = _layer_norm(DEEPNORM_ALPHA * x_ref[...] + mix, g_ref[...], b_ref[...])
    x1_ref[...] = x1
    half = D_MODEL // 2
    x1p_ref[:, 0, :] = _pack_bf16_pair(x1[:, :half], x1[:, half:])

    logits = jnp.dot(x1, wr_ref[...], preferred_element_type=F32, precision=lax.Precision.HIGHEST) + br_ref[...]
    lane = lax.broadcasted_iota(jnp.int32, logits.shape, 1).astype(F32)
    cur = logits
    vals, idxs = [], []
    for _ in range(TOP_K_EXPERTS):
        m = jnp.max(cur, axis=1, keepdims=True)
        ix = jnp.min(jnp.where(cur == m, lane, float(LANES)), axis=1, keepdims=True)
        vals.append(m)
        idxs.append(ix)
        cur = jnp.where(lane == ix, -jnp.inf, cur)
    exps = [jnp.exp(v - vals[0]) for v in vals]
    denom = exps[0]
    for e in exps[1:]:
        denom = denom + e
    ridx = jnp.zeros_like(logits)
    rgate = jnp.zeros_like(logits)
    for j in range(TOP_K_EXPERTS):
        ridx = jnp.where(lane == float(j), idxs[j], ridx)
        rgate = jnp.where(lane == float(j), exps[j] / denom, rgate)
    ridx_ref[...] = ridx.astype(jnp.int32)
    rgate_ref[...] = rgate


def _post(ya, sg, pc, x, woa, wout, g, b, wr, br):
    S = x.shape[0]
    full = lambda a: pl.BlockSpec(a.shape, lambda i: (0, 0))
    row = lambda n: pl.BlockSpec((TM, n), lambda i: (i, 0))
    return pl.pallas_call(
        _post_kernel,
        grid=(S // TM,),
        in_specs=[row(ATTN_WIDTH), row(D_MODEL), row(D_MODEL), row(D_MODEL)] + [full(a) for a in (woa, wout, g, b, wr, br)],
        out_specs=[row(D_MODEL), pl.BlockSpec((TM, 1, D_MODEL // 2), lambda i: (i, 0, 0)), row(LANES), row(LANES)],
        out_shape=[jax.ShapeDtypeStruct((S, D_MODEL), F32), jax.ShapeDtypeStruct((S, 1, D_MODEL // 2), jnp.uint32),
                   jax.ShapeDtypeStruct((S, LANES), jnp.int32), jax.ShapeDtypeStruct((S, LANES), F32)],
        compiler_params=_params(("parallel",)),
        name="post",
    )(ya, sg, pc, x, woa, wout, g, b, wr, br)


def _expert_kernel(te_ref, nv_ref, nused_ref, tok_ref, tokn_ref, dst_ref, x_hbm, wgu_ref, bgu_ref, wd_ref, bd_ref,
                   ys_hbm, xs_buf, y_buf, wgu_sc, wd_sc, gsem, ssem):
    t = pl.program_id(0)
    nused = nused_ref[0]
    cur = t % 2
    nxt = 1 - cur

    def start_gather(idx_ref, slot):
        def body(r, carry):
            pltpu.make_async_copy(x_hbm.at[idx_ref[0, 0, r]], xs_buf.at[slot, pl.ds(r, 1)], gsem.at[slot]).start()
            return carry
        lax.fori_loop(0, TME, body, 0, unroll=DMA_UNROLL)

    def wait_gather(slot):
        pltpu.make_async_copy(xs_buf.at[slot], xs_buf.at[slot], gsem.at[slot]).wait()

    def wait_scatter(slot, n):
        pltpu.make_async_copy(ys_hbm.at[pl.ds(0, n)], ys_hbm.at[pl.ds(0, n)], ssem.at[slot]).wait()

    @pl.when(t == 0)
    def _():
        start_gather(tok_ref, 0)

    @pl.when(t + 1 < nused)
    def _():
        start_gather(tokn_ref, nxt)

    e = te_ref[t]
    e_prev = te_ref[jnp.maximum(t - 1, 0)]

    @pl.when(jnp.logical_or(t == 0, e != e_prev))
    def _():
        wgu_sc[...] = wgu_ref[0].astype(BF16)
        wd_sc[...] = wd_ref[0].astype(BF16)

    @pl.when(t < nused)
    def _():
        wait_gather(cur)

        @pl.when(t >= 2)
        def _():
            wait_scatter(cur, nv_ref[jnp.maximum(t - 2, 0)])

        half = D_MODEL // 2
        lo, hi = _unpack_bf16_pair(xs_buf[cur])
        gu = _dot(lo, wgu_sc[0:half, :]) + _dot(hi, wgu_sc[half:, :]) + bgu_ref[0]
        g = jnp.minimum(gu[:, :D_EXPERT], SWIGLU_LIMIT)
        u = jnp.clip(gu[:, D_EXPERT:], -SWIGLU_LIMIT, SWIGLU_LIMIT)
        act = (u + 1.0) * (g * _sigmoid(SWIGLU_ALPHA * g))
        y_buf[cur] = _dot(act.astype(BF16), wd_sc[...]) + bd_ref[0]

        def scatter(r, carry):
            pltpu.make_async_copy(y_buf.at[cur, pl.ds(r, 1)], ys_hbm.at[dst_ref[0, 0, r]], ssem.at[cur]).start()
            return carry

        def scatter_group(c, carry):
            for j in range(DMA_UNROLL):
                scatter(c * DMA_UNROLL + j, carry)
            return carry
        n_groups = nv_ref[t] // DMA_UNROLL
        lax.fori_loop(0, n_groups, scatter_group, 0)
        lax.fori_loop(n_groups * DMA_UNROLL, nv_ref[t], scatter, 0)

        @pl.when(t == nused - 1)
        def _():
            @pl.when(t >= 1)
            def _():
                wait_scatter(nxt, nv_ref[jnp.maximum(t - 1, 0)])
            wait_scatter(cur, nv_ref[t])


def _experts(tile_e, tile_nv, nused, row_token, row_dst, x1p, wgu, bgu, wd, bd, n_tiles, n_dst):
    idx = lambda a: a.reshape(n_tiles, 1, TME)
    smem_tile = lambda f: pl.BlockSpec((1, 1, TME), f, memory_space=pltpu.SMEM)
    grid_spec = pltpu.PrefetchScalarGridSpec(
        num_scalar_prefetch=3,
        grid=(n_tiles,),
        in_specs=[smem_tile(lambda t, te, nv, nu: (t, 0, 0)),
                  smem_tile(lambda t, te, nv, nu: (jnp.minimum(t + 1, n_tiles - 1), 0, 0)),
                  smem_tile(lambda t, te, nv, nu: (t, 0, 0)),
                  pl.BlockSpec(memory_space=pl.ANY),
                  pl.BlockSpec((1, D_MODEL, 2 * D_EXPERT), lambda t, te, nv, nu: (te[t], 0, 0)),
                  pl.BlockSpec((1, 1, 2 * D_EXPERT), lambda t, te, nv, nu: (te[t], 0, 0)),
                  pl.BlockSpec((1, D_EXPERT, D_MODEL), lambda t, te, nv, nu: (te[t], 0, 0)),
                  pl.BlockSpec((1, 1, D_MODEL), lambda t, te, nv, nu: (te[t], 0, 0))],
        out_specs=pl.BlockSpec(memory_space=pl.ANY),
        scratch_shapes=[pltpu.VMEM((2, TME, D_MODEL // 2), jnp.uint32), pltpu.VMEM((2, TME, D_MODEL), F32),
                        pltpu.VMEM((D_MODEL, 2 * D_EXPERT), BF16), pltpu.VMEM((D_EXPERT, D_MODEL), BF16),
                        pltpu.SemaphoreType.DMA((2,)), pltpu.SemaphoreType.DMA((2,))],
    )
    return pl.pallas_call(
        _expert_kernel,
        grid_spec=grid_spec,
        out_shape=jax.ShapeDtypeStruct((n_dst, 1, D_MODEL), F32),
        compiler_params=_params(("arbitrary",)),
        name="moe_experts",
    )(tile_e, tile_nv, nused, idx(row_token), idx(row_token), idx(row_dst), x1p, wgu, bgu, wd, bd)


def _routing_tables(ridx, S):
    n_flat = S * TOP_K_EXPERTS
    n_tiles = n_flat // TME + N_EXPERTS
    flat_e = ridx.reshape(n_flat)
    order = jnp.argsort(flat_e, stable=True).astype(jnp.int32)
    counts = jnp.sum(flat_e[:, None] == jnp.arange(N_EXPERTS, dtype=jnp.int32)[None, :], axis=0, dtype=jnp.int32)
    padded = ((counts + TME - 1) // TME) * TME
    pad_end = jnp.cumsum(padded)
    pad_start = pad_end - padded
    grp_start = jnp.cumsum(counts) - counts
    nused = (pad_end[-1] // TME).astype(jnp.int32).reshape(1)
    tile_row0 = jnp.arange(n_tiles, dtype=jnp.int32) * TME
    tile_e = jnp.minimum(jnp.sum(tile_row0[:, None] >= pad_end[None, :], axis=1), N_EXPERTS - 1).astype(jnp.int32)
    tile_rank0 = tile_row0 - pad_start[tile_e]
    tile_nv = jnp.where(tile_row0 < pad_end[-1], jnp.clip(counts[tile_e] - tile_rank0, 0, TME), 0).astype(jnp.int32)
    within = jnp.arange(TME, dtype=jnp.int32)[None, :]
    valid = within < tile_nv[:, None]
    src_flat = lax.optimization_barrier(order[jnp.clip((grp_start[tile_e] + tile_rank0)[:, None] + within, 0, n_flat - 1)])
    row_token = jnp.where(valid, src_flat // TOP_K_EXPERTS, 0).astype(jnp.int32)
    row_dst = jnp.where(valid, (src_flat % TOP_K_EXPERTS) * S + src_flat // TOP_K_EXPERTS, 0).astype(jnp.int32)
    return tile_e, tile_nv, nused, row_token, row_dst, n_tiles


def _final_kernel(x1_ref, y0_ref, y1_ref, y2_ref, y3_ref, rg_ref, p_ref, wpg_ref, wpp_ref, g_ref, b_ref, o_ref):
    h = DEEPNORM_ALPHA * x1_ref[...]
    rg = rg_ref[...]
    for j, y_ref in enumerate((y0_ref, y1_ref, y2_ref, y3_ref)):
        h = h + rg[:, j:j + 1] * y_ref[:, 0, :]
    ple = _sigmoid(_dot(h.astype(BF16), wpg_ref[...])) * _dot(p_ref[...].astype(BF16), wpp_ref[...])
    o_ref[...] = _layer_norm(h + ple, g_ref[...], b_ref[...])


def _final(x1, ys, rgate, p, wpg, wpp, g, b):
    S = x1.shape[0]
    full = lambda a: pl.BlockSpec(a.shape, lambda i: (0, 0))
    row = lambda n: pl.BlockSpec((TM, n), lambda i: (i, 0))
    return pl.pallas_call(
        _final_kernel,
        grid=(S // TM,),
        in_specs=[row(D_MODEL)]
        + [pl.BlockSpec((TM, 1, D_MODEL), lambda i, j=j: (j * (S // TM) + i, 0, 0)) for j in range(TOP_K_EXPERTS)]
        + [row(LANES), row(PLE_DIM), full(wpg), full(wpp), full(g), full(b)],
        out_specs=row(D_MODEL),
        out_shape=jax.ShapeDtypeStruct((S, D_MODEL), F32),
        compiler_params=_params(("parallel",)),
        name="final",
    )(x1, ys, ys, ys, ys, rgate, p, wpg, wpp, g, b)


def _layer(x, p, w_in, b_in, w_o_attn, w_dw, b_dw, conv_ln_g, conv_ln_b, w_o_conv, w_out, ln1_g, ln1_b,
           w_router, b_router, w_gate_up, b_gate_up, w_down, b_down, w_ple_gate, w_ple_proj, ln2_g, ln2_b,
           rel_bias):
    S = x.shape[0]
    assert S % TM == 0 and S % TA == 0 and S % TKI == 0 and (S * TOP_K_EXPERTS) % TME == 0
    top_k = min(TOPK_MAX, S // 4)
    row2 = lambda a: a.reshape(1, -1).astype(F32)

    o_q, o_qi, o_ki, o_cv, o_g = 0, 3 * ATTN_WIDTH, 3 * ATTN_WIDTH + IDX_HEADS * IDX_DIM, \
        3 * ATTN_WIDTH + IDX_HEADS * IDX_DIM + IDX_DIM + IDX_HEADS, \
        3 * ATTN_WIDTH + IDX_HEADS * IDX_DIM + IDX_DIM + IDX_HEADS + 2 * CONV_CH
    kw_pad = LANES - (IDX_DIM + IDX_HEADS)
    wkw = jnp.pad(w_in[:, o_ki:o_cv], ((0, 0), (0, kw_pad)))
    bkw = jnp.pad(b_in[o_ki:o_cv], (0, kw_pad))
    q, k, v, qi, kw, u, sg = _proj(
        x, w_in[:, o_q:o_qi].astype(BF16), row2(b_in[o_q:o_qi]),
        w_in[:, o_qi:o_ki].astype(BF16), row2(b_in[o_qi:o_ki]),
        wkw.astype(BF16), row2(bkw),
        w_in[:, o_cv:o_g].astype(BF16), row2(b_in[o_cv:o_g]),
        w_in[:, o_g:].astype(BF16), row2(b_in[o_g:]))

    part_conv = _conv(u, sg, w_dw, row2(b_dw), row2(conv_ln_g), row2(conv_ln_b), w_o_conv.astype(BF16))

    ki = kw[:, :IDX_DIM].astype(BF16)
    mask4 = _index_mask(qi, kw[:, IDX_DIM:IDX_DIM + IDX_HEADS].T, jnp.concatenate([ki, ki], axis=1), top_k)
    bias_d, bias_e = _relative_bias_blocks(rel_bias)
    y_attn = _attention(q, k, v, mask4, bias_d, bias_e)

    wr = jnp.pad(w_router, ((0, 0), (0, LANES - N_EXPERTS)))
    br = jnp.pad(b_router, (0, LANES - N_EXPERTS), constant_values=-jnp.inf)
    x1, x1p, ridx, rgate = _post(y_attn, sg, part_conv, x, w_o_attn.astype(BF16), w_out.astype(BF16),
                                 row2(ln1_g), row2(ln1_b), wr, row2(br))

    tile_e, tile_nv, nused, row_token, row_dst, n_tiles = _routing_tables(ridx[:, :TOP_K_EXPERTS], S)
    ys = _experts(tile_e, tile_nv, nused, row_token, row_dst, x1p, w_gate_up, b_gate_up.reshape(N_EXPERTS, 1, -1),
                  w_down, b_down.reshape(N_EXPERTS, 1, -1), n_tiles, S * TOP_K_EXPERTS)
    return _final(x1, ys, rgate, p, w_ple_gate.astype(BF16), w_ple_proj.astype(BF16), row2(ln2_g), row2(ln2_b))


def kernel(x, p, w_in, b_in, w_o_attn, w_dw, b_dw, conv_ln_g, conv_ln_b, w_o_conv, w_out, ln1_g, ln1_b, w_router, b_router, w_gate_up, b_gate_up, w_down, b_down, w_ple_gate, w_ple_proj, ln2_g, ln2_b, rel_bias):
    assert x.shape[0] == 1 and p.shape[0] == DEPTH
    out = _layer(x[0], p[0, 0], w_in[0], b_in[0], w_o_attn[0], w_dw[0], b_dw[0], conv_ln_g[0], conv_ln_b[0],
                 w_o_conv[0], w_out[0], ln1_g[0], ln1_b[0], w_router[0], b_router[0], w_gate_up[0], b_gate_up[0],
                 w_down[0], b_down[0], w_ple_gate[0], w_ple_proj[0], ln2_g[0], ln2_b[0], rel_bias)
    return out[None]
```

```python
import functools
import math

import jax
import jax.numpy as jnp
import numpy as np
from jax import lax
from jax.experimental import pallas as pl
from jax.experimental.pallas import tpu as pltpu

F32 = jnp.float32
BF16 = jnp.bfloat16

D_MODEL = 1024
N_HEADS = 8
HEAD_DIM = 64
ATTN_WIDTH = N_HEADS * HEAD_DIM
ATTN_SCALE = HEAD_DIM ** -0.5
LOG2E = math.log2(math.e)
IDX_HEADS = 8
IDX_DIM = 64
IDX_SCALE = (IDX_HEADS ** -0.5) * (IDX_DIM ** -0.5)
TOPK_MAX = 256
CONV_CH = 512
CONV_WIDTH = 31
N_BUCKETS = 32
MAX_DISTANCE = 128
N_EXPERTS = 32
TOP_K_EXPERTS = 4
D_EXPERT = 1024
SWIGLU_LIMIT = 7.0
SWIGLU_ALPHA = 1.702
PLE_DIM = 256
LN_EPS = 1e-5
DEPTH = 1
DEEPNORM_ALPHA = (2 * DEPTH) ** 0.25

LANES = 128
NEG_BIG = -1e30
F32_TINY = float(np.finfo(np.float32).tiny)
VMEM_LIMIT = 56 * 1024 * 1024

TM = 512
TQI = 256
TKI = 512
EDGE_FIRST = 9
SEARCH_CAP = 128
TA = 512
SUB = 128
TME = 256
DMA_UNROLL = 8
CONV_HALO = 32


def _params(sem, vmem=VMEM_LIMIT):
    return pltpu.CompilerParams(dimension_semantics=sem, vmem_limit_bytes=vmem)


def _sigmoid(x):
    return 1.0 / (1.0 + jnp.exp(-x))


def _layer_norm(x, g, b):
    mu = jnp.mean(x, axis=-1, keepdims=True)
    xc = x - mu
    var = jnp.mean(xc * xc, axis=-1, keepdims=True)
    return xc * lax.rsqrt(var + LN_EPS) * g + b


def _dot(a, b):
    return jnp.dot(a, b, preferred_element_type=F32)


def _dot_nt(a, b):
    return lax.dot_general(a, b, (((1,), (1,)), ((), ())), preferred_element_type=F32)


def _proj_kernel(x_ref, wqkv_ref, bqkv_ref, wqi_ref, bqi_ref, wkw_ref, bkw_ref, wcv_ref, bcv_ref,
                 wg_ref, bg_ref, q_ref, k_ref, v_ref, qi_ref, kw_ref, u_ref, sg_ref):
    xb = x_ref[...].astype(BF16)
    qkv = _dot(xb, wqkv_ref[...]) + bqkv_ref[...]
    q_ref[...] = (qkv[:, :ATTN_WIDTH] * (ATTN_SCALE * LOG2E)).astype(BF16)
    k_ref[...] = qkv[:, ATTN_WIDTH:2 * ATTN_WIDTH].astype(BF16)
    v_ref[...] = qkv[:, 2 * ATTN_WIDTH:].astype(BF16)
    qi_ref[...] = (_dot(xb, wqi_ref[...]) + bqi_ref[...]).astype(BF16)
    kw = _dot(xb, wkw_ref[...]) + bkw_ref[...]
    lane = lax.broadcasted_iota(jnp.int32, kw.shape, 1)
    kw_ref[...] = jnp.where(lane >= IDX_DIM, kw * IDX_SCALE, kw)
    cv = _dot(xb, wcv_ref[...]) + bcv_ref[...]
    u_ref[...] = cv[:, :CONV_CH] * _sigmoid(cv[:, CONV_CH:])
    sg_ref[...] = _sigmoid(_dot(xb, wg_ref[...]) + bg_ref[...])


def _proj(x, wqkv, bqkv, wqi, bqi, wkw, bkw, wcv, bcv, wg, bg):
    S = x.shape[0]
    full = lambda a: pl.BlockSpec(a.shape, lambda i: (0, 0))
    row = lambda n: pl.BlockSpec((TM, n), lambda i: (i, 0))
    outs = [(ATTN_WIDTH, BF16)] * 3 + [(IDX_HEADS * IDX_DIM, BF16), (LANES, F32), (CONV_CH, F32),
                                        (2 * D_MODEL, F32)]
    return pl.pallas_call(
        _proj_kernel,
        grid=(S // TM,),
        in_specs=[row(D_MODEL)] + [full(a) for a in (wqkv, bqkv, wqi, bqi, wkw, bkw, wcv, bcv, wg, bg)],
        out_specs=[row(n) for n, _ in outs],
        out_shape=[jax.ShapeDtypeStruct((S, n), dt) for n, dt in outs],
        compiler_params=_params(("parallel",)),
        name="proj",
    )(x, wqkv, bqkv, wqi, bqi, wkw, bkw, wcv, bcv, wg, bg)


def _conv_kernel(u_ref, wdw_ref, bdw_ref, lng_ref, lnb_ref, wo_ref, sgc_ref, o_ref, buf_ref, sh_ref):
    @pl.when(pl.program_id(0) == 0)
    def _():
        buf_ref[0:CONV_HALO, :] = jnp.zeros((CONV_HALO, CONV_CH), F32)

    buf_ref[CONV_HALO:CONV_HALO + TM, :] = u_ref[...]
    base = CONV_HALO - (CONV_WIDTH - 1)
    acc = jnp.zeros((TM, CONV_CH), F32) + bdw_ref[...]
    sub = 8
    for r in range(sub):
        taps = [j for j in range(CONV_WIDTH) if (base + j) % sub == r]
        span = max(base + j - r for j in taps) + TM
        sh_ref[0:span, :] = buf_ref[r:r + span, :]
        for j in taps:
            off = base + j - r
            acc = acc + wdw_ref[j:j + 1, :] * sh_ref[off:off + TM, :]
    buf_ref[0:CONV_HALO, :] = buf_ref[TM:TM + CONV_HALO, :]
    y = _layer_norm(acc, lng_ref[...], lnb_ref[...])
    y = y * _sigmoid(y)
    o_ref[...] = sgc_ref[...] * _dot(y.astype(BF16), wo_ref[...])


def _conv(u, sg, wdw, bdw, lng, lnb, wo):
    S = u.shape[0]
    full = lambda a: pl.BlockSpec(a.shape, lambda i: (0, 0))
    return pl.pallas_call(
        _conv_kernel,
        grid=(S // TM,),
        in_specs=[pl.BlockSpec((TM, CONV_CH), lambda i: (i, 0)), full(wdw), full(bdw), full(lng), full(lnb),
                  full(wo), pl.BlockSpec((TM, D_MODEL), lambda i: (i, 1))],
        out_specs=pl.BlockSpec((TM, D_MODEL), lambda i: (i, 0)),
        out_shape=jax.ShapeDtypeStruct((S, D_MODEL), F32),
        scratch_shapes=[pltpu.VMEM((TM + CONV_HALO, CONV_CH), F32), pltpu.VMEM((TM + CONV_HALO, CONV_CH), F32)],
        compiler_params=_params(("arbitrary",)),
        name="conv",
    )(u, wdw, bdw, lng, lnb, wo, sg)


def _float_key(f):
    b = lax.bitcast_convert_type(f, jnp.int32)
    return b ^ ((b >> 31) & jnp.int32(0x7FFFFFFF))


def _key_float(k):
    b = k ^ ((k >> 31) & jnp.int32(0x7FFFFFFF))
    return lax.bitcast_convert_type(b, F32)


def _index_kernel(qi_ref, wt_ref, ki_ref, mask_ref, sc_ref, qh_ref, *, top_k):
    i = pl.program_id(0)
    nkt = (i * TQI + TQI + TKI - 1) // TKI
    ntile = sc_ref.shape[0]
    q_g = i * TQI + lax.broadcasted_iota(jnp.int32, (1, TQI), 1)
    lane = lax.broadcasted_iota(jnp.int32, (TQI, LANES), 1)

    for h in range(IDX_HEADS):
        qp = qi_ref[:, (h // 2) * LANES:(h // 2 + 1) * LANES]
        keep = (lane < IDX_DIM) if h % 2 == 0 else (lane >= IDX_DIM)
        qh_ref[h] = jnp.where(keep, qp, jnp.zeros_like(qp))

    def score_tile(kt, causal):
        kb = ki_ref[pl.ds(pl.multiple_of(kt * TKI, TKI), TKI), :]
        acc = jnp.zeros((TKI, TQI), F32)
        for h in range(IDX_HEADS):
            acc = acc + wt_ref[h:h + 1, :] * jnp.maximum(_dot_nt(kb, qh_ref[h]), 0.0)
        if causal:
            key_g = kt * TKI + lax.broadcasted_iota(jnp.int32, (TKI, TQI), 0)
            ok = key_g <= q_g
            lo_src = jnp.where(ok, acc, jnp.inf)
            acc = jnp.where(ok, acc, -jnp.inf)
        else:
            lo_src = acc
        sc_ref[kt] = acc
        return jnp.max(acc, axis=0, keepdims=True), jnp.min(lo_src, axis=0, keepdims=True)

    def score_body(kt, carry):
        mx, mn = carry
        tmx, tmn = score_tile(kt, False)
        return jnp.maximum(mx, tmx), jnp.minimum(mn, tmn)

    mx0 = jnp.full((1, TQI), -jnp.inf, F32)
    mn0 = jnp.full((1, TQI), jnp.inf, F32)
    mx, mn = lax.fori_loop(0, nkt - 1, score_body, (mx0, mn0))
    tmx, tmn = score_tile(nkt - 1, True)
    mx = jnp.maximum(mx, tmx)
    mn = jnp.minimum(mn, tmn)
    kf = float(top_k)
    sub = 8

    def count(pivot, strict):
        def tile(kt, cnt):
            for r in range(TKI // sub):
                blk = sc_ref[kt, r * sub:(r + 1) * sub, :]
                hit = (blk > pivot) if strict else (blk >= pivot)
                cnt = cnt + jnp.where(hit, 1.0, 0.0)
            return cnt

        def pair(j, cnts):
            return tile(2 * j, cnts[0]), tile(2 * j + 1, cnts[1])

        zero = jnp.zeros((sub, TQI), F32)
        c0, c1 = lax.fori_loop(0, nkt // 2, pair, (zero, zero))
        c0 = lax.fori_loop(2 * (nkt // 2), nkt, tile, c0)
        return jnp.sum(c0 + c1, axis=0, keepdims=True)

    n_causal = (q_g + 1).astype(F32)
    all_sel = n_causal <= kf
    zeros = jnp.zeros((1, TQI), F32)
    state0 = dict(
        it=jnp.int32(0),
        lo=mn, hi=_key_float(_float_key(mx) + 1), clo=n_causal, chi=zeros,
        glo=jnp.log(jnp.maximum(n_causal, kf + 1.0) / kf), ghi=jnp.full((1, TQI), math.log(0.5 / kf), F32),
        thr=jnp.where(all_sel, NEG_BIG, 0.0).astype(F32),
        done=all_sel.astype(F32), tie=zeros, side=zeros, forced=zeros, use_forced=zeros,
    )

    def above(x):
        return jnp.where(jnp.abs(x) < F32_TINY, F32_TINY, _key_float(_float_key(x) + 1))

    def bracket_edges(lo, hi):
        def tile(kt, carry):
            a, b = carry
            for r in range(TKI // sub):
                blk = sc_ref[kt, r * sub:(r + 1) * sub, :]
                a = jnp.maximum(a, jnp.where(blk < hi, blk, -jnp.inf))
                b = jnp.minimum(b, jnp.where(blk >= lo, blk, jnp.inf))
            return a, b
        a, b = lax.fori_loop(0, nkt, tile, (jnp.full((sub, TQI), -jnp.inf, F32), jnp.full((sub, TQI), jnp.inf, F32)))
        return jnp.max(a, axis=0, keepdims=True), jnp.min(b, axis=0, keepdims=True)

    def cond(st):
        return jnp.logical_and(st["it"] < SEARCH_CAP, jnp.min(st["done"]) < 0.5)

    def count_step(st):
        it, lo, hi, glo, ghi = st["it"], st["lo"], st["hi"], st["glo"], st["ghi"]
        lo_k, hi_k = _float_key(lo), _float_key(hi)
        above_lo = above(lo)
        below_hi = jnp.where(hi == F32_TINY, 0.0, _key_float(hi_k - 1))
        probe = jnp.logical_and(st["done"] < 0.5, above_lo < hi)
        tie_now = jnp.logical_and(st["done"] < 0.5, above_lo >= hi)
        frac = jnp.where(it < 24, glo / (glo - ghi), 0.5)
        pf = lo + (hi - lo) * frac
        pf = jnp.where(it < 64, pf, _key_float((lo_k >> 1) + (hi_k >> 1) + (lo_k & hi_k & 1)))
        pf = jnp.where(it == 0, 0.0, jnp.where(it == 1, F32_TINY, pf))
        pf = jnp.where(st["use_forced"] > 0.5, st["forced"], pf)
        pf = jnp.where(probe, jnp.minimum(jnp.maximum(pf, above_lo), below_hi), lo)
        c = count(pf, False)
        hit = jnp.logical_and(probe, c == kf)
        up = jnp.logical_and(probe, c > kf)
        dn = jnp.logical_and(probe, c < kf)
        g = jnp.log(jnp.maximum(c, 0.5) / kf)
        return dict(
            it=it + 1,
            lo=jnp.where(up, pf, lo), hi=jnp.where(dn, pf, hi),
            clo=jnp.where(up, c, st["clo"]), chi=jnp.where(dn, c, st["chi"]),
            glo=jnp.where(up, g, jnp.where(jnp.logical_and(dn, st["side"] < -0.5), 0.5 * glo, glo)),
            ghi=jnp.where(dn, g, jnp.where(jnp.logical_and(up, st["side"] > 0.5), 0.5 * ghi, ghi)),
            thr=jnp.where(hit, pf, jnp.where(tie_now, lo, st["thr"])),
            done=jnp.where(jnp.logical_or(hit, tie_now), 1.0, st["done"]),
            tie=jnp.where(tie_now, 1.0, st["tie"]),
            side=jnp.where(up, 1.0, jnp.where(dn, -1.0, st["side"])),
            forced=st["forced"], use_forced=zeros,
        )

    def edge_step(st):
        lo, hi = st["lo"], st["hi"]
        probe = jnp.logical_and(st["done"] < 0.5, above(lo) < hi)
        a, b = bracket_edges(lo, hi)
        one_above = st["chi"] == kf - 1.0
        one_below = st["clo"] == kf + 1.0
        new = dict(st)
        new.update(
            it=st["it"] + 1,
            lo=jnp.where(probe, b, lo), hi=jnp.where(probe, above(a), hi),
            forced=jnp.where(one_above, a, above(b)),
            use_forced=jnp.where(jnp.logical_and(probe, jnp.logical_or(one_above, one_below)), 1.0, 0.0),
        )
        return new

    def step(st):
        it = st["it"]
        is_edge = jnp.logical_and(it >= EDGE_FIRST, (it - EDGE_FIRST) % 3 == 0)
        return lax.cond(is_edge, edge_step, count_step, st)

    st = lax.while_loop(cond, step, state0)
    thr = st["thr"]
    tie = st["tie"]
    any_tie = jnp.max(tie) > 0.5

    def emit(kt, sel):
        mask_ref[0, kt] = jnp.where(sel, 0.0, NEG_BIG).astype(mask_ref.dtype).T

    @pl.when(jnp.logical_not(any_tie))
    def _():
        def body(kt, carry):
            emit(kt, sc_ref[kt] >= thr)
            return carry
        lax.fori_loop(0, nkt, body, 0)

    @pl.when(any_tie)
    def _():
        need = jnp.where(tie > 0.5, kf - count(thr, True), float(2 * ntile * TKI))
        r = lax.broadcasted_iota(jnp.int32, (TKI, TKI), 0)
        c = lax.broadcasted_iota(jnp.int32, (TKI, TKI), 1)
        prefix = jnp.where(c <= r, 1.0, 0.0).astype(BF16)

        def body(kt, seen):
            s = sc_ref[kt]
            eq = s == thr
            rank = seen + _dot(prefix, jnp.where(eq, 1.0, 0.0).astype(BF16))
            emit(kt, jnp.logical_or(s > thr, jnp.logical_and(eq, rank <= need)))
            return seen + jnp.sum(jnp.where(eq, 1.0, 0.0), axis=0, keepdims=True)
        lax.fori_loop(0, nkt, body, jnp.zeros((1, TQI), F32))

    def fill(kt, carry):
        mask_ref[0, kt] = jnp.full((TQI, TKI), NEG_BIG, mask_ref.dtype)
        return carry
    lax.fori_loop(nkt, ntile, fill, 0)


def _index_mask(qi, wt, ki2, top_k):
    S = qi.shape[0]
    nq, nk = S // TQI, S // TKI
    return pl.pallas_call(
        functools.partial(_index_kernel, top_k=top_k),
        grid=(nq,),
        in_specs=[pl.BlockSpec((TQI, IDX_HEADS * IDX_DIM), lambda i: (i, 0)),
                  pl.BlockSpec((IDX_HEADS, TQI), lambda i: (0, i)),
                  pl.BlockSpec((S, LANES), lambda i: (0, 0))],
        out_specs=pl.BlockSpec((1, nk, TQI, TKI), lambda i: (i, 0, 0, 0)),
        out_shape=jax.ShapeDtypeStruct((nq, nk, TQI, TKI), BF16),
        scratch_shapes=[pltpu.VMEM((nk, TKI, TQI), F32),
                        pltpu.VMEM((IDX_HEADS, TQI, LANES), BF16)],
        compiler_params=_params(("parallel",)),
        name="index_mask",
    )(qi, wt, ki2)


def _attn_kernel(qt_ref, kt_ref, q_ref, k_ref, v_ref, m_ref, bd_ref, be_ref, o_ref, *state):
    m_sc, acc_sc = state[:N_HEADS], state[N_HEADS:]
    step = pl.program_id(0)
    qi = qt_ref[step]
    ki = kt_ref[step]
    nsub = TA // SUB

    @pl.when(ki == 0)
    def _():
        for h in range(N_HEADS):
            m_sc[h][...] = jnp.full((TA, LANES), NEG_BIG, F32)
            acc_sc[h][...] = jnp.zeros((TA, LANES), F32)

    lane = lax.broadcasted_iota(jnp.int32, (TA, LANES), 1)
    first = lane < HEAD_DIM

    def bias_tile(h, diagonal):
        zero = jnp.zeros((SUB, SUB), F32)
        rows = []
        for a in range(nsub):
            if diagonal:
                blks = [bd_ref[h] if b == a else (be_ref[h] if b == a - 1 else zero) for b in range(nsub)]
            else:
                blks = [be_ref[h] if (a == 0 and b == nsub - 1) else zero for b in range(nsub)]
            rows.append(jnp.concatenate(blks, axis=1))
        return jnp.concatenate(rows, axis=0)

    def update(near, diagonal):
        maskf = jnp.concatenate([m_ref[a, 0] for a in range(TA // TQI)], axis=0).astype(F32)

        def logits(h):
            cols = slice((h // 2) * LANES, (h // 2 + 1) * LANES)
            qp = q_ref[:, cols]
            mine = first if h % 2 == 0 else jnp.logical_not(first)
            s = _dot_nt(jnp.where(mine, qp, jnp.zeros_like(qp)), k_ref[:, cols]) + maskf
            return s + bias_tile(h, diagonal) if near else s

        s = logits(0)
        for h in range(N_HEADS):
            s_next = logits(h + 1) if h + 1 < N_HEADS else None
            vp = v_ref[:, (h // 2) * LANES:(h // 2 + 1) * LANES]
            mine = first if h % 2 == 0 else jnp.logical_not(first)
            vh = jnp.where(mine, vp, jnp.ones_like(vp))
            m_prev = m_sc[h][...]
            m_next = jnp.maximum(m_prev, jnp.max(s, axis=1, keepdims=True))
            pexp = jnp.exp2(s - jnp.concatenate([m_next] * (TA // LANES), axis=1)).astype(BF16)
            acc_sc[h][...] = jnp.exp2(m_prev - m_next) * acc_sc[h][...] + _dot(pexp, vh)
            m_sc[h][...] = m_next
            s = s_next

    @pl.when(ki < qi - 1)
    def _():
        update(False, False)

    @pl.when(ki == qi - 1)
    def _():
        update(True, False)

    @pl.when(ki == qi)
    def _():
        update(True, True)
        for p in range(N_HEADS // 2):
            a0, a1 = acc_sc[2 * p][...], acc_sc[2 * p + 1][...]
            d0 = pltpu.roll(a0, HEAD_DIM, axis=1)
            d1 = pltpu.roll(a1, HEAD_DIM, axis=1)
            o_ref[:, p * LANES:(p + 1) * LANES] = jnp.where(first, a0 / d0, a1 / d1).astype(o_ref.dtype)


def _attention(q, k, v, mask4, bias_d, bias_e):
    S = q.shape[0]
    nb = S // TA
    pairs = [(a, b) for a in range(nb) for b in range(a + 1)]
    qtab = jnp.asarray(np.array([a for a, _ in pairs], np.int32))
    ktab = jnp.asarray(np.array([b for _, b in pairs], np.int32))
    grid_spec = pltpu.PrefetchScalarGridSpec(
        num_scalar_prefetch=2,
        grid=(len(pairs),),
        in_specs=[pl.BlockSpec((TA, ATTN_WIDTH), lambda s, qt, kt: (qt[s], 0)),
                  pl.BlockSpec((TA, ATTN_WIDTH), lambda s, qt, kt: (kt[s], 0)),
                  pl.BlockSpec((TA, ATTN_WIDTH), lambda s, qt, kt: (kt[s], 0)),
                  pl.BlockSpec((TA // TQI, TA // TKI, TQI, TKI), lambda s, qt, kt: (qt[s], kt[s], 0, 0)),
                  pl.BlockSpec(bias_d.shape, lambda s, qt, kt: (0, 0, 0)),
                  pl.BlockSpec(bias_e.shape, lambda s, qt, kt: (0, 0, 0))],
        out_specs=pl.BlockSpec((TA, ATTN_WIDTH), lambda s, qt, kt: (qt[s], 0)),
        scratch_shapes=[pltpu.VMEM((TA, LANES), F32)] * (2 * N_HEADS),
    )
    return pl.pallas_call(
        _attn_kernel,
        grid_spec=grid_spec,
        out_shape=jax.ShapeDtypeStruct((S, ATTN_WIDTH), BF16),
        compiler_params=_params(("arbitrary",)),
        name="attn",
    )(qtab, ktab, q, k, v, mask4, bias_d, bias_e)


def _relative_bias_blocks(rel_bias):
    dist = jnp.arange(2 * SUB, dtype=jnp.int32)
    max_exact = N_BUCKETS // 2
    dist_f = jnp.maximum(dist, 1).astype(F32)
    large = max_exact + (jnp.log(dist_f / max_exact) / math.log(MAX_DISTANCE / max_exact)
                         * (N_BUCKETS - max_exact)).astype(jnp.int32)
    bucket = jnp.where(dist < max_exact, dist, jnp.minimum(large, N_BUCKETS - 1))
    table = ((rel_bias[bucket] - rel_bias[N_BUCKETS - 1]) * LOG2E).astype(F32)
    i = jnp.arange(SUB)[:, None]
    j = jnp.arange(SUB)[None, :]

    def toeplitz(d):
        onehot = (d[:, :, None] == dist[None, None, :]).astype(F32)
        return jnp.einsum("ijd,dh->hij", onehot, table, precision=lax.Precision.HIGHEST)

    return toeplitz(jnp.clip(i - j, 0, 2 * SUB - 1)), toeplitz(SUB + i - j)


def _pack_bf16_pair(a, b):
    def rnd(x):
        bits = lax.bitcast_convert_type(x, jnp.uint32)
        return bits + jnp.uint32(0x7FFF) + ((bits >> 16) & jnp.uint32(1))
    return (rnd(a) >> 16) | (rnd(b) & jnp.uint32(0xFFFF0000))


def _unpack_bf16_pair(p):
    lo = lax.bitcast_convert_type(p << 16, F32)
    hi = lax.bitcast_convert_type(p & jnp.uint32(0xFFFF0000), F32)
    return lo.astype(BF16), hi.astype(BF16)


def _post_kernel(ya_ref, sga_ref, pc_ref, x_ref, woa_ref, wout_ref, g_ref, b_ref, wr_ref, br_ref,
                 x1_ref, x1p_ref, ridx_ref, rgate_ref):
    y_attn = _dot(ya_ref[...], woa_ref[...])
    merged = sga_ref[...] * y_attn + pc_ref[...]
    mix = _dot(merged.astype(BF16), wout_ref[...])
    x1 = _layer_norm(DEEPNORM_ALPHA * x_ref[...] + mix, g_ref[...], b_ref[...])
    x1_ref[...] = x1
    half = D_MODEL // 2
    x1p_ref[:, 0, :] = _pack_bf16_pair(x1[:, :half], x1[:, half:])

    logits = jnp.dot(x1, wr_ref[...], preferred_element_type=F32, precision=lax.Precision.HIGHEST) + br_ref[...]
    lane = lax.broadcasted_iota(jnp.int32, logits.shape, 1).astype(F32)
    cur = logits
    vals, idxs = [], []
    for _ in range(TOP_K_EXPERTS):
        m = jnp.max(cur, axis=1, keepdims=True)
        ix = jnp.min(jnp.where(cur == m, lane, float(LANES)), axis=1, keepdims=True)
        vals.append(m)
        idxs.append(ix)
        cur = jnp.where(lane == ix, -jnp.inf, cur)
    exps = [jnp.exp(v - vals[0]) for v in vals]
    denom = exps[0]
    for e in exps[1:]:
        denom = denom + e
    ridx = jnp.zeros_like(logits)
    rgate = jnp.zeros_like(logits)
    for j in range(TOP_K_EXPERTS):
        ridx = jnp.where(lane == float(j), idxs[j], ridx)
        rgate = jnp.where(lane == float(j), exps[j] / denom, rgate)
    ridx_ref[...] = ridx.astype(jnp.int32)
    rgate_ref[...] = rgate


def _post(ya, sg, pc, x, woa, wout, g, b, wr, br):
    S = x.shape[0]
    full = lambda a: pl.BlockSpec(a.shape, lambda i: (0, 0))
    row = lambda n: pl.BlockSpec((TM, n), lambda i: (i, 0))
    return pl.pallas_call(
        _post_kernel,
        grid=(S // TM,),
        in_specs=[row(ATTN_WIDTH), row(D_MODEL), row(D_MODEL), row(D_MODEL)] + [full(a) for a in (woa, wout, g, b, wr, br)],
        out_specs=[row(D_MODEL), pl.BlockSpec((TM, 1, D_MODEL // 2), lambda i: (i, 0, 0)), row(LANES), row(LANES)],
        out_shape=[jax.ShapeDtypeStruct((S, D_MODEL), F32), jax.ShapeDtypeStruct((S, 1, D_MODEL // 2), jnp.uint32),
                   jax.ShapeDtypeStruct((S, LANES), jnp.int32), jax.ShapeDtypeStruct((S, LANES), F32)],
        compiler_params=_params(("parallel",)),
        name="post",
    )(ya, sg, pc, x, woa, wout, g, b, wr, br)


def _expert_kernel(te_ref, nv_ref, nused_ref, tok_ref, tokn_ref, dst_ref, x_hbm, wgu_ref, bgu_ref, wd_ref, bd_ref,
                   ys_hbm, xs_buf, y_buf, wgu_sc, wd_sc, gsem, ssem):
    t = pl.program_id(0)
    nused = nused_ref[0]
    cur = t % 2
    nxt = 1 - cur

    def start_gather(idx_ref, slot):
        def body(r, carry):
            pltpu.make_async_copy(x_hbm.at[idx_ref[0, 0, r]], xs_buf.at[slot, pl.ds(r, 1)], gsem.at[slot]).start()
            return carry
        lax.fori_loop(0, TME, body, 0, unroll=DMA_UNROLL)

    def wait_gather(slot):
        pltpu.make_async_copy(xs_buf.at[slot], xs_buf.at[slot], gsem.at[slot]).wait()

    def wait_scatter(slot, n):
        pltpu.make_async_copy(ys_hbm.at[pl.ds(0, n)], ys_hbm.at[pl.ds(0, n)], ssem.at[slot]).wait()

    @pl.when(t == 0)
    def _():
        start_gather(tok_ref, 0)

    @pl.when(t + 1 < nused)
    def _():
        start_gather(tokn_ref, nxt)

    e = te_ref[t]
    e_prev = te_ref[jnp.maximum(t - 1, 0)]

    @pl.when(jnp.logical_or(t == 0, e != e_prev))
    def _():
        wgu_sc[...] = wgu_ref[0].astype(BF16)
        wd_sc[...] = wd_ref[0].astype(BF16)

    @pl.when(t < nused)
    def _():
        wait_gather(cur)

        @pl.when(t >= 2)
        def _():
            wait_scatter(cur, nv_ref[jnp.maximum(t - 2, 0)])

        half = D_MODEL // 2
        lo, hi = _unpack_bf16_pair(xs_buf[cur])
        gu = _dot(lo, wgu_sc[0:half, :]) + _dot(hi, wgu_sc[half:, :]) + bgu_ref[0]
        g = jnp.minimum(gu[:, :D_EXPERT], SWIGLU_LIMIT)
        u = jnp.clip(gu[:, D_EXPERT:], -SWIGLU_LIMIT, SWIGLU_LIMIT)
        act = (u + 1.0) * (g * _sigmoid(SWIGLU_ALPHA * g))
        y_buf[cur] = _dot(act.astype(BF16), wd_sc[...]) + bd_ref[0]

        def scatter(r, carry):
            pltpu.make_async_copy(y_buf.at[cur, pl.ds(r, 1)], ys_hbm.at[dst_ref[0, 0, r]], ssem.at[cur]).start()
            return carry

        def scatter_group(c, carry):
            for j in range(DMA_UNROLL):
                scatter(c * DMA_UNROLL + j, carry)
            return carry
        n_groups = nv_ref[t] // DMA_UNROLL
        lax.fori_loop(0, n_groups, scatter_group, 0)
        lax.fori_loop(n_groups * DMA_UNROLL, nv_ref[t], scatter, 0)

        @pl.when(t == nused - 1)
        def _():
            @pl.when(t >= 1)
            def _():
                wait_scatter(nxt, nv_ref[jnp.maximum(t - 1, 0)])
            wait_scatter(cur, nv_ref[t])


def _experts(tile_e, tile_nv, nused, row_token, row_dst, x1p, wgu, bgu, wd, bd, n_tiles, n_dst):
    idx = lambda a: a.reshape(n_tiles, 1, TME)
    smem_tile = lambda f: pl.BlockSpec((1, 1, TME), f, memory_space=pltpu.SMEM)
    grid_spec = pltpu.PrefetchScalarGridSpec(
        num_scalar_prefetch=3,
        grid=(n_tiles,),
        in_specs=[smem_tile(lambda t, te, nv, nu: (t, 0, 0)),
                  smem_tile(lambda t, te, nv, nu: (jnp.minimum(t + 1, n_tiles - 1), 0, 0)),
                  smem_tile(lambda t, te, nv, nu: (t, 0, 0)),
                  pl.BlockSpec(memory_space=pl.ANY),
                  pl.BlockSpec((1, D_MODEL, 2 * D_EXPERT), lambda t, te, nv, nu: (te[t], 0, 0)),
                  pl.BlockSpec((1, 1, 2 * D_EXPERT), lambda t, te, nv, nu: (te[t], 0, 0)),
                  pl.BlockSpec((1, D_EXPERT, D_MODEL), lambda t, te, nv, nu: (te[t], 0, 0)),
                  pl.BlockSpec((1, 1, D_MODEL), lambda t, te, nv, nu: (te[t], 0, 0))],
        out_specs=pl.BlockSpec(memory_space=pl.ANY),
        scratch_shapes=[pltpu.VMEM((2, TME, D_MODEL // 2), jnp.uint32), pltpu.VMEM((2, TME, D_MODEL), F32),
                        pltpu.VMEM((D_MODEL, 2 * D_EXPERT), BF16), pltpu.VMEM((D_EXPERT, D_MODEL), BF16),
                        pltpu.SemaphoreType.DMA((2,)), pltpu.SemaphoreType.DMA((2,))],
    )
    return pl.pallas_call(
        _expert_kernel,
        grid_spec=grid_spec,
        out_shape=jax.ShapeDtypeStruct((n_dst, 1, D_MODEL), F32),
        compiler_params=_params(("arbitrary",)),
        name="moe_experts",
    )(tile_e, tile_nv, nused, idx(row_token), idx(row_token), idx(row_dst), x1p, wgu, bgu, wd, bd)


def _routing_tables(ridx, S):
    n_flat = S * TOP_K_EXPERTS
    n_tiles = n_flat // TME + N_EXPERTS
    flat_e = ridx.reshape(n_flat)
    order = jnp.argsort(flat_e, stable=True).astype(jnp.int32)
    counts = jnp.sum(flat_e[:, None] == jnp.arange(N_EXPERTS, dtype=jnp.int32)[None, :], axis=0, dtype=jnp.int32)
    padded = ((counts + TME - 1) // TME) * TME
    pad_end = jnp.cumsum(padded)
    pad_start = pad_end - padded
    grp_start = jnp.cumsum(counts) - counts
    nused = (pad_end[-1] // TME).astype(jnp.int32).reshape(1)
    tile_row0 = jnp.arange(n_tiles, dtype=jnp.int32) * TME
    tile_e = jnp.minimum(jnp.sum(tile_row0[:, None] >= pad_end[None, :], axis=1), N_EXPERTS - 1).astype(jnp.int32)
    tile_rank0 = tile_row0 - pad_start[tile_e]
    tile_nv = jnp.where(tile_row0 < pad_end[-1], jnp.clip(counts[tile_e] - tile_rank0, 0, TME), 0).astype(jnp.int32)
    within = jnp.arange(TME, dtype=jnp.int32)[None, :]
    valid = within < tile_nv[:, None]
    src_flat = lax.optimization_barrier(order[jnp.clip((grp_start[tile_e] + tile_rank0)[:, None] + within, 0, n_flat - 1)])
    row_token = jnp.where(valid, src_flat // TOP_K_EXPERTS, 0).astype(jnp.int32)
    row_dst = jnp.where(valid, (src_flat % TOP_K_EXPERTS) * S + src_flat // TOP_K_EXPERTS, 0).astype(jnp.int32)
    return tile_e, tile_nv, nused, row_token, row_dst, n_tiles


def _final_kernel(x1_ref, y0_ref, y1_ref, y2_ref, y3_ref, rg_ref, p_ref, wpg_ref, wpp_ref, g_ref, b_ref, o_ref):
    h = DEEPNORM_ALPHA * x1_ref[...]
    rg = rg_ref[...]
    for j, y_ref in enumerate((y0_ref, y1_ref, y2_ref, y3_ref)):
        h = h + rg[:, j:j + 1] * y_ref[:, 0, :]
    ple = _sigmoid(_dot(h.astype(BF16), wpg_ref[...])) * _dot(p_ref[...].astype(BF16), wpp_ref[...])
    o_ref[...] = _layer_norm(h + ple, g_ref[...], b_ref[...])


def _final(x1, ys, rgate, p, wpg, wpp, g, b):
    S = x1.shape[0]
    full = lambda a: pl.BlockSpec(a.shape, lambda i: (0, 0))
    row = lambda n: pl.BlockSpec((TM, n), lambda i: (i, 0))
    return pl.pallas_call(
        _final_kernel,
        grid=(S // TM,),
        in_specs=[row(D_MODEL)]
        + [pl.BlockSpec((TM, 1, D_MODEL), lambda i, j=j: (j * (S // TM) + i, 0, 0)) for j in range(TOP_K_EXPERTS)]
        + [row(LANES), row(PLE_DIM), full(wpg), full(wpp), full(g), full(b)],
        out_specs=row(D_MODEL),
        out_shape=jax.ShapeDtypeStruct((S, D_MODEL), F32),
        compiler_params=_params(("parallel",)),
        name="final",
    )(x1, ys, ys, ys, ys, rgate, p, wpg, wpp, g, b)


def _layer(x, p, w_in, b_in, w_o_attn, w_dw, b_dw, conv_ln_g, conv_ln_b, w_o_conv, w_out, ln1_g, ln1_b,
           w_router, b_router, w_gate_up, b_gate_up, w_down, b_down, w_ple_gate, w_ple_proj, ln2_g, ln2_b,
           rel_bias):
    S = x.shape[0]
    assert S % TM == 0 and S % TA == 0 and S % TKI == 0 and (S * TOP_K_EXPERTS) % TME == 0
    top_k = min(TOPK_MAX, S // 4)
    row2 = lambda a: a.reshape(1, -1).astype(F32)

    o_q, o_qi, o_ki, o_cv, o_g = 0, 3 * ATTN_WIDTH, 3 * ATTN_WIDTH + IDX_HEADS * IDX_DIM, \
        3 * ATTN_WIDTH + IDX_HEADS * IDX_DIM + IDX_DIM + IDX_HEADS, \
        3 * ATTN_WIDTH + IDX_HEADS * IDX_DIM + IDX_DIM + IDX_HEADS + 2 * CONV_CH
    kw_pad = LANES - (IDX_DIM + IDX_HEADS)
    wkw = jnp.pad(w_in[:, o_ki:o_cv], ((0, 0), (0, kw_pad)))
    bkw = jnp.pad(b_in[o_ki:o_cv], (0, kw_pad))
    q, k, v, qi, kw, u, sg = _proj(
        x, w_in[:, o_q:o_qi].astype(BF16), row2(b_in[o_q:o_qi]),
        w_in[:, o_qi:o_ki].astype(BF16), row2(b_in[o_qi:o_ki]),
        wkw.astype(BF16), row2(bkw),
        w_in[:, o_cv:o_g].astype(BF16), row2(b_in[o_cv:o_g]),
        w_in[:, o_g:].astype(BF16), row2(b_in[o_g:]))

    part_conv = _conv(u, sg, w_dw, row2(b_dw), row2(conv_ln_g), row2(conv_ln_b), w_o_conv.astype(BF16))

    ki = kw[:, :IDX_DIM].astype(BF16)
    mask4 = _index_mask(qi, kw[:, IDX_DIM:IDX_DIM + IDX_HEADS].T, jnp.concatenate([ki, ki], axis=1), top_k)
    bias_d, bias_e = _relative_bias_blocks(rel_bias)
    y_attn = _attention(q, k, v, mask4, bias_d, bias_e)

    wr = jnp.pad(w_router, ((0, 0), (0, LANES - N_EXPERTS)))
    br = jnp.pad(b_router, (0, LANES - N_EXPERTS), constant_values=-jnp.inf)
    x1, x1p, ridx, rgate = _post(y_attn, sg, part_conv, x, w_o_attn.astype(BF16), w_out.astype(BF16),
                                 row2(ln1_g), row2(ln1_b), wr, row2(br))

    tile_e, tile_nv, nused, row_token, row_dst, n_tiles = _routing_tables(ridx[:, :TOP_K_EXPERTS], S)
    ys = _experts(tile_e, tile_nv, nused, row_token, row_dst, x1p, w_gate_up, b_gate_up.reshape(N_EXPERTS, 1, -1),
                  w_down, b_down.reshape(N_EXPERTS, 1, -1), n_tiles, S * TOP_K_EXPERTS)
    return _final(x1, ys, rgate, p, w_ple_gate.astype(BF16), w_ple_proj.astype(BF16), row2(ln2_g), row2(ln2_b))


def kernel(x, p, w_in, b_in, w_o_attn, w_dw, b_dw, conv_ln_g, conv_ln_b, w_o_conv, w_out, ln1_g, ln1_b, w_router, b_router, w_gate_up, b_gate_up, w_down, b_down, w_ple_gate, w_ple_proj, ln2_g, ln2_b, rel_bias):
    assert x.shape[0] == 1 and p.shape[0] == DEPTH
    out = _layer(x[0], p[0, 0], w_in[0], b_in[0], w_o_attn[0], w_dw[0], b_dw[0], conv_ln_g[0], conv_ln_b[0],
                 w_o_conv[0], w_out[0], ln1_g[0], ln1_b[0], w_router[0], b_router[0], w_gate_up[0], b_gate_up[0],
                 w_down[0], b_down[0], w_ple_gate[0], w_ple_proj[0], ln2_g[0], ln2_b[0], rel_bias)
    return out[None]
```

```python
import functools
import math

import jax
import jax.numpy as jnp
import numpy as np
from jax import lax
from jax.experimental import pallas as pl
from jax.experimental.pallas import tpu as pltpu

F32 = jnp.float32
BF16 = jnp.bfloat16

D_MODEL = 1024
N_HEADS = 8
HEAD_DIM = 64
ATTN_WIDTH = N_HEADS * HEAD_DIM
ATTN_SCALE = HEAD_DIM ** -0.5
LOG2E = math.log2(math.e)
IDX_HEADS = 8
IDX_DIM = 64
IDX_SCALE = (IDX_HEADS ** -0.5) * (IDX_DIM ** -0.5)
TOPK_MAX = 256
CONV_CH = 512
CONV_WIDTH = 31
N_BUCKETS = 32
MAX_DISTANCE = 128
N_EXPERTS = 32
TOP_K_EXPERTS = 4
D_EXPERT = 1024
SWIGLU_LIMIT = 7.0
SWIGLU_ALPHA = 1.702
PLE_DIM = 256
LN_EPS = 1e-5
DEPTH = 1
DEEPNORM_ALPHA = (2 * DEPTH) ** 0.25

LANES = 128
NEG_BIG = -1e30
F32_TINY = float(np.finfo(np.float32).tiny)
VMEM_LIMIT = 56 * 1024 * 1024

TM = 512
TQI = 256
TKI = 512
EDGE_FIRST = 9
SEARCH_CAP = 128
TA = 512
SUB = 128
TME = 256
DMA_UNROLL = 8
EXPERT_CHUNK = 256
CONV_HALO = 32


def _params(sem, vmem=VMEM_LIMIT):
    return pltpu.CompilerParams(dimension_semantics=sem, vmem_limit_bytes=vmem)


def _sigmoid(x):
    return 1.0 / (1.0 + jnp.exp(-x))


def _layer_norm(x, g, b):
    mu = jnp.mean(x, axis=-1, keepdims=True)
    xc = x - mu
    var = jnp.mean(xc * xc, axis=-1, keepdims=True)
    return xc * lax.rsqrt(var + LN_EPS) * g + b


def _dot(a, b):
    return jnp.dot(a, b, preferred_element_type=F32)


def _dot_nt(a, b):
    return lax.dot_general(a, b, (((1,), (1,)), ((), ())), preferred_element_type=F32)


def _proj_kernel(x_ref, wqkv_ref, bqkv_ref, wqi_ref, bqi_ref, wkw_ref, bkw_ref, wcv_ref, bcv_ref,
                 wg_ref, bg_ref, q_ref, k_ref, v_ref, qi_ref, kw_ref, u_ref, sg_ref):
    xb = x_ref[...].astype(BF16)
    qkv = _dot(xb, wqkv_ref[...]) + bqkv_ref[...]
    q_ref[...] = (qkv[:, :ATTN_WIDTH] * (ATTN_SCALE * LOG2E)).astype(BF16)
    k_ref[...] = qkv[:, ATTN_WIDTH:2 * ATTN_WIDTH].astype(BF16)
    v_ref[...] = qkv[:, 2 * ATTN_WIDTH:].astype(BF16)
    qi_ref[...] = (_dot(xb, wqi_ref[...]) + bqi_ref[...]).astype(BF16)
    kw = _dot(xb, wkw_ref[...]) + bkw_ref[...]
    lane = lax.broadcasted_iota(jnp.int32, kw.shape, 1)
    kw_ref[...] = jnp.where(lane >= IDX_DIM, kw * IDX_SCALE, kw)
    cv = _dot(xb, wcv_ref[...]) + bcv_ref[...]
    u_ref[...] = cv[:, :CONV_CH] * _sigmoid(cv[:, CONV_CH:])
    sg_ref[...] = _sigmoid(_dot(xb, wg_ref[...]) + bg_ref[...])


def _proj(x, wqkv, bqkv, wqi, bqi, wkw, bkw, wcv, bcv, wg, bg):
    S = x.shape[0]
    full = lambda a: pl.BlockSpec(a.shape, lambda i: (0, 0))
    row = lambda n: pl.BlockSpec((TM, n), lambda i: (i, 0))
    outs = [(ATTN_WIDTH, BF16)] * 3 + [(IDX_HEADS * IDX_DIM, BF16), (LANES, F32), (CONV_CH, F32),
                                        (2 * D_MODEL, F32)]
    return pl.pallas_call(
        _proj_kernel,
        grid=(S // TM,),
        in_specs=[row(D_MODEL)] + [full(a) for a in (wqkv, bqkv, wqi, bqi, wkw, bkw, wcv, bcv, wg, bg)],
        out_specs=[row(n) for n, _ in outs],
        out_shape=[jax.ShapeDtypeStruct((S, n), dt) for n, dt in outs],
        compiler_params=_params(("parallel",)),
        name="proj",
    )(x, wqkv, bqkv, wqi, bqi, wkw, bkw, wcv, bcv, wg, bg)


def _conv_kernel(u_ref, wdw_ref, bdw_ref, lng_ref, lnb_ref, wo_ref, sgc_ref, o_ref, buf_ref, sh_ref):
    @pl.when(pl.program_id(0) == 0)
    def _():
        buf_ref[0:CONV_HALO, :] = jnp.zeros((CONV_HALO, CONV_CH), F32)

    buf_ref[CONV_HALO:CONV_HALO + TM, :] = u_ref[...]
    base = CONV_HALO - (CONV_WIDTH - 1)
    acc = jnp.zeros((TM, CONV_CH), F32) + bdw_ref[...]
    sub = 8
    for r in range(sub):
        taps = [j for j in range(CONV_WIDTH) if (base + j) % sub == r]
        span = max(base + j - r for j in taps) + TM
        sh_ref[0:span, :] = buf_ref[r:r + span, :]
        for j in taps:
            off = base + j - r
            acc = acc + wdw_ref[j:j + 1, :] * sh_ref[off:off + TM, :]
    buf_ref[0:CONV_HALO, :] = buf_ref[TM:TM + CONV_HALO, :]
    y = _layer_norm(acc, lng_ref[...], lnb_ref[...])
    y = y * _sigmoid(y)
    o_ref[...] = sgc_ref[...] * _dot(y.astype(BF16), wo_ref[...])


def _conv(u, sg, wdw, bdw, lng, lnb, wo):
    S = u.shape[0]
    full = lambda a: pl.BlockSpec(a.shape, lambda i: (0, 0))
    return pl.pallas_call(
        _conv_kernel,
        grid=(S // TM,),
        in_specs=[pl.BlockSpec((TM, CONV_CH), lambda i: (i, 0)), full(wdw), full(bdw), full(lng), full(lnb),
                  full(wo), pl.BlockSpec((TM, D_MODEL), lambda i: (i, 1))],
        out_specs=pl.BlockSpec((TM, D_MODEL), lambda i: (i, 0)),
        out_shape=jax.ShapeDtypeStruct((S, D_MODEL), F32),
        scratch_shapes=[pltpu.VMEM((TM + CONV_HALO, CONV_CH), F32), pltpu.VMEM((TM + CONV_HALO, CONV_CH), F32)],
        compiler_params=_params(("arbitrary",)),
        name="conv",
    )(u, wdw, bdw, lng, lnb, wo, sg)


def _float_key(f):
    b = lax.bitcast_convert_type(f, jnp.int32)
    return b ^ ((b >> 31) & jnp.int32(0x7FFFFFFF))


def _key_float(k):
    b = k ^ ((k >> 31) & jnp.int32(0x7FFFFFFF))
    return lax.bitcast_convert_type(b, F32)


def _index_kernel(qi_ref, wt_ref, ki_ref, mask_ref, sc_ref, qh_ref, *, top_k):
    i = pl.program_id(0)
    nkt = (i * TQI + TQI + TKI - 1) // TKI
    ntile = sc_ref.shape[0]
    q_g = i * TQI + lax.broadcasted_iota(jnp.int32, (1, TQI), 1)
    lane = lax.broadcasted_iota(jnp.int32, (TQI, LANES), 1)

    for h in range(IDX_HEADS):
        qp = qi_ref[:, (h // 2) * LANES:(h // 2 + 1) * LANES]
        keep = (lane < IDX_DIM) if h % 2 == 0 else (lane >= IDX_DIM)
        qh_ref[h] = jnp.where(keep, qp, jnp.zeros_like(qp))

    def score_tile(kt, causal):
        kb = ki_ref[pl.ds(pl.multiple_of(kt * TKI, TKI), TKI), :]
        acc = jnp.zeros((TKI, TQI), F32)
        for h in range(IDX_HEADS):
            acc = acc + wt_ref[h:h + 1, :] * jnp.maximum(_dot_nt(kb, qh_ref[h]), 0.0)
        if causal:
            key_g = kt * TKI + lax.broadcasted_iota(jnp.int32, (TKI, TQI), 0)
            ok = key_g <= q_g
            lo_src = jnp.where(ok, acc, jnp.inf)
            acc = jnp.where(ok, acc, -jnp.inf)
        else:
            lo_src = acc
        sc_ref[kt] = acc
        return jnp.max(acc, axis=0, keepdims=True), jnp.min(lo_src, axis=0, keepdims=True)

    def score_body(kt, carry):
        mx, mn = carry
        tmx, tmn = score_tile(kt, False)
        return jnp.maximum(mx, tmx), jnp.minimum(mn, tmn)

    mx0 = jnp.full((1, TQI), -jnp.inf, F32)
    mn0 = jnp.full((1, TQI), jnp.inf, F32)
    mx, mn = lax.fori_loop(0, nkt - 1, score_body, (mx0, mn0))
    tmx, tmn = score_tile(nkt - 1, True)
    mx = jnp.maximum(mx, tmx)
    mn = jnp.minimum(mn, tmn)
    kf = float(top_k)
    sub = 8

    def count(pivot, strict):
        def tile(kt, cnt):
            for r in range(TKI // sub):
                blk = sc_ref[kt, r * sub:(r + 1) * sub, :]
                hit = (blk > pivot) if strict else (blk >= pivot)
                cnt = cnt + jnp.where(hit, 1.0, 0.0)
            return cnt

        def pair(j, cnts):
            return tile(2 * j, cnts[0]), tile(2 * j + 1, cnts[1])

        zero = jnp.zeros((sub, TQI), F32)
        c0, c1 = lax.fori_loop(0, nkt // 2, pair, (zero, zero))
        c0 = lax.fori_loop(2 * (nkt // 2), nkt, tile, c0)
        return jnp.sum(c0 + c1, axis=0, keepdims=True)

    n_causal = (q_g + 1).astype(F32)
    all_sel = n_causal <= kf
    zeros = jnp.zeros((1, TQI), F32)
    state0 = dict(
        it=jnp.int32(0),
        lo=mn, hi=_key_float(_float_key(mx) + 1), clo=n_causal, chi=zeros,
        glo=jnp.log(jnp.maximum(n_causal, kf + 1.0) / kf), ghi=jnp.full((1, TQI), math.log(0.5 / kf), F32),
        thr=jnp.where(all_sel, NEG_BIG, 0.0).astype(F32),
        done=all_sel.astype(F32), tie=zeros, side=zeros, forced=zeros, use_forced=zeros,
    )

    def above(x):
        return jnp.where(jnp.abs(x) < F32_TINY, F32_TINY, _key_float(_float_key(x) + 1))

    def bracket_edges(lo, hi):
        def tile(kt, carry):
            a, b = carry
            for r in range(TKI // sub):
                blk = sc_ref[kt, r * sub:(r + 1) * sub, :]
                a = jnp.maximum(a, jnp.where(blk < hi, blk, -jnp.inf))
                b = jnp.minimum(b, jnp.where(blk >= lo, blk, jnp.inf))
            return a, b
        a, b = lax.fori_loop(0, nkt, tile, (jnp.full((sub, TQI), -jnp.inf, F32), jnp.full((sub, TQI), jnp.inf, F32)))
        return jnp.max(a, axis=0, keepdims=True), jnp.min(b, axis=0, keepdims=True)

    def cond(st):
        return jnp.logical_and(st["it"] < SEARCH_CAP, jnp.min(st["done"]) < 0.5)

    def count_step(st):
        it, lo, hi, glo, ghi = st["it"], st["lo"], st["hi"], st["glo"], st["ghi"]
        lo_k, hi_k = _float_key(lo), _float_key(hi)
        above_lo = above(lo)
        below_hi = jnp.where(hi == F32_TINY, 0.0, _key_float(hi_k - 1))
        probe = jnp.logical_and(st["done"] < 0.5, above_lo < hi)
        tie_now = jnp.logical_and(st["done"] < 0.5, above_lo >= hi)
        frac = jnp.where(it < 24, glo / (glo - ghi), 0.5)
        pf = lo + (hi - lo) * frac
        pf = jnp.where(it < 64, pf, _key_float((lo_k >> 1) + (hi_k >> 1) + (lo_k & hi_k & 1)))
        pf = jnp.where(it == 0, 0.0, jnp.where(it == 1, F32_TINY, pf))
        pf = jnp.where(st["use_forced"] > 0.5, st["forced"], pf)
        pf = jnp.where(probe, jnp.minimum(jnp.maximum(pf, above_lo), below_hi), lo)
        c = count(pf, False)
        hit = jnp.logical_and(probe, c == kf)
        up = jnp.logical_and(probe, c > kf)
        dn = jnp.logical_and(probe, c < kf)
        g = jnp.log(jnp.maximum(c, 0.5) / kf)
        return dict(
            it=it + 1,
            lo=jnp.where(up, pf, lo), hi=jnp.where(dn, pf, hi),
            clo=jnp.where(up, c, st["clo"]), chi=jnp.where(dn, c, st["chi"]),
            glo=jnp.where(up, g, jnp.where(jnp.logical_and(dn, st["side"] < -0.5), 0.5 * glo, glo)),
            ghi=jnp.where(dn, g, jnp.where(jnp.logical_and(up, st["side"] > 0.5), 0.5 * ghi, ghi)),
            thr=jnp.where(hit, pf, jnp.where(tie_now, lo, st["thr"])),
            done=jnp.where(jnp.logical_or(hit, tie_now), 1.0, st["done"]),
            tie=jnp.where(tie_now, 1.0, st["tie"]),
            side=jnp.where(up, 1.0, jnp.where(dn, -1.0, st["side"])),
            forced=st["forced"], use_forced=zeros,
        )

    def edge_step(st):
        lo, hi = st["lo"], st["hi"]
        probe = jnp.logical_and(st["done"] < 0.5, above(lo) < hi)
        a, b = bracket_edges(lo, hi)
        one_above = st["chi"] == kf - 1.0
        one_below = st["clo"] == kf + 1.0
        new = dict(st)
        new.update(
            it=st["it"] + 1,
            lo=jnp.where(probe, b, lo), hi=jnp.where(probe, above(a), hi),
            forced=jnp.where(one_above, a, above(b)),
            use_forced=jnp.where(jnp.logical_and(probe, jnp.logical_or(one_above, one_below)), 1.0, 0.0),
        )
        return new

    def step(st):
        it = st["it"]
        is_edge = jnp.logical_and(it >= EDGE_FIRST, (it - EDGE_FIRST) % 3 == 0)
        return lax.cond(is_edge, edge_step, count_step, st)

    st = lax.while_loop(cond, step, state0)
    thr = st["thr"]
    tie = st["tie"]
    any_tie = jnp.max(tie) > 0.5

    def emit(kt, sel):
        mask_ref[0, kt] = jnp.where(sel, 0.0, NEG_BIG).astype(mask_ref.dtype).T

    @pl.when(jnp.logical_not(any_tie))
    def _():
        def body(kt, carry):
            emit(kt, sc_ref[kt] >= thr)
            return carry
        lax.fori_loop(0, nkt, body, 0)

    @pl.when(any_tie)
    def _():
        need = jnp.where(tie > 0.5, kf - count(thr, True), float(2 * ntile * TKI))
        r = lax.broadcasted_iota(jnp.int32, (TKI, TKI), 0)
        c = lax.broadcasted_iota(jnp.int32, (TKI, TKI), 1)
        prefix = jnp.where(c <= r, 1.0, 0.0).astype(BF16)

        def body(kt, seen):
            s = sc_ref[kt]
            eq = s == thr
            rank = seen + _dot(prefix, jnp.where(eq, 1.0, 0.0).astype(BF16))
            emit(kt, jnp.logical_or(s > thr, jnp.logical_and(eq, rank <= need)))
            return seen + jnp.sum(jnp.where(eq, 1.0, 0.0), axis=0, keepdims=True)
        lax.fori_loop(0, nkt, body, jnp.zeros((1, TQI), F32))

    def fill(kt, carry):
        mask_ref[0, kt] = jnp.full((TQI, TKI), NEG_BIG, mask_ref.dtype)
        return carry
    lax.fori_loop(nkt, ntile, fill, 0)


def _index_mask(qi, wt, ki2, top_k):
    S = qi.shape[0]
    nq, nk = S // TQI, S // TKI
    return pl.pallas_call(
        functools.partial(_index_kernel, top_k=top_k),
        grid=(nq,),
        in_specs=[pl.BlockSpec((TQI, IDX_HEADS * IDX_DIM), lambda i: (i, 0)),
                  pl.BlockSpec((IDX_HEADS, TQI), lambda i: (0, i)),
                  pl.BlockSpec((S, LANES), lambda i: (0, 0))],
        out_specs=pl.BlockSpec((1, nk, TQI, TKI), lambda i: (i, 0, 0, 0)),
        out_shape=jax.ShapeDtypeStruct((nq, nk, TQI, TKI), BF16),
        scratch_shapes=[pltpu.VMEM((nk, TKI, TQI), F32),
                        pltpu.VMEM((IDX_HEADS, TQI, LANES), BF16)],
        compiler_params=_params(("parallel",)),
        name="index_mask",
    )(qi, wt, ki2)


def _attn_kernel(qt_ref, kt_ref, q_ref, k_ref, v_ref, m_ref, bd_ref, be_ref, o_ref, *state):
    m_sc, acc_sc = state[:N_HEADS], state[N_HEADS:]
    step = pl.program_id(0)
    qi = qt_ref[step]
    ki = kt_ref[step]
    nsub = TA // SUB

    @pl.when(ki == 0)
    def _():
        for h in range(N_HEADS):
            m_sc[h][...] = jnp.full((TA, LANES), NEG_BIG, F32)
            acc_sc[h][...] = jnp.zeros((TA, LANES), F32)

    lane = lax.broadcasted_iota(jnp.int32, (TA, LANES), 1)
    first = lane < HEAD_DIM

    def bias_tile(h, diagonal):
        zero = jnp.zeros((SUB, SUB), F32)
        rows = []
        for a in range(nsub):
            if diagonal:
                blks = [bd_ref[h] if b == a else (be_ref[h] if b == a - 1 else zero) for b in range(nsub)]
            else:
                blks = [be_ref[h] if (a == 0 and b == nsub - 1) else zero for b in range(nsub)]
            rows.append(jnp.concatenate(blks, axis=1))
        return jnp.concatenate(rows, axis=0)

    def update(near, diagonal):
        maskf = jnp.concatenate([m_ref[a, 0] for a in range(TA // TQI)], axis=0).astype(F32)

        def logits(h):
            cols = slice((h // 2) * LANES, (h // 2 + 1) * LANES)
            qp = q_ref[:, cols]
            mine = first if h % 2 == 0 else jnp.logical_not(first)
            s = _dot_nt(jnp.where(mine, qp, jnp.zeros_like(qp)), k_ref[:, cols]) + maskf
            return s + bias_tile(h, diagonal) if near else s

        s = logits(0)
        for h in range(N_HEADS):
            s_next = logits(h + 1) if h + 1 < N_HEADS else None
            vp = v_ref[:, (h // 2) * LANES:(h // 2 + 1) * LANES]
            mine = first if h % 2 == 0 else jnp.logical_not(first)
            vh = jnp.where(mine, vp, jnp.ones_like(vp))
            m_prev = m_sc[h][...]
            m_next = jnp.maximum(m_prev, jnp.max(s, axis=1, keepdims=True))
            pexp = jnp.exp2(s - jnp.concatenate([m_next] * (TA // LANES), axis=1)).astype(BF16)
            acc_sc[h][...] = jnp.exp2(m_prev - m_next) * acc_sc[h][...] + _dot(pexp, vh)
            m_sc[h][...] = m_next
            s = s_next

    @pl.when(ki < qi - 1)
    def _():
        update(False, False)

    @pl.when(ki == qi - 1)
    def _():
        update(True, False)

    @pl.when(ki == qi)
    def _():
        update(True, True)
        for p in range(N_HEADS // 2):
            a0, a1 = acc_sc[2 * p][...], acc_sc[2 * p + 1][...]
            d0 = pltpu.roll(a0, HEAD_DIM, axis=1)
            d1 = pltpu.roll(a1, HEAD_DIM, axis=1)
            o_ref[:, p * LANES:(p + 1) * LANES] = jnp.where(first, a0 / d0, a1 / d1).astype(o_ref.dtype)


def _attention(q, k, v, mask4, bias_d, bias_e):
    S = q.shape[0]
    nb = S // TA
    pairs = [(a, b) for a in range(nb) for b in range(a + 1)]
    qtab = jnp.asarray(np.array([a for a, _ in pairs], np.int32))
    ktab = jnp.asarray(np.array([b for _, b in pairs], np.int32))
    grid_spec = pltpu.PrefetchScalarGridSpec(
        num_scalar_prefetch=2,
        grid=(len(pairs),),
        in_specs=[pl.BlockSpec((TA, ATTN_WIDTH), lambda s, qt, kt: (qt[s], 0)),
                  pl.BlockSpec((TA, ATTN_WIDTH), lambda s, qt, kt: (kt[s], 0)),
                  pl.BlockSpec((TA, ATTN_WIDTH), lambda s, qt, kt: (kt[s], 0)),
                  pl.BlockSpec((TA // TQI, TA // TKI, TQI, TKI), lambda s, qt, kt: (qt[s], kt[s], 0, 0)),
                  pl.BlockSpec(bias_d.shape, lambda s, qt, kt: (0, 0, 0)),
                  pl.BlockSpec(bias_e.shape, lambda s, qt, kt: (0, 0, 0))],
        out_specs=pl.BlockSpec((TA, ATTN_WIDTH), lambda s, qt, kt: (qt[s], 0)),
        scratch_shapes=[pltpu.VMEM((TA, LANES), F32)] * (2 * N_HEADS),
    )
    return pl.pallas_call(
        _attn_kernel,
        grid_spec=grid_spec,
        out_shape=jax.ShapeDtypeStruct((S, ATTN_WIDTH), BF16),
        compiler_params=_params(("arbitrary",)),
        name="attn",
    )(qtab, ktab, q, k, v, mask4, bias_d, bias_e)


def _relative_bias_blocks(rel_bias):
    dist = jnp.arange(2 * SUB, dtype=jnp.int32)
    max_exact = N_BUCKETS // 2
    dist_f = jnp.maximum(dist, 1).astype(F32)
    large = max_exact + (jnp.log(dist_f / max_exact) / math.log(MAX_DISTANCE / max_exact)
                         * (N_BUCKETS - max_exact)).astype(jnp.int32)
    bucket = jnp.where(dist < max_exact, dist, jnp.minimum(large, N_BUCKETS - 1))
    table = ((rel_bias[bucket] - rel_bias[N_BUCKETS - 1]) * LOG2E).astype(F32)
    i = jnp.arange(SUB)[:, None]
    j = jnp.arange(SUB)[None, :]

    def toeplitz(d):
        onehot = (d[:, :, None] == dist[None, None, :]).astype(F32)
        return jnp.einsum("ijd,dh->hij", onehot, table, precision=lax.Precision.HIGHEST)

    return toeplitz(jnp.clip(i - j, 0, 2 * SUB - 1)), toeplitz(SUB + i - j)


def _pack_bf16_pair(a, b):
    def rnd(x):
        bits = lax.bitcast_convert_type(x, jnp.uint32)
        return bits + jnp.uint32(0x7FFF) + ((bits >> 16) & jnp.uint32(1))
    return (rnd(a) >> 16) | (rnd(b) & jnp.uint32(0xFFFF0000))


def _unpack_bf16_pair(p):
    lo = lax.bitcast_convert_type(p << 16, F32)
    hi = lax.bitcast_convert_type(p & jnp.uint32(0xFFFF0000), F32)
    return lo.astype(BF16), hi.astype(BF16)


def _post_kernel(ya_ref, sga_ref, pc_ref, x_ref, woa_ref, wout_ref, g_ref, b_ref, wr_ref, br_ref,
                 x1_ref, x1p_ref, ridx_ref, rgate_ref):
    y_attn = _dot(ya_ref[...], woa_ref[...])
    merged = sga_ref[...] * y_attn + pc_ref[...]
    mix = _dot(merged.astype(BF16), wout_ref[...])
    x1 = _layer_norm(DEEPNORM_ALPHA * x_ref[...] + mix, g_ref[...], b_ref[...])
    x1_ref[...] = x1
    half = D_MODEL // 2
    x1p_ref[:, 0, :] = _pack_bf16_pair(x1[:, :half], x1[:, half:])

    logits = jnp.dot(x1, wr_ref[...], preferred_element_type=F32, precision=lax.Precision.HIGHEST) + br_ref[...]
    lane = lax.broadcasted_iota(jnp.int32, logits.shape, 1).astype(F32)
    cur = logits
    vals, idxs = [], []
    for _ in range(TOP_K_EXPERTS):
        m = jnp.max(cur, axis=1, keepdims=True)
        ix = jnp.min(jnp.where(cur == m, lane, float(LANES)), axis=1, keepdims=True)
        vals.append(m)
        idxs.append(ix)
        cur = jnp.where(lane == ix, -jnp.inf, cur)
    exps = [jnp.exp(v - vals[0]) for v in vals]
    denom = exps[0]
    for e in exps[1:]:
        denom = denom + e
    ridx = jnp.zeros_like(logits)
    rgate = jnp.zeros_like(logits)
    for j in range(TOP_K_EXPERTS):
        ridx = jnp.where(lane == float(j), idxs[j], ridx)
        rgate = jnp.where(lane == float(j), exps[j] / denom, rgate)
    ridx_ref[...] = ridx.astype(jnp.int32)
    rgate_ref[...] = rgate


def _post(ya, sg, pc, x, woa, wout, g, b, wr, br):
    S = x.shape[0]
    full = lambda a: pl.BlockSpec(a.shape, lambda i: (0, 0))
    row = lambda n: pl.BlockSpec((TM, n), lambda i: (i, 0))
    return pl.pallas_call(
        _post_kernel,
        grid=(S // TM,),
        in_specs=[row(ATTN_WIDTH), row(D_MODEL), row(D_MODEL), row(D_MODEL)] + [full(a) for a in (woa, wout, g, b, wr, br)],
        out_specs=[row(D_MODEL), pl.BlockSpec((TM, 1, D_MODEL // 2), lambda i: (i, 0, 0)), row(LANES), row(LANES)],
        out_shape=[jax.ShapeDtypeStruct((S, D_MODEL), F32), jax.ShapeDtypeStruct((S, 1, D_MODEL // 2), jnp.uint32),
                   jax.ShapeDtypeStruct((S, LANES), jnp.int32), jax.ShapeDtypeStruct((S, LANES), F32)],
        compiler_params=_params(("parallel",)),
        name="post",
    )(ya, sg, pc, x, woa, wout, g, b, wr, br)


def _expert_kernel(te_ref, nv_ref, nused_ref, tok_ref, tokn_ref, dst_ref, x_hbm, wgu_ref, bgu_ref, wd_ref, bd_ref,
                   ys_hbm, xs_buf, y_buf, wgu_sc, wd_sc, gsem, ssem):
    t = pl.program_id(0)
    nused = nused_ref[0]
    cur = t % 2
    nxt = 1 - cur

    def start_gather(idx_ref, slot):
        def body(r, carry):
            pltpu.make_async_copy(x_hbm.at[idx_ref[0, 0, r]], xs_buf.at[slot, pl.ds(r, 1)], gsem.at[slot]).start()
            return carry
        lax.fori_loop(0, TME, body, 0, unroll=DMA_UNROLL)

    def wait_gather(slot):
        pltpu.make_async_copy(xs_buf.at[slot], xs_buf.at[slot], gsem.at[slot]).wait()

    def wait_scatter(slot, n):
        pltpu.make_async_copy(ys_hbm.at[pl.ds(0, n)], ys_hbm.at[pl.ds(0, n)], ssem.at[slot]).wait()

    @pl.when(t == 0)
    def _():
        start_gather(tok_ref, 0)

    @pl.when(t + 1 < nused)
    def _():
        start_gather(tokn_ref, nxt)

    e = te_ref[t]
    e_prev = te_ref[jnp.maximum(t - 1, 0)]

    @pl.when(jnp.logical_or(t == 0, e != e_prev))
    def _():
        wgu_sc[...] = wgu_ref[0].astype(BF16)
        wd_sc[...] = wd_ref[0].astype(BF16)

    @pl.when(t < nused)
    def _():
        wait_gather(cur)

        @pl.when(t >= 2)
        def _():
            wait_scatter(cur, nv_ref[jnp.maximum(t - 2, 0)])

        half = D_MODEL // 2
        lo, hi = _unpack_bf16_pair(xs_buf[cur])
        gu = _dot(lo, wgu_sc[0:half, :]) + _dot(hi, wgu_sc[half:, :]) + bgu_ref[0]
        g = jnp.minimum(gu[:, :D_EXPERT], SWIGLU_LIMIT)
        u = jnp.clip(gu[:, D_EXPERT:], -SWIGLU_LIMIT, SWIGLU_LIMIT)
        act = (u + 1.0) * (g * _sigmoid(SWIGLU_ALPHA * g))
        y_buf[cur] = _dot(act.astype(BF16), wd_sc[...]) + bd_ref[0]

        def scatter(r, carry):
            pltpu.make_async_copy(y_buf.at[cur, pl.ds(r, 1)], ys_hbm.at[dst_ref[0, 0, r]], ssem.at[cur]).start()
            return carry

        def scatter_group(c, carry):
            for j in range(DMA_UNROLL):
                scatter(c * DMA_UNROLL + j, carry)
            return carry
        n_groups = nv_ref[t] // DMA_UNROLL
        lax.fori_loop(0, n_groups, scatter_group, 0)
        lax.fori_loop(n_groups * DMA_UNROLL, nv_ref[t], scatter, 0)

        @pl.when(t == nused - 1)
        def _():
            @pl.when(t >= 1)
            def _():
                wait_scatter(nxt, nv_ref[jnp.maximum(t - 1, 0)])
            wait_scatter(cur, nv_ref[t])


def _experts(tile_e, tile_nv, nused, row_token, row_dst, x1p, wgu, bgu, wd, bd, n_tiles, n_dst):
    idx = lambda a: a.reshape(n_tiles, 1, TME)
    smem_tile = lambda f: pl.BlockSpec((1, 1, TME), f, memory_space=pltpu.SMEM)
    grid_spec = pltpu.PrefetchScalarGridSpec(
        num_scalar_prefetch=3,
        grid=(n_tiles,),
        in_specs=[smem_tile(lambda t, te, nv, nu: (t, 0, 0)),
                  smem_tile(lambda t, te, nv, nu: (jnp.minimum(t + 1, n_tiles - 1), 0, 0)),
                  smem_tile(lambda t, te, nv, nu: (t, 0, 0)),
                  pl.BlockSpec(memory_space=pl.ANY),
                  pl.BlockSpec((1, D_MODEL, 2 * D_EXPERT), lambda t, te, nv, nu: (te[t], 0, 0)),
                  pl.BlockSpec((1, 1, 2 * D_EXPERT), lambda t, te, nv, nu: (te[t], 0, 0)),
                  pl.BlockSpec((1, D_EXPERT, D_MODEL), lambda t, te, nv, nu: (te[t], 0, 0)),
                  pl.BlockSpec((1, 1, D_MODEL), lambda t, te, nv, nu: (te[t], 0, 0))],
        out_specs=pl.BlockSpec(memory_space=pl.ANY),
        scratch_shapes=[pltpu.VMEM((2, TME, D_MODEL // 2), jnp.uint32), pltpu.VMEM((2, TME, D_MODEL), F32),
                        pltpu.VMEM((D_MODEL, 2 * D_EXPERT), BF16), pltpu.VMEM((D_EXPERT, D_MODEL), BF16),
                        pltpu.SemaphoreType.DMA((2,)), pltpu.SemaphoreType.DMA((2,))],
    )
    return pl.pallas_call(
        _expert_kernel,
        grid_spec=grid_spec,
        out_shape=jax.ShapeDtypeStruct((n_dst, 1, D_MODEL), F32),
        compiler_params=_params(("arbitrary",)),
        name="moe_experts",
    )(tile_e, tile_nv, nused, idx(row_token), idx(row_token), idx(row_dst), x1p, wgu, bgu, wd, bd)


def _expert_kernel_il(te_ref, tok0_ref, tokn_ref, dstp_ref, x_hbm, wgu_ref, bgu_ref, wd_ref, bd_ref, ys_hbm,
                      xs_a, xs_b, y_a, y_b, wg_sc, wu_sc, wd_sc, gsem, ssem, *, n_tiles):
    t = pl.program_id(0)
    half = D_MODEL // 2
    n_chunk = D_EXPERT // EXPERT_CHUNK
    rows_per_chunk = TME // n_chunk

    def gather_row(idx_ref, r, dst_buf, sem):
        pltpu.make_async_copy(x_hbm.at[idx_ref[0, 0, r]], dst_buf.at[pl.ds(r, 1)], sem).start()

    def scatter_row(r, src_buf, sem):
        pltpu.make_async_copy(src_buf.at[pl.ds(r, 1)], ys_hbm.at[dstp_ref[0, 0, r]], sem).start()

    def wait_gather(buf, sem):
        pltpu.make_async_copy(buf, buf, sem).wait()

    def wait_scatter(sem):
        rows = ys_hbm.at[pl.ds(0, TME)]
        pltpu.make_async_copy(rows, rows, sem).wait()

    @pl.when(t == 0)
    def _():
        def body(r, carry):
            gather_row(tok0_ref, r, xs_a, gsem.at[0])
            return carry
        lax.fori_loop(0, TME, body, 0, unroll=DMA_UNROLL)
        y_b[...] = jnp.zeros(y_b.shape, F32)

    e = te_ref[t]
    e_prev = te_ref[jnp.maximum(t - 1, 0)]

    @pl.when(jnp.logical_and(t < n_tiles, jnp.logical_or(t == 0, e != e_prev)))
    def _():
        for j in range(n_chunk):
            cols = slice(j * EXPERT_CHUNK, (j + 1) * EXPERT_CHUNK)
            wg_sc[j] = wgu_ref[0, :, cols].astype(BF16)
            wu_sc[j] = wgu_ref[0, :, D_EXPERT + j * EXPERT_CHUNK:D_EXPERT + (j + 1) * EXPERT_CHUNK].astype(BF16)
            wd_sc[j] = wd_ref[0, cols, :].astype(BF16)

    def compute_step(xs_cur, xs_nxt, y_cur, y_prv, g_cur, g_nxt, s_cur, s_prv):
        wait_gather(xs_cur, g_cur)

        @pl.when(t >= 1)
        def _():
            wait_scatter(s_cur)

        lo, hi = _unpack_bf16_pair(xs_cur[...])
        y = jnp.zeros((TME, D_MODEL), F32) + bd_ref[0]
        for j in range(n_chunk):
            cols = slice(j * EXPERT_CHUNK, (j + 1) * EXPERT_CHUNK)
            g = _dot(lo, wg_sc[j, 0:half, :]) + _dot(hi, wg_sc[j, half:, :]) + bgu_ref[0, :, cols]
            u = (_dot(lo, wu_sc[j, 0:half, :]) + _dot(hi, wu_sc[j, half:, :])
                 + bgu_ref[0, :, D_EXPERT + j * EXPERT_CHUNK:D_EXPERT + (j + 1) * EXPERT_CHUNK])
            g = jnp.minimum(g, SWIGLU_LIMIT)
            u = jnp.clip(u, -SWIGLU_LIMIT, SWIGLU_LIMIT)
            act = (u + 1.0) * (g * _sigmoid(SWIGLU_ALPHA * g))
            y = y + _dot(act.astype(BF16), wd_sc[j])
            for r in range(j * rows_per_chunk, (j + 1) * rows_per_chunk):
                gather_row(tokn_ref, r, xs_nxt, g_nxt)
                scatter_row(r, y_prv, s_prv)
        y_cur[...] = y

    @pl.when(jnp.logical_and(t < n_tiles, t % 2 == 0))
    def _():
        compute_step(xs_a, xs_b, y_a, y_b, gsem.at[0], gsem.at[1], ssem.at[0], ssem.at[1])

    @pl.when(jnp.logical_and(t < n_tiles, t % 2 == 1))
    def _():
        compute_step(xs_b, xs_a, y_b, y_a, gsem.at[1], gsem.at[0], ssem.at[1], ssem.at[0])

    @pl.when(t == n_tiles)
    def _():
        cur = n_tiles % 2
        xs_cur, y_prv = (xs_a, y_b) if cur == 0 else (xs_b, y_a)
        wait_gather(xs_cur, gsem.at[cur])
        wait_scatter(ssem.at[cur])

        def body(r, carry):
            scatter_row(r, y_prv, ssem.at[1 - cur])
            return carry
        lax.fori_loop(0, TME, body, 0, unroll=DMA_UNROLL)
        wait_scatter(ssem.at[1 - cur])


def _experts_il(tile_e, row_token, row_dst, x1p, wgu, bgu, wd, bd, n_tiles, n_flat):
    n_chunk = D_EXPERT // EXPERT_CHUNK
    te = jnp.concatenate([tile_e, tile_e[-1:]])
    zero_tiles = jnp.zeros((2, TME), jnp.int32)
    pad_rows = n_flat + jnp.arange(TME, dtype=jnp.int32)[None, :]
    tok_next = jnp.concatenate([row_token[1:], zero_tiles]).reshape(n_tiles + 1, 1, TME)
    dst_prev = jnp.concatenate([pad_rows, row_dst]).reshape(n_tiles + 1, 1, TME)
    smem_tile = lambda f: pl.BlockSpec((1, 1, TME), f, memory_space=pltpu.SMEM)
    grid_spec = pltpu.PrefetchScalarGridSpec(
        num_scalar_prefetch=1,
        grid=(n_tiles + 1,),
        in_specs=[smem_tile(lambda t, te: (0, 0, 0)),
                  smem_tile(lambda t, te: (t, 0, 0)),
                  smem_tile(lambda t, te: (t, 0, 0)),
                  pl.BlockSpec(memory_space=pl.ANY),
                  pl.BlockSpec((1, D_MODEL, 2 * D_EXPERT), lambda t, te: (te[t], 0, 0)),
                  pl.BlockSpec((1, 1, 2 * D_EXPERT), lambda t, te: (te[t], 0, 0)),
                  pl.BlockSpec((1, D_EXPERT, D_MODEL), lambda t, te: (te[t], 0, 0)),
                  pl.BlockSpec((1, 1, D_MODEL), lambda t, te: (te[t], 0, 0))],
        out_specs=pl.BlockSpec(memory_space=pl.ANY),
        scratch_shapes=[pltpu.VMEM((TME, D_MODEL // 2), jnp.uint32), pltpu.VMEM((TME, D_MODEL // 2), jnp.uint32),
                        pltpu.VMEM((TME, D_MODEL), F32), pltpu.VMEM((TME, D_MODEL), F32),
                        pltpu.VMEM((n_chunk, D_MODEL, EXPERT_CHUNK), BF16),
                        pltpu.VMEM((n_chunk, D_MODEL, EXPERT_CHUNK), BF16),
                        pltpu.VMEM((n_chunk, EXPERT_CHUNK, D_MODEL), BF16),
                        pltpu.SemaphoreType.DMA((2,)), pltpu.SemaphoreType.DMA((2,))],
    )
    return pl.pallas_call(
        functools.partial(_expert_kernel_il, n_tiles=n_tiles),
        grid_spec=grid_spec,
        out_shape=jax.ShapeDtypeStruct((n_flat + TME, 1, D_MODEL), F32),
        compiler_params=_params(("arbitrary",)),
        name="moe_experts",
    )(te, row_token[0:1].reshape(1, 1, TME), tok_next, dst_prev, x1p, wgu, bgu, wd, bd)


def _routing_tables(ridx, S):
    n_flat = S * TOP_K_EXPERTS
    n_tiles = n_flat // TME + N_EXPERTS
    flat_e = ridx.reshape(n_flat)
    order = jnp.argsort(flat_e, stable=True).astype(jnp.int32)
    counts = jnp.sum(flat_e[:, None] == jnp.arange(N_EXPERTS, dtype=jnp.int32)[None, :], axis=0, dtype=jnp.int32)
    padded = ((counts + TME - 1) // TME) * TME
    pad_end = jnp.cumsum(padded)
    pad_start = pad_end - padded
    grp_start = jnp.cumsum(counts) - counts
    nused = (pad_end[-1] // TME).astype(jnp.int32).reshape(1)
    tile_row0 = jnp.arange(n_tiles, dtype=jnp.int32) * TME
    tile_e = jnp.minimum(jnp.sum(tile_row0[:, None] >= pad_end[None, :], axis=1), N_EXPERTS - 1).astype(jnp.int32)
    tile_rank0 = tile_row0 - pad_start[tile_e]
    tile_nv = jnp.where(tile_row0 < pad_end[-1], jnp.clip(counts[tile_e] - tile_rank0, 0, TME), 0).astype(jnp.int32)
    within = jnp.arange(TME, dtype=jnp.int32)[None, :]
    valid = within < tile_nv[:, None]
    src_flat = lax.optimization_barrier(order[jnp.clip((grp_start[tile_e] + tile_rank0)[:, None] + within, 0, n_flat - 1)])
    row_token = jnp.where(valid, src_flat // TOP_K_EXPERTS, 0).astype(jnp.int32)
    row_dst = jnp.where(valid, (src_flat % TOP_K_EXPERTS) * S + src_flat // TOP_K_EXPERTS, n_flat + within).astype(jnp.int32)
    return tile_e, tile_nv, nused, row_token, row_dst, n_tiles


def _final_kernel(x1_ref, y0_ref, y1_ref, y2_ref, y3_ref, rg_ref, p_ref, wpg_ref, wpp_ref, g_ref, b_ref, o_ref):
    h = DEEPNORM_ALPHA * x1_ref[...]
    rg = rg_ref[...]
    for j, y_ref in enumerate((y0_ref, y1_ref, y2_ref, y3_ref)):
        h = h + rg[:, j:j + 1] * y_ref[:, 0, :]
    ple = _sigmoid(_dot(h.astype(BF16), wpg_ref[...])) * _dot(p_ref[...].astype(BF16), wpp_ref[...])
    o_ref[...] = _layer_norm(h + ple, g_ref[...], b_ref[...])


def _final(x1, ys, rgate, p, wpg, wpp, g, b):
    S = x1.shape[0]
    full = lambda a: pl.BlockSpec(a.shape, lambda i: (0, 0))
    row = lambda n: pl.BlockSpec((TM, n), lambda i: (i, 0))
    return pl.pallas_call(
        _final_kernel,
        grid=(S // TM,),
        in_specs=[row(D_MODEL)]
        + [pl.BlockSpec((TM, 1, D_MODEL), lambda i, j=j: (j * (S // TM) + i, 0, 0)) for j in range(TOP_K_EXPERTS)]
        + [row(LANES), row(PLE_DIM), full(wpg), full(wpp), full(g), full(b)],
        out_specs=row(D_MODEL),
        out_shape=jax.ShapeDtypeStruct((S, D_MODEL), F32),
        compiler_params=_params(("parallel",)),
        name="final",
    )(x1, ys, ys, ys, ys, rgate, p, wpg, wpp, g, b)


def _layer(x, p, w_in, b_in, w_o_attn, w_dw, b_dw, conv_ln_g, conv_ln_b, w_o_conv, w_out, ln1_g, ln1_b,
           w_router, b_router, w_gate_up, b_gate_up, w_down, b_down, w_ple_gate, w_ple_proj, ln2_g, ln2_b,
           rel_bias):
    S = x.shape[0]
    assert S % TM == 0 and S % TA == 0 and S % TKI == 0 and (S * TOP_K_EXPERTS) % TME == 0
    top_k = min(TOPK_MAX, S // 4)
    row2 = lambda a: a.reshape(1, -1).astype(F32)

    o_q, o_qi, o_ki, o_cv, o_g = 0, 3 * ATTN_WIDTH, 3 * ATTN_WIDTH + IDX_HEADS * IDX_DIM, \
        3 * ATTN_WIDTH + IDX_HEADS * IDX_DIM + IDX_DIM + IDX_HEADS, \
        3 * ATTN_WIDTH + IDX_HEADS * IDX_DIM + IDX_DIM + IDX_HEADS + 2 * CONV_CH
    kw_pad = LANES - (IDX_DIM + IDX_HEADS)
    wkw = jnp.pad(w_in[:, o_ki:o_cv], ((0, 0), (0, kw_pad)))
    bkw = jnp.pad(b_in[o_ki:o_cv], (0, kw_pad))
    q, k, v, qi, kw, u, sg = _proj(
        x, w_in[:, o_q:o_qi].astype(BF16), row2(b_in[o_q:o_qi]),
        w_in[:, o_qi:o_ki].astype(BF16), row2(b_in[o_qi:o_ki]),
        wkw.astype(BF16), row2(bkw),
        w_in[:, o_cv:o_g].astype(BF16), row2(b_in[o_cv:o_g]),
        w_in[:, o_g:].astype(BF16), row2(b_in[o_g:]))

    part_conv = _conv(u, sg, w_dw, row2(b_dw), row2(conv_ln_g), row2(conv_ln_b), w_o_conv.astype(BF16))

    ki = kw[:, :IDX_DIM].astype(BF16)
    mask4 = _index_mask(qi, kw[:, IDX_DIM:IDX_DIM + IDX_HEADS].T, jnp.concatenate([ki, ki], axis=1), top_k)
    bias_d, bias_e = _relative_bias_blocks(rel_bias)
    y_attn = _attention(q, k, v, mask4, bias_d, bias_e)

    wr = jnp.pad(w_router, ((0, 0), (0, LANES - N_EXPERTS)))
    br = jnp.pad(b_router, (0, LANES - N_EXPERTS), constant_values=-jnp.inf)
    x1, x1p, ridx, rgate = _post(y_attn, sg, part_conv, x, w_o_attn.astype(BF16), w_out.astype(BF16),
                                 row2(ln1_g), row2(ln1_b), wr, row2(br))

    tile_e, tile_nv, nused, row_token, row_dst, n_tiles = _routing_tables(ridx[:, :TOP_K_EXPERTS], S)
    ys = _experts_il(tile_e, row_token, row_dst, x1p, w_gate_up, b_gate_up.reshape(N_EXPERTS, 1, -1),
                     w_down, b_down.reshape(N_EXPERTS, 1, -1), n_tiles, S * TOP_K_EXPERTS)
    return _final(x1, ys, rgate, p, w_ple_gate.astype(BF16), w_ple_proj.astype(BF16), row2(ln2_g), row2(ln2_b))


def kernel(x, p, w_in, b_in, w_o_attn, w_dw, b_dw, conv_ln_g, conv_ln_b, w_o_conv, w_out, ln1_g, ln1_b, w_router, b_router, w_gate_up, b_gate_up, w_down, b_down, w_ple_gate, w_ple_proj, ln2_g, ln2_b, rel_bias):
    assert x.shape[0] == 1 and p.shape[0] == DEPTH
    out = _layer(x[0], p[0, 0], w_in[0], b_in[0], w_o_attn[0], w_dw[0], b_dw[0], conv_ln_g[0], conv_ln_b[0],
                 w_o_conv[0], w_out[0], ln1_g[0], ln1_b[0], w_router[0], b_router[0], w_gate_up[0], b_gate_up[0],
                 w_down[0], b_down[0], w_ple_gate[0], w_ple_proj[0], ln2_g[0], ln2_b[0], rel_bias)
    return out[None]
```

```python
import functools
import math

import jax
import jax.numpy as jnp
import numpy as np
from jax import lax
from jax.experimental import pallas as pl
from jax.experimental.pallas import tpu as pltpu

F32 = jnp.float32
BF16 = jnp.bfloat16

D_MODEL = 1024
N_HEADS = 8
HEAD_DIM = 64
ATTN_WIDTH = N_HEADS * HEAD_DIM
ATTN_SCALE = HEAD_DIM ** -0.5
LOG2E = math.log2(math.e)
IDX_HEADS = 8
IDX_DIM = 64
IDX_SCALE = (IDX_HEADS ** -0.5) * (IDX_DIM ** -0.5)
TOPK_MAX = 256
CONV_CH = 512
CONV_WIDTH = 31
N_BUCKETS = 32
MAX_DISTANCE = 128
N_EXPERTS = 32
TOP_K_EXPERTS = 4
D_EXPERT = 1024
SWIGLU_LIMIT = 7.0
SWIGLU_ALPHA = 1.702
PLE_DIM = 256
LN_EPS = 1e-5
DEPTH = 1
DEEPNORM_ALPHA = (2 * DEPTH) ** 0.25

LANES = 128
NEG_BIG = -1e30
F32_TINY = float(np.finfo(np.float32).tiny)
VMEM_LIMIT = 56 * 1024 * 1024

TM = 512
TQI = 256
TKI = 512
EDGE_FIRST = 9
SEARCH_CAP = 128
TA = 512
SUB = 128
TME = 256
DMA_UNROLL = 8
CONV_HALO = 32


def _params(sem, vmem=VMEM_LIMIT):
    return pltpu.CompilerParams(dimension_semantics=sem, vmem_limit_bytes=vmem)


def _sigmoid(x):
    return 1.0 / (1.0 + jnp.exp(-x))


def _layer_norm(x, g, b):
    mu = jnp.mean(x, axis=-1, keepdims=True)
    xc = x - mu
    var = jnp.mean(xc * xc, axis=-1, keepdims=True)
    return xc * lax.rsqrt(var + LN_EPS) * g + b


def _dot(a, b):
    return jnp.dot(a, b, preferred_element_type=F32)


def _dot_nt(a, b):
    return lax.dot_general(a, b, (((1,), (1,)), ((), ())), preferred_element_type=F32)


def _proj_kernel(x_ref, wqkv_ref, bqkv_ref, wqi_ref, bqi_ref, wkw_ref, bkw_ref, wcv_ref, bcv_ref,
                 wg_ref, bg_ref, q_ref, k_ref, v_ref, qi_ref, kw_ref, u_ref, sg_ref):
    xb = x_ref[...].astype(BF16)
    qkv = _dot(xb, wqkv_ref[...]) + bqkv_ref[...]
    q_ref[...] = (qkv[:, :ATTN_WIDTH] * (ATTN_SCALE * LOG2E)).astype(BF16)
    k_ref[...] = qkv[:, ATTN_WIDTH:2 * ATTN_WIDTH].astype(BF16)
    v_ref[...] = qkv[:, 2 * ATTN_WIDTH:].astype(BF16)
    qi_ref[...] = (_dot(xb, wqi_ref[...]) + bqi_ref[...]).astype(BF16)
    kw = _dot(xb, wkw_ref[...]) + bkw_ref[...]
    lane = lax.broadcasted_iota(jnp.int32, kw.shape, 1)
    kw_ref[...] = jnp.where(lane >= IDX_DIM, kw * IDX_SCALE, kw)
    cv = _dot(xb, wcv_ref[...]) + bcv_ref[...]
    u_ref[...] = cv[:, :CONV_CH] * _sigmoid(cv[:, CONV_CH:])
    sg_ref[...] = _sigmoid(_dot(xb, wg_ref[...]) + bg_ref[...])


def _proj(x, wqkv, bqkv, wqi, bqi, wkw, bkw, wcv, bcv, wg, bg):
    S = x.shape[0]
    full = lambda a: pl.BlockSpec(a.shape, lambda i: (0, 0))
    row = lambda n: pl.BlockSpec((TM, n), lambda i: (i, 0))
    outs = [(ATTN_WIDTH, BF16)] * 3 + [(IDX_HEADS * IDX_DIM, BF16), (LANES, F32), (CONV_CH, F32),
                                        (2 * D_MODEL, F32)]
    return pl.pallas_call(
        _proj_kernel,
        grid=(S // TM,),
        in_specs=[row(D_MODEL)] + [full(a) for a in (wqkv, bqkv, wqi, bqi, wkw, bkw, wcv, bcv, wg, bg)],
        out_specs=[row(n) for n, _ in outs],
        out_shape=[jax.ShapeDtypeStruct((S, n), dt) for n, dt in outs],
        compiler_params=_params(("parallel",)),
        name="proj",
    )(x, wqkv, bqkv, wqi, bqi, wkw, bkw, wcv, bcv, wg, bg)


def _conv_kernel(u_ref, wdw_ref, bdw_ref, lng_ref, lnb_ref, wo_ref, sgc_ref, o_ref, buf_ref, sh_ref):
    @pl.when(pl.program_id(0) == 0)
    def _():
        buf_ref[0:CONV_HALO, :] = jnp.zeros((CONV_HALO, CONV_CH), F32)

    buf_ref[CONV_HALO:CONV_HALO + TM, :] = u_ref[...]
    base = CONV_HALO - (CONV_WIDTH - 1)
    acc = jnp.zeros((TM, CONV_CH), F32) + bdw_ref[...]
    sub = 8
    for r in range(sub):
        taps = [j for j in range(CONV_WIDTH) if (base + j) % sub == r]
        span = max(base + j - r for j in taps) + TM
        sh_ref[0:span, :] = buf_ref[r:r + span, :]
        for j in taps:
            off = base + j - r
            acc = acc + wdw_ref[j:j + 1, :] * sh_ref[off:off + TM, :]
    buf_ref[0:CONV_HALO, :] = buf_ref[TM:TM + CONV_HALO, :]
    y = _layer_norm(acc, lng_ref[...], lnb_ref[...])
    y = y * _sigmoid(y)
    o_ref[...] = sgc_ref[...] * _dot(y.astype(BF16), wo_ref[...])


def _conv(u, sg, wdw, bdw, lng, lnb, wo):
    S = u.shape[0]
    full = lambda a: pl.BlockSpec(a.shape, lambda i: (0, 0))
    return pl.pallas_call(
        _conv_kernel,
        grid=(S // TM,),
        in_specs=[pl.BlockSpec((TM, CONV_CH), lambda i: (i, 0)), full(wdw), full(bdw), full(lng), full(lnb),
                  full(wo), pl.BlockSpec((TM, D_MODEL), lambda i: (i, 1))],
        out_specs=pl.BlockSpec((TM, D_MODEL), lambda i: (i, 0)),
        out_shape=jax.ShapeDtypeStruct((S, D_MODEL), F32),
        scratch_shapes=[pltpu.VMEM((TM + CONV_HALO, CONV_CH), F32), pltpu.VMEM((TM + CONV_HALO, CONV_CH), F32)],
        compiler_params=_params(("arbitrary",)),
        name="conv",
    )(u, wdw, bdw, lng, lnb, wo, sg)


def _float_key(f):
    b = lax.bitcast_convert_type(f, jnp.int32)
    return b ^ ((b >> 31) & jnp.int32(0x7FFFFFFF))


def _key_float(k):
    b = k ^ ((k >> 31) & jnp.int32(0x7FFFFFFF))
    return lax.bitcast_convert_type(b, F32)


def _index_kernel(qi_ref, wt_ref, ki_ref, mask_ref, sc_ref, qh_ref, *, top_k):
    i = pl.program_id(0)
    nkt = (i * TQI + TQI + TKI - 1) // TKI
    ntile = sc_ref.shape[0]
    q_g = i * TQI + lax.broadcasted_iota(jnp.int32, (1, TQI), 1)
    lane = lax.broadcasted_iota(jnp.int32, (TQI, LANES), 1)

    for h in range(IDX_HEADS):
        qp = qi_ref[:, (h // 2) * LANES:(h // 2 + 1) * LANES]
        keep = (lane < IDX_DIM) if h % 2 == 0 else (lane >= IDX_DIM)
        qh_ref[h] = jnp.where(keep, qp, jnp.zeros_like(qp))

    def score_tile(kt, causal):
        kb = ki_ref[pl.ds(pl.multiple_of(kt * TKI, TKI), TKI), :]
        acc = jnp.zeros((TKI, TQI), F32)
        for h in range(IDX_HEADS):
            acc = acc + wt_ref[h:h + 1, :] * jnp.maximum(_dot_nt(kb, qh_ref[h]), 0.0)
        if causal:
            key_g = kt * TKI + lax.broadcasted_iota(jnp.int32, (TKI, TQI), 0)
            ok = key_g <= q_g
            lo_src = jnp.where(ok, acc, jnp.inf)
            acc = jnp.where(ok, acc, -jnp.inf)
        else:
            lo_src = acc
        sc_ref[kt] = acc
        return jnp.max(acc, axis=0, keepdims=True), jnp.min(lo_src, axis=0, keepdims=True)

    def score_body(kt, carry):
        mx, mn = carry
        tmx, tmn = score_tile(kt, False)
        return jnp.maximum(mx, tmx), jnp.minimum(mn, tmn)

    mx0 = jnp.full((1, TQI), -jnp.inf, F32)
    mn0 = jnp.full((1, TQI), jnp.inf, F32)
    mx, mn = lax.fori_loop(0, nkt - 1, score_body, (mx0, mn0))
    tmx, tmn = score_tile(nkt - 1, True)
    mx = jnp.maximum(mx, tmx)
    mn = jnp.minimum(mn, tmn)
    kf = float(top_k)
    sub = 8

    def count(pivot, strict):
        def tile(kt, cnt):
            for r in range(TKI // sub):
                blk = sc_ref[kt, r * sub:(r + 1) * sub, :]
                hit = (blk > pivot) if strict else (blk >= pivot)
                cnt = cnt + jnp.where(hit, 1.0, 0.0)
            return cnt

        def pair(j, cnts):
            return tile(2 * j, cnts[0]), tile(2 * j + 1, cnts[1])

        zero = jnp.zeros((sub, TQI), F32)
        c0, c1 = lax.fori_loop(0, nkt // 2, pair, (zero, zero))
        c0 = lax.fori_loop(2 * (nkt // 2), nkt, tile, c0)
        return jnp.sum(c0 + c1, axis=0, keepdims=True)

    n_causal = (q_g + 1).astype(F32)
    all_sel = n_causal <= kf
    zeros = jnp.zeros((1, TQI), F32)
    state0 = dict(
        it=jnp.int32(0),
        lo=mn, hi=_key_float(_float_key(mx) + 1), clo=n_causal, chi=zeros,
        glo=jnp.log(jnp.maximum(n_causal, kf + 1.0) / kf), ghi=jnp.full((1, TQI), math.log(0.5 / kf), F32),
        thr=jnp.where(all_sel, NEG_BIG, 0.0).astype(F32),
        done=all_sel.astype(F32), tie=zeros, side=zeros, forced=zeros, use_forced=zeros,
    )

    def above(x):
        return jnp.where(jnp.abs(x) < F32_TINY, F32_TINY, _key_float(_float_key(x) + 1))

    def bracket_edges(lo, hi):
        def tile(kt, carry):
            a, b = carry
            for r in range(TKI // sub):
                blk = sc_ref[kt, r * sub:(r + 1) * sub, :]
                a = jnp.maximum(a, jnp.where(blk < hi, blk, -jnp.inf))
                b = jnp.minimum(b, jnp.where(blk >= lo, blk, jnp.inf))
            return a, b
        a, b = lax.fori_loop(0, nkt, tile, (jnp.full((sub, TQI), -jnp.inf, F32), jnp.full((sub, TQI), jnp.inf, F32)))
        return jnp.max(a, axis=0, keepdims=True), jnp.min(b, axis=0, keepdims=True)

    def cond(st):
        return jnp.logical_and(st["it"] < SEARCH_CAP, jnp.min(st["done"]) < 0.5)

    def count_step(st):
        it, lo, hi, glo, ghi = st["it"], st["lo"], st["hi"], st["glo"], st["ghi"]
        lo_k, hi_k = _float_key(lo), _float_key(hi)
        above_lo = above(lo)
        below_hi = jnp.where(hi == F32_TINY, 0.0, _key_float(hi_k - 1))
        probe = jnp.logical_and(st["done"] < 0.5, above_lo < hi)
        tie_now = jnp.logical_and(st["done"] < 0.5, above_lo >= hi)
        frac = jnp.where(it < 24, glo / (glo - ghi), 0.5)
        pf = lo + (hi - lo) * frac
        pf = jnp.where(it < 64, pf, _key_float((lo_k >> 1) + (hi_k >> 1) + (lo_k & hi_k & 1)))
        pf = jnp.where(it == 0, 0.0, jnp.where(it == 1, F32_TINY, pf))
        pf = jnp.where(st["use_forced"] > 0.5, st["forced"], pf)
        pf = jnp.where(probe, jnp.minimum(jnp.maximum(pf, above_lo), below_hi), lo)
        c = count(pf, False)
        hit = jnp.logical_and(probe, c == kf)
        up = jnp.logical_and(probe, c > kf)
        dn = jnp.logical_and(probe, c < kf)
        g = jnp.log(jnp.maximum(c, 0.5) / kf)
        return dict(
            it=it + 1,
            lo=jnp.where(up, pf, lo), hi=jnp.where(dn, pf, hi),
            clo=jnp.where(up, c, st["clo"]), chi=jnp.where(dn, c, st["chi"]),
            glo=jnp.where(up, g, jnp.where(jnp.logical_and(dn, st["side"] < -0.5), 0.5 * glo, glo)),
            ghi=jnp.where(dn, g, jnp.where(jnp.logical_and(up, st["side"] > 0.5), 0.5 * ghi, ghi)),
            thr=jnp.where(hit, pf, jnp.where(tie_now, lo, st["thr"])),
            done=jnp.where(jnp.logical_or(hit, tie_now), 1.0, st["done"]),
            tie=jnp.where(tie_now, 1.0, st["tie"]),
            side=jnp.where(up, 1.0, jnp.where(dn, -1.0, st["side"])),
            forced=st["forced"], use_forced=zeros,
        )

    def edge_step(st):
        lo, hi = st["lo"], st["hi"]
        probe = jnp.logical_and(st["done"] < 0.5, above(lo) < hi)
        a, b = bracket_edges(lo, hi)
        one_above = st["chi"] == kf - 1.0
        one_below = st["clo"] == kf + 1.0
        new = dict(st)
        new.update(
            it=st["it"] + 1,
            lo=jnp.where(probe, b, lo), hi=jnp.where(probe, above(a), hi),
            forced=jnp.where(one_above, a, above(b)),
            use_forced=jnp.where(jnp.logical_and(probe, jnp.logical_or(one_above, one_below)), 1.0, 0.0),
        )
        return new

    def step(st):
        it = st["it"]
        is_edge = jnp.logical_and(it >= EDGE_FIRST, (it - EDGE_FIRST) % 3 == 0)
        return lax.cond(is_edge, edge_step, count_step, st)

    st = lax.while_loop(cond, step, state0)
    thr = st["thr"]
    tie = st["tie"]
    any_tie = jnp.max(tie) > 0.5

    def emit(kt, sel):
        mask_ref[0, kt] = jnp.where(sel, 0.0, NEG_BIG).astype(mask_ref.dtype).T

    @pl.when(jnp.logical_not(any_tie))
    def _():
        def body(kt, carry):
            emit(kt, sc_ref[kt] >= thr)
            return carry
        lax.fori_loop(0, nkt, body, 0)

    @pl.when(any_tie)
    def _():
        need = jnp.where(tie > 0.5, kf - count(thr, True), float(2 * ntile * TKI))
        r = lax.broadcasted_iota(jnp.int32, (TKI, TKI), 0)
        c = lax.broadcasted_iota(jnp.int32, (TKI, TKI), 1)
        prefix = jnp.where(c <= r, 1.0, 0.0).astype(BF16)

        def body(kt, seen):
            s = sc_ref[kt]
            eq = s == thr
            rank = seen + _dot(prefix, jnp.where(eq, 1.0, 0.0).astype(BF16))
            emit(kt, jnp.logical_or(s > thr, jnp.logical_and(eq, rank <= need)))
            return seen + jnp.sum(jnp.where(eq, 1.0, 0.0), axis=0, keepdims=True)
        lax.fori_loop(0, nkt, body, jnp.zeros((1, TQI), F32))

    def fill(kt, carry):
        mask_ref[0, kt] = jnp.full((TQI, TKI), NEG_BIG, mask_ref.dtype)
        return carry
    lax.fori_loop(nkt, ntile, fill, 0)


def _index_mask(qi, wt, ki2, top_k):
    S = qi.shape[0]
    nq, nk = S // TQI, S // TKI
    return pl.pallas_call(
        functools.partial(_index_kernel, top_k=top_k),
        grid=(nq,),
        in_specs=[pl.BlockSpec((TQI, IDX_HEADS * IDX_DIM), lambda i: (i, 0)),
                  pl.BlockSpec((IDX_HEADS, TQI), lambda i: (0, i)),
                  pl.BlockSpec((S, LANES), lambda i: (0, 0))],
        out_specs=pl.BlockSpec((1, nk, TQI, TKI), lambda i: (i, 0, 0, 0)),
        out_shape=jax.ShapeDtypeStruct((nq, nk, TQI, TKI), BF16),
        scratch_shapes=[pltpu.VMEM((nk, TKI, TQI), F32),
                        pltpu.VMEM((IDX_HEADS, TQI, LANES), BF16)],
        compiler_params=_params(("parallel",)),
        name="index_mask",
    )(qi, wt, ki2)


def _attn_kernel(qt_ref, kt_ref, q_ref, k_ref, v_ref, m_ref, bd_ref, be_ref, o_ref, *state):
    m_sc, acc_sc = state[:N_HEADS], state[N_HEADS:]
    step = pl.program_id(0)
    qi = qt_ref[step]
    ki = kt_ref[step]
    nsub = TA // SUB

    @pl.when(ki == 0)
    def _():
        for h in range(N_HEADS):
            m_sc[h][...] = jnp.full((TA, LANES), NEG_BIG, F32)
            acc_sc[h][...] = jnp.zeros((TA, LANES), F32)

    lane = lax.broadcasted_iota(jnp.int32, (TA, LANES), 1)
    first = lane < HEAD_DIM

    def bias_tile(h, diagonal):
        zero = jnp.zeros((SUB, SUB), F32)
        rows = []
        for a in range(nsub):
            if diagonal:
                blks = [bd_ref[h] if b == a else (be_ref[h] if b == a - 1 else zero) for b in range(nsub)]
            else:
                blks = [be_ref[h] if (a == 0 and b == nsub - 1) else zero for b in range(nsub)]
            rows.append(jnp.concatenate(blks, axis=1))
        return jnp.concatenate(rows, axis=0)

    def update(near, diagonal):
        maskf = jnp.concatenate([m_ref[a, 0] for a in range(TA // TQI)], axis=0).astype(F32)

        def logits(h):
            cols = slice((h // 2) * LANES, (h // 2 + 1) * LANES)
            qp = q_ref[:, cols]
            mine = first if h % 2 == 0 else jnp.logical_not(first)
            s = _dot_nt(jnp.where(mine, qp, jnp.zeros_like(qp)), k_ref[:, cols]) + maskf
            return s + bias_tile(h, diagonal) if near else s

        s = logits(0)
        for h in range(N_HEADS):
            s_next = logits(h + 1) if h + 1 < N_HEADS else None
            vp = v_ref[:, (h // 2) * LANES:(h // 2 + 1) * LANES]
            mine = first if h % 2 == 0 else jnp.logical_not(first)
            vh = jnp.where(mine, vp, jnp.ones_like(vp))
            m_prev = m_sc[h][...]
            m_next = jnp.maximum(m_prev, jnp.max(s, axis=1, keepdims=True))
            pexp = jnp.exp2(s - jnp.concatenate([m_next] * (TA // LANES), axis=1)).astype(BF16)
            acc_sc[h][...] = jnp.exp2(m_prev - m_next) * acc_sc[h][...] + _dot(pexp, vh)
            m_sc[h][...] = m_next
            s = s_next

    @pl.when(ki < qi - 1)
    def _():
        update(False, False)

    @pl.when(ki == qi - 1)
    def _():
        update(True, False)

    @pl.when(ki == qi)
    def _():
        update(True, True)
        for p in range(N_HEADS // 2):
            a0, a1 = acc_sc[2 * p][...], acc_sc[2 * p + 1][...]
            d0 = pltpu.roll(a0, HEAD_DIM, axis=1)
            d1 = pltpu.roll(a1, HEAD_DIM, axis=1)
            o_ref[:, p * LANES:(p + 1) * LANES] = jnp.where(first, a0 / d0, a1 / d1).astype(o_ref.dtype)


def _attention(q, k, v, mask4, bias_d, bias_e):
    S = q.shape[0]
    nb = S // TA
    pairs = [(a, b) for a in range(nb) for b in range(a + 1)]
    qtab = jnp.asarray(np.array([a for a, _ in pairs], np.int32))
    ktab = jnp.asarray(np.array([b for _, b in pairs], np.int32))
    grid_spec = pltpu.PrefetchScalarGridSpec(
        num_scalar_prefetch=2,
        grid=(len(pairs),),
        in_specs=[pl.BlockSpec((TA, ATTN_WIDTH), lambda s, qt, kt: (qt[s], 0)),
                  pl.BlockSpec((TA, ATTN_WIDTH), lambda s, qt, kt: (kt[s], 0)),
                  pl.BlockSpec((TA, ATTN_WIDTH), lambda s, qt, kt: (kt[s], 0)),
                  pl.BlockSpec((TA // TQI, TA // TKI, TQI, TKI), lambda s, qt, kt: (qt[s], kt[s], 0, 0)),
                  pl.BlockSpec(bias_d.shape, lambda s, qt, kt: (0, 0, 0)),
                  pl.BlockSpec(bias_e.shape, lambda s, qt, kt: (0, 0, 0))],
        out_specs=pl.BlockSpec((TA, ATTN_WIDTH), lambda s, qt, kt: (qt[s], 0)),
        scratch_shapes=[pltpu.VMEM((TA, LANES), F32)] * (2 * N_HEADS),
    )
    return pl.pallas_call(
        _attn_kernel,
        grid_spec=grid_spec,
        out_shape=jax.ShapeDtypeStruct((S, ATTN_WIDTH), BF16),
        compiler_params=_params(("arbitrary",)),
        name="attn",
    )(qtab, ktab, q, k, v, mask4, bias_d, bias_e)


def _relative_bias_blocks(rel_bias):
    dist = jnp.arange(2 * SUB, dtype=jnp.int32)
    max_exact = N_BUCKETS // 2
    dist_f = jnp.maximum(dist, 1).astype(F32)
    large = max_exact + (jnp.log(dist_f / max_exact) / math.log(MAX_DISTANCE / max_exact)
                         * (N_BUCKETS - max_exact)).astype(jnp.int32)
    bucket = jnp.where(dist < max_exact, dist, jnp.minimum(large, N_BUCKETS - 1))
    table = ((rel_bias[bucket] - rel_bias[N_BUCKETS - 1]) * LOG2E).astype(F32)
    i = jnp.arange(SUB)[:, None]
    j = jnp.arange(SUB)[None, :]

    def toeplitz(d):
        onehot = (d[:, :, None] == dist[None, None, :]).astype(F32)
        return jnp.einsum("ijd,dh->hij", onehot, table, precision=lax.Precision.HIGHEST)

    return toeplitz(jnp.clip(i - j, 0, 2 * SUB - 1)), toeplitz(SUB + i - j)


def _pack_bf16_pair(a, b):
    def rnd(x):
        bits = lax.bitcast_convert_type(x, jnp.uint32)
        return bits + jnp.uint32(0x7FFF) + ((bits >> 16) & jnp.uint32(1))
    return (rnd(a) >> 16) | (rnd(b) & jnp.uint32(0xFFFF0000))


def _unpack_bf16_pair(p):
    lo = lax.bitcast_convert_type(p << 16, F32)
    hi = lax.bitcast_convert_type(p & jnp.uint32(0xFFFF0000), F32)
    return lo.astype(BF16), hi.astype(BF16)


def _post_kernel(ya_ref, sga_ref, pc_ref, x_ref, woa_ref, wout_ref, g_ref, b_ref, wr_ref, wrl_ref, br_ref,
                 x1_ref, x1p_ref, ridx_ref, rgate_ref):
    y_attn = _dot(ya_ref[...], woa_ref[...])
    merged = sga_ref[...] * y_attn + pc_ref[...]
    mix = _dot(merged.astype(BF16), wout_ref[...])
    x1 = _layer_norm(DEEPNORM_ALPHA * x_ref[...] + mix, g_ref[...], b_ref[...])
    x1_ref[...] = x1
    half = D_MODEL // 2
    x1p_ref[:, 0, :] = _pack_bf16_pair(x1[:, :half], x1[:, half:])

    x1_hi = x1.astype(BF16)
    x1_lo = (x1 - x1_hi.astype(F32)).astype(BF16)
    logits = (_dot(x1_hi, wr_ref[...]) + (_dot(x1_hi, wrl_ref[...]) + _dot(x1_lo, wr_ref[...]))) + br_ref[...]
    lane = lax.broadcasted_iota(jnp.int32, logits.shape, 1).astype(F32)
    cur = logits
    vals, idxs = [], []
    for _ in range(TOP_K_EXPERTS):
        m = jnp.max(cur, axis=1, keepdims=True)
        ix = jnp.min(jnp.where(cur == m, lane, float(LANES)), axis=1, keepdims=True)
        vals.append(m)
        idxs.append(ix)
        cur = jnp.where(lane == ix, -jnp.inf, cur)
    exps = [jnp.exp(v - vals[0]) for v in vals]
    denom = exps[0]
    for e in exps[1:]:
        denom = denom + e
    ridx = jnp.zeros_like(logits)
    rgate = jnp.zeros_like(logits)
    for j in range(TOP_K_EXPERTS):
        ridx = jnp.where(lane == float(j), idxs[j], ridx)
        rgate = jnp.where(lane == float(j), exps[j] / denom, rgate)
    ridx_ref[...] = ridx.astype(jnp.int32)
    rgate_ref[...] = rgate


def _post(ya, sg, pc, x, woa, wout, g, b, wr, wrl, br):
    S = x.shape[0]
    full = lambda a: pl.BlockSpec(a.shape, lambda i: (0, 0))
    row = lambda n: pl.BlockSpec((TM, n), lambda i: (i, 0))
    return pl.pallas_call(
        _post_kernel,
        grid=(S // TM,),
        in_specs=[row(ATTN_WIDTH), row(D_MODEL), row(D_MODEL), row(D_MODEL)] + [full(a) for a in (woa, wout, g, b, wr, wrl, br)],
        out_specs=[row(D_MODEL), pl.BlockSpec((TM, 1, D_MODEL // 2), lambda i: (i, 0, 0)), row(LANES), row(LANES)],
        out_shape=[jax.ShapeDtypeStruct((S, D_MODEL), F32), jax.ShapeDtypeStruct((S, 1, D_MODEL // 2), jnp.uint32),
                   jax.ShapeDtypeStruct((S, LANES), jnp.int32), jax.ShapeDtypeStruct((S, LANES), F32)],
        compiler_params=_params(("parallel",)),
        name="post",
    )(ya, sg, pc, x, woa, wout, g, b, wr, wrl, br)


def _expert_kernel(te_ref, nv_ref, nused_ref, tok_ref, tokn_ref, dst_ref, x_hbm, wgu_ref, bgu_ref, wd_ref, bd_ref,
                   ys_hbm, xs_buf, y_buf, wgu_sc, wd_sc, gsem, ssem):
    t = pl.program_id(0)
    nused = nused_ref[0]
    cur = t % 2
    nxt = 1 - cur

    def start_gather(idx_ref, slot):
        def body(r, carry):
            pltpu.make_async_copy(x_hbm.at[idx_ref[0, 0, r]], xs_buf.at[slot, r], gsem.at[slot]).start()
            return carry
        lax.fori_loop(0, TME, body, 0, unroll=DMA_UNROLL)

    def wait_gather(slot):
        pltpu.make_async_copy(xs_buf.at[slot], xs_buf.at[slot], gsem.at[slot]).wait()

    def wait_scatter(slot, n):
        pltpu.make_async_copy(ys_hbm.at[pl.ds(0, n)], ys_hbm.at[pl.ds(0, n)], ssem.at[slot]).wait()

    @pl.when(t == 0)
    def _():
        start_gather(tok_ref, 0)

    @pl.when(t + 1 < nused)
    def _():
        start_gather(tokn_ref, nxt)

    e = te_ref[t]
    e_prev = te_ref[jnp.maximum(t - 1, 0)]

    @pl.when(jnp.logical_or(t == 0, e != e_prev))
    def _():
        wgu_sc[...] = wgu_ref[0].astype(BF16)
        wd_sc[...] = wd_ref[0].astype(BF16)

    @pl.when(t < nused)
    def _():
        wait_gather(cur)

        @pl.when(t >= 2)
        def _():
            wait_scatter(cur, nv_ref[jnp.maximum(t - 2, 0)])

        half = D_MODEL // 2
        lo, hi = _unpack_bf16_pair(xs_buf[cur, :, 0, :])
        gu = _dot(lo, wgu_sc[0:half, :]) + _dot(hi, wgu_sc[half:, :]) + bgu_ref[0]
        g = jnp.minimum(gu[:, :D_EXPERT], SWIGLU_LIMIT)
        u = jnp.clip(gu[:, D_EXPERT:], -SWIGLU_LIMIT, SWIGLU_LIMIT)
        act = (u + 1.0) * (g * _sigmoid(SWIGLU_ALPHA * g))
        y_buf[cur, :, 0, :] = _dot(act.astype(BF16), wd_sc[...]) + bd_ref[0]

        def scatter(r, carry):
            pltpu.make_async_copy(y_buf.at[cur, r], ys_hbm.at[dst_ref[0, 0, r]], ssem.at[cur]).start()
            return carry

        def scatter_group(c, carry):
            for j in range(DMA_UNROLL):
                scatter(c * DMA_UNROLL + j, carry)
            return carry
        n_groups = nv_ref[t] // DMA_UNROLL
        lax.fori_loop(0, n_groups, scatter_group, 0)
        lax.fori_loop(n_groups * DMA_UNROLL, nv_ref[t], scatter, 0)

        @pl.when(t == nused - 1)
        def _():
            @pl.when(t >= 1)
            def _():
                wait_scatter(nxt, nv_ref[jnp.maximum(t - 1, 0)])
            wait_scatter(cur, nv_ref[t])


def _experts(tile_e, tile_nv, nused, row_token, row_dst, x1p, wgu, bgu, wd, bd, n_tiles, n_dst):
    idx = lambda a: a.reshape(n_tiles, 1, TME)
    smem_tile = lambda f: pl.BlockSpec((1, 1, TME), f, memory_space=pltpu.SMEM)
    grid_spec = pltpu.PrefetchScalarGridSpec(
        num_scalar_prefetch=3,
        grid=(n_tiles,),
        in_specs=[smem_tile(lambda t, te, nv, nu: (t, 0, 0)),
                  smem_tile(lambda t, te, nv, nu: (jnp.minimum(t + 1, n_tiles - 1), 0, 0)),
                  smem_tile(lambda t, te, nv, nu: (t, 0, 0)),
                  pl.BlockSpec(memory_space=pl.ANY),
                  pl.BlockSpec((1, D_MODEL, 2 * D_EXPERT), lambda t, te, nv, nu: (te[t], 0, 0)),
                  pl.BlockSpec((1, 1, 2 * D_EXPERT), lambda t, te, nv, nu: (te[t], 0, 0)),
                  pl.BlockSpec((1, D_EXPERT, D_MODEL), lambda t, te, nv, nu: (te[t], 0, 0)),
                  pl.BlockSpec((1, 1, D_MODEL), lambda t, te, nv, nu: (te[t], 0, 0))],
        out_specs=pl.BlockSpec(memory_space=pl.ANY),
        scratch_shapes=[pltpu.VMEM((2, TME, 1, D_MODEL // 2), jnp.uint32), pltpu.VMEM((2, TME, 1, D_MODEL), F32),
                        pltpu.VMEM((D_MODEL, 2 * D_EXPERT), BF16), pltpu.VMEM((D_EXPERT, D_MODEL), BF16),
                        pltpu.SemaphoreType.DMA((2,)), pltpu.SemaphoreType.DMA((2,))],
    )
    return pl.pallas_call(
        _expert_kernel,
        grid_spec=grid_spec,
        out_shape=jax.ShapeDtypeStruct((n_dst, 1, D_MODEL), F32),
        compiler_params=_params(("arbitrary",)),
        name="moe_experts",
    )(tile_e, tile_nv, nused, idx(row_token), idx(row_token), idx(row_dst), x1p, wgu, bgu, wd, bd)


def _routing_tables(ridx, S):
    n_flat = S * TOP_K_EXPERTS
    n_tiles = n_flat // TME + N_EXPERTS
    flat_e = ridx.reshape(n_flat)
    order = jnp.argsort(flat_e, stable=True).astype(jnp.int32)
    counts = jnp.sum(flat_e[:, None] == jnp.arange(N_EXPERTS, dtype=jnp.int32)[None, :], axis=0, dtype=jnp.int32)
    padded = ((counts + TME - 1) // TME) * TME
    pad_end = jnp.cumsum(padded)
    pad_start = pad_end - padded
    grp_start = jnp.cumsum(counts) - counts
    nused = (pad_end[-1] // TME).astype(jnp.int32).reshape(1)
    tile_row0 = jnp.arange(n_tiles, dtype=jnp.int32) * TME
    tile_e = jnp.minimum(jnp.sum(tile_row0[:, None] >= pad_end[None, :], axis=1), N_EXPERTS - 1).astype(jnp.int32)
    tile_rank0 = tile_row0 - pad_start[tile_e]
    tile_nv = jnp.where(tile_row0 < pad_end[-1], jnp.clip(counts[tile_e] - tile_rank0, 0, TME), 0).astype(jnp.int32)
    within = jnp.arange(TME, dtype=jnp.int32)[None, :]
    valid = within < tile_nv[:, None]
    src_flat = lax.optimization_barrier(order[jnp.clip((grp_start[tile_e] + tile_rank0)[:, None] + within, 0, n_flat - 1)])
    row_token = jnp.where(valid, src_flat // TOP_K_EXPERTS, 0).astype(jnp.int32)
    row_dst = jnp.where(valid, (src_flat % TOP_K_EXPERTS) * S + src_flat // TOP_K_EXPERTS, 0).astype(jnp.int32)
    return tile_e, tile_nv, nused, row_token, row_dst, n_tiles


def _final_kernel(x1_ref, y0_ref, y1_ref, y2_ref, y3_ref, rg_ref, p_ref, wpg_ref, wpp_ref, g_ref, b_ref, o_ref):
    h = DEEPNORM_ALPHA * x1_ref[...]
    rg = rg_ref[...]
    for j, y_ref in enumerate((y0_ref, y1_ref, y2_ref, y3_ref)):
        h = h + rg[:, j:j + 1] * y_ref[:, 0, :]
    ple = _sigmoid(_dot(h.astype(BF16), wpg_ref[...])) * _dot(p_ref[...].astype(BF16), wpp_ref[...])
    o_ref[...] = _layer_norm(h + ple, g_ref[...], b_ref[...])


def _final(x1, ys, rgate, p, wpg, wpp, g, b):
    S = x1.shape[0]
    full = lambda a: pl.BlockSpec(a.shape, lambda i: (0, 0))
    row = lambda n: pl.BlockSpec((TM, n), lambda i: (i, 0))
    return pl.pallas_call(
        _final_kernel,
        grid=(S // TM,),
        in_specs=[row(D_MODEL)]
        + [pl.BlockSpec((TM, 1, D_MODEL), lambda i, j=j: (j * (S // TM) + i, 0, 0)) for j in range(TOP_K_EXPERTS)]
        + [row(LANES), row(PLE_DIM), full(wpg), full(wpp), full(g), full(b)],
        out_specs=row(D_MODEL),
        out_shape=jax.ShapeDtypeStruct((S, D_MODEL), F32),
        compiler_params=_params(("parallel",)),
        name="final",
    )(x1, ys, ys, ys, ys, rgate, p, wpg, wpp, g, b)


def _layer(x, p, w_in, b_in, w_o_attn, w_dw, b_dw, conv_ln_g, conv_ln_b, w_o_conv, w_out, ln1_g, ln1_b,
           w_router, b_router, w_gate_up, b_gate_up, w_down, b_down, w_ple_gate, w_ple_proj, ln2_g, ln2_b,
           rel_bias):
    S = x.shape[0]
    assert S % TM == 0 and S % TA == 0 and S % TKI == 0 and (S * TOP_K_EXPERTS) % TME == 0
    top_k = min(TOPK_MAX, S // 4)
    row2 = lambda a: a.reshape(1, -1).astype(F32)

    o_q, o_qi, o_ki, o_cv, o_g = 0, 3 * ATTN_WIDTH, 3 * ATTN_WIDTH + IDX_HEADS * IDX_DIM, \
        3 * ATTN_WIDTH + IDX_HEADS * IDX_DIM + IDX_DIM + IDX_HEADS, \
        3 * ATTN_WIDTH + IDX_HEADS * IDX_DIM + IDX_DIM + IDX_HEADS + 2 * CONV_CH
    kw_pad = LANES - (IDX_DIM + IDX_HEADS)
    wkw = jnp.pad(w_in[:, o_ki:o_cv], ((0, 0), (0, kw_pad)))
    bkw = jnp.pad(b_in[o_ki:o_cv], (0, kw_pad))
    q, k, v, qi, kw, u, sg = _proj(
        x, w_in[:, o_q:o_qi].astype(BF16), row2(b_in[o_q:o_qi]),
        w_in[:, o_qi:o_ki].astype(BF16), row2(b_in[o_qi:o_ki]),
        wkw.astype(BF16), row2(bkw),
        w_in[:, o_cv:o_g].astype(BF16), row2(b_in[o_cv:o_g]),
        w_in[:, o_g:].astype(BF16), row2(b_in[o_g:]))

    part_conv = _conv(u, sg, w_dw, row2(b_dw), row2(conv_ln_g), row2(conv_ln_b), w_o_conv.astype(BF16))

    ki = kw[:, :IDX_DIM].astype(BF16)
    mask4 = _index_mask(qi, kw[:, IDX_DIM:IDX_DIM + IDX_HEADS].T, jnp.concatenate([ki, ki], axis=1), top_k)
    bias_d, bias_e = _relative_bias_blocks(rel_bias)
    y_attn = _attention(q, k, v, mask4, bias_d, bias_e)

    wr = jnp.pad(w_router, ((0, 0), (0, LANES - N_EXPERTS)))
    wr_hi = wr.astype(BF16)
    wr_lo = (wr - wr_hi.astype(F32)).astype(BF16)
    br = jnp.pad(b_router, (0, LANES - N_EXPERTS), constant_values=-jnp.inf)
    x1, x1p, ridx, rgate = _post(y_attn, sg, part_conv, x, w_o_attn.astype(BF16), w_out.astype(BF16),
                                 row2(ln1_g), row2(ln1_b), wr_hi, wr_lo, row2(br))

    tile_e, tile_nv, nused, row_token, row_dst, n_tiles = _routing_tables(ridx[:, :TOP_K_EXPERTS], S)
    ys = _experts(tile_e, tile_nv, nused, row_token, row_dst, x1p, w_gate_up, b_gate_up.reshape(N_EXPERTS, 1, -1),
                  w_down, b_down.reshape(N_EXPERTS, 1, -1), n_tiles, S * TOP_K_EXPERTS)
    return _final(x1, ys, rgate, p, w_ple_gate.astype(BF16), w_ple_proj.astype(BF16), row2(ln2_g), row2(ln2_b))


def kernel(x, p, w_in, b_in, w_o_attn, w_dw, b_dw, conv_ln_g, conv_ln_b, w_o_conv, w_out, ln1_g, ln1_b, w_router, b_router, w_gate_up, b_gate_up, w_down, b_down, w_ple_gate, w_ple_proj, ln2_g, ln2_b, rel_bias):
    assert x.shape[0] == 1 and p.shape[0] == DEPTH
    out = _layer(x[0], p[0, 0], w_in[0], b_in[0], w_o_attn[0], w_dw[0], b_dw[0], conv_ln_g[0], conv_ln_b[0],
                 w_o_conv[0], w_out[0], ln1_g[0], ln1_b[0], w_router[0], b_router[0], w_gate_up[0], b_gate_up[0],
                 w_down[0], b_down[0], w_ple_gate[0], w_ple_proj[0], ln2_g[0], ln2_b[0], rel_bias)
    return out[None]
```

```python
import functools
import math

import jax
import jax.numpy as jnp
import numpy as np
from jax import lax
from jax.experimental import pallas as pl
from jax.experimental.pallas import tpu as pltpu

F32 = jnp.float32
BF16 = jnp.bfloat16

D_MODEL = 1024
N_HEADS = 8
HEAD_DIM = 64
ATTN_WIDTH = N_HEADS * HEAD_DIM
ATTN_SCALE = HEAD_DIM ** -0.5
LOG2E = math.log2(math.e)
IDX_HEADS = 8
IDX_DIM = 64
IDX_SCALE = (IDX_HEADS ** -0.5) * (IDX_DIM ** -0.5)
TOPK_MAX = 256
CONV_CH = 512
CONV_WIDTH = 31
N_BUCKETS = 32
MAX_DISTANCE = 128
N_EXPERTS = 32
TOP_K_EXPERTS = 4
D_EXPERT = 1024
SWIGLU_LIMIT = 7.0
SWIGLU_ALPHA = 1.702
PLE_DIM = 256
LN_EPS = 1e-5
DEPTH = 1
DEEPNORM_ALPHA = (2 * DEPTH) ** 0.25

LANES = 128
NEG_BIG = -1e30
F32_TINY = float(np.finfo(np.float32).tiny)
VMEM_LIMIT = 56 * 1024 * 1024

TM = 512
TQI = 256
TKI = 512
WARM_PASSES = 5
EDGE_FIRST = 9
SEARCH_CAP = 128
TA = 512
SUB = 128
TME = 512
DMA_UNROLL = 8
CONV_HALO = 32


def _params(sem, vmem=VMEM_LIMIT):
    return pltpu.CompilerParams(dimension_semantics=sem, vmem_limit_bytes=vmem)


def _sigmoid(x):
    return 1.0 / (1.0 + jnp.exp(-x))


def _layer_norm(x, g, b):
    mu = jnp.mean(x, axis=-1, keepdims=True)
    xc = x - mu
    var = jnp.mean(xc * xc, axis=-1, keepdims=True)
    return xc * lax.rsqrt(var + LN_EPS) * g + b


def _dot(a, b):
    return jnp.dot(a, b, preferred_element_type=F32)


def _dot_nt(a, b):
    return lax.dot_general(a, b, (((1,), (1,)), ((), ())), preferred_element_type=F32)


def _proj_kernel(x_ref, wqkv_ref, bqkv_ref, wqi_ref, bqi_ref, wkw_ref, bkw_ref, wcv_ref, bcv_ref,
                 wg_ref, bg_ref, q_ref, k_ref, v_ref, qi_ref, kw_ref, u_ref, sg_ref):
    xb = x_ref[...].astype(BF16)
    qkv = _dot(xb, wqkv_ref[...]) + bqkv_ref[...]
    q_ref[...] = (qkv[:, :ATTN_WIDTH] * (ATTN_SCALE * LOG2E)).astype(BF16)
    k_ref[...] = qkv[:, ATTN_WIDTH:2 * ATTN_WIDTH].astype(BF16)
    v_ref[...] = qkv[:, 2 * ATTN_WIDTH:].astype(BF16)
    qi_ref[...] = (_dot(xb, wqi_ref[...]) + bqi_ref[...]).astype(BF16)
    kw = _dot(xb, wkw_ref[...]) + bkw_ref[...]
    lane = lax.broadcasted_iota(jnp.int32, kw.shape, 1)
    kw_ref[...] = jnp.where(lane >= IDX_DIM, kw * IDX_SCALE, kw)
    cv = _dot(xb, wcv_ref[...]) + bcv_ref[...]
    u_ref[...] = cv[:, :CONV_CH] * _sigmoid(cv[:, CONV_CH:])
    sg_ref[...] = _sigmoid(_dot(xb, wg_ref[...]) + bg_ref[...])


def _proj(x, wqkv, bqkv, wqi, bqi, wkw, bkw, wcv, bcv, wg, bg):
    S = x.shape[0]
    full = lambda a: pl.BlockSpec(a.shape, lambda i: (0, 0))
    row = lambda n: pl.BlockSpec((TM, n), lambda i: (i, 0))
    outs = [(ATTN_WIDTH, BF16)] * 3 + [(IDX_HEADS * IDX_DIM, BF16), (LANES, F32), (CONV_CH, F32),
                                        (2 * D_MODEL, F32)]
    return pl.pallas_call(
        _proj_kernel,
        grid=(S // TM,),
        in_specs=[row(D_MODEL)] + [full(a) for a in (wqkv, bqkv, wqi, bqi, wkw, bkw, wcv, bcv, wg, bg)],
        out_specs=[row(n) for n, _ in outs],
        out_shape=[jax.ShapeDtypeStruct((S, n), dt) for n, dt in outs],
        compiler_params=_params(("parallel",)),
        name="proj",
    )(x, wqkv, bqkv, wqi, bqi, wkw, bkw, wcv, bcv, wg, bg)


def _conv_kernel(u_ref, wdw_ref, bdw_ref, lng_ref, lnb_ref, wo_ref, sgc_ref, o_ref, buf_ref, sh_ref):
    @pl.when(pl.program_id(0) == 0)
    def _():
        buf_ref[0:CONV_HALO, :] = jnp.zeros((CONV_HALO, CONV_CH), F32)

    buf_ref[CONV_HALO:CONV_HALO + TM, :] = u_ref[...]
    base = CONV_HALO - (CONV_WIDTH - 1)
    acc = jnp.zeros((TM, CONV_CH), F32) + bdw_ref[...]
    sub = 8
    for r in range(sub):
        taps = [j for j in range(CONV_WIDTH) if (base + j) % sub == r]
        span = max(base + j - r for j in taps) + TM
        sh_ref[0:span, :] = buf_ref[r:r + span, :]
        for j in taps:
            off = base + j - r
            acc = acc + wdw_ref[j:j + 1, :] * sh_ref[off:off + TM, :]
    buf_ref[0:CONV_HALO, :] = buf_ref[TM:TM + CONV_HALO, :]
    y = _layer_norm(acc, lng_ref[...], lnb_ref[...])
    y = y * _sigmoid(y)
    o_ref[...] = sgc_ref[...] * _dot(y.astype(BF16), wo_ref[...])


def _conv(u, sg, wdw, bdw, lng, lnb, wo):
    S = u.shape[0]
    full = lambda a: pl.BlockSpec(a.shape, lambda i: (0, 0))
    return pl.pallas_call(
        _conv_kernel,
        grid=(S // TM,),
        in_specs=[pl.BlockSpec((TM, CONV_CH), lambda i: (i, 0)), full(wdw), full(bdw), full(lng), full(lnb),
                  full(wo), pl.BlockSpec((TM, D_MODEL), lambda i: (i, 1))],
        out_specs=pl.BlockSpec((TM, D_MODEL), lambda i: (i, 0)),
        out_shape=jax.ShapeDtypeStruct((S, D_MODEL), F32),
        scratch_shapes=[pltpu.VMEM((TM + CONV_HALO, CONV_CH), F32), pltpu.VMEM((TM + CONV_HALO, CONV_CH), F32)],
        compiler_params=_params(("arbitrary",)),
        name="conv",
    )(u, wdw, bdw, lng, lnb, wo, sg)


def _float_key(f):
    b = lax.bitcast_convert_type(f, jnp.int32)
    return b ^ ((b >> 31) & jnp.int32(0x7FFFFFFF))


def _key_float(k):
    b = k ^ ((k >> 31) & jnp.int32(0x7FFFFFFF))
    return lax.bitcast_convert_type(b, F32)


def _index_kernel(qi_ref, wt_ref, ki_ref, mask_ref, sc_ref, qh_ref, *, top_k):
    i = pl.program_id(0)
    nkt = (i * TQI + TQI + TKI - 1) // TKI
    ntile = sc_ref.shape[0]
    q_g = i * TQI + lax.broadcasted_iota(jnp.int32, (1, TQI), 1)
    lane = lax.broadcasted_iota(jnp.int32, (TQI, LANES), 1)

    for h in range(IDX_HEADS):
        qp = qi_ref[:, (h // 2) * LANES:(h // 2 + 1) * LANES]
        keep = (lane < IDX_DIM) if h % 2 == 0 else (lane >= IDX_DIM)
        qh_ref[h] = jnp.where(keep, qp, jnp.zeros_like(qp))

    def score_tile(kt, causal):
        kb = ki_ref[pl.ds(pl.multiple_of(kt * TKI, TKI), TKI), :]
        acc = jnp.zeros((TKI, TQI), F32)
        for h in range(IDX_HEADS):
            acc = acc + wt_ref[h:h + 1, :] * jnp.maximum(_dot_nt(kb, qh_ref[h]), 0.0)
        if causal:
            key_g = kt * TKI + lax.broadcasted_iota(jnp.int32, (TKI, TQI), 0)
            ok = key_g <= q_g
            lo_src = jnp.where(ok, acc, jnp.inf)
            acc = jnp.where(ok, acc, -jnp.inf)
        else:
            lo_src = acc
        sc_ref[kt] = acc
        return jnp.max(acc, axis=0, keepdims=True), jnp.min(lo_src, axis=0, keepdims=True)

    def score_body(kt, carry):
        mx, mn = carry
        tmx, tmn = score_tile(kt, False)
        return jnp.maximum(mx, tmx), jnp.minimum(mn, tmn)

    mx0 = jnp.full((1, TQI), -jnp.inf, F32)
    mn0 = jnp.full((1, TQI), jnp.inf, F32)
    mx, mn = lax.fori_loop(0, nkt - 1, score_body, (mx0, mn0))
    tmx, tmn = score_tile(nkt - 1, True)
    mx = jnp.maximum(mx, tmx)
    mn = jnp.minimum(mn, tmn)
    kf = float(top_k)
    sub = 8

    def count(pivot, strict):
        def tile(kt, cnt):
            for r in range(TKI // sub):
                blk = sc_ref[kt, r * sub:(r + 1) * sub, :]
                hit = (blk > pivot) if strict else (blk >= pivot)
                cnt = cnt + jnp.where(hit, 1.0, 0.0)
            return cnt

        def pair(j, cnts):
            return tile(2 * j, cnts[0]), tile(2 * j + 1, cnts[1])

        zero = jnp.zeros((sub, TQI), F32)
        c0, c1 = lax.fori_loop(0, nkt // 2, pair, (zero, zero))
        c0 = lax.fori_loop(2 * (nkt // 2), nkt, tile, c0)
        return jnp.sum(c0 + c1, axis=0, keepdims=True)

    n_causal = (q_g + 1).astype(F32)
    all_sel = n_causal <= kf
    zeros = jnp.zeros((1, TQI), F32)
    state0 = dict(
        it=jnp.int32(0),
        lo=mn, hi=_key_float(_float_key(mx) + 1), clo=n_causal, chi=zeros,
        glo=jnp.log(jnp.maximum(n_causal, kf + 1.0) / kf), ghi=jnp.full((1, TQI), math.log(0.5 / kf), F32),
        thr=jnp.where(all_sel, NEG_BIG, 0.0).astype(F32),
        done=all_sel.astype(F32), tie=zeros, side=zeros, forced=zeros, use_forced=zeros,
    )

    def above(x):
        return jnp.where(jnp.abs(x) < F32_TINY, F32_TINY, _key_float(_float_key(x) + 1))

    def bracket_edges(lo, hi):
        def tile(kt, carry):
            a, b = carry
            for r in range(TKI // sub):
                blk = sc_ref[kt, r * sub:(r + 1) * sub, :]
                a = jnp.maximum(a, jnp.where(blk < hi, blk, -jnp.inf))
                b = jnp.minimum(b, jnp.where(blk >= lo, blk, jnp.inf))
            return a, b
        a, b = lax.fori_loop(0, nkt, tile, (jnp.full((sub, TQI), -jnp.inf, F32), jnp.full((sub, TQI), jnp.inf, F32)))
        return jnp.max(a, axis=0, keepdims=True), jnp.min(b, axis=0, keepdims=True)

    def cond(st):
        return jnp.logical_and(st["it"] < SEARCH_CAP, jnp.min(st["done"]) < 0.5)

    def count_step(st):
        it, lo, hi, glo, ghi = st["it"], st["lo"], st["hi"], st["glo"], st["ghi"]
        lo_k, hi_k = _float_key(lo), _float_key(hi)
        above_lo = above(lo)
        below_hi = jnp.where(hi == F32_TINY, 0.0, _key_float(hi_k - 1))
        probe = jnp.logical_and(st["done"] < 0.5, above_lo < hi)
        tie_now = jnp.logical_and(st["done"] < 0.5, above_lo >= hi)
        frac = jnp.where(it < 24, glo / (glo - ghi), 0.5)
        pf = lo + (hi - lo) * frac
        pf = jnp.where(it < 64, pf, _key_float((lo_k >> 1) + (hi_k >> 1) + (lo_k & hi_k & 1)))
        pf = jnp.where(it == 0, 0.0, jnp.where(it == 1, F32_TINY, pf))
        pf = jnp.where(st["use_forced"] > 0.5, st["forced"], pf)
        pf = jnp.where(probe, jnp.minimum(jnp.maximum(pf, above_lo), below_hi), lo)
        c = count(pf, False)
        hit = jnp.logical_and(probe, c == kf)
        up = jnp.logical_and(probe, c > kf)
        dn = jnp.logical_and(probe, c < kf)
        g = jnp.log(jnp.maximum(c, 0.5) / kf)
        return dict(
            it=it + 1,
            lo=jnp.where(up, pf, lo), hi=jnp.where(dn, pf, hi),
            clo=jnp.where(up, c, st["clo"]), chi=jnp.where(dn, c, st["chi"]),
            glo=jnp.where(up, g, jnp.where(jnp.logical_and(dn, st["side"] < -0.5), 0.5 * glo, glo)),
            ghi=jnp.where(dn, g, jnp.where(jnp.logical_and(up, st["side"] > 0.5), 0.5 * ghi, ghi)),
            thr=jnp.where(hit, pf, jnp.where(tie_now, lo, st["thr"])),
            done=jnp.where(jnp.logical_or(hit, tie_now), 1.0, st["done"]),
            tie=jnp.where(tie_now, 1.0, st["tie"]),
            side=jnp.where(up, 1.0, jnp.where(dn, -1.0, st["side"])),
            forced=st["forced"], use_forced=zeros,
        )

    def edge_step(st):
        lo, hi = st["lo"], st["hi"]
        probe = jnp.logical_and(st["done"] < 0.5, above(lo) < hi)
        a, b = bracket_edges(lo, hi)
        one_above = st["chi"] == kf - 1.0
        one_below = st["clo"] == kf + 1.0
        new = dict(st)
        new.update(
            it=st["it"] + 1,
            lo=jnp.where(probe, b, lo), hi=jnp.where(probe, above(a), hi),
            forced=jnp.where(one_above, a, above(b)),
            use_forced=jnp.where(jnp.logical_and(probe, jnp.logical_or(one_above, one_below)), 1.0, 0.0),
        )
        return new

    def step(st):
        it = st["it"]
        is_edge = jnp.logical_and(it >= EDGE_FIRST, (it - EDGE_FIRST) % 3 == 0)
        return lax.cond(is_edge, edge_step, count_step, st)

    st = lax.fori_loop(0, WARM_PASSES, lambda _, s: count_step(s), state0)
    st = lax.while_loop(cond, step, st)
    thr = st["thr"]
    tie = st["tie"]
    any_tie = jnp.max(tie) > 0.5

    def emit(kt, sel):
        mask_ref[0, kt] = jnp.where(sel, 0.0, NEG_BIG).astype(mask_ref.dtype).T

    @pl.when(jnp.logical_not(any_tie))
    def _():
        def body(kt, carry):
            emit(kt, sc_ref[kt] >= thr)
            return carry
        lax.fori_loop(0, nkt, body, 0)

    @pl.when(any_tie)
    def _():
        need = jnp.where(tie > 0.5, kf - count(thr, True), float(2 * ntile * TKI))
        r = lax.broadcasted_iota(jnp.int32, (TKI, TKI), 0)
        c = lax.broadcasted_iota(jnp.int32, (TKI, TKI), 1)
        prefix = jnp.where(c <= r, 1.0, 0.0).astype(BF16)

        def body(kt, seen):
            s = sc_ref[kt]
            eq = s == thr
            rank = seen + _dot(prefix, jnp.where(eq, 1.0, 0.0).astype(BF16))
            emit(kt, jnp.logical_or(s > thr, jnp.logical_and(eq, rank <= need)))
            return seen + jnp.sum(jnp.where(eq, 1.0, 0.0), axis=0, keepdims=True)
        lax.fori_loop(0, nkt, body, jnp.zeros((1, TQI), F32))

    def fill(kt, carry):
        mask_ref[0, kt] = jnp.full((TQI, TKI), NEG_BIG, mask_ref.dtype)
        return carry
    lax.fori_loop(nkt, ntile, fill, 0)


def _index_mask(qi, wt, ki2, top_k):
    S = qi.shape[0]
    nq, nk = S // TQI, S // TKI
    return pl.pallas_call(
        functools.partial(_index_kernel, top_k=top_k),
        grid=(nq,),
        in_specs=[pl.BlockSpec((TQI, IDX_HEADS * IDX_DIM), lambda i: (i, 0)),
                  pl.BlockSpec((IDX_HEADS, TQI), lambda i: (0, i)),
                  pl.BlockSpec((S, LANES), lambda i: (0, 0))],
        out_specs=pl.BlockSpec((1, nk, TQI, TKI), lambda i: (i, 0, 0, 0)),
        out_shape=jax.ShapeDtypeStruct((nq, nk, TQI, TKI), BF16),
        scratch_shapes=[pltpu.VMEM((nk, TKI, TQI), F32),
                        pltpu.VMEM((IDX_HEADS, TQI, LANES), BF16)],
        compiler_params=_params(("parallel",)),
        name="index_mask",
    )(qi, wt, ki2)


def _attn_kernel(qt_ref, kt_ref, q_ref, k_ref, v_ref, m_ref, bd_ref, be_ref, o_ref, *state):
    m_sc, acc_sc = state[:N_HEADS], state[N_HEADS:]
    step = pl.program_id(0)
    qi = qt_ref[step]
    ki = kt_ref[step]
    nsub = TA // SUB

    @pl.when(ki == 0)
    def _():
        for h in range(N_HEADS):
            m_sc[h][...] = jnp.full((TA, LANES), NEG_BIG, F32)
            acc_sc[h][...] = jnp.zeros((TA, LANES), F32)

    lane = lax.broadcasted_iota(jnp.int32, (TA, LANES), 1)
    first = lane < HEAD_DIM

    def bias_tile(h, diagonal):
        zero = jnp.zeros((SUB, SUB), F32)
        rows = []
        for a in range(nsub):
            if diagonal:
                blks = [bd_ref[h] if b == a else (be_ref[h] if b == a - 1 else zero) for b in range(nsub)]
            else:
                blks = [be_ref[h] if (a == 0 and b == nsub - 1) else zero for b in range(nsub)]
            rows.append(jnp.concatenate(blks, axis=1))
        return jnp.concatenate(rows, axis=0)

    def update(near, diagonal):
        maskf = jnp.concatenate([m_ref[a, 0] for a in range(TA // TQI)], axis=0).astype(F32)

        def logits(h):
            cols = slice((h // 2) * LANES, (h // 2 + 1) * LANES)
            qp = q_ref[:, cols]
            mine = first if h % 2 == 0 else jnp.logical_not(first)
            s = _dot_nt(jnp.where(mine, qp, jnp.zeros_like(qp)), k_ref[:, cols]) + maskf
            return s + bias_tile(h, diagonal) if near else s

        s = logits(0)
        for h in range(N_HEADS):
            s_next = logits(h + 1) if h + 1 < N_HEADS else None
            vp = v_ref[:, (h // 2) * LANES:(h // 2 + 1) * LANES]
            mine = first if h % 2 == 0 else jnp.logical_not(first)
            vh = jnp.where(mine, vp, jnp.ones_like(vp))
            m_prev = m_sc[h][...]
            m_next = jnp.maximum(m_prev, jnp.max(s, axis=1, keepdims=True))
            pexp = jnp.exp2(s - jnp.concatenate([m_next] * (TA // LANES), axis=1)).astype(BF16)
            acc_sc[h][...] = jnp.exp2(m_prev - m_next) * acc_sc[h][...] + _dot(pexp, vh)
            m_sc[h][...] = m_next
            s = s_next

    @pl.when(ki < qi - 1)
    def _():
        update(False, False)

    @pl.when(ki == qi - 1)
    def _():
        update(True, False)

    @pl.when(ki == qi)
    def _():
        update(True, True)
        for p in range(N_HEADS // 2):
            a0, a1 = acc_sc[2 * p][...], acc_sc[2 * p + 1][...]
            d0 = pltpu.roll(a0, HEAD_DIM, axis=1)
            d1 = pltpu.roll(a1, HEAD_DIM, axis=1)
            o_ref[:, p * LANES:(p + 1) * LANES] = jnp.where(first, a0 / d0, a1 / d1).astype(o_ref.dtype)


def _attention(q, k, v, mask4, bias_d, bias_e):
    S = q.shape[0]
    nb = S // TA
    pairs = [(a, b) for a in range(nb) for b in range(a + 1)]
    qtab = jnp.asarray(np.array([a for a, _ in pairs], np.int32))
    ktab = jnp.asarray(np.array([b for _, b in pairs], np.int32))
    grid_spec = pltpu.PrefetchScalarGridSpec(
        num_scalar_prefetch=2,
        grid=(len(pairs),),
        in_specs=[pl.BlockSpec((TA, ATTN_WIDTH), lambda s, qt, kt: (qt[s], 0)),
                  pl.BlockSpec((TA, ATTN_WIDTH), lambda s, qt, kt: (kt[s], 0)),
                  pl.BlockSpec((TA, ATTN_WIDTH), lambda s, qt, kt: (kt[s], 0)),
                  pl.BlockSpec((TA // TQI, TA // TKI, TQI, TKI), lambda s, qt, kt: (qt[s], kt[s], 0, 0)),
                  pl.BlockSpec(bias_d.shape, lambda s, qt, kt: (0, 0, 0)),
                  pl.BlockSpec(bias_e.shape, lambda s, qt, kt: (0, 0, 0))],
        out_specs=pl.BlockSpec((TA, ATTN_WIDTH), lambda s, qt, kt: (qt[s], 0)),
        scratch_shapes=[pltpu.VMEM((TA, LANES), F32)] * (2 * N_HEADS),
    )
    return pl.pallas_call(
        _attn_kernel,
        grid_spec=grid_spec,
        out_shape=jax.ShapeDtypeStruct((S, ATTN_WIDTH), BF16),
        compiler_params=_params(("arbitrary",)),
        name="attn",
    )(qtab, ktab, q, k, v, mask4, bias_d, bias_e)


def _relative_bias_blocks(rel_bias):
    dist = jnp.arange(2 * SUB, dtype=jnp.int32)
    max_exact = N_BUCKETS // 2
    dist_f = jnp.maximum(dist, 1).astype(F32)
    large = max_exact + (jnp.log(dist_f / max_exact) / math.log(MAX_DISTANCE / max_exact)
                         * (N_BUCKETS - max_exact)).astype(jnp.int32)
    bucket = jnp.where(dist < max_exact, dist, jnp.minimum(large, N_BUCKETS - 1))
    table = ((rel_bias[bucket] - rel_bias[N_BUCKETS - 1]) * LOG2E).astype(F32)
    i = jnp.arange(SUB)[:, None]
    j = jnp.arange(SUB)[None, :]

    def toeplitz(d):
        onehot = (d[:, :, None] == dist[None, None, :]).astype(F32)
        return jnp.einsum("ijd,dh->hij", onehot, table, precision=lax.Precision.HIGHEST)

    return toeplitz(jnp.clip(i - j, 0, 2 * SUB - 1)), toeplitz(SUB + i - j)


def _pack_bf16_pair(a, b):
    def rnd(x):
        bits = lax.bitcast_convert_type(x, jnp.uint32)
        return bits + jnp.uint32(0x7FFF) + ((bits >> 16) & jnp.uint32(1))
    return (rnd(a) >> 16) | (rnd(b) & jnp.uint32(0xFFFF0000))


def _unpack_bf16_pair(p):
    lo = lax.bitcast_convert_type(p << 16, F32)
    hi = lax.bitcast_convert_type(p & jnp.uint32(0xFFFF0000), F32)
    return lo.astype(BF16), hi.astype(BF16)


def _post_kernel(ya_ref, sga_ref, pc_ref, x_ref, woa_ref, wout_ref, g_ref, b_ref, wr_ref, wrl_ref, br_ref,
                 x1_ref, x1p_ref, ridx_ref, rgate_ref):
    y_attn = _dot(ya_ref[...], woa_ref[...])
    merged = sga_ref[...] * y_attn + pc_ref[...]
    mix = _dot(merged.astype(BF16), wout_ref[...])
    x1 = _layer_norm(DEEPNORM_ALPHA * x_ref[...] + mix, g_ref[...], b_ref[...])
    x1_ref[...] = x1
    half = D_MODEL // 2
    x1p_ref[:, 0, :] = _pack_bf16_pair(x1[:, :half], x1[:, half:])

    x1_hi = x1.astype(BF16)
    x1_lo = (x1 - x1_hi.astype(F32)).astype(BF16)
    logits = (_dot(x1_hi, wr_ref[...]) + (_dot(x1_hi, wrl_ref[...]) + _dot(x1_lo, wr_ref[...]))) + br_ref[...]
    lane = lax.broadcasted_iota(jnp.int32, logits.shape, 1).astype(F32)
    cur = logits
    vals, idxs = [], []
    for _ in range(TOP_K_EXPERTS):
        m = jnp.max(cur, axis=1, keepdims=True)
        ix = jnp.min(jnp.where(cur == m, lane, float(LANES)), axis=1, keepdims=True)
        vals.append(m)
        idxs.append(ix)
        cur = jnp.where(lane == ix, -jnp.inf, cur)
    exps = [jnp.exp(v - vals[0]) for v in vals]
    denom = exps[0]
    for e in exps[1:]:
        denom = denom + e
    ridx = jnp.zeros_like(logits)
    rgate = jnp.zeros_like(logits)
    for j in range(TOP_K_EXPERTS):
        ridx = jnp.where(lane == float(j), idxs[j], ridx)
        rgate = jnp.where(lane == float(j), exps[j] / denom, rgate)
    ridx_ref[...] = ridx.astype(jnp.int32)
    rgate_ref[...] = rgate


def _post(ya, sg, pc, x, woa, wout, g, b, wr, wrl, br):
    S = x.shape[0]
    full = lambda a: pl.BlockSpec(a.shape, lambda i: (0, 0))
    row = lambda n: pl.BlockSpec((TM, n), lambda i: (i, 0))
    return pl.pallas_call(
        _post_kernel,
        grid=(S // TM,),
        in_specs=[row(ATTN_WIDTH), row(D_MODEL), row(D_MODEL), row(D_MODEL)] + [full(a) for a in (woa, wout, g, b, wr, wrl, br)],
        out_specs=[row(D_MODEL), pl.BlockSpec((TM, 1, D_MODEL // 2), lambda i: (i, 0, 0)), row(LANES), row(LANES)],
        out_shape=[jax.ShapeDtypeStruct((S, D_MODEL), F32), jax.ShapeDtypeStruct((S, 1, D_MODEL // 2), jnp.uint32),
                   jax.ShapeDtypeStruct((S, LANES), jnp.int32), jax.ShapeDtypeStruct((S, LANES), F32)],
        compiler_params=_params(("parallel",)),
        name="post",
    )(ya, sg, pc, x, woa, wout, g, b, wr, wrl, br)


def _expert_kernel(te_ref, nv_ref, nused_ref, tok_ref, tokn_ref, dst_ref, x_hbm, wgu_ref, bgu_ref, wd_ref, bd_ref,
                   ys_hbm, xs_buf, y_buf, wgu_sc, wd_sc, gsem, ssem):
    t = pl.program_id(0)
    nused = nused_ref[0]
    cur = t % 2
    nxt = 1 - cur

    def start_gather(idx_ref, slot):
        def body(r, carry):
            pltpu.make_async_copy(x_hbm.at[idx_ref[0, 0, r]], xs_buf.at[slot, pl.ds(r, 1)], gsem.at[slot]).start()
            return carry
        lax.fori_loop(0, TME, body, 0, unroll=DMA_UNROLL)

    def wait_gather(slot):
        pltpu.make_async_copy(xs_buf.at[slot], xs_buf.at[slot], gsem.at[slot]).wait()

    def wait_scatter(slot, n):
        pltpu.make_async_copy(ys_hbm.at[pl.ds(0, n)], ys_hbm.at[pl.ds(0, n)], ssem.at[slot]).wait()

    @pl.when(t == 0)
    def _():
        start_gather(tok_ref, 0)

    @pl.when(t + 1 < nused)
    def _():
        start_gather(tokn_ref, nxt)

    e = te_ref[t]
    e_prev = te_ref[jnp.maximum(t - 1, 0)]

    @pl.when(jnp.logical_or(t == 0, e != e_prev))
    def _():
        wgu_sc[...] = wgu_ref[0].astype(BF16)
        wd_sc[...] = wd_ref[0].astype(BF16)

    @pl.when(t < nused)
    def _():
        wait_gather(cur)

        @pl.when(t >= 2)
        def _():
            wait_scatter(cur, nv_ref[jnp.maximum(t - 2, 0)])

        half = D_MODEL // 2
        lo, hi = _unpack_bf16_pair(xs_buf[cur])
        gu = _dot(lo, wgu_sc[0:half, :]) + _dot(hi, wgu_sc[half:, :]) + bgu_ref[0]
        g = jnp.minimum(gu[:, :D_EXPERT], SWIGLU_LIMIT)
        u = jnp.clip(gu[:, D_EXPERT:], -SWIGLU_LIMIT, SWIGLU_LIMIT)
        act = (u + 1.0) * (g * _sigmoid(SWIGLU_ALPHA * g))
        y_buf[cur] = _dot(act.astype(BF16), wd_sc[...]) + bd_ref[0]

        def scatter(r, carry):
            pltpu.make_async_copy(y_buf.at[cur, pl.ds(r, 1)], ys_hbm.at[dst_ref[0, 0, r]], ssem.at[cur]).start()
            return carry

        def scatter_group(c, carry):
            for j in range(DMA_UNROLL):
                scatter(c * DMA_UNROLL + j, carry)
            return carry
        n_groups = nv_ref[t] // DMA_UNROLL
        lax.fori_loop(0, n_groups, scatter_group, 0)
        lax.fori_loop(n_groups * DMA_UNROLL, nv_ref[t], scatter, 0)

        @pl.when(t == nused - 1)
        def _():
            @pl.when(t >= 1)
            def _():
                wait_scatter(nxt, nv_ref[jnp.maximum(t - 1, 0)])
            wait_scatter(cur, nv_ref[t])


def _experts(tile_e, tile_nv, nused, row_token, row_dst, x1p, wgu, bgu, wd, bd, n_tiles, n_dst):
    idx = lambda a: a.reshape(n_tiles, 1, TME)
    smem_tile = lambda f: pl.BlockSpec((1, 1, TME), f, memory_space=pltpu.SMEM)
    grid_spec = pltpu.PrefetchScalarGridSpec(
        num_scalar_prefetch=3,
        grid=(n_tiles,),
        in_specs=[smem_tile(lambda t, te, nv, nu: (t, 0, 0)),
                  smem_tile(lambda t, te, nv, nu: (jnp.minimum(t + 1, n_tiles - 1), 0, 0)),
                  smem_tile(lambda t, te, nv, nu: (t, 0, 0)),
                  pl.BlockSpec(memory_space=pl.ANY),
                  pl.BlockSpec((1, D_MODEL, 2 * D_EXPERT), lambda t, te, nv, nu: (te[t], 0, 0)),
                  pl.BlockSpec((1, 1, 2 * D_EXPERT), lambda t, te, nv, nu: (te[t], 0, 0)),
                  pl.BlockSpec((1, D_EXPERT, D_MODEL), lambda t, te, nv, nu: (te[t], 0, 0)),
                  pl.BlockSpec((1, 1, D_MODEL), lambda t, te, nv, nu: (te[t], 0, 0))],
        out_specs=pl.BlockSpec(memory_space=pl.ANY),
        scratch_shapes=[pltpu.VMEM((2, TME, D_MODEL // 2), jnp.uint32), pltpu.VMEM((2, TME, D_MODEL), F32),
                        pltpu.VMEM((D_MODEL, 2 * D_EXPERT), BF16), pltpu.VMEM((D_EXPERT, D_MODEL), BF16),
                        pltpu.SemaphoreType.DMA((2,)), pltpu.SemaphoreType.DMA((2,))],
    )
    return pl.pallas_call(
        _expert_kernel,
        grid_spec=grid_spec,
        out_shape=jax.ShapeDtypeStruct((n_dst, 1, D_MODEL), F32),
        compiler_params=_params(("arbitrary",)),
        name="moe_experts",
    )(tile_e, tile_nv, nused, idx(row_token), idx(row_token), idx(row_dst), x1p, wgu, bgu, wd, bd)


def _routing_tables(ridx, S):
    n_flat = S * TOP_K_EXPERTS
    n_tiles = n_flat // TME + N_EXPERTS
    flat_e = ridx.reshape(n_flat)
    order = jnp.argsort(flat_e, stable=True).astype(jnp.int32)
    counts = jnp.sum(flat_e[:, None] == jnp.arange(N_EXPERTS, dtype=jnp.int32)[None, :], axis=0, dtype=jnp.int32)
    padded = ((counts + TME - 1) // TME) * TME
    pad_end = jnp.cumsum(padded)
    pad_start = pad_end - padded
    grp_start = jnp.cumsum(counts) - counts
    nused = (pad_end[-1] // TME).astype(jnp.int32).reshape(1)
    tile_row0 = jnp.arange(n_tiles, dtype=jnp.int32) * TME
    tile_e = jnp.minimum(jnp.sum(tile_row0[:, None] >= pad_end[None, :], axis=1), N_EXPERTS - 1).astype(jnp.int32)
    tile_rank0 = tile_row0 - pad_start[tile_e]
    tile_nv = jnp.where(tile_row0 < pad_end[-1], jnp.clip(counts[tile_e] - tile_rank0, 0, TME), 0).astype(jnp.int32)
    within = jnp.arange(TME, dtype=jnp.int32)[None, :]
    valid = within < tile_nv[:, None]
    src_flat = lax.optimization_barrier(order[jnp.clip((grp_start[tile_e] + tile_rank0)[:, None] + within, 0, n_flat - 1)])
    row_token = jnp.where(valid, src_flat // TOP_K_EXPERTS, 0).astype(jnp.int32)
    row_dst = jnp.where(valid, (src_flat % TOP_K_EXPERTS) * S + src_flat // TOP_K_EXPERTS, 0).astype(jnp.int32)
    return tile_e, tile_nv, nused, row_token, row_dst, n_tiles


def _final_kernel(x1_ref, y0_ref, y1_ref, y2_ref, y3_ref, rg_ref, p_ref, wpg_ref, wpp_ref, g_ref, b_ref, o_ref):
    h = DEEPNORM_ALPHA * x1_ref[...]
    rg = rg_ref[...]
    for j, y_ref in enumerate((y0_ref, y1_ref, y2_ref, y3_ref)):
        h = h + rg[:, j:j + 1] * y_ref[:, 0, :]
    ple = _sigmoid(_dot(h.astype(BF16), wpg_ref[...])) * _dot(p_ref[...].astype(BF16), wpp_ref[...])
    o_ref[...] = _layer_norm(h + ple, g_ref[...], b_ref[...])


def _final(x1, ys, rgate, p, wpg, wpp, g, b):
    S = x1.shape[0]
    full = lambda a: pl.BlockSpec(a.shape, lambda i: (0, 0))
    row = lambda n: pl.BlockSpec((TM, n), lambda i: (i, 0))
    return pl.pallas_call(
        _final_kernel,
        grid=(S // TM,),
        in_specs=[row(D_MODEL)]
        + [pl.BlockSpec((TM, 1, D_MODEL), lambda i, j=j: (j * (S // TM) + i, 0, 0)) for j in range(TOP_K_EXPERTS)]
        + [row(LANES), row(PLE_DIM), full(wpg), full(wpp), full(g), full(b)],
        out_specs=row(D_MODEL),
        out_shape=jax.ShapeDtypeStruct((S, D_MODEL), F32),
        compiler_params=_params(("parallel",)),
        name="final",
    )(x1, ys, ys, ys, ys, rgate, p, wpg, wpp, g, b)


def _layer(x, p, w_in, b_in, w_o_attn, w_dw, b_dw, conv_ln_g, conv_ln_b, w_o_conv, w_out, ln1_g, ln1_b,
           w_router, b_router, w_gate_up, b_gate_up, w_down, b_down, w_ple_gate, w_ple_proj, ln2_g, ln2_b,
           rel_bias):
    S = x.shape[0]
    assert S % TM == 0 and S % TA == 0 and S % TKI == 0 and (S * TOP_K_EXPERTS) % TME == 0
    top_k = min(TOPK_MAX, S // 4)
    row2 = lambda a: a.reshape(1, -1).astype(F32)

    o_q, o_qi, o_ki, o_cv, o_g = 0, 3 * ATTN_WIDTH, 3 * ATTN_WIDTH + IDX_HEADS * IDX_DIM, \
        3 * ATTN_WIDTH + IDX_HEADS * IDX_DIM + IDX_DIM + IDX_HEADS, \
        3 * ATTN_WIDTH + IDX_HEADS * IDX_DIM + IDX_DIM + IDX_HEADS + 2 * CONV_CH
    kw_pad = LANES - (IDX_DIM + IDX_HEADS)
    wkw = jnp.pad(w_in[:, o_ki:o_cv], ((0, 0), (0, kw_pad)))
    bkw = jnp.pad(b_in[o_ki:o_cv], (0, kw_pad))
    q, k, v, qi, kw, u, sg = _proj(
        x, w_in[:, o_q:o_qi].astype(BF16), row2(b_in[o_q:o_qi]),
        w_in[:, o_qi:o_ki].astype(BF16), row2(b_in[o_qi:o_ki]),
        wkw.astype(BF16), row2(bkw),
        w_in[:, o_cv:o_g].astype(BF16), row2(b_in[o_cv:o_g]),
        w_in[:, o_g:].astype(BF16), row2(b_in[o_g:]))

    part_conv = _conv(u, sg, w_dw, row2(b_dw), row2(conv_ln_g), row2(conv_ln_b), w_o_conv.astype(BF16))

    ki = kw[:, :IDX_DIM].astype(BF16)
    mask4 = _index_mask(qi, kw[:, IDX_DIM:IDX_DIM + IDX_HEADS].T, jnp.concatenate([ki, ki], axis=1), top_k)
    bias_d, bias_e = _relative_bias_blocks(rel_bias)
    y_attn = _attention(q, k, v, mask4, bias_d, bias_e)

    wr = jnp.pad(w_router, ((0, 0), (0, LANES - N_EXPERTS)))
    wr_hi = wr.astype(BF16)
    wr_lo = (wr - wr_hi.astype(F32)).astype(BF16)
    br = jnp.pad(b_router, (0, LANES - N_EXPERTS), constant_values=-jnp.inf)
    x1, x1p, ridx, rgate = _post(y_attn, sg, part_conv, x, w_o_attn.astype(BF16), w_out.astype(BF16),
                                 row2(ln1_g), row2(ln1_b), wr_hi, wr_lo, row2(br))

    tile_e, tile_nv, nused, row_token, row_dst, n_tiles = _routing_tables(ridx[:, :TOP_K_EXPERTS], S)
    ys = _experts(tile_e, tile_nv, nused, row_token, row_dst, x1p, w_gate_up, b_gate_up.reshape(N_EXPERTS, 1, -1),
                  w_down, b_down.reshape(N_EXPERTS, 1, -1), n_tiles, S * TOP_K_EXPERTS)
    return _final(x1, ys, rgate, p, w_ple_gate.astype(BF16), w_ple_proj.astype(BF16), row2(ln2_g), row2(ln2_b))


def kernel(x, p, w_in, b_in, w_o_attn, w_dw, b_dw, conv_ln_g, conv_ln_b, w_o_conv, w_out, ln1_g, ln1_b, w_router, b_router, w_gate_up, b_gate_up, w_down, b_down, w_ple_gate, w_ple_proj, ln2_g, ln2_b, rel_bias):
    assert x.shape[0] == 1 and p.shape[0] == DEPTH
    out = _layer(x[0], p[0, 0], w_in[0], b_in[0], w_o_attn[0], w_dw[0], b_dw[0], conv_ln_g[0], conv_ln_b[0],
                 w_o_conv[0], w_out[0], ln1_g[0], ln1_b[0], w_router[0], b_router[0], w_gate_up[0], b_gate_up[0],
                 w_down[0], b_down[0], w_ple_gate[0], w_ple_proj[0], ln2_g[0], ln2_b[0], rel_bias)
    return out[None]
```

```python
import functools
import math

import jax
import jax.numpy as jnp
import numpy as np
from jax import lax
from jax.experimental import pallas as pl
from jax.experimental.pallas import tpu as pltpu

F32 = jnp.float32
BF16 = jnp.bfloat16

D_MODEL = 1024
N_HEADS = 8
HEAD_DIM = 64
ATTN_WIDTH = N_HEADS * HEAD_DIM
ATTN_SCALE = HEAD_DIM ** -0.5
LOG2E = math.log2(math.e)
IDX_HEADS = 8
IDX_DIM = 64
IDX_SCALE = (IDX_HEADS ** -0.5) * (IDX_DIM ** -0.5)
TOPK_MAX = 256
CONV_CH = 512
CONV_WIDTH = 31
N_BUCKETS = 32
MAX_DISTANCE = 128
N_EXPERTS = 32
TOP_K_EXPERTS = 4
D_EXPERT = 1024
SWIGLU_LIMIT = 7.0
SWIGLU_ALPHA = 1.702
PLE_DIM = 256
LN_EPS = 1e-5
DEPTH = 1
DEEPNORM_ALPHA = (2 * DEPTH) ** 0.25

LANES = 128
NEG_BIG = -1e30
F32_TINY = float(np.finfo(np.float32).tiny)
VMEM_LIMIT = 56 * 1024 * 1024

TM = 512
TQI = 256
TKI = 512
WARM_PASSES = 8
EDGE_FIRST = 9
SEARCH_CAP = 128
TA = 512
SUB = 128
TME = 256
DMA_UNROLL = 8
CONV_HALO = 32


def _params(sem, vmem=VMEM_LIMIT):
    return pltpu.CompilerParams(dimension_semantics=sem, vmem_limit_bytes=vmem)


def _sigmoid(x):
    return 1.0 / (1.0 + jnp.exp(-x))


def _layer_norm(x, g, b):
    mu = jnp.mean(x, axis=-1, keepdims=True)
    xc = x - mu
    var = jnp.mean(xc * xc, axis=-1, keepdims=True)
    return xc * lax.rsqrt(var + LN_EPS) * g + b


def _dot(a, b):
    return jnp.dot(a, b, preferred_element_type=F32)


def _dot_nt(a, b):
    return lax.dot_general(a, b, (((1,), (1,)), ((), ())), preferred_element_type=F32)


def _proj_kernel(x_ref, wqkv_ref, bqkv_ref, wqi_ref, bqi_ref, wkw_ref, bkw_ref, wcv_ref, bcv_ref,
                 wg_ref, bg_ref, q_ref, k_ref, v_ref, qi_ref, kw_ref, u_ref, sg_ref):
    xb = x_ref[...].astype(BF16)
    qkv = _dot(xb, wqkv_ref[...]) + bqkv_ref[...]
    q_ref[...] = (qkv[:, :ATTN_WIDTH] * (ATTN_SCALE * LOG2E)).astype(BF16)
    k_ref[...] = qkv[:, ATTN_WIDTH:2 * ATTN_WIDTH].astype(BF16)
    v_ref[...] = qkv[:, 2 * ATTN_WIDTH:].astype(BF16)
    qi_ref[...] = (_dot(xb, wqi_ref[...]) + bqi_ref[...]).astype(BF16)
    kw = _dot(xb, wkw_ref[...]) + bkw_ref[...]
    lane = lax.broadcasted_iota(jnp.int32, kw.shape, 1)
    kw_ref[...] = jnp.where(lane >= IDX_DIM, kw * IDX_SCALE, kw)
    cv = _dot(xb, wcv_ref[...]) + bcv_ref[...]
    u_ref[...] = cv[:, :CONV_CH] * _sigmoid(cv[:, CONV_CH:])
    sg_ref[...] = _sigmoid(_dot(xb, wg_ref[...]) + bg_ref[...])


def _proj(x, wqkv, bqkv, wqi, bqi, wkw, bkw, wcv, bcv, wg, bg):
    S = x.shape[0]
    full = lambda a: pl.BlockSpec(a.shape, lambda i: (0, 0))
    row = lambda n: pl.BlockSpec((TM, n), lambda i: (i, 0))
    outs = [(ATTN_WIDTH, BF16)] * 3 + [(IDX_HEADS * IDX_DIM, BF16), (LANES, F32), (CONV_CH, F32),
                                        (2 * D_MODEL, F32)]
    return pl.pallas_call(
        _proj_kernel,
        grid=(S // TM,),
        in_specs=[row(D_MODEL)] + [full(a) for a in (wqkv, bqkv, wqi, bqi, wkw, bkw, wcv, bcv, wg, bg)],
        out_specs=[row(n) for n, _ in outs],
        out_shape=[jax.ShapeDtypeStruct((S, n), dt) for n, dt in outs],
        compiler_params=_params(("parallel",)),
        name="proj",
    )(x, wqkv, bqkv, wqi, bqi, wkw, bkw, wcv, bcv, wg, bg)


def _conv_kernel(u_ref, wdw_ref, bdw_ref, lng_ref, lnb_ref, wo_ref, sgc_ref, o_ref, buf_ref, sh_ref):
    @pl.when(pl.program_id(0) == 0)
    def _():
        buf_ref[0:CONV_HALO, :] = jnp.zeros((CONV_HALO, CONV_CH), F32)

    buf_ref[CONV_HALO:CONV_HALO + TM, :] = u_ref[...]
    base = CONV_HALO - (CONV_WIDTH - 1)
    acc = jnp.zeros((TM, CONV_CH), F32) + bdw_ref[...]
    sub = 8
    for r in range(sub):
        taps = [j for j in range(CONV_WIDTH) if (base + j) % sub == r]
        span = max(base + j - r for j in taps) + TM
        sh_ref[0:span, :] = buf_ref[r:r + span, :]
        for j in taps:
            off = base + j - r
            acc = acc + wdw_ref[j:j + 1, :] * sh_ref[off:off + TM, :]
    buf_ref[0:CONV_HALO, :] = buf_ref[TM:TM + CONV_HALO, :]
    y = _layer_norm(acc, lng_ref[...], lnb_ref[...])
    y = y * _sigmoid(y)
    o_ref[...] = sgc_ref[...] * _dot(y.astype(BF16), wo_ref[...])


def _conv(u, sg, wdw, bdw, lng, lnb, wo):
    S = u.shape[0]
    full = lambda a: pl.BlockSpec(a.shape, lambda i: (0, 0))
    return pl.pallas_call(
        _conv_kernel,
        grid=(S // TM,),
        in_specs=[pl.BlockSpec((TM, CONV_CH), lambda i: (i, 0)), full(wdw), full(bdw), full(lng), full(lnb),
                  full(wo), pl.BlockSpec((TM, D_MODEL), lambda i: (i, 1))],
        out_specs=pl.BlockSpec((TM, D_MODEL), lambda i: (i, 0)),
        out_shape=jax.ShapeDtypeStruct((S, D_MODEL), F32),
        scratch_shapes=[pltpu.VMEM((TM + CONV_HALO, CONV_CH), F32), pltpu.VMEM((TM + CONV_HALO, CONV_CH), F32)],
        compiler_params=_params(("arbitrary",)),
        name="conv",
    )(u, wdw, bdw, lng, lnb, wo, sg)


def _float_key(f):
    b = lax.bitcast_convert_type(f, jnp.int32)
    return b ^ ((b >> 31) & jnp.int32(0x7FFFFFFF))


def _key_float(k):
    b = k ^ ((k >> 31) & jnp.int32(0x7FFFFFFF))
    return lax.bitcast_convert_type(b, F32)


def _index_kernel(qi_ref, wt_ref, ki_ref, mask_ref, sc_ref, qh_ref, *, top_k):
    i = pl.program_id(0)
    nkt = (i * TQI + TQI + TKI - 1) // TKI
    ntile = sc_ref.shape[0]
    q_g = i * TQI + lax.broadcasted_iota(jnp.int32, (1, TQI), 1)
    lane = lax.broadcasted_iota(jnp.int32, (TQI, LANES), 1)

    for h in range(IDX_HEADS):
        qp = qi_ref[:, (h // 2) * LANES:(h // 2 + 1) * LANES]
        keep = (lane < IDX_DIM) if h % 2 == 0 else (lane >= IDX_DIM)
        qh_ref[h] = jnp.where(keep, qp, jnp.zeros_like(qp))

    def score_tile(kt, causal):
        kb = ki_ref[pl.ds(pl.multiple_of(kt * TKI, TKI), TKI), :]
        acc = jnp.zeros((TKI, TQI), F32)
        for h in range(IDX_HEADS):
            acc = acc + wt_ref[h:h + 1, :] * jnp.maximum(_dot_nt(kb, qh_ref[h]), 0.0)
        if causal:
            key_g = kt * TKI + lax.broadcasted_iota(jnp.int32, (TKI, TQI), 0)
            ok = key_g <= q_g
            lo_src = jnp.where(ok, acc, jnp.inf)
            acc = jnp.where(ok, acc, -jnp.inf)
        else:
            lo_src = acc
        sc_ref[kt] = acc
        return jnp.max(acc, axis=0, keepdims=True), jnp.min(lo_src, axis=0, keepdims=True)

    def score_body(kt, carry):
        mx, mn = carry
        tmx, tmn = score_tile(kt, False)
        return jnp.maximum(mx, tmx), jnp.minimum(mn, tmn)

    mx0 = jnp.full((1, TQI), -jnp.inf, F32)
    mn0 = jnp.full((1, TQI), jnp.inf, F32)
    mx, mn = lax.fori_loop(0, nkt - 1, score_body, (mx0, mn0))
    tmx, tmn = score_tile(nkt - 1, True)
    mx = jnp.maximum(mx, tmx)
    mn = jnp.minimum(mn, tmn)
    kf = float(top_k)
    sub = 8

    def count(pivot, strict):
        def tile(kt, cnt):
            for r in range(TKI // sub):
                blk = sc_ref[kt, r * sub:(r + 1) * sub, :]
                hit = (blk > pivot) if strict else (blk >= pivot)
                cnt = cnt + jnp.where(hit, 1.0, 0.0)
            return cnt

        def pair(j, cnts):
            return tile(2 * j, cnts[0]), tile(2 * j + 1, cnts[1])

        zero = jnp.zeros((sub, TQI), F32)
        c0, c1 = lax.fori_loop(0, nkt // 2, pair, (zero, zero))
        c0 = lax.fori_loop(2 * (nkt // 2), nkt, tile, c0)
        return jnp.sum(c0 + c1, axis=0, keepdims=True)

    n_causal = (q_g + 1).astype(F32)
    all_sel = n_causal <= kf
    zeros = jnp.zeros((1, TQI), F32)
    state0 = dict(
        it=jnp.int32(0),
        lo=mn, hi=_key_float(_float_key(mx) + 1), clo=n_causal, chi=zeros,
        glo=jnp.log(jnp.maximum(n_causal, kf + 1.0) / kf), ghi=jnp.full((1, TQI), math.log(0.5 / kf), F32),
        thr=jnp.where(all_sel, NEG_BIG, 0.0).astype(F32),
        done=all_sel.astype(F32), tie=zeros, side=zeros, forced=zeros, use_forced=zeros,
    )

    def above(x):
        return jnp.where(jnp.abs(x) < F32_TINY, F32_TINY, _key_float(_float_key(x) + 1))

    def bracket_edges(lo, hi):
        def tile(kt, carry):
            a, b = carry
            for r in range(TKI // sub):
                blk = sc_ref[kt, r * sub:(r + 1) * sub, :]
                a = jnp.maximum(a, jnp.where(blk < hi, blk, -jnp.inf))
                b = jnp.minimum(b, jnp.where(blk >= lo, blk, jnp.inf))
            return a, b
        a, b = lax.fori_loop(0, nkt, tile, (jnp.full((sub, TQI), -jnp.inf, F32), jnp.full((sub, TQI), jnp.inf, F32)))
        return jnp.max(a, axis=0, keepdims=True), jnp.min(b, axis=0, keepdims=True)

    def cond(st):
        return jnp.logical_and(st["it"] < SEARCH_CAP, jnp.min(st["done"]) < 0.5)

    def count_step(st):
        it, lo, hi, glo, ghi = st["it"], st["lo"], st["hi"], st["glo"], st["ghi"]
        lo_k, hi_k = _float_key(lo), _float_key(hi)
        above_lo = above(lo)
        below_hi = jnp.where(hi == F32_TINY, 0.0, _key_float(hi_k - 1))
        probe = jnp.logical_and(st["done"] < 0.5, above_lo < hi)
        tie_now = jnp.logical_and(st["done"] < 0.5, above_lo >= hi)
        frac = jnp.where(it < 24, glo / (glo - ghi), 0.5)
        pf = lo + (hi - lo) * frac
        pf = jnp.where(it < 64, pf, _key_float((lo_k >> 1) + (hi_k >> 1) + (lo_k & hi_k & 1)))
        pf = jnp.where(it == 0, 0.0, jnp.where(it == 1, F32_TINY, pf))
        pf = jnp.where(st["use_forced"] > 0.5, st["forced"], pf)
        pf = jnp.where(probe, jnp.minimum(jnp.maximum(pf, above_lo), below_hi), lo)
        c = count(pf, False)
        hit = jnp.logical_and(probe, c == kf)
        up = jnp.logical_and(probe, c > kf)
        dn = jnp.logical_and(probe, c < kf)
        g = jnp.log(jnp.maximum(c, 0.5) / kf)
        return dict(
            it=it + 1,
            lo=jnp.where(up, pf, lo), hi=jnp.where(dn, pf, hi),
            clo=jnp.where(up, c, st["clo"]), chi=jnp.where(dn, c, st["chi"]),
            glo=jnp.where(up, g, jnp.where(jnp.logical_and(dn, st["side"] < -0.5), 0.5 * glo, glo)),
            ghi=jnp.where(dn, g, jnp.where(jnp.logical_and(up, st["side"] > 0.5), 0.5 * ghi, ghi)),
            thr=jnp.where(hit, pf, jnp.where(tie_now, lo, st["thr"])),
            done=jnp.where(jnp.logical_or(hit, tie_now), 1.0, st["done"]),
            tie=jnp.where(tie_now, 1.0, st["tie"]),
            side=jnp.where(up, 1.0, jnp.where(dn, -1.0, st["side"])),
            forced=st["forced"], use_forced=zeros,
        )

    def edge_step(st):
        lo, hi = st["lo"], st["hi"]
        probe = jnp.logical_and(st["done"] < 0.5, above(lo) < hi)
        a, b = bracket_edges(lo, hi)
        one_above = st["chi"] == kf - 1.0
        one_below = st["clo"] == kf + 1.0
        new = dict(st)
        new.update(
            it=st["it"] + 1,
            lo=jnp.where(probe, b, lo), hi=jnp.where(probe, above(a), hi),
            forced=jnp.where(one_above, a, above(b)),
            use_forced=jnp.where(jnp.logical_and(probe, jnp.logical_or(one_above, one_below)), 1.0, 0.0),
        )
        return new

    def step(st):
        it = st["it"]
        is_edge = jnp.logical_and(it >= EDGE_FIRST, (it - EDGE_FIRST) % 3 == 0)
        return lax.cond(is_edge, edge_step, count_step, st)

    st = lax.fori_loop(0, WARM_PASSES, lambda _, s: count_step(s), state0)
    st = lax.while_loop(cond, step, st)
    thr = st["thr"]
    tie = st["tie"]
    any_tie = jnp.max(tie) > 0.5

    def emit(kt, sel):
        mask_ref[0, kt] = jnp.where(sel, 0.0, NEG_BIG).astype(mask_ref.dtype).T

    @pl.when(jnp.logical_not(any_tie))
    def _():
        def body(kt, carry):
            emit(kt, sc_ref[kt] >= thr)
            return carry
        lax.fori_loop(0, nkt, body, 0)

    @pl.when(any_tie)
    def _():
        need = jnp.where(tie > 0.5, kf - count(thr, True), float(2 * ntile * TKI))
        r = lax.broadcasted_iota(jnp.int32, (TKI, TKI), 0)
        c = lax.broadcasted_iota(jnp.int32, (TKI, TKI), 1)
        prefix = jnp.where(c <= r, 1.0, 0.0).astype(BF16)

        def body(kt, seen):
            s = sc_ref[kt]
            eq = s == thr
            rank = seen + _dot(prefix, jnp.where(eq, 1.0, 0.0).astype(BF16))
            emit(kt, jnp.logical_or(s > thr, jnp.logical_and(eq, rank <= need)))
            return seen + jnp.sum(jnp.where(eq, 1.0, 0.0), axis=0, keepdims=True)
        lax.fori_loop(0, nkt, body, jnp.zeros((1, TQI), F32))

    def fill(kt, carry):
        mask_ref[0, kt] = jnp.full((TQI, TKI), NEG_BIG, mask_ref.dtype)
        return carry
    lax.fori_loop(nkt, ntile, fill, 0)


def _index_mask(qi, wt, ki2, top_k):
    S = qi.shape[0]
    nq, nk = S // TQI, S // TKI
    return pl.pallas_call(
        functools.partial(_index_kernel, top_k=top_k),
        grid=(nq,),
        in_specs=[pl.BlockSpec((TQI, IDX_HEADS * IDX_DIM), lambda i: (i, 0)),
                  pl.BlockSpec((IDX_HEADS, TQI), lambda i: (0, i)),
                  pl.BlockSpec((S, LANES), lambda i: (0, 0))],
        out_specs=pl.BlockSpec((1, nk, TQI, TKI), lambda i: (i, 0, 0, 0)),
        out_shape=jax.ShapeDtypeStruct((nq, nk, TQI, TKI), BF16),
        scratch_shapes=[pltpu.VMEM((nk, TKI, TQI), F32),
                        pltpu.VMEM((IDX_HEADS, TQI, LANES), BF16)],
        compiler_params=_params(("parallel",)),
        name="index_mask",
    )(qi, wt, ki2)


def _attn_kernel(qt_ref, kt_ref, q_ref, k_ref, v_ref, m_ref, bd_ref, be_ref, o_ref, *state):
    m_sc, acc_sc = state[:N_HEADS], state[N_HEADS:]
    step = pl.program_id(0)
    qi = qt_ref[step]
    ki = kt_ref[step]
    nsub = TA // SUB

    @pl.when(ki == 0)
    def _():
        for h in range(N_HEADS):
            m_sc[h][...] = jnp.full((TA, LANES), NEG_BIG, F32)
            acc_sc[h][...] = jnp.zeros((TA, LANES), F32)

    lane = lax.broadcasted_iota(jnp.int32, (TA, LANES), 1)
    first = lane < HEAD_DIM

    def bias_tile(h, diagonal):
        zero = jnp.zeros((SUB, SUB), F32)
        rows = []
        for a in range(nsub):
            if diagonal:
                blks = [bd_ref[h] if b == a else (be_ref[h] if b == a - 1 else zero) for b in range(nsub)]
            else:
                blks = [be_ref[h] if (a == 0 and b == nsub - 1) else zero for b in range(nsub)]
            rows.append(jnp.concatenate(blks, axis=1))
        return jnp.concatenate(rows, axis=0)

    def update(near, diagonal):
        maskf = jnp.concatenate([m_ref[a, 0] for a in range(TA // TQI)], axis=0).astype(F32)

        def logits(h):
            cols = slice((h // 2) * LANES, (h // 2 + 1) * LANES)
            qp = q_ref[:, cols]
            mine = first if h % 2 == 0 else jnp.logical_not(first)
            s = _dot_nt(jnp.where(mine, qp, jnp.zeros_like(qp)), k_ref[:, cols]) + maskf
            return s + bias_tile(h, diagonal) if near else s

        s = logits(0)
        for h in range(N_HEADS):
            s_next = logits(h + 1) if h + 1 < N_HEADS else None
            vp = v_ref[:, (h // 2) * LANES:(h // 2 + 1) * LANES]
            mine = first if h % 2 == 0 else jnp.logical_not(first)
            vh = jnp.where(mine, vp, jnp.ones_like(vp))
            m_prev = m_sc[h][...]
            m_next = jnp.maximum(m_prev, jnp.max(s, axis=1, keepdims=True))
            pexp = jnp.exp2(s - jnp.concatenate([m_next] * (TA // LANES), axis=1)).astype(BF16)
            acc_sc[h][...] = jnp.exp2(m_prev - m_next) * acc_sc[h][...] + _dot(pexp, vh)
            m_sc[h][...] = m_next
            s = s_next

    @pl.when(ki < qi - 1)
    def _():
        update(False, False)

    @pl.when(ki == qi - 1)
    def _():
        update(True, False)

    @pl.when(ki == qi)
    def _():
        update(True, True)
        for p in range(N_HEADS // 2):
            a0, a1 = acc_sc[2 * p][...], acc_sc[2 * p + 1][...]
            d0 = pltpu.roll(a0, HEAD_DIM, axis=1)
            d1 = pltpu.roll(a1, HEAD_DIM, axis=1)
            o_ref[:, p * LANES:(p + 1) * LANES] = jnp.where(first, a0 / d0, a1 / d1).astype(o_ref.dtype)


def _attention(q, k, v, mask4, bias_d, bias_e):
    S = q.shape[0]
    nb = S // TA
    pairs = [(a, b) for a in range(nb) for b in range(a + 1)]
    qtab = jnp.asarray(np.array([a for a, _ in pairs], np.int32))
    ktab = jnp.asarray(np.array([b for _, b in pairs], np.int32))
    grid_spec = pltpu.PrefetchScalarGridSpec(
        num_scalar_prefetch=2,
        grid=(len(pairs),),
        in_specs=[pl.BlockSpec((TA, ATTN_WIDTH), lambda s, qt, kt: (qt[s], 0)),
                  pl.BlockSpec((TA, ATTN_WIDTH), lambda s, qt, kt: (kt[s], 0)),
                  pl.BlockSpec((TA, ATTN_WIDTH), lambda s, qt, kt: (kt[s], 0)),
                  pl.BlockSpec((TA // TQI, TA // TKI, TQI, TKI), lambda s, qt, kt: (qt[s], kt[s], 0, 0)),
                  pl.BlockSpec(bias_d.shape, lambda s, qt, kt: (0, 0, 0)),
                  pl.BlockSpec(bias_e.shape, lambda s, qt, kt: (0, 0, 0))],
        out_specs=pl.BlockSpec((TA, ATTN_WIDTH), lambda s, qt, kt: (qt[s], 0)),
        scratch_shapes=[pltpu.VMEM((TA, LANES), F32)] * (2 * N_HEADS),
    )
    return pl.pallas_call(
        _attn_kernel,
        grid_spec=grid_spec,
        out_shape=jax.ShapeDtypeStruct((S, ATTN_WIDTH), BF16),
        compiler_params=_params(("arbitrary",)),
        name="attn",
    )(qtab, ktab, q, k, v, mask4, bias_d, bias_e)


def _relative_bias_blocks(rel_bias):
    dist = jnp.arange(2 * SUB, dtype=jnp.int32)
    max_exact = N_BUCKETS // 2
    dist_f = jnp.maximum(dist, 1).astype(F32)
    large = max_exact + (jnp.log(dist_f / max_exact) / math.log(MAX_DISTANCE / max_exact)
                         * (N_BUCKETS - max_exact)).astype(jnp.int32)
    bucket = jnp.where(dist < max_exact, dist, jnp.minimum(large, N_BUCKETS - 1))
    table = ((rel_bias[bucket] - rel_bias[N_BUCKETS - 1]) * LOG2E).astype(F32)
    i = jnp.arange(SUB)[:, None]
    j = jnp.arange(SUB)[None, :]

    def toeplitz(d):
        onehot = (d[:, :, None] == dist[None, None, :]).astype(F32)
        return jnp.einsum("ijd,dh->hij", onehot, table, precision=lax.Precision.HIGHEST)

    return toeplitz(jnp.clip(i - j, 0, 2 * SUB - 1)), toeplitz(SUB + i - j)


def _pack_bf16_pair(a, b):
    def rnd(x):
        bits = lax.bitcast_convert_type(x, jnp.uint32)
        return bits + jnp.uint32(0x7FFF) + ((bits >> 16) & jnp.uint32(1))
    return (rnd(a) >> 16) | (rnd(b) & jnp.uint32(0xFFFF0000))


def _unpack_bf16_pair(p):
    lo = lax.bitcast_convert_type(p << 16, F32)
    hi = lax.bitcast_convert_type(p & jnp.uint32(0xFFFF0000), F32)
    return lo.astype(BF16), hi.astype(BF16)


def _post_kernel(ya_ref, sga_ref, pc_ref, x_ref, woa_ref, wout_ref, g_ref, b_ref, wr_ref, wrl_ref, br_ref,
                 x1_ref, x1p_ref, ridx_ref, rgate_ref):
    y_attn = _dot(ya_ref[...], woa_ref[...])
    merged = sga_ref[...] * y_attn + pc_ref[...]
    mix = _dot(merged.astype(BF16), wout_ref[...])
    x1 = _layer_norm(DEEPNORM_ALPHA * x_ref[...] + mix, g_ref[...], b_ref[...])
    x1_ref[...] = x1
    half = D_MODEL // 2
    x1p_ref[:, 0, :] = _pack_bf16_pair(x1[:, :half], x1[:, half:])

    x1_hi = x1.astype(BF16)
    x1_lo = (x1 - x1_hi.astype(F32)).astype(BF16)
    logits = (_dot(x1_hi, wr_ref[...]) + (_dot(x1_hi, wrl_ref[...]) + _dot(x1_lo, wr_ref[...]))) + br_ref[...]
    lane = lax.broadcasted_iota(jnp.int32, logits.shape, 1).astype(F32)
    cur = logits
    vals, idxs = [], []
    for _ in range(TOP_K_EXPERTS):
        m = jnp.max(cur, axis=1, keepdims=True)
        ix = jnp.min(jnp.where(cur == m, lane, float(LANES)), axis=1, keepdims=True)
        vals.append(m)
        idxs.append(ix)
        cur = jnp.where(lane == ix, -jnp.inf, cur)
    exps = [jnp.exp(v - vals[0]) for v in vals]
    denom = exps[0]
    for e in exps[1:]:
        denom = denom + e
    ridx = jnp.zeros_like(logits)
    rgate = jnp.zeros_like(logits)
    for j in range(TOP_K_EXPERTS):
        ridx = jnp.where(lane == float(j), idxs[j], ridx)
        rgate = jnp.where(lane == float(j), exps[j] / denom, rgate)
    ridx_ref[...] = ridx.astype(jnp.int32)
    rgate_ref[...] = rgate


def _post(ya, sg, pc, x, woa, wout, g, b, wr, wrl, br):
    S = x.shape[0]
    full = lambda a: pl.BlockSpec(a.shape, lambda i: (0, 0))
    row = lambda n: pl.BlockSpec((TM, n), lambda i: (i, 0))
    return pl.pallas_call(
        _post_kernel,
        grid=(S // TM,),
        in_specs=[row(ATTN_WIDTH), row(D_MODEL), row(D_MODEL), row(D_MODEL)] + [full(a) for a in (woa, wout, g, b, wr, wrl, br)],
        out_specs=[row(D_MODEL), pl.BlockSpec((TM, 1, D_MODEL // 2), lambda i: (i, 0, 0)), row(LANES), row(LANES)],
        out_shape=[jax.ShapeDtypeStruct((S, D_MODEL), F32), jax.ShapeDtypeStruct((S, 1, D_MODEL // 2), jnp.uint32),
                   jax.ShapeDtypeStruct((S, LANES), jnp.int32), jax.ShapeDtypeStruct((S, LANES), F32)],
        compiler_params=_params(("parallel",)),
        name="post",
    )(ya, sg, pc, x, woa, wout, g, b, wr, wrl, br)


def _expert_kernel(te_ref, nv_ref, nused_ref, tok_ref, tokn_ref, dst_ref, x_hbm, wgu_ref, bgu_ref, wd_ref, bd_ref,
                   ys_hbm, xs_buf, y_buf, wgu_sc, wd_sc, gsem, ssem):
    t = pl.program_id(0)
    nused = nused_ref[0]
    cur = t % 2
    nxt = 1 - cur

    def start_gather(idx_ref, slot):
        def body(r, carry):
            pltpu.make_async_copy(x_hbm.at[idx_ref[0, 0, r]], xs_buf.at[slot, pl.ds(r, 1)], gsem.at[slot]).start()
            return carry
        lax.fori_loop(0, TME, body, 0, unroll=DMA_UNROLL)

    def wait_gather(slot):
        pltpu.make_async_copy(xs_buf.at[slot], xs_buf.at[slot], gsem.at[slot]).wait()

    def wait_scatter(slot, n):
        pltpu.make_async_copy(ys_hbm.at[pl.ds(0, n)], ys_hbm.at[pl.ds(0, n)], ssem.at[slot]).wait()

    @pl.when(t == 0)
    def _():
        start_gather(tok_ref, 0)

    @pl.when(t + 1 < nused)
    def _():
        start_gather(tokn_ref, nxt)

    e = te_ref[t]
    e_prev = te_ref[jnp.maximum(t - 1, 0)]

    @pl.when(jnp.logical_or(t == 0, e != e_prev))
    def _():
        wgu_sc[...] = wgu_ref[0].astype(BF16)
        wd_sc[...] = wd_ref[0].astype(BF16)

    @pl.when(t < nused)
    def _():
        wait_gather(cur)

        @pl.when(t >= 2)
        def _():
            wait_scatter(cur, nv_ref[jnp.maximum(t - 2, 0)])

        half = D_MODEL // 2
        lo, hi = _unpack_bf16_pair(xs_buf[cur])
        gu = _dot(lo, wgu_sc[0:half, :]) + _dot(hi, wgu_sc[half:, :]) + bgu_ref[0]
        g = jnp.minimum(gu[:, :D_EXPERT], SWIGLU_LIMIT)
        u = jnp.clip(gu[:, D_EXPERT:], -SWIGLU_LIMIT, SWIGLU_LIMIT)
        act = (u + 1.0) * (g * _sigmoid(SWIGLU_ALPHA * g))
        y_buf[cur] = _dot(act.astype(BF16), wd_sc[...]) + bd_ref[0]

        def scatter(r, carry):
            pltpu.make_async_copy(y_buf.at[cur, pl.ds(r, 1)], ys_hbm.at[dst_ref[0, 0, r]], ssem.at[cur]).start()
            return carry

        def scatter_group(c, carry):
            for j in range(DMA_UNROLL):
                scatter(c * DMA_UNROLL + j, carry)
            return carry
        n_groups = nv_ref[t] // DMA_UNROLL
        lax.fori_loop(0, n_groups, scatter_group, 0)
        lax.fori_loop(n_groups * DMA_UNROLL, nv_ref[t], scatter, 0)

        @pl.when(t == nused - 1)
        def _():
            @pl.when(t >= 1)
            def _():
                wait_scatter(nxt, nv_ref[jnp.maximum(t - 1, 0)])
            wait_scatter(cur, nv_ref[t])


def _experts(tile_e, tile_nv, nused, row_token, row_dst, x1p, wgu, bgu, wd, bd, n_tiles, n_dst):
    idx = lambda a: a.reshape(n_tiles, 1, TME)
    smem_tile = lambda f: pl.BlockSpec((1, 1, TME), f, memory_space=pltpu.SMEM)
    grid_spec = pltpu.PrefetchScalarGridSpec(
        num_scalar_prefetch=3,
        grid=(n_tiles,),
        in_specs=[smem_tile(lambda t, te, nv, nu: (t, 0, 0)),
                  smem_tile(lambda t, te, nv, nu: (jnp.minimum(t + 1, n_tiles - 1), 0, 0)),
                  smem_tile(lambda t, te, nv, nu: (t, 0, 0)),
                  pl.BlockSpec(memory_space=pl.ANY),
                  pl.BlockSpec((1, D_MODEL, 2 * D_EXPERT), lambda t, te, nv, nu: (te[t], 0, 0)),
                  pl.BlockSpec((1, 1, 2 * D_EXPERT), lambda t, te, nv, nu: (te[t], 0, 0)),
                  pl.BlockSpec((1, D_EXPERT, D_MODEL), lambda t, te, nv, nu: (te[t], 0, 0)),
                  pl.BlockSpec((1, 1, D_MODEL), lambda t, te, nv, nu: (te[t], 0, 0))],
        out_specs=pl.BlockSpec(memory_space=pl.ANY),
        scratch_shapes=[pltpu.VMEM((2, TME, D_MODEL // 2), jnp.uint32), pltpu.VMEM((2, TME, D_MODEL), F32),
                        pltpu.VMEM((D_MODEL, 2 * D_EXPERT), BF16), pltpu.VMEM((D_EXPERT, D_MODEL), BF16),
                        pltpu.SemaphoreType.DMA((2,)), pltpu.SemaphoreType.DMA((2,))],
    )
    return pl.pallas_call(
        _expert_kernel,
        grid_spec=grid_spec,
        out_shape=jax.ShapeDtypeStruct((n_dst, 1, D_MODEL), F32),
        compiler_params=_params(("arbitrary",)),
        name="moe_experts",
    )(tile_e, tile_nv, nused, idx(row_token), idx(row_token), idx(row_dst), x1p, wgu, bgu, wd, bd)


def _routing_tables(ridx, S):
    n_flat = S * TOP_K_EXPERTS
    n_tiles = n_flat // TME + N_EXPERTS
    flat_e = ridx.reshape(n_flat)
    assert N_EXPERTS * n_flat < 2 ** 31
    order = jnp.sort(flat_e * n_flat + jnp.arange(n_flat, dtype=jnp.int32)) % n_flat
    counts = jnp.sum(flat_e[:, None] == jnp.arange(N_EXPERTS, dtype=jnp.int32)[None, :], axis=0, dtype=jnp.int32)
    padded = ((counts + TME - 1) // TME) * TME
    pad_end = jnp.cumsum(padded)
    pad_start = pad_end - padded
    grp_start = jnp.cumsum(counts) - counts
    nused = (pad_end[-1] // TME).astype(jnp.int32).reshape(1)
    tile_row0 = jnp.arange(n_tiles, dtype=jnp.int32) * TME
    tile_e = jnp.minimum(jnp.sum(tile_row0[:, None] >= pad_end[None, :], axis=1), N_EXPERTS - 1).astype(jnp.int32)
    tile_rank0 = tile_row0 - pad_start[tile_e]
    tile_nv = jnp.where(tile_row0 < pad_end[-1], jnp.clip(counts[tile_e] - tile_rank0, 0, TME), 0).astype(jnp.int32)
    within = jnp.arange(TME, dtype=jnp.int32)[None, :]
    valid = within < tile_nv[:, None]
    src_flat = lax.optimization_barrier(order[jnp.clip((grp_start[tile_e] + tile_rank0)[:, None] + within, 0, n_flat - 1)])
    row_token = jnp.where(valid, src_flat // TOP_K_EXPERTS, 0).astype(jnp.int32)
    row_dst = jnp.where(valid, (src_flat % TOP_K_EXPERTS) * S + src_flat // TOP_K_EXPERTS, 0).astype(jnp.int32)
    return tile_e, tile_nv, nused, row_token, row_dst, n_tiles


def _final_kernel(x1_ref, y0_ref, y1_ref, y2_ref, y3_ref, rg_ref, p_ref, wpg_ref, wpp_ref, g_ref, b_ref, o_ref):
    h = DEEPNORM_ALPHA * x1_ref[...]
    rg = rg_ref[...]
    for j, y_ref in enumerate((y0_ref, y1_ref, y2_ref, y3_ref)):
        h = h + rg[:, j:j + 1] * y_ref[:, 0, :]
    ple = _sigmoid(_dot(h.astype(BF16), wpg_ref[...])) * _dot(p_ref[...].astype(BF16), wpp_ref[...])
    o_ref[...] = _layer_norm(h + ple, g_ref[...], b_ref[...])


def _final(x1, ys, rgate, p, wpg, wpp, g, b):
    S = x1.shape[0]
    full = lambda a: pl.BlockSpec(a.shape, lambda i: (0, 0))
    row = lambda n: pl.BlockSpec((TM, n), lambda i: (i, 0))
    return pl.pallas_call(
        _final_kernel,
        grid=(S // TM,),
        in_specs=[row(D_MODEL)]
        + [pl.BlockSpec((TM, 1, D_MODEL), lambda i, j=j: (j * (S // TM) + i, 0, 0)) for j in range(TOP_K_EXPERTS)]
        + [row(LANES), row(PLE_DIM), full(wpg), full(wpp), full(g), full(b)],
        out_specs=row(D_MODEL),
        out_shape=jax.ShapeDtypeStruct((S, D_MODEL), F32),
        compiler_params=_params(("parallel",)),
        name="final",
    )(x1, ys, ys, ys, ys, rgate, p, wpg, wpp, g, b)


def _layer(x, p, w_in, b_in, w_o_attn, w_dw, b_dw, conv_ln_g, conv_ln_b, w_o_conv, w_out, ln1_g, ln1_b,
           w_router, b_router, w_gate_up, b_gate_up, w_down, b_down, w_ple_gate, w_ple_proj, ln2_g, ln2_b,
           rel_bias):
    S = x.shape[0]
    assert S % TM == 0 and S % TA == 0 and S % TKI == 0 and (S * TOP_K_EXPERTS) % TME == 0
    top_k = min(TOPK_MAX, S // 4)
    row2 = lambda a: a.reshape(1, -1).astype(F32)

    o_q, o_qi, o_ki, o_cv, o_g = 0, 3 * ATTN_WIDTH, 3 * ATTN_WIDTH + IDX_HEADS * IDX_DIM, \
        3 * ATTN_WIDTH + IDX_HEADS * IDX_DIM + IDX_DIM + IDX_HEADS, \
        3 * ATTN_WIDTH + IDX_HEADS * IDX_DIM + IDX_DIM + IDX_HEADS + 2 * CONV_CH
    kw_pad = LANES - (IDX_DIM + IDX_HEADS)
    wkw = jnp.pad(w_in[:, o_ki:o_cv], ((0, 0), (0, kw_pad)))
    bkw = jnp.pad(b_in[o_ki:o_cv], (0, kw_pad))
    q, k, v, qi, kw, u, sg = _proj(
        x, w_in[:, o_q:o_qi].astype(BF16), row2(b_in[o_q:o_qi]),
        w_in[:, o_qi:o_ki].astype(BF16), row2(b_in[o_qi:o_ki]),
        wkw.astype(BF16), row2(bkw),
        w_in[:, o_cv:o_g].astype(BF16), row2(b_in[o_cv:o_g]),
        w_in[:, o_g:].astype(BF16), row2(b_in[o_g:]))

    part_conv = _conv(u, sg, w_dw, row2(b_dw), row2(conv_ln_g), row2(conv_ln_b), w_o_conv.astype(BF16))

    ki = kw[:, :IDX_DIM].astype(BF16)
    mask4 = _index_mask(qi, kw[:, IDX_DIM:IDX_DIM + IDX_HEADS].T, jnp.concatenate([ki, ki], axis=1), top_k)
    bias_d, bias_e = _relative_bias_blocks(rel_bias)
    y_attn = _attention(q, k, v, mask4, bias_d, bias_e)

    wr = jnp.pad(w_router, ((0, 0), (0, LANES - N_EXPERTS)))
    wr_hi = wr.astype(BF16)
    wr_lo = (wr - wr_hi.astype(F32)).astype(BF16)
    br = jnp.pad(b_router, (0, LANES - N_EXPERTS), constant_values=-jnp.inf)
    x1, x1p, ridx, rgate = _post(y_attn, sg, part_conv, x, w_o_attn.astype(BF16), w_out.astype(BF16),
                                 row2(ln1_g), row2(ln1_b), wr_hi, wr_lo, row2(br))

    tile_e, tile_nv, nused, row_token, row_dst, n_tiles = _routing_tables(ridx[:, :TOP_K_EXPERTS], S)
    ys = _experts(tile_e, tile_nv, nused, row_token, row_dst, x1p, w_gate_up, b_gate_up.reshape(N_EXPERTS, 1, -1),
                  w_down, b_down.reshape(N_EXPERTS, 1, -1), n_tiles, S * TOP_K_EXPERTS)
    return _final(x1, ys, rgate, p, w_ple_gate.astype(BF16), w_ple_proj.astype(BF16), row2(ln2_g), row2(ln2_b))


def kernel(x, p, w_in, b_in, w_o_attn, w_dw, b_dw, conv_ln_g, conv_ln_b, w_o_conv, w_out, ln1_g, ln1_b, w_router, b_router, w_gate_up, b_gate_up, w_down, b_down, w_ple_gate, w_ple_proj, ln2_g, ln2_b, rel_bias):
    assert x.shape[0] == 1 and p.shape[0] == DEPTH
    out = _layer(x[0], p[0, 0], w_in[0], b_in[0], w_o_attn[0], w_dw[0], b_dw[0], conv_ln_g[0], conv_ln_b[0],
                 w_o_conv[0], w_out[0], ln1_g[0], ln1_b[0], w_router[0], b_router[0], w_gate_up[0], b_gate_up[0],
                 w_down[0], b_down[0], w_ple_gate[0], w_ple_proj[0], ln2_g[0], ln2_b[0], rel_bias)
    return out[None]
```

```python
import functools
import math

import jax
import jax.numpy as jnp
import numpy as np
from jax import lax
from jax.experimental import pallas as pl
from jax.experimental.pallas import tpu as pltpu

F32 = jnp.float32
BF16 = jnp.bfloat16

D_MODEL = 1024
N_HEADS = 8
HEAD_DIM = 64
ATTN_WIDTH = N_HEADS * HEAD_DIM
ATTN_SCALE = HEAD_DIM ** -0.5
LOG2E = math.log2(math.e)
IDX_HEADS = 8
IDX_DIM = 64
IDX_SCALE = (IDX_HEADS ** -0.5) * (IDX_DIM ** -0.5)
TOPK_MAX = 256
CONV_CH = 512
CONV_WIDTH = 31
N_BUCKETS = 32
MAX_DISTANCE = 128
N_EXPERTS = 32
TOP_K_EXPERTS = 4
D_EXPERT = 1024
SWIGLU_LIMIT = 7.0
SWIGLU_ALPHA = 1.702
PLE_DIM = 256
LN_EPS = 1e-5
DEPTH = 1
DEEPNORM_ALPHA = (2 * DEPTH) ** 0.25

LANES = 128
NEG_BIG = -1e30
F32_TINY = float(np.finfo(np.float32).tiny)
VMEM_LIMIT = 56 * 1024 * 1024

TM = 512
TQI = 256
TKI = 512
WARM_PASSES = 8
EDGE_FIRST = 9
SEARCH_CAP = 128
TA = 512
SUB = 128
TME = 256
DMA_UNROLL = 8
CONV_HALO = 32


def _params(sem, vmem=VMEM_LIMIT):
    return pltpu.CompilerParams(dimension_semantics=sem, vmem_limit_bytes=vmem)


def _sigmoid(x):
    return 1.0 / (1.0 + jnp.exp(-x))


def _layer_norm(x, g, b):
    mu = jnp.mean(x, axis=-1, keepdims=True)
    xc = x - mu
    var = jnp.mean(xc * xc, axis=-1, keepdims=True)
    return xc * lax.rsqrt(var + LN_EPS) * g + b


def _dot(a, b):
    return jnp.dot(a, b, preferred_element_type=F32)


def _dot_nt(a, b):
    return lax.dot_general(a, b, (((1,), (1,)), ((), ())), preferred_element_type=F32)


def _proj_kernel(x_ref, wqkv_ref, bqkv_ref, wqi_ref, bqi_ref, wkw_ref, bkw_ref, wcv_ref, bcv_ref,
                 wg_ref, bg_ref, q_ref, k_ref, v_ref, qi_ref, kw_ref, u_ref, sg_ref):
    xb = x_ref[...].astype(BF16)
    qkv = _dot(xb, wqkv_ref[...]) + bqkv_ref[...]
    q_ref[...] = (qkv[:, :ATTN_WIDTH] * (ATTN_SCALE * LOG2E)).astype(BF16)
    k_ref[...] = qkv[:, ATTN_WIDTH:2 * ATTN_WIDTH].astype(BF16)
    v_ref[...] = qkv[:, 2 * ATTN_WIDTH:].astype(BF16)
    qi_ref[...] = (_dot(xb, wqi_ref[...]) + bqi_ref[...]).astype(BF16)
    kw = _dot(xb, wkw_ref[...]) + bkw_ref[...]
    lane = lax.broadcasted_iota(jnp.int32, kw.shape, 1)
    kw_ref[...] = jnp.where(lane >= IDX_DIM, kw * IDX_SCALE, kw)
    cv = _dot(xb, wcv_ref[...]) + bcv_ref[...]
    u_ref[...] = cv[:, :CONV_CH] * _sigmoid(cv[:, CONV_CH:])
    sg_ref[...] = _sigmoid(_dot(xb, wg_ref[...]) + bg_ref[...])


def _proj(x, wqkv, bqkv, wqi, bqi, wkw, bkw, wcv, bcv, wg, bg):
    S = x.shape[0]
    full = lambda a: pl.BlockSpec(a.shape, lambda i: (0, 0))
    row = lambda n: pl.BlockSpec((TM, n), lambda i: (i, 0))
    outs = [(ATTN_WIDTH, BF16)] * 3 + [(IDX_HEADS * IDX_DIM, BF16), (LANES, F32), (CONV_CH, F32),
                                        (2 * D_MODEL, F32)]
    return pl.pallas_call(
        _proj_kernel,
        grid=(S // TM,),
        in_specs=[row(D_MODEL)] + [full(a) for a in (wqkv, bqkv, wqi, bqi, wkw, bkw, wcv, bcv, wg, bg)],
        out_specs=[row(n) for n, _ in outs],
        out_shape=[jax.ShapeDtypeStruct((S, n), dt) for n, dt in outs],
        compiler_params=_params(("parallel",)),
        name="proj",
    )(x, wqkv, bqkv, wqi, bqi, wkw, bkw, wcv, bcv, wg, bg)


def _conv_kernel(u_ref, wdw_ref, bdw_ref, lng_ref, lnb_ref, wo_ref, sgc_ref, o_ref, buf_ref, sh_ref):
    @pl.when(pl.program_id(0) == 0)
    def _():
        buf_ref[0:CONV_HALO, :] = jnp.zeros((CONV_HALO, CONV_CH), F32)

    buf_ref[CONV_HALO:CONV_HALO + TM, :] = u_ref[...]
    base = CONV_HALO - (CONV_WIDTH - 1)
    acc = jnp.zeros((TM, CONV_CH), F32) + bdw_ref[...]
    sub = 8
    for r in range(sub):
        taps = [j for j in range(CONV_WIDTH) if (base + j) % sub == r]
        span = max(base + j - r for j in taps) + TM
        sh_ref[0:span, :] = buf_ref[r:r + span, :]
        for j in taps:
            off = base + j - r
            acc = acc + wdw_ref[j:j + 1, :] * sh_ref[off:off + TM, :]
    buf_ref[0:CONV_HALO, :] = buf_ref[TM:TM + CONV_HALO, :]
    y = _layer_norm(acc, lng_ref[...], lnb_ref[...])
    y = y * _sigmoid(y)
    o_ref[...] = sgc_ref[...] * _dot(y.astype(BF16), wo_ref[...])


def _conv(u, sg, wdw, bdw, lng, lnb, wo):
    S = u.shape[0]
    full = lambda a: pl.BlockSpec(a.shape, lambda i: (0, 0))
    return pl.pallas_call(
        _conv_kernel,
        grid=(S // TM,),
        in_specs=[pl.BlockSpec((TM, CONV_CH), lambda i: (i, 0)), full(wdw), full(bdw), full(lng), full(lnb),
                  full(wo), pl.BlockSpec((TM, D_MODEL), lambda i: (i, 1))],
        out_specs=pl.BlockSpec((TM, D_MODEL), lambda i: (i, 0)),
        out_shape=jax.ShapeDtypeStruct((S, D_MODEL), F32),
        scratch_shapes=[pltpu.VMEM((TM + CONV_HALO, CONV_CH), F32), pltpu.VMEM((TM + CONV_HALO, CONV_CH), F32)],
        compiler_params=_params(("arbitrary",)),
        name="conv",
    )(u, wdw, bdw, lng, lnb, wo, sg)


def _float_key(f):
    b = lax.bitcast_convert_type(f, jnp.int32)
    return b ^ ((b >> 31) & jnp.int32(0x7FFFFFFF))


def _key_float(k):
    b = k ^ ((k >> 31) & jnp.int32(0x7FFFFFFF))
    return lax.bitcast_convert_type(b, F32)


def _index_kernel(qi_ref, wt_ref, ki_ref, mask_ref, sc_ref, qh_ref, *, top_k):
    i = pl.program_id(0)
    nkt = (i * TQI + TQI + TKI - 1) // TKI
    ntile = sc_ref.shape[0]
    q_g = i * TQI + lax.broadcasted_iota(jnp.int32, (1, TQI), 1)
    lane = lax.broadcasted_iota(jnp.int32, (TQI, LANES), 1)

    for h in range(IDX_HEADS):
        qp = qi_ref[:, (h // 2) * LANES:(h // 2 + 1) * LANES]
        keep = (lane < IDX_DIM) if h % 2 == 0 else (lane >= IDX_DIM)
        qh_ref[h] = jnp.where(keep, qp, jnp.zeros_like(qp))

    def score_tile(kt, causal):
        kb = ki_ref[pl.ds(pl.multiple_of(kt * TKI, TKI), TKI), :]
        acc = jnp.zeros((TKI, TQI), F32)
        for h in range(IDX_HEADS):
            acc = acc + wt_ref[h:h + 1, :] * jnp.maximum(_dot_nt(kb, qh_ref[h]), 0.0)
        if causal:
            key_g = kt * TKI + lax.broadcasted_iota(jnp.int32, (TKI, TQI), 0)
            ok = key_g <= q_g
            lo_src = jnp.where(ok, acc, jnp.inf)
            acc = jnp.where(ok, acc, -jnp.inf)
        else:
            lo_src = acc
        sc_ref[kt] = acc
        return jnp.max(acc, axis=0, keepdims=True), jnp.min(lo_src, axis=0, keepdims=True)

    def score_body(kt, carry):
        mx, mn = carry
        tmx, tmn = score_tile(kt, False)
        return jnp.maximum(mx, tmx), jnp.minimum(mn, tmn)

    mx0 = jnp.full((1, TQI), -jnp.inf, F32)
    mn0 = jnp.full((1, TQI), jnp.inf, F32)
    mx, mn = lax.fori_loop(0, nkt - 1, score_body, (mx0, mn0))
    tmx, tmn = score_tile(nkt - 1, True)
    mx = jnp.maximum(mx, tmx)
    mn = jnp.minimum(mn, tmn)
    kf = float(top_k)
    sub = 8

    def count(pivot, strict):
        def tile(kt, cnt):
            for r in range(TKI // sub):
                blk = sc_ref[kt, r * sub:(r + 1) * sub, :]
                hit = (blk > pivot) if strict else (blk >= pivot)
                cnt = cnt + jnp.where(hit, 1.0, 0.0)
            return cnt

        group = 4

        def tiles(j, cnts):
            return tuple(tile(group * j + g, cnts[g]) for g in range(group))

        zero = jnp.zeros((sub, TQI), F32)
        cnts = lax.fori_loop(0, nkt // group, tiles, (zero,) * group)
        rest = lax.fori_loop(group * (nkt // group), nkt, tile, cnts[0])
        return jnp.sum(rest + sum(cnts[1:]), axis=0, keepdims=True)

    n_causal = (q_g + 1).astype(F32)
    all_sel = n_causal <= kf
    zeros = jnp.zeros((1, TQI), F32)
    state0 = dict(
        it=jnp.int32(0),
        lo=mn, hi=_key_float(_float_key(mx) + 1), clo=n_causal, chi=zeros,
        glo=jnp.log(jnp.maximum(n_causal, kf + 1.0) / kf), ghi=jnp.full((1, TQI), math.log(0.5 / kf), F32),
        thr=jnp.where(all_sel, NEG_BIG, 0.0).astype(F32),
        done=all_sel.astype(F32), tie=zeros, side=zeros, forced=zeros, use_forced=zeros,
    )

    def above(x):
        return jnp.where(jnp.abs(x) < F32_TINY, F32_TINY, _key_float(_float_key(x) + 1))

    def bracket_edges(lo, hi):
        def tile(kt, carry):
            a, b = carry
            for r in range(TKI // sub):
                blk = sc_ref[kt, r * sub:(r + 1) * sub, :]
                a = jnp.maximum(a, jnp.where(blk < hi, blk, -jnp.inf))
                b = jnp.minimum(b, jnp.where(blk >= lo, blk, jnp.inf))
            return a, b
        a, b = lax.fori_loop(0, nkt, tile, (jnp.full((sub, TQI), -jnp.inf, F32), jnp.full((sub, TQI), jnp.inf, F32)))
        return jnp.max(a, axis=0, keepdims=True), jnp.min(b, axis=0, keepdims=True)

    def cond(st):
        return jnp.logical_and(st["it"] < SEARCH_CAP, jnp.min(st["done"]) < 0.5)

    def count_step(st):
        it, lo, hi, glo, ghi = st["it"], st["lo"], st["hi"], st["glo"], st["ghi"]
        lo_k, hi_k = _float_key(lo), _float_key(hi)
        above_lo = above(lo)
        below_hi = jnp.where(hi == F32_TINY, 0.0, _key_float(hi_k - 1))
        probe = jnp.logical_and(st["done"] < 0.5, above_lo < hi)
        tie_now = jnp.logical_and(st["done"] < 0.5, above_lo >= hi)
        frac = jnp.where(it < 24, glo / (glo - ghi), 0.5)
        pf = lo + (hi - lo) * frac
        pf = jnp.where(it < 64, pf, _key_float((lo_k >> 1) + (hi_k >> 1) + (lo_k & hi_k & 1)))
        pf = jnp.where(it == 0, F32_TINY, jnp.where(jnp.logical_and(it == 1, hi == F32_TINY), 0.0, pf))
        pf = jnp.where(st["use_forced"] > 0.5, st["forced"], pf)
        pf = jnp.where(probe, jnp.minimum(jnp.maximum(pf, above_lo), below_hi), lo)
        c = count(pf, False)
        hit = jnp.logical_and(probe, c == kf)
        up = jnp.logical_and(probe, c > kf)
        dn = jnp.logical_and(probe, c < kf)
        g = jnp.log(jnp.maximum(c, 0.5) / kf)
        return dict(
            it=it + 1,
            lo=jnp.where(up, pf, lo), hi=jnp.where(dn, pf, hi),
            clo=jnp.where(up, c, st["clo"]), chi=jnp.where(dn, c, st["chi"]),
            glo=jnp.where(up, g, jnp.where(jnp.logical_and(dn, st["side"] < -0.5), 0.5 * glo, glo)),
            ghi=jnp.where(dn, g, jnp.where(jnp.logical_and(up, st["side"] > 0.5), 0.5 * ghi, ghi)),
            thr=jnp.where(hit, pf, jnp.where(tie_now, lo, st["thr"])),
            done=jnp.where(jnp.logical_or(hit, tie_now), 1.0, st["done"]),
            tie=jnp.where(tie_now, 1.0, st["tie"]),
            side=jnp.where(up, 1.0, jnp.where(dn, -1.0, st["side"])),
            forced=st["forced"], use_forced=zeros,
        )

    def edge_step(st):
        lo, hi = st["lo"], st["hi"]
        probe = jnp.logical_and(st["done"] < 0.5, above(lo) < hi)
        a, b = bracket_edges(lo, hi)
        one_above = st["chi"] == kf - 1.0
        one_below = st["clo"] == kf + 1.0
        new = dict(st)
        new.update(
            it=st["it"] + 1,
            lo=jnp.where(probe, b, lo), hi=jnp.where(probe, above(a), hi),
            forced=jnp.where(one_above, a, above(b)),
            use_forced=jnp.where(jnp.logical_and(probe, jnp.logical_or(one_above, one_below)), 1.0, 0.0),
        )
        return new

    def step(st):
        it = st["it"]
        is_edge = jnp.logical_and(it >= EDGE_FIRST, (it - EDGE_FIRST) % 3 == 0)
        return lax.cond(is_edge, edge_step, count_step, st)

    st = lax.fori_loop(0, WARM_PASSES, lambda _, s: count_step(s), state0)
    st = lax.while_loop(cond, step, st)
    thr = st["thr"]
    tie = st["tie"]
    any_tie = jnp.max(tie) > 0.5

    def emit(kt, sel):
        mask_ref[0, kt] = jnp.where(sel, 0.0, NEG_BIG).astype(mask_ref.dtype).T

    @pl.when(jnp.logical_not(any_tie))
    def _():
        def body(kt, carry):
            emit(kt, sc_ref[kt] >= thr)
            return carry
        lax.fori_loop(0, nkt, body, 0)

    @pl.when(any_tie)
    def _():
        need = jnp.where(tie > 0.5, kf - count(thr, True), float(2 * ntile * TKI))
        r = lax.broadcasted_iota(jnp.int32, (TKI, TKI), 0)
        c = lax.broadcasted_iota(jnp.int32, (TKI, TKI), 1)
        prefix = jnp.where(c <= r, 1.0, 0.0).astype(BF16)

        def body(kt, seen):
            s = sc_ref[kt]
            eq = s == thr
            rank = seen + _dot(prefix, jnp.where(eq, 1.0, 0.0).astype(BF16))
            emit(kt, jnp.logical_or(s > thr, jnp.logical_and(eq, rank <= need)))
            return seen + jnp.sum(jnp.where(eq, 1.0, 0.0), axis=0, keepdims=True)
        lax.fori_loop(0, nkt, body, jnp.zeros((1, TQI), F32))

    def fill(kt, carry):
        mask_ref[0, kt] = jnp.full((TQI, TKI), NEG_BIG, mask_ref.dtype)
        return carry
    lax.fori_loop(nkt, ntile, fill, 0)


def _index_mask(qi, wt, ki2, top_k):
    S = qi.shape[0]
    nq, nk = S // TQI, S // TKI
    return pl.pallas_call(
        functools.partial(_index_kernel, top_k=top_k),
        grid=(nq,),
        in_specs=[pl.BlockSpec((TQI, IDX_HEADS * IDX_DIM), lambda i: (i, 0)),
                  pl.BlockSpec((IDX_HEADS, TQI), lambda i: (0, i)),
                  pl.BlockSpec((S, LANES), lambda i: (0, 0))],
        out_specs=pl.BlockSpec((1, nk, TQI, TKI), lambda i: (i, 0, 0, 0)),
        out_shape=jax.ShapeDtypeStruct((nq, nk, TQI, TKI), BF16),
        scratch_shapes=[pltpu.VMEM((nk, TKI, TQI), F32),
                        pltpu.VMEM((IDX_HEADS, TQI, LANES), BF16)],
        compiler_params=_params(("parallel",)),
        name="index_mask",
    )(qi, wt, ki2)


def _attn_kernel(qt_ref, kt_ref, q_ref, k_ref, v_ref, m_ref, bd_ref, be_ref, o_ref, *state):
    m_sc, acc_sc = state[:N_HEADS], state[N_HEADS:]
    step = pl.program_id(0)
    qi = qt_ref[step]
    ki = kt_ref[step]
    nsub = TA // SUB

    @pl.when(ki == 0)
    def _():
        for h in range(N_HEADS):
            m_sc[h][...] = jnp.full((TA, LANES), NEG_BIG, F32)
            acc_sc[h][...] = jnp.zeros((TA, LANES), F32)

    lane = lax.broadcasted_iota(jnp.int32, (TA, LANES), 1)
    first = lane < HEAD_DIM

    def bias_tile(h, diagonal):
        zero = jnp.zeros((SUB, SUB), F32)
        rows = []
        for a in range(nsub):
            if diagonal:
                blks = [bd_ref[h] if b == a else (be_ref[h] if b == a - 1 else zero) for b in range(nsub)]
            else:
                blks = [be_ref[h] if (a == 0 and b == nsub - 1) else zero for b in range(nsub)]
            rows.append(jnp.concatenate(blks, axis=1))
        return jnp.concatenate(rows, axis=0)

    def update(near, diagonal):
        maskf = jnp.concatenate([m_ref[a, 0] for a in range(TA // TQI)], axis=0).astype(F32)

        def logits(h):
            cols = slice((h // 2) * LANES, (h // 2 + 1) * LANES)
            qp = q_ref[:, cols]
            mine = first if h % 2 == 0 else jnp.logical_not(first)
            s = _dot_nt(jnp.where(mine, qp, jnp.zeros_like(qp)), k_ref[:, cols]) + maskf
            return s + bias_tile(h, diagonal) if near else s

        s = logits(0)
        for h in range(N_HEADS):
            s_next = logits(h + 1) if h + 1 < N_HEADS else None
            vp = v_ref[:, (h // 2) * LANES:(h // 2 + 1) * LANES]
            mine = first if h % 2 == 0 else jnp.logical_not(first)
            vh = jnp.where(mine, vp, jnp.ones_like(vp))
            m_prev = m_sc[h][...]
            m_next = jnp.maximum(m_prev, jnp.max(s, axis=1, keepdims=True))
            pexp = jnp.exp2(s - jnp.concatenate([m_next] * (TA // LANES), axis=1)).astype(BF16)
            acc_sc[h][...] = jnp.exp2(m_prev - m_next) * acc_sc[h][...] + _dot(pexp, vh)
            m_sc[h][...] = m_next
            s = s_next

    @pl.when(ki < qi - 1)
    def _():
        update(False, False)

    @pl.when(ki == qi - 1)
    def _():
        update(True, False)

    @pl.when(ki == qi)
    def _():
        update(True, True)
        for p in range(N_HEADS // 2):
            a0, a1 = acc_sc[2 * p][...], acc_sc[2 * p + 1][...]
            d0 = pltpu.roll(a0, HEAD_DIM, axis=1)
            d1 = pltpu.roll(a1, HEAD_DIM, axis=1)
            o_ref[:, p * LANES:(p + 1) * LANES] = jnp.where(first, a0 / d0, a1 / d1).astype(o_ref.dtype)


def _attention(q, k, v, mask4, bias_d, bias_e):
    S = q.shape[0]
    nb = S // TA
    pairs = [(a, b) for a in range(nb) for b in range(a + 1)]
    qtab = jnp.asarray(np.array([a for a, _ in pairs], np.int32))
    ktab = jnp.asarray(np.array([b for _, b in pairs], np.int32))
    grid_spec = pltpu.PrefetchScalarGridSpec(
        num_scalar_prefetch=2,
        grid=(len(pairs),),
        in_specs=[pl.BlockSpec((TA, ATTN_WIDTH), lambda s, qt, kt: (qt[s], 0)),
                  pl.BlockSpec((TA, ATTN_WIDTH), lambda s, qt, kt: (kt[s], 0)),
                  pl.BlockSpec((TA, ATTN_WIDTH), lambda s, qt, kt: (kt[s], 0)),
                  pl.BlockSpec((TA // TQI, TA // TKI, TQI, TKI), lambda s, qt, kt: (qt[s], kt[s], 0, 0)),
                  pl.BlockSpec(bias_d.shape, lambda s, qt, kt: (0, 0, 0)),
                  pl.BlockSpec(bias_e.shape, lambda s, qt, kt: (0, 0, 0))],
        out_specs=pl.BlockSpec((TA, ATTN_WIDTH), lambda s, qt, kt: (qt[s], 0)),
        scratch_shapes=[pltpu.VMEM((TA, LANES), F32)] * (2 * N_HEADS),
    )
    return pl.pallas_call(
        _attn_kernel,
        grid_spec=grid_spec,
        out_shape=jax.ShapeDtypeStruct((S, ATTN_WIDTH), BF16),
        compiler_params=_params(("arbitrary",)),
        name="attn",
    )(qtab, ktab, q, k, v, mask4, bias_d, bias_e)


def _relative_bias_blocks(rel_bias):
    dist = jnp.arange(2 * SUB, dtype=jnp.int32)
    max_exact = N_BUCKETS // 2
    dist_f = jnp.maximum(dist, 1).astype(F32)
    large = max_exact + (jnp.log(dist_f / max_exact) / math.log(MAX_DISTANCE / max_exact)
                         * (N_BUCKETS - max_exact)).astype(jnp.int32)
    bucket = jnp.where(dist < max_exact, dist, jnp.minimum(large, N_BUCKETS - 1))
    table = ((rel_bias[bucket] - rel_bias[N_BUCKETS - 1]) * LOG2E).astype(F32)
    i = jnp.arange(SUB)[:, None]
    j = jnp.arange(SUB)[None, :]

    def toeplitz(d):
        onehot = (d[:, :, None] == dist[None, None, :]).astype(F32)
        return jnp.einsum("ijd,dh->hij", onehot, table, precision=lax.Precision.HIGHEST)

    return toeplitz(jnp.clip(i - j, 0, 2 * SUB - 1)), toeplitz(SUB + i - j)


def _pack_bf16_pair(a, b):
    def rnd(x):
        bits = lax.bitcast_convert_type(x, jnp.uint32)
        return bits + jnp.uint32(0x7FFF) + ((bits >> 16) & jnp.uint32(1))
    return (rnd(a) >> 16) | (rnd(b) & jnp.uint32(0xFFFF0000))


def _unpack_bf16_pair(p):
    lo = lax.bitcast_convert_type(p << 16, F32)
    hi = lax.bitcast_convert_type(p & jnp.uint32(0xFFFF0000), F32)
    return lo.astype(BF16), hi.astype(BF16)


def _post_kernel(ya_ref, sga_ref, pc_ref, x_ref, woa_ref, wout_ref, g_ref, b_ref, wr_ref, wrl_ref, br_ref,
                 x1_ref, x1p_ref, ridx_ref, rgate_ref):
    y_attn = _dot(ya_ref[...], woa_ref[...])
    merged = sga_ref[...] * y_attn + pc_ref[...]
    mix = _dot(merged.astype(BF16), wout_ref[...])
    x1 = _layer_norm(DEEPNORM_ALPHA * x_ref[...] + mix, g_ref[...], b_ref[...])
    x1_ref[...] = x1
    half = D_MODEL // 2
    x1p_ref[:, 0, :] = _pack_bf16_pair(x1[:, :half], x1[:, half:])

    x1_hi = x1.astype(BF16)
    x1_lo = (x1 - x1_hi.astype(F32)).astype(BF16)
    logits = (_dot(x1_hi, wr_ref[...]) + (_dot(x1_hi, wrl_ref[...]) + _dot(x1_lo, wr_ref[...]))) + br_ref[...]
    lane = lax.broadcasted_iota(jnp.int32, logits.shape, 1).astype(F32)
    cur = logits
    vals, idxs = [], []
    for _ in range(TOP_K_EXPERTS):
        m = jnp.max(cur, axis=1, keepdims=True)
        ix = jnp.min(jnp.where(cur == m, lane, float(LANES)), axis=1, keepdims=True)
        vals.append(m)
        idxs.append(ix)
        cur = jnp.where(lane == ix, -jnp.inf, cur)
    exps = [jnp.exp(v - vals[0]) for v in vals]
    denom = exps[0]
    for e in exps[1:]:
        denom = denom + e
    ridx = jnp.zeros_like(logits)
    rgate = jnp.zeros_like(logits)
    for j in range(TOP_K_EXPERTS):
        ridx = jnp.where(lane == float(j), idxs[j], ridx)
        rgate = jnp.where(lane == float(j), exps[j] / denom, rgate)
    ridx_ref[...] = ridx.astype(jnp.int32)
    rgate_ref[...] = rgate


def _post(ya, sg, pc, x, woa, wout, g, b, wr, wrl, br):
    S = x.shape[0]
    full = lambda a: pl.BlockSpec(a.shape, lambda i: (0, 0))
    row = lambda n: pl.BlockSpec((TM, n), lambda i: (i, 0))
    return pl.pallas_call(
        _post_kernel,
        grid=(S // TM,),
        in_specs=[row(ATTN_WIDTH), row(D_MODEL), row(D_MODEL), row(D_MODEL)] + [full(a) for a in (woa, wout, g, b, wr, wrl, br)],
        out_specs=[row(D_MODEL), pl.BlockSpec((TM, 1, D_MODEL // 2), lambda i: (i, 0, 0)), row(LANES), row(LANES)],
        out_shape=[jax.ShapeDtypeStruct((S, D_MODEL), F32), jax.ShapeDtypeStruct((S, 1, D_MODEL // 2), jnp.uint32),
                   jax.ShapeDtypeStruct((S, LANES), jnp.int32), jax.ShapeDtypeStruct((S, LANES), F32)],
        compiler_params=_params(("parallel",)),
        name="post",
    )(ya, sg, pc, x, woa, wout, g, b, wr, wrl, br)


def _expert_kernel(te_ref, nv_ref, nused_ref, tok_ref, tokn_ref, dst_ref, x_hbm, wgu_ref, bgu_ref, wd_ref, bd_ref,
                   ys_hbm, xs_buf, y_buf, wgu_sc, wd_sc, gsem, ssem):
    t = pl.program_id(0)
    nused = nused_ref[0]
    cur = t % 2
    nxt = 1 - cur

    def start_gather(idx_ref, slot):
        def body(r, carry):
            pltpu.make_async_copy(x_hbm.at[idx_ref[0, 0, r]], xs_buf.at[slot, pl.ds(r, 1)], gsem.at[slot]).start()
            return carry
        lax.fori_loop(0, TME, body, 0, unroll=DMA_UNROLL)

    def wait_gather(slot):
        pltpu.make_async_copy(xs_buf.at[slot], xs_buf.at[slot], gsem.at[slot]).wait()

    def wait_scatter(slot, n):
        pltpu.make_async_copy(ys_hbm.at[pl.ds(0, n)], ys_hbm.at[pl.ds(0, n)], ssem.at[slot]).wait()

    @pl.when(t == 0)
    def _():
        start_gather(tok_ref, 0)

    @pl.when(t + 1 < nused)
    def _():
        start_gather(tokn_ref, nxt)

    e = te_ref[t]
    e_prev = te_ref[jnp.maximum(t - 1, 0)]

    @pl.when(jnp.logical_or(t == 0, e != e_prev))
    def _():
        wgu_sc[...] = wgu_ref[0].astype(BF16)
        wd_sc[...] = wd_ref[0].astype(BF16)

    @pl.when(t < nused)
    def _():
        wait_gather(cur)

        @pl.when(t >= 2)
        def _():
            wait_scatter(cur, nv_ref[jnp.maximum(t - 2, 0)])

        half = D_MODEL // 2
        lo, hi = _unpack_bf16_pair(xs_buf[cur])
        gu = _dot(lo, wgu_sc[0:half, :]) + _dot(hi, wgu_sc[half:, :]) + bgu_ref[0]
        g = jnp.minimum(gu[:, :D_EXPERT], SWIGLU_LIMIT)
        u = jnp.clip(gu[:, D_EXPERT:], -SWIGLU_LIMIT, SWIGLU_LIMIT)
        act = (u + 1.0) * (g * _sigmoid(SWIGLU_ALPHA * g))
        y_buf[cur] = _dot(act.astype(BF16), wd_sc[...]) + bd_ref[0]

        def scatter(r, carry):
            pltpu.make_async_copy(y_buf.at[cur, pl.ds(r, 1)], ys_hbm.at[dst_ref[0, 0, r]], ssem.at[cur]).start()
            return carry

        def scatter_group(c, carry):
            for j in range(DMA_UNROLL):
                scatter(c * DMA_UNROLL + j, carry)
            return carry
        n_groups = nv_ref[t] // DMA_UNROLL
        lax.fori_loop(0, n_groups, scatter_group, 0)
        lax.fori_loop(n_groups * DMA_UNROLL, nv_ref[t], scatter, 0)

        @pl.when(t == nused - 1)
        def _():
            @pl.when(t >= 1)
            def _():
                wait_scatter(nxt, nv_ref[jnp.maximum(t - 1, 0)])
            wait_scatter(cur, nv_ref[t])


def _experts(tile_e, tile_nv, nused, row_token, row_dst, x1p, wgu, bgu, wd, bd, n_tiles, n_dst):
    idx = lambda a: a.reshape(n_tiles, 1, TME)
    smem_tile = lambda f: pl.BlockSpec((1, 1, TME), f, memory_space=pltpu.SMEM)
    grid_spec = pltpu.PrefetchScalarGridSpec(
        num_scalar_prefetch=3,
        grid=(n_tiles,),
        in_specs=[smem_tile(lambda t, te, nv, nu: (t, 0, 0)),
                  smem_tile(lambda t, te, nv, nu: (jnp.minimum(t + 1, n_tiles - 1), 0, 0)),
                  smem_tile(lambda t, te, nv, nu: (t, 0, 0)),
                  pl.BlockSpec(memory_space=pl.ANY),
                  pl.BlockSpec((1, D_MODEL, 2 * D_EXPERT), lambda t, te, nv, nu: (te[t], 0, 0)),
                  pl.BlockSpec((1, 1, 2 * D_EXPERT), lambda t, te, nv, nu: (te[t], 0, 0)),
                  pl.BlockSpec((1, D_EXPERT, D_MODEL), lambda t, te, nv, nu: (te[t], 0, 0)),
                  pl.BlockSpec((1, 1, D_MODEL), lambda t, te, nv, nu: (te[t], 0, 0))],
        out_specs=pl.BlockSpec(memory_space=pl.ANY),
        scratch_shapes=[pltpu.VMEM((2, TME, D_MODEL // 2), jnp.uint32), pltpu.VMEM((2, TME, D_MODEL), F32),
                        pltpu.VMEM((D_MODEL, 2 * D_EXPERT), BF16), pltpu.VMEM((D_EXPERT, D_MODEL), BF16),
                        pltpu.SemaphoreType.DMA((2,)), pltpu.SemaphoreType.DMA((2,))],
    )
    return pl.pallas_call(
        _expert_kernel,
        grid_spec=grid_spec,
        out_shape=jax.ShapeDtypeStruct((n_dst, 1, D_MODEL), F32),
        compiler_params=_params(("arbitrary",)),
        name="moe_experts",
    )(tile_e, tile_nv, nused, idx(row_token), idx(row_token), idx(row_dst), x1p, wgu, bgu, wd, bd)


def _routing_tables(ridx, S):
    n_flat = S * TOP_K_EXPERTS
    n_tiles = n_flat // TME + N_EXPERTS
    flat_e = ridx.reshape(n_flat)
    assert N_EXPERTS * n_flat < 2 ** 31
    order = jnp.sort(flat_e * n_flat + jnp.arange(n_flat, dtype=jnp.int32)) % n_flat
    counts = jnp.sum(flat_e[:, None] == jnp.arange(N_EXPERTS, dtype=jnp.int32)[None, :], axis=0, dtype=jnp.int32)
    padded = ((counts + TME - 1) // TME) * TME
    pad_end = jnp.cumsum(padded)
    pad_start = pad_end - padded
    grp_start = jnp.cumsum(counts) - counts
    nused = (pad_end[-1] // TME).astype(jnp.int32).reshape(1)
    tile_row0 = jnp.arange(n_tiles, dtype=jnp.int32) * TME
    tile_e = jnp.minimum(jnp.sum(tile_row0[:, None] >= pad_end[None, :], axis=1), N_EXPERTS - 1).astype(jnp.int32)
    tile_rank0 = tile_row0 - pad_start[tile_e]
    tile_nv = jnp.where(tile_row0 < pad_end[-1], jnp.clip(counts[tile_e] - tile_rank0, 0, TME), 0).astype(jnp.int32)
    within = jnp.arange(TME, dtype=jnp.int32)[None, :]
    valid = within < tile_nv[:, None]
    src_flat = lax.optimization_barrier(order[jnp.clip((grp_start[tile_e] + tile_rank0)[:, None] + within, 0, n_flat - 1)])
    row_token = jnp.where(valid, src_flat // TOP_K_EXPERTS, 0).astype(jnp.int32)
    row_dst = jnp.where(valid, (src_flat % TOP_K_EXPERTS) * S + src_flat // TOP_K_EXPERTS, 0).astype(jnp.int32)
    return tile_e, tile_nv, nused, row_token, row_dst, n_tiles


def _final_kernel(x1_ref, y0_ref, y1_ref, y2_ref, y3_ref, rg_ref, p_ref, wpg_ref, wpp_ref, g_ref, b_ref, o_ref):
    h = DEEPNORM_ALPHA * x1_ref[...]
    rg = rg_ref[...]
    for j, y_ref in enumerate((y0_ref, y1_ref, y2_ref, y3_ref)):
        h = h + rg[:, j:j + 1] * y_ref[:, 0, :]
    ple = _sigmoid(_dot(h.astype(BF16), wpg_ref[...])) * _dot(p_ref[...].astype(BF16), wpp_ref[...])
    o_ref[...] = _layer_norm(h + ple, g_ref[...], b_ref[...])


def _final(x1, ys, rgate, p, wpg, wpp, g, b):
    S = x1.shape[0]
    full = lambda a: pl.BlockSpec(a.shape, lambda i: (0, 0))
    row = lambda n: pl.BlockSpec((TM, n), lambda i: (i, 0))
    return pl.pallas_call(
        _final_kernel,
        grid=(S // TM,),
        in_specs=[row(D_MODEL)]
        + [pl.BlockSpec((TM, 1, D_MODEL), lambda i, j=j: (j * (S // TM) + i, 0, 0)) for j in range(TOP_K_EXPERTS)]
        + [row(LANES), row(PLE_DIM), full(wpg), full(wpp), full(g), full(b)],
        out_specs=row(D_MODEL),
        out_shape=jax.ShapeDtypeStruct((S, D_MODEL), F32),
        compiler_params=_params(("parallel",)),
        name="final",
    )(x1, ys, ys, ys, ys, rgate, p, wpg, wpp, g, b)


def _layer(x, p, w_in, b_in, w_o_attn, w_dw, b_dw, conv_ln_g, conv_ln_b, w_o_conv, w_out, ln1_g, ln1_b,
           w_router, b_router, w_gate_up, b_gate_up, w_down, b_down, w_ple_gate, w_ple_proj, ln2_g, ln2_b,
           rel_bias):
    S = x.shape[0]
    assert S % TM == 0 and S % TA == 0 and S % TKI == 0 and (S * TOP_K_EXPERTS) % TME == 0
    top_k = min(TOPK_MAX, S // 4)
    row2 = lambda a: a.reshape(1, -1).astype(F32)

    o_q, o_qi, o_ki, o_cv, o_g = 0, 3 * ATTN_WIDTH, 3 * ATTN_WIDTH + IDX_HEADS * IDX_DIM, \
        3 * ATTN_WIDTH + IDX_HEADS * IDX_DIM + IDX_DIM + IDX_HEADS, \
        3 * ATTN_WIDTH + IDX_HEADS * IDX_DIM + IDX_DIM + IDX_HEADS + 2 * CONV_CH
    kw_pad = LANES - (IDX_DIM + IDX_HEADS)
    wkw = jnp.pad(w_in[:, o_ki:o_cv], ((0, 0), (0, kw_pad)))
    bkw = jnp.pad(b_in[o_ki:o_cv], (0, kw_pad))
    q, k, v, qi, kw, u, sg = _proj(
        x, w_in[:, o_q:o_qi].astype(BF16), row2(b_in[o_q:o_qi]),
        w_in[:, o_qi:o_ki].astype(BF16), row2(b_in[o_qi:o_ki]),
        wkw.astype(BF16), row2(bkw),
        w_in[:, o_cv:o_g].astype(BF16), row2(b_in[o_cv:o_g]),
        w_in[:, o_g:].astype(BF16), row2(b_in[o_g:]))

    part_conv = _conv(u, sg, w_dw, row2(b_dw), row2(conv_ln_g), row2(conv_ln_b), w_o_conv.astype(BF16))

    ki = kw[:, :IDX_DIM].astype(BF16)
    mask4 = _index_mask(qi, kw[:, IDX_DIM:IDX_DIM + IDX_HEADS].T, jnp.concatenate([ki, ki], axis=1), top_k)
    bias_d, bias_e = _relative_bias_blocks(rel_bias)
    y_attn = _attention(q, k, v, mask4, bias_d, bias_e)

    wr = jnp.pad(w_router, ((0, 0), (0, LANES - N_EXPERTS)))
    wr_hi = wr.astype(BF16)
    wr_lo = (wr - wr_hi.astype(F32)).astype(BF16)
    br = jnp.pad(b_router, (0, LANES - N_EXPERTS), constant_values=-jnp.inf)
    x1, x1p, ridx, rgate = _post(y_attn, sg, part_conv, x, w_o_attn.astype(BF16), w_out.astype(BF16),
                                 row2(ln1_g), row2(ln1_b), wr_hi, wr_lo, row2(br))

    tile_e, tile_nv, nused, row_token, row_dst, n_tiles = _routing_tables(ridx[:, :TOP_K_EXPERTS], S)
    ys = _experts(tile_e, tile_nv, nused, row_token, row_dst, x1p, w_gate_up, b_gate_up.reshape(N_EXPERTS, 1, -1),
                  w_down, b_down.reshape(N_EXPERTS, 1, -1), n_tiles, S * TOP_K_EXPERTS)
    return _final(x1, ys, rgate, p, w_ple_gate.astype(BF16), w_ple_proj.astype(BF16), row2(ln2_g), row2(ln2_b))


def kernel(x, p, w_in, b_in, w_o_attn, w_dw, b_dw, conv_ln_g, conv_ln_b, w_o_conv, w_out, ln1_g, ln1_b, w_router, b_router, w_gate_up, b_gate_up, w_down, b_down, w_ple_gate, w_ple_proj, ln2_g, ln2_b, rel_bias):
    assert x.shape[0] == 1 and p.shape[0] == DEPTH
    out = _layer(x[0], p[0, 0], w_in[0], b_in[0], w_o_attn[0], w_dw[0], b_dw[0], conv_ln_g[0], conv_ln_b[0],
                 w_o_conv[0], w_out[0], ln1_g[0], ln1_b[0], w_router[0], b_router[0], w_gate_up[0], b_gate_up[0],
                 w_down[0], b_down[0], w_ple_gate[0], w_ple_proj[0], ln2_g[0], ln2_b[0], rel_bias)
    return out[None]
```

```python
import functools
import math

import jax
import jax.numpy as jnp
import numpy as np
from jax import lax
from jax.experimental import pallas as pl
from jax.experimental.pallas import tpu as pltpu

F32 = jnp.float32
BF16 = jnp.bfloat16

D_MODEL = 1024
N_HEADS = 8
HEAD_DIM = 64
ATTN_WIDTH = N_HEADS * HEAD_DIM
ATTN_SCALE = HEAD_DIM ** -0.5
LOG2E = math.log2(math.e)
IDX_HEADS = 8
IDX_DIM = 64
IDX_SCALE = (IDX_HEADS ** -0.5) * (IDX_DIM ** -0.5)
TOPK_MAX = 256
CONV_CH = 512
CONV_WIDTH = 31
N_BUCKETS = 32
MAX_DISTANCE = 128
N_EXPERTS = 32
TOP_K_EXPERTS = 4
D_EXPERT = 1024
SWIGLU_LIMIT = 7.0
SWIGLU_ALPHA = 1.702
PLE_DIM = 256
LN_EPS = 1e-5
DEPTH = 1
DEEPNORM_ALPHA = (2 * DEPTH) ** 0.25

LANES = 128
NEG_BIG = -1e30
F32_TINY = float(np.finfo(np.float32).tiny)
VMEM_LIMIT = 56 * 1024 * 1024

TM = 512
TQI = 256
TKI = 512
WARM_PASSES = 8
EDGE_FIRST = 9
SEARCH_CAP = 128
TA = 512
SUB = 128
TME = 256
DMA_UNROLL = 8
ROUTE_ROWS = 8
CONV_HALO = 32


def _params(sem, vmem=VMEM_LIMIT):
    return pltpu.CompilerParams(dimension_semantics=sem, vmem_limit_bytes=vmem)


def _sigmoid(x):
    return 1.0 / (1.0 + jnp.exp(-x))


def _layer_norm(x, g, b):
    mu = jnp.mean(x, axis=-1, keepdims=True)
    xc = x - mu
    var = jnp.mean(xc * xc, axis=-1, keepdims=True)
    return xc * lax.rsqrt(var + LN_EPS) * g + b


def _dot(a, b):
    return jnp.dot(a, b, preferred_element_type=F32)


def _dot_nt(a, b):
    return lax.dot_general(a, b, (((1,), (1,)), ((), ())), preferred_element_type=F32)


def _proj_kernel(x_ref, wqkv_ref, bqkv_ref, wqi_ref, bqi_ref, wkw_ref, bkw_ref, wcv_ref, bcv_ref,
                 wg_ref, bg_ref, q_ref, k_ref, v_ref, qi_ref, kw_ref, u_ref, sg_ref):
    xb = x_ref[...].astype(BF16)
    qkv = _dot(xb, wqkv_ref[...]) + bqkv_ref[...]
    q_ref[...] = (qkv[:, :ATTN_WIDTH] * (ATTN_SCALE * LOG2E)).astype(BF16)
    k_ref[...] = qkv[:, ATTN_WIDTH:2 * ATTN_WIDTH].astype(BF16)
    v_ref[...] = qkv[:, 2 * ATTN_WIDTH:].astype(BF16)
    qi_ref[...] = (_dot(xb, wqi_ref[...]) + bqi_ref[...]).astype(BF16)
    kw = _dot(xb, wkw_ref[...]) + bkw_ref[...]
    lane = lax.broadcasted_iota(jnp.int32, kw.shape, 1)
    kw_ref[...] = jnp.where(lane >= IDX_DIM, kw * IDX_SCALE, kw)
    cv = _dot(xb, wcv_ref[...]) + bcv_ref[...]
    u_ref[...] = cv[:, :CONV_CH] * _sigmoid(cv[:, CONV_CH:])
    sg_ref[...] = _sigmoid(_dot(xb, wg_ref[...]) + bg_ref[...])


def _proj(x, wqkv, bqkv, wqi, bqi, wkw, bkw, wcv, bcv, wg, bg):
    S = x.shape[0]
    full = lambda a: pl.BlockSpec(a.shape, lambda i: (0, 0))
    row = lambda n: pl.BlockSpec((TM, n), lambda i: (i, 0))
    outs = [(ATTN_WIDTH, BF16)] * 3 + [(IDX_HEADS * IDX_DIM, BF16), (LANES, F32), (CONV_CH, F32),
                                        (2 * D_MODEL, F32)]
    return pl.pallas_call(
        _proj_kernel,
        grid=(S // TM,),
        in_specs=[row(D_MODEL)] + [full(a) for a in (wqkv, bqkv, wqi, bqi, wkw, bkw, wcv, bcv, wg, bg)],
        out_specs=[row(n) for n, _ in outs],
        out_shape=[jax.ShapeDtypeStruct((S, n), dt) for n, dt in outs],
        compiler_params=_params(("parallel",)),
        name="proj",
    )(x, wqkv, bqkv, wqi, bqi, wkw, bkw, wcv, bcv, wg, bg)


def _conv_kernel(u_ref, wdw_ref, bdw_ref, lng_ref, lnb_ref, wo_ref, sgc_ref, o_ref, buf_ref, sh_ref):
    @pl.when(pl.program_id(0) == 0)
    def _():
        buf_ref[0:CONV_HALO, :] = jnp.zeros((CONV_HALO, CONV_CH), F32)

    buf_ref[CONV_HALO:CONV_HALO + TM, :] = u_ref[...]
    base = CONV_HALO - (CONV_WIDTH - 1)
    acc = jnp.zeros((TM, CONV_CH), F32) + bdw_ref[...]
    sub = 8
    for r in range(sub):
        taps = [j for j in range(CONV_WIDTH) if (base + j) % sub == r]
        span = max(base + j - r for j in taps) + TM
        sh_ref[0:span, :] = buf_ref[r:r + span, :]
        for j in taps:
            off = base + j - r
            acc = acc + wdw_ref[j:j + 1, :] * sh_ref[off:off + TM, :]
    buf_ref[0:CONV_HALO, :] = buf_ref[TM:TM + CONV_HALO, :]
    y = _layer_norm(acc, lng_ref[...], lnb_ref[...])
    y = y * _sigmoid(y)
    o_ref[...] = sgc_ref[...] * _dot(y.astype(BF16), wo_ref[...])


def _conv(u, sg, wdw, bdw, lng, lnb, wo):
    S = u.shape[0]
    full = lambda a: pl.BlockSpec(a.shape, lambda i: (0, 0))
    return pl.pallas_call(
        _conv_kernel,
        grid=(S // TM,),
        in_specs=[pl.BlockSpec((TM, CONV_CH), lambda i: (i, 0)), full(wdw), full(bdw), full(lng), full(lnb),
                  full(wo), pl.BlockSpec((TM, D_MODEL), lambda i: (i, 1))],
        out_specs=pl.BlockSpec((TM, D_MODEL), lambda i: (i, 0)),
        out_shape=jax.ShapeDtypeStruct((S, D_MODEL), F32),
        scratch_shapes=[pltpu.VMEM((TM + CONV_HALO, CONV_CH), F32), pltpu.VMEM((TM + CONV_HALO, CONV_CH), F32)],
        compiler_params=_params(("arbitrary",)),
        name="conv",
    )(u, wdw, bdw, lng, lnb, wo, sg)


def _float_key(f):
    b = lax.bitcast_convert_type(f, jnp.int32)
    return b ^ ((b >> 31) & jnp.int32(0x7FFFFFFF))


def _key_float(k):
    b = k ^ ((k >> 31) & jnp.int32(0x7FFFFFFF))
    return lax.bitcast_convert_type(b, F32)


def _index_kernel(qi_ref, wt_ref, ki_ref, mask_ref, sc_ref, qh_ref, *, top_k):
    i = pl.program_id(0)
    nkt = (i * TQI + TQI + TKI - 1) // TKI
    ntile = sc_ref.shape[0]
    q_g = i * TQI + lax.broadcasted_iota(jnp.int32, (1, TQI), 1)
    lane = lax.broadcasted_iota(jnp.int32, (TQI, LANES), 1)

    for h in range(IDX_HEADS):
        qp = qi_ref[:, (h // 2) * LANES:(h // 2 + 1) * LANES]
        keep = (lane < IDX_DIM) if h % 2 == 0 else (lane >= IDX_DIM)
        qh_ref[h] = jnp.where(keep, qp, jnp.zeros_like(qp))

    def score_tile(kt, causal):
        kb = ki_ref[pl.ds(pl.multiple_of(kt * TKI, TKI), TKI), :]
        acc = jnp.zeros((TKI, TQI), F32)
        for h in range(IDX_HEADS):
            acc = acc + wt_ref[h:h + 1, :] * jnp.maximum(_dot_nt(kb, qh_ref[h]), 0.0)
        if causal:
            key_g = kt * TKI + lax.broadcasted_iota(jnp.int32, (TKI, TQI), 0)
            ok = key_g <= q_g
            lo_src = jnp.where(ok, acc, jnp.inf)
            acc = jnp.where(ok, acc, -jnp.inf)
        else:
            lo_src = acc
        sc_ref[kt] = acc
        return jnp.max(acc, axis=0, keepdims=True), jnp.min(lo_src, axis=0, keepdims=True)

    def score_body(kt, carry):
        mx, mn = carry
        tmx, tmn = score_tile(kt, False)
        return jnp.maximum(mx, tmx), jnp.minimum(mn, tmn)

    mx0 = jnp.full((1, TQI), -jnp.inf, F32)
    mn0 = jnp.full((1, TQI), jnp.inf, F32)
    mx, mn = lax.fori_loop(0, nkt - 1, score_body, (mx0, mn0))
    tmx, tmn = score_tile(nkt - 1, True)
    mx = jnp.maximum(mx, tmx)
    mn = jnp.minimum(mn, tmn)
    kf = float(top_k)
    sub = 8

    def count(pivot, strict):
        def tile(kt, cnt):
            for r in range(TKI // sub):
                blk = sc_ref[kt, r * sub:(r + 1) * sub, :]
                hit = (blk > pivot) if strict else (blk >= pivot)
                cnt = cnt + jnp.where(hit, 1.0, 0.0)
            return cnt

        group = 4

        def tiles(j, cnts):
            return tuple(tile(group * j + g, cnts[g]) for g in range(group))

        zero = jnp.zeros((sub, TQI), F32)
        cnts = lax.fori_loop(0, nkt // group, tiles, (zero,) * group)
        rest = lax.fori_loop(group * (nkt // group), nkt, tile, cnts[0])
        return jnp.sum(rest + sum(cnts[1:]), axis=0, keepdims=True)

    n_causal = (q_g + 1).astype(F32)
    all_sel = n_causal <= kf
    zeros = jnp.zeros((1, TQI), F32)
    state0 = dict(
        it=jnp.int32(0),
        lo=mn, hi=_key_float(_float_key(mx) + 1), clo=n_causal, chi=zeros,
        glo=jnp.log(jnp.maximum(n_causal, kf + 1.0) / kf), ghi=jnp.full((1, TQI), math.log(0.5 / kf), F32),
        thr=jnp.where(all_sel, NEG_BIG, 0.0).astype(F32),
        done=all_sel.astype(F32), tie=zeros, side=zeros, forced=zeros, use_forced=zeros,
    )

    def above(x):
        return jnp.where(jnp.abs(x) < F32_TINY, F32_TINY, _key_float(_float_key(x) + 1))

    def bracket_edges(lo, hi):
        def tile(kt, carry):
            a, b = carry
            for r in range(TKI // sub):
                blk = sc_ref[kt, r * sub:(r + 1) * sub, :]
                a = jnp.maximum(a, jnp.where(blk < hi, blk, -jnp.inf))
                b = jnp.minimum(b, jnp.where(blk >= lo, blk, jnp.inf))
            return a, b
        a, b = lax.fori_loop(0, nkt, tile, (jnp.full((sub, TQI), -jnp.inf, F32), jnp.full((sub, TQI), jnp.inf, F32)))
        return jnp.max(a, axis=0, keepdims=True), jnp.min(b, axis=0, keepdims=True)

    def cond(st):
        return jnp.logical_and(st["it"] < SEARCH_CAP, jnp.min(st["done"]) < 0.5)

    def count_step(st):
        it, lo, hi, glo, ghi = st["it"], st["lo"], st["hi"], st["glo"], st["ghi"]
        lo_k, hi_k = _float_key(lo), _float_key(hi)
        above_lo = above(lo)
        below_hi = jnp.where(hi == F32_TINY, 0.0, _key_float(hi_k - 1))
        probe = jnp.logical_and(st["done"] < 0.5, above_lo < hi)
        tie_now = jnp.logical_and(st["done"] < 0.5, above_lo >= hi)
        frac = jnp.where(it < 24, glo / (glo - ghi), 0.5)
        pf = lo + (hi - lo) * frac
        pf = jnp.where(it < 64, pf, _key_float((lo_k >> 1) + (hi_k >> 1) + (lo_k & hi_k & 1)))
        pf = jnp.where(it == 0, F32_TINY, jnp.where(jnp.logical_and(it == 1, hi == F32_TINY), 0.0, pf))
        pf = jnp.where(st["use_forced"] > 0.5, st["forced"], pf)
        pf = jnp.where(probe, jnp.minimum(jnp.maximum(pf, above_lo), below_hi), lo)
        c = count(pf, False)
        hit = jnp.logical_and(probe, c == kf)
        up = jnp.logical_and(probe, c > kf)
        dn = jnp.logical_and(probe, c < kf)
        g = jnp.log(jnp.maximum(c, 0.5) / kf)
        return dict(
            it=it + 1,
            lo=jnp.where(up, pf, lo), hi=jnp.where(dn, pf, hi),
            clo=jnp.where(up, c, st["clo"]), chi=jnp.where(dn, c, st["chi"]),
            glo=jnp.where(up, g, jnp.where(jnp.logical_and(dn, st["side"] < -0.5), 0.5 * glo, glo)),
            ghi=jnp.where(dn, g, jnp.where(jnp.logical_and(up, st["side"] > 0.5), 0.5 * ghi, ghi)),
            thr=jnp.where(hit, pf, jnp.where(tie_now, lo, st["thr"])),
            done=jnp.where(jnp.logical_or(hit, tie_now), 1.0, st["done"]),
            tie=jnp.where(tie_now, 1.0, st["tie"]),
            side=jnp.where(up, 1.0, jnp.where(dn, -1.0, st["side"])),
            forced=st["forced"], use_forced=zeros,
        )

    def edge_step(st):
        lo, hi = st["lo"], st["hi"]
        probe = jnp.logical_and(st["done"] < 0.5, above(lo) < hi)
        a, b = bracket_edges(lo, hi)
        one_above = st["chi"] == kf - 1.0
        one_below = st["clo"] == kf + 1.0
        new = dict(st)
        new.update(
            it=st["it"] + 1,
            lo=jnp.where(probe, b, lo), hi=jnp.where(probe, above(a), hi),
            forced=jnp.where(one_above, a, above(b)),
            use_forced=jnp.where(jnp.logical_and(probe, jnp.logical_or(one_above, one_below)), 1.0, 0.0),
        )
        return new

    def step(st):
        it = st["it"]
        is_edge = jnp.logical_and(it >= EDGE_FIRST, (it - EDGE_FIRST) % 3 == 0)
        return lax.cond(is_edge, edge_step, count_step, st)

    st = lax.fori_loop(0, WARM_PASSES, lambda _, s: count_step(s), state0)
    st = lax.while_loop(cond, step, st)
    thr = st["thr"]
    tie = st["tie"]
    any_tie = jnp.max(tie) > 0.5

    def emit(kt, sel):
        mask_ref[0, kt] = jnp.where(sel, 0.0, NEG_BIG).astype(mask_ref.dtype).T

    @pl.when(jnp.logical_not(any_tie))
    def _():
        def body(kt, carry):
            emit(kt, sc_ref[kt] >= thr)
            return carry
        lax.fori_loop(0, nkt, body, 0)

    @pl.when(any_tie)
    def _():
        need = jnp.where(tie > 0.5, kf - count(thr, True), float(2 * ntile * TKI))
        r = lax.broadcasted_iota(jnp.int32, (TKI, TKI), 0)
        c = lax.broadcasted_iota(jnp.int32, (TKI, TKI), 1)
        prefix = jnp.where(c <= r, 1.0, 0.0).astype(BF16)

        def body(kt, seen):
            s = sc_ref[kt]
            eq = s == thr
            rank = seen + _dot(prefix, jnp.where(eq, 1.0, 0.0).astype(BF16))
            emit(kt, jnp.logical_or(s > thr, jnp.logical_and(eq, rank <= need)))
            return seen + jnp.sum(jnp.where(eq, 1.0, 0.0), axis=0, keepdims=True)
        lax.fori_loop(0, nkt, body, jnp.zeros((1, TQI), F32))

    def fill(kt, carry):
        mask_ref[0, kt] = jnp.full((TQI, TKI), NEG_BIG, mask_ref.dtype)
        return carry
    lax.fori_loop(nkt, ntile, fill, 0)


def _index_mask(qi, wt, ki2, top_k):
    S = qi.shape[0]
    nq, nk = S // TQI, S // TKI
    return pl.pallas_call(
        functools.partial(_index_kernel, top_k=top_k),
        grid=(nq,),
        in_specs=[pl.BlockSpec((TQI, IDX_HEADS * IDX_DIM), lambda i: (i, 0)),
                  pl.BlockSpec((IDX_HEADS, TQI), lambda i: (0, i)),
                  pl.BlockSpec((S, LANES), lambda i: (0, 0))],
        out_specs=pl.BlockSpec((1, nk, TQI, TKI), lambda i: (i, 0, 0, 0)),
        out_shape=jax.ShapeDtypeStruct((nq, nk, TQI, TKI), BF16),
        scratch_shapes=[pltpu.VMEM((nk, TKI, TQI), F32),
                        pltpu.VMEM((IDX_HEADS, TQI, LANES), BF16)],
        compiler_params=_params(("parallel",)),
        name="index_mask",
    )(qi, wt, ki2)


def _attn_kernel(qt_ref, kt_ref, q_ref, k_ref, v_ref, m_ref, bd_ref, be_ref, o_ref, *state):
    m_sc, acc_sc = state[:N_HEADS], state[N_HEADS:]
    step = pl.program_id(0)
    qi = qt_ref[step]
    ki = kt_ref[step]
    nsub = TA // SUB

    @pl.when(ki == 0)
    def _():
        for h in range(N_HEADS):
            m_sc[h][...] = jnp.full((TA, LANES), NEG_BIG, F32)
            acc_sc[h][...] = jnp.zeros((TA, LANES), F32)

    lane = lax.broadcasted_iota(jnp.int32, (TA, LANES), 1)
    first = lane < HEAD_DIM

    def bias_tile(h, diagonal):
        zero = jnp.zeros((SUB, SUB), F32)
        rows = []
        for a in range(nsub):
            if diagonal:
                blks = [bd_ref[h] if b == a else (be_ref[h] if b == a - 1 else zero) for b in range(nsub)]
            else:
                blks = [be_ref[h] if (a == 0 and b == nsub - 1) else zero for b in range(nsub)]
            rows.append(jnp.concatenate(blks, axis=1))
        return jnp.concatenate(rows, axis=0)

    def update(near, diagonal):
        maskf = jnp.concatenate([m_ref[a, 0] for a in range(TA // TQI)], axis=0).astype(F32)

        def logits(h):
            cols = slice((h // 2) * LANES, (h // 2 + 1) * LANES)
            qp = q_ref[:, cols]
            mine = first if h % 2 == 0 else jnp.logical_not(first)
            s = _dot_nt(jnp.where(mine, qp, jnp.zeros_like(qp)), k_ref[:, cols]) + maskf
            return s + bias_tile(h, diagonal) if near else s

        s = logits(0)
        for h in range(N_HEADS):
            s_next = logits(h + 1) if h + 1 < N_HEADS else None
            vp = v_ref[:, (h // 2) * LANES:(h // 2 + 1) * LANES]
            mine = first if h % 2 == 0 else jnp.logical_not(first)
            vh = jnp.where(mine, vp, jnp.ones_like(vp))
            m_prev = m_sc[h][...]
            m_next = jnp.maximum(m_prev, jnp.max(s, axis=1, keepdims=True))
            pexp = jnp.exp2(s - jnp.concatenate([m_next] * (TA // LANES), axis=1)).astype(BF16)
            acc_sc[h][...] = jnp.exp2(m_prev - m_next) * acc_sc[h][...] + _dot(pexp, vh)
            m_sc[h][...] = m_next
            s = s_next

    @pl.when(ki < qi - 1)
    def _():
        update(False, False)

    @pl.when(ki == qi - 1)
    def _():
        update(True, False)

    @pl.when(ki == qi)
    def _():
        update(True, True)
        for p in range(N_HEADS // 2):
            a0, a1 = acc_sc[2 * p][...], acc_sc[2 * p + 1][...]
            d0 = pltpu.roll(a0, HEAD_DIM, axis=1)
            d1 = pltpu.roll(a1, HEAD_DIM, axis=1)
            o_ref[:, p * LANES:(p + 1) * LANES] = jnp.where(first, a0 / d0, a1 / d1).astype(o_ref.dtype)


def _attention(q, k, v, mask4, bias_d, bias_e):
    S = q.shape[0]
    nb = S // TA
    pairs = [(a, b) for a in range(nb) for b in range(a + 1)]
    qtab = jnp.asarray(np.array([a for a, _ in pairs], np.int32))
    ktab = jnp.asarray(np.array([b for _, b in pairs], np.int32))
    grid_spec = pltpu.PrefetchScalarGridSpec(
        num_scalar_prefetch=2,
        grid=(len(pairs),),
        in_specs=[pl.BlockSpec((TA, ATTN_WIDTH), lambda s, qt, kt: (qt[s], 0)),
                  pl.BlockSpec((TA, ATTN_WIDTH), lambda s, qt, kt: (kt[s], 0)),
                  pl.BlockSpec((TA, ATTN_WIDTH), lambda s, qt, kt: (kt[s], 0)),
                  pl.BlockSpec((TA // TQI, TA // TKI, TQI, TKI), lambda s, qt, kt: (qt[s], kt[s], 0, 0)),
                  pl.BlockSpec(bias_d.shape, lambda s, qt, kt: (0, 0, 0)),
                  pl.BlockSpec(bias_e.shape, lambda s, qt, kt: (0, 0, 0))],
        out_specs=pl.BlockSpec((TA, ATTN_WIDTH), lambda s, qt, kt: (qt[s], 0)),
        scratch_shapes=[pltpu.VMEM((TA, LANES), F32)] * (2 * N_HEADS),
    )
    return pl.pallas_call(
        _attn_kernel,
        grid_spec=grid_spec,
        out_shape=jax.ShapeDtypeStruct((S, ATTN_WIDTH), BF16),
        compiler_params=_params(("arbitrary",)),
        name="attn",
    )(qtab, ktab, q, k, v, mask4, bias_d, bias_e)


def _relative_bias_blocks(rel_bias):
    dist = jnp.arange(2 * SUB, dtype=jnp.int32)
    max_exact = N_BUCKETS // 2
    dist_f = jnp.maximum(dist, 1).astype(F32)
    large = max_exact + (jnp.log(dist_f / max_exact) / math.log(MAX_DISTANCE / max_exact)
                         * (N_BUCKETS - max_exact)).astype(jnp.int32)
    bucket = jnp.where(dist < max_exact, dist, jnp.minimum(large, N_BUCKETS - 1))
    table = ((rel_bias[bucket] - rel_bias[N_BUCKETS - 1]) * LOG2E).astype(F32)
    i = jnp.arange(SUB)[:, None]
    j = jnp.arange(SUB)[None, :]

    def toeplitz(d):
        onehot = (d[:, :, None] == dist[None, None, :]).astype(F32)
        return jnp.einsum("ijd,dh->hij", onehot, table, precision=lax.Precision.HIGHEST)

    return toeplitz(jnp.clip(i - j, 0, 2 * SUB - 1)), toeplitz(SUB + i - j)


def _pack_bf16_pair(a, b):
    def rnd(x):
        bits = lax.bitcast_convert_type(x, jnp.uint32)
        return bits + jnp.uint32(0x7FFF) + ((bits >> 16) & jnp.uint32(1))
    return (rnd(a) >> 16) | (rnd(b) & jnp.uint32(0xFFFF0000))


def _unpack_bf16_pair(p):
    lo = lax.bitcast_convert_type(p << 16, F32)
    hi = lax.bitcast_convert_type(p & jnp.uint32(0xFFFF0000), F32)
    return lo.astype(BF16), hi.astype(BF16)


def _post_kernel(ya_ref, sga_ref, pc_ref, x_ref, woa_ref, wout_ref, g_ref, b_ref, wr_ref, wrl_ref, br_ref,
                 x1_ref, x1p_ref, ridx_ref, rgate_ref):
    y_attn = _dot(ya_ref[...], woa_ref[...])
    merged = sga_ref[...] * y_attn + pc_ref[...]
    mix = _dot(merged.astype(BF16), wout_ref[...])
    x1 = _layer_norm(DEEPNORM_ALPHA * x_ref[...] + mix, g_ref[...], b_ref[...])
    x1_ref[...] = x1
    half = D_MODEL // 2
    x1p_ref[:, 0, :] = _pack_bf16_pair(x1[:, :half], x1[:, half:])

    x1_hi = x1.astype(BF16)
    x1_lo = (x1 - x1_hi.astype(F32)).astype(BF16)
    logits = (_dot(x1_hi, wr_ref[...]) + (_dot(x1_hi, wrl_ref[...]) + _dot(x1_lo, wr_ref[...]))) + br_ref[...]
    lane = lax.broadcasted_iota(jnp.int32, logits.shape, 1).astype(F32)
    cur = logits
    vals, idxs = [], []
    for _ in range(TOP_K_EXPERTS):
        m = jnp.max(cur, axis=1, keepdims=True)
        ix = jnp.min(jnp.where(cur == m, lane, float(LANES)), axis=1, keepdims=True)
        vals.append(m)
        idxs.append(ix)
        cur = jnp.where(lane == ix, -jnp.inf, cur)
    exps = [jnp.exp(v - vals[0]) for v in vals]
    denom = exps[0]
    for e in exps[1:]:
        denom = denom + e
    ridx = jnp.zeros_like(logits)
    rgate = jnp.zeros_like(logits)
    for j in range(TOP_K_EXPERTS):
        ridx = jnp.where(lane == float(j), idxs[j], ridx)
        rgate = jnp.where(lane == float(j), exps[j] / denom, rgate)
    ridx_ref[...] = ridx.T[:ridx_ref.shape[0], :].astype(jnp.int32)
    rgate_ref[...] = rgate


def _post(ya, sg, pc, x, woa, wout, g, b, wr, wrl, br):
    S = x.shape[0]
    full = lambda a: pl.BlockSpec(a.shape, lambda i: (0, 0))
    row = lambda n: pl.BlockSpec((TM, n), lambda i: (i, 0))
    return pl.pallas_call(
        _post_kernel,
        grid=(S // TM,),
        in_specs=[row(ATTN_WIDTH), row(D_MODEL), row(D_MODEL), row(D_MODEL)] + [full(a) for a in (woa, wout, g, b, wr, wrl, br)],
        out_specs=[row(D_MODEL), pl.BlockSpec((TM, 1, D_MODEL // 2), lambda i: (i, 0, 0)),
                   pl.BlockSpec((ROUTE_ROWS, TM), lambda i: (0, i)), row(LANES)],
        out_shape=[jax.ShapeDtypeStruct((S, D_MODEL), F32), jax.ShapeDtypeStruct((S, 1, D_MODEL // 2), jnp.uint32),
                   jax.ShapeDtypeStruct((ROUTE_ROWS, S), jnp.int32), jax.ShapeDtypeStruct((S, LANES), F32)],
        compiler_params=_params(("parallel",)),
        name="post",
    )(ya, sg, pc, x, woa, wout, g, b, wr, wrl, br)


def _expert_kernel(te_ref, nv_ref, nused_ref, tok_ref, tokn_ref, dst_ref, x_hbm, wgu_ref, bgu_ref, wd_ref, bd_ref,
                   ys_hbm, xs_buf, y_buf, wgu_sc, wd_sc, gsem, ssem):
    t = pl.program_id(0)
    nused = nused_ref[0]
    cur = t % 2
    nxt = 1 - cur

    def start_gather(idx_ref, slot):
        def body(r, carry):
            pltpu.make_async_copy(x_hbm.at[idx_ref[0, 0, r]], xs_buf.at[slot, pl.ds(r, 1)], gsem.at[slot]).start()
            return carry
        lax.fori_loop(0, TME, body, 0, unroll=DMA_UNROLL)

    def wait_gather(slot):
        pltpu.make_async_copy(xs_buf.at[slot], xs_buf.at[slot], gsem.at[slot]).wait()

    def wait_scatter(slot, n):
        pltpu.make_async_copy(ys_hbm.at[pl.ds(0, n)], ys_hbm.at[pl.ds(0, n)], ssem.at[slot]).wait()

    @pl.when(t == 0)
    def _():
        start_gather(tok_ref, 0)

    @pl.when(t + 1 < nused)
    def _():
        start_gather(tokn_ref, nxt)

    e = te_ref[t]
    e_prev = te_ref[jnp.maximum(t - 1, 0)]

    @pl.when(jnp.logical_or(t == 0, e != e_prev))
    def _():
        wgu_sc[...] = wgu_ref[0].astype(BF16)
        wd_sc[...] = wd_ref[0].astype(BF16)

    @pl.when(t < nused)
    def _():
        wait_gather(cur)

        @pl.when(t >= 2)
        def _():
            wait_scatter(cur, nv_ref[jnp.maximum(t - 2, 0)])

        half = D_MODEL // 2
        lo, hi = _unpack_bf16_pair(xs_buf[cur])
        gu = _dot(lo, wgu_sc[0:half, :]) + _dot(hi, wgu_sc[half:, :]) + bgu_ref[0]
        g = jnp.minimum(gu[:, :D_EXPERT], SWIGLU_LIMIT)
        u = jnp.clip(gu[:, D_EXPERT:], -SWIGLU_LIMIT, SWIGLU_LIMIT)
        act = (u + 1.0) * (g * _sigmoid(SWIGLU_ALPHA * g))
        y_buf[cur] = _dot(act.astype(BF16), wd_sc[...]) + bd_ref[0]

        def scatter(r, carry):
            pltpu.make_async_copy(y_buf.at[cur, pl.ds(r, 1)], ys_hbm.at[dst_ref[0, 0, r]], ssem.at[cur]).start()
            return carry

        def scatter_group(c, carry):
            for j in range(DMA_UNROLL):
                scatter(c * DMA_UNROLL + j, carry)
            return carry
        n_groups = nv_ref[t] // DMA_UNROLL
        lax.fori_loop(0, n_groups, scatter_group, 0)
        lax.fori_loop(n_groups * DMA_UNROLL, nv_ref[t], scatter, 0)

        @pl.when(t == nused - 1)
        def _():
            @pl.when(t >= 1)
            def _():
                wait_scatter(nxt, nv_ref[jnp.maximum(t - 1, 0)])
            wait_scatter(cur, nv_ref[t])


def _experts(tile_e, tile_nv, nused, row_token, row_dst, x1p, wgu, bgu, wd, bd, n_tiles, n_dst):
    idx = lambda a: a.reshape(n_tiles, 1, TME)
    smem_tile = lambda f: pl.BlockSpec((1, 1, TME), f, memory_space=pltpu.SMEM)
    grid_spec = pltpu.PrefetchScalarGridSpec(
        num_scalar_prefetch=3,
        grid=(n_tiles,),
        in_specs=[smem_tile(lambda t, te, nv, nu: (t, 0, 0)),
                  smem_tile(lambda t, te, nv, nu: (jnp.minimum(t + 1, n_tiles - 1), 0, 0)),
                  smem_tile(lambda t, te, nv, nu: (t, 0, 0)),
                  pl.BlockSpec(memory_space=pl.ANY),
                  pl.BlockSpec((1, D_MODEL, 2 * D_EXPERT), lambda t, te, nv, nu: (te[t], 0, 0)),
                  pl.BlockSpec((1, 1, 2 * D_EXPERT), lambda t, te, nv, nu: (te[t], 0, 0)),
                  pl.BlockSpec((1, D_EXPERT, D_MODEL), lambda t, te, nv, nu: (te[t], 0, 0)),
                  pl.BlockSpec((1, 1, D_MODEL), lambda t, te, nv, nu: (te[t], 0, 0))],
        out_specs=pl.BlockSpec(memory_space=pl.ANY),
        scratch_shapes=[pltpu.VMEM((2, TME, D_MODEL // 2), jnp.uint32), pltpu.VMEM((2, TME, D_MODEL), F32),
                        pltpu.VMEM((D_MODEL, 2 * D_EXPERT), BF16), pltpu.VMEM((D_EXPERT, D_MODEL), BF16),
                        pltpu.SemaphoreType.DMA((2,)), pltpu.SemaphoreType.DMA((2,))],
    )
    return pl.pallas_call(
        _expert_kernel,
        grid_spec=grid_spec,
        out_shape=jax.ShapeDtypeStruct((n_dst, 1, D_MODEL), F32),
        compiler_params=_params(("arbitrary",)),
        name="moe_experts",
    )(tile_e, tile_nv, nused, idx(row_token), idx(row_token), idx(row_dst), x1p, wgu, bgu, wd, bd)


def _routing_tables(ridx, S):
    n_flat = S * TOP_K_EXPERTS
    n_tiles = n_flat // TME + N_EXPERTS
    flat_e = ridx.reshape(n_flat)
    assert N_EXPERTS * n_flat < 2 ** 31
    keys = jnp.sort(flat_e * n_flat + jnp.arange(n_flat, dtype=jnp.int32))
    order = keys % n_flat
    grp_bound = jnp.searchsorted(keys, jnp.arange(N_EXPERTS + 1, dtype=jnp.int32) * n_flat).astype(jnp.int32)
    grp_start = grp_bound[:-1]
    counts = grp_bound[1:] - grp_start
    padded = ((counts + TME - 1) // TME) * TME
    pad_end = jnp.cumsum(padded)
    pad_start = pad_end - padded
    nused = (pad_end[-1] // TME).astype(jnp.int32).reshape(1)
    tile_row0 = jnp.arange(n_tiles, dtype=jnp.int32) * TME
    tile_e = jnp.minimum(jnp.sum(tile_row0[:, None] >= pad_end[None, :], axis=1), N_EXPERTS - 1).astype(jnp.int32)
    tile_rank0 = tile_row0 - pad_start[tile_e]
    tile_nv = jnp.where(tile_row0 < pad_end[-1], jnp.clip(counts[tile_e] - tile_rank0, 0, TME), 0).astype(jnp.int32)
    within = jnp.arange(TME, dtype=jnp.int32)[None, :]
    valid = within < tile_nv[:, None]
    src_flat = lax.optimization_barrier(order[jnp.clip((grp_start[tile_e] + tile_rank0)[:, None] + within, 0, n_flat - 1)])
    row_token = jnp.where(valid, src_flat % S, 0).astype(jnp.int32)
    row_dst = jnp.where(valid, src_flat, 0).astype(jnp.int32)
    return tile_e, tile_nv, nused, row_token, row_dst, n_tiles


def _final_kernel(x1_ref, y0_ref, y1_ref, y2_ref, y3_ref, rg_ref, p_ref, wpg_ref, wpp_ref, g_ref, b_ref, o_ref):
    h = DEEPNORM_ALPHA * x1_ref[...]
    rg = rg_ref[...]
    for j, y_ref in enumerate((y0_ref, y1_ref, y2_ref, y3_ref)):
        h = h + rg[:, j:j + 1] * y_ref[:, 0, :]
    ple = _sigmoid(_dot(h.astype(BF16), wpg_ref[...])) * _dot(p_ref[...].astype(BF16), wpp_ref[...])
    o_ref[...] = _layer_norm(h + ple, g_ref[...], b_ref[...])


def _final(x1, ys, rgate, p, wpg, wpp, g, b):
    S = x1.shape[0]
    full = lambda a: pl.BlockSpec(a.shape, lambda i: (0, 0))
    row = lambda n: pl.BlockSpec((TM, n), lambda i: (i, 0))
    return pl.pallas_call(
        _final_kernel,
        grid=(S // TM,),
        in_specs=[row(D_MODEL)]
        + [pl.BlockSpec((TM, 1, D_MODEL), lambda i, j=j: (j * (S // TM) + i, 0, 0)) for j in range(TOP_K_EXPERTS)]
        + [row(LANES), row(PLE_DIM), full(wpg), full(wpp), full(g), full(b)],
        out_specs=row(D_MODEL),
        out_shape=jax.ShapeDtypeStruct((S, D_MODEL), F32),
        compiler_params=_params(("parallel",)),
        name="final",
    )(x1, ys, ys, ys, ys, rgate, p, wpg, wpp, g, b)


def _layer(x, p, w_in, b_in, w_o_attn, w_dw, b_dw, conv_ln_g, conv_ln_b, w_o_conv, w_out, ln1_g, ln1_b,
           w_router, b_router, w_gate_up, b_gate_up, w_down, b_down, w_ple_gate, w_ple_proj, ln2_g, ln2_b,
           rel_bias):
    S = x.shape[0]
    assert S % TM == 0 and S % TA == 0 and S % TKI == 0 and (S * TOP_K_EXPERTS) % TME == 0
    top_k = min(TOPK_MAX, S // 4)
    row2 = lambda a: a.reshape(1, -1).astype(F32)

    o_q, o_qi, o_ki, o_cv, o_g = 0, 3 * ATTN_WIDTH, 3 * ATTN_WIDTH + IDX_HEADS * IDX_DIM, \
        3 * ATTN_WIDTH + IDX_HEADS * IDX_DIM + IDX_DIM + IDX_HEADS, \
        3 * ATTN_WIDTH + IDX_HEADS * IDX_DIM + IDX_DIM + IDX_HEADS + 2 * CONV_CH
    kw_pad = LANES - (IDX_DIM + IDX_HEADS)
    wkw = jnp.pad(w_in[:, o_ki:o_cv], ((0, 0), (0, kw_pad)))
    bkw = jnp.pad(b_in[o_ki:o_cv], (0, kw_pad))
    q, k, v, qi, kw, u, sg = _proj(
        x, w_in[:, o_q:o_qi].astype(BF16), row2(b_in[o_q:o_qi]),
        w_in[:, o_qi:o_ki].astype(BF16), row2(b_in[o_qi:o_ki]),
        wkw.astype(BF16), row2(bkw),
        w_in[:, o_cv:o_g].astype(BF16), row2(b_in[o_cv:o_g]),
        w_in[:, o_g:].astype(BF16), row2(b_in[o_g:]))

    part_conv = _conv(u, sg, w_dw, row2(b_dw), row2(conv_ln_g), row2(conv_ln_b), w_o_conv.astype(BF16))

    ki = kw[:, :IDX_DIM].astype(BF16)
    mask4 = _index_mask(qi, kw[:, IDX_DIM:IDX_DIM + IDX_HEADS].T, jnp.concatenate([ki, ki], axis=1), top_k)
    bias_d, bias_e = _relative_bias_blocks(rel_bias)
    y_attn = _attention(q, k, v, mask4, bias_d, bias_e)

    wr = jnp.pad(w_router, ((0, 0), (0, LANES - N_EXPERTS)))
    wr_hi = wr.astype(BF16)
    wr_lo = (wr - wr_hi.astype(F32)).astype(BF16)
    br = jnp.pad(b_router, (0, LANES - N_EXPERTS), constant_values=-jnp.inf)
    x1, x1p, ridx, rgate = _post(y_attn, sg, part_conv, x, w_o_attn.astype(BF16), w_out.astype(BF16),
                                 row2(ln1_g), row2(ln1_b), wr_hi, wr_lo, row2(br))

    tile_e, tile_nv, nused, row_token, row_dst, n_tiles = _routing_tables(ridx[:TOP_K_EXPERTS], S)
    ys = _experts(tile_e, tile_nv, nused, row_token, row_dst, x1p, w_gate_up, b_gate_up.reshape(N_EXPERTS, 1, -1),
                  w_down, b_down.reshape(N_EXPERTS, 1, -1), n_tiles, S * TOP_K_EXPERTS)
    return _final(x1, ys, rgate, p, w_ple_gate.astype(BF16), w_ple_proj.astype(BF16), row2(ln2_g), row2(ln2_b))


def kernel(x, p, w_in, b_in, w_o_attn, w_dw, b_dw, conv_ln_g, conv_ln_b, w_o_conv, w_out, ln1_g, ln1_b, w_router, b_router, w_gate_up, b_gate_up, w_down, b_down, w_ple_gate, w_ple_proj, ln2_g, ln2_b, rel_bias):
    assert x.shape[0] == 1 and p.shape[0] == DEPTH
    out = _layer(x[0], p[0, 0], w_in[0], b_in[0], w_o_attn[0], w_dw[0], b_dw[0], conv_ln_g[0], conv_ln_b[0],
                 w_o_conv[0], w_out[0], ln1_g[0], ln1_b[0], w_router[0], b_router[0], w_gate_up[0], b_gate_up[0],
                 w_down[0], b_down[0], w_ple_gate[0], w_ple_proj[0], ln2_g[0], ln2_b[0], rel_bias)
    return out[None]
```

```python
import functools
import math

import jax
import jax.numpy as jnp
import numpy as np
from jax import lax
from jax.experimental import pallas as pl
from jax.experimental.pallas import tpu as pltpu

F32 = jnp.float32
BF16 = jnp.bfloat16

D_MODEL = 1024
N_HEADS = 8
HEAD_DIM = 64
ATTN_WIDTH = N_HEADS * HEAD_DIM
ATTN_SCALE = HEAD_DIM ** -0.5
LOG2E = math.log2(math.e)
IDX_HEADS = 8
IDX_DIM = 64
IDX_SCALE = (IDX_HEADS ** -0.5) * (IDX_DIM ** -0.5)
TOPK_MAX = 256
CONV_CH = 512
CONV_WIDTH = 31
N_BUCKETS = 32
MAX_DISTANCE = 128
N_EXPERTS = 32
TOP_K_EXPERTS = 4
D_EXPERT = 1024
SWIGLU_LIMIT = 7.0
SWIGLU_ALPHA = 1.702
PLE_DIM = 256
LN_EPS = 1e-5
DEPTH = 1
DEEPNORM_ALPHA = (2 * DEPTH) ** 0.25

LANES = 128
NEG_BIG = -1e30
F32_TINY = float(np.finfo(np.float32).tiny)
VMEM_LIMIT = 56 * 1024 * 1024

TM = 512
TQI = 256
TKI = 512
WARM_PASSES = 8
EDGE_FIRST = 9
SEARCH_CAP = 128
TA = 512
SUB = 128
TME = 256
DMA_UNROLL = 8
ROUTE_ROWS = 8
CONV_HALO = 32


def _params(sem, vmem=VMEM_LIMIT):
    return pltpu.CompilerParams(dimension_semantics=sem, vmem_limit_bytes=vmem)


def _sigmoid(x):
    return 1.0 / (1.0 + jnp.exp(-x))


def _layer_norm(x, g, b):
    mu = jnp.mean(x, axis=-1, keepdims=True)
    xc = x - mu
    var = jnp.mean(xc * xc, axis=-1, keepdims=True)
    return xc * lax.rsqrt(var + LN_EPS) * g + b


def _dot(a, b):
    return jnp.dot(a, b, preferred_element_type=F32)


def _dot_nt(a, b):
    return lax.dot_general(a, b, (((1,), (1,)), ((), ())), preferred_element_type=F32)


def _proj_kernel(x_ref, wqkv_ref, bqkv_ref, wqi_ref, bqi_ref, wkw_ref, bkw_ref, wcv_ref, bcv_ref,
                 wg_ref, bg_ref, q_ref, k_ref, v_ref, qi_ref, kw_ref, u_ref, sg_ref):
    xb = x_ref[...].astype(BF16)
    qkv = _dot(xb, wqkv_ref[...]) + bqkv_ref[...]
    q_ref[...] = (qkv[:, :ATTN_WIDTH] * (ATTN_SCALE * LOG2E)).astype(BF16)
    k_ref[...] = qkv[:, ATTN_WIDTH:2 * ATTN_WIDTH].astype(BF16)
    v_ref[...] = qkv[:, 2 * ATTN_WIDTH:].astype(BF16)
    qi_ref[...] = (_dot(xb, wqi_ref[...]) + bqi_ref[...]).astype(BF16)
    kw = _dot(xb, wkw_ref[...]) + bkw_ref[...]
    lane = lax.broadcasted_iota(jnp.int32, kw.shape, 1)
    kw_ref[...] = jnp.where(lane >= IDX_DIM, kw * IDX_SCALE, kw)
    cv = _dot(xb, wcv_ref[...]) + bcv_ref[...]
    u_ref[...] = cv[:, :CONV_CH] * _sigmoid(cv[:, CONV_CH:])
    sg_ref[...] = _sigmoid(_dot(xb, wg_ref[...]) + bg_ref[...])


def _proj(x, wqkv, bqkv, wqi, bqi, wkw, bkw, wcv, bcv, wg, bg):
    S = x.shape[0]
    full = lambda a: pl.BlockSpec(a.shape, lambda i: (0, 0))
    row = lambda n: pl.BlockSpec((TM, n), lambda i: (i, 0))
    outs = [(ATTN_WIDTH, BF16)] * 3 + [(IDX_HEADS * IDX_DIM, BF16), (LANES, F32), (CONV_CH, F32),
                                        (2 * D_MODEL, F32)]
    return pl.pallas_call(
        _proj_kernel,
        grid=(S // TM,),
        in_specs=[row(D_MODEL)] + [full(a) for a in (wqkv, bqkv, wqi, bqi, wkw, bkw, wcv, bcv, wg, bg)],
        out_specs=[row(n) for n, _ in outs],
        out_shape=[jax.ShapeDtypeStruct((S, n), dt) for n, dt in outs],
        compiler_params=_params(("parallel",)),
        name="proj",
    )(x, wqkv, bqkv, wqi, bqi, wkw, bkw, wcv, bcv, wg, bg)


def _conv_kernel(u_ref, wdw_ref, bdw_ref, lng_ref, lnb_ref, wo_ref, sgc_ref, o_ref, buf_ref, sh_ref):
    @pl.when(pl.program_id(0) == 0)
    def _():
        buf_ref[0:CONV_HALO, :] = jnp.zeros((CONV_HALO, CONV_CH), F32)

    buf_ref[CONV_HALO:CONV_HALO + TM, :] = u_ref[...]
    base = CONV_HALO - (CONV_WIDTH - 1)
    acc = jnp.zeros((TM, CONV_CH), F32) + bdw_ref[...]
    sub = 8
    for r in range(sub):
        taps = [j for j in range(CONV_WIDTH) if (base + j) % sub == r]
        span = max(base + j - r for j in taps) + TM
        sh_ref[0:span, :] = buf_ref[r:r + span, :]
        for j in taps:
            off = base + j - r
            acc = acc + wdw_ref[j:j + 1, :] * sh_ref[off:off + TM, :]
    buf_ref[0:CONV_HALO, :] = buf_ref[TM:TM + CONV_HALO, :]
    y = _layer_norm(acc, lng_ref[...], lnb_ref[...])
    y = y * _sigmoid(y)
    o_ref[...] = sgc_ref[...] * _dot(y.astype(BF16), wo_ref[...])


def _conv(u, sg, wdw, bdw, lng, lnb, wo):
    S = u.shape[0]
    full = lambda a: pl.BlockSpec(a.shape, lambda i: (0, 0))
    return pl.pallas_call(
        _conv_kernel,
        grid=(S // TM,),
        in_specs=[pl.BlockSpec((TM, CONV_CH), lambda i: (i, 0)), full(wdw), full(bdw), full(lng), full(lnb),
                  full(wo), pl.BlockSpec((TM, D_MODEL), lambda i: (i, 1))],
        out_specs=pl.BlockSpec((TM, D_MODEL), lambda i: (i, 0)),
        out_shape=jax.ShapeDtypeStruct((S, D_MODEL), F32),
        scratch_shapes=[pltpu.VMEM((TM + CONV_HALO, CONV_CH), F32), pltpu.VMEM((TM + CONV_HALO, CONV_CH), F32)],
        compiler_params=_params(("arbitrary",)),
        name="conv",
    )(u, wdw, bdw, lng, lnb, wo, sg)


def _float_key(f):
    b = lax.bitcast_convert_type(f, jnp.int32)
    return b ^ ((b >> 31) & jnp.int32(0x7FFFFFFF))


def _key_float(k):
    b = k ^ ((k >> 31) & jnp.int32(0x7FFFFFFF))
    return lax.bitcast_convert_type(b, F32)


def _index_kernel(qi_ref, wt_ref, ki_ref, mask_ref, sc_ref, qh_ref, *, top_k):
    i = pl.program_id(0)
    nkt = (i * TQI + TQI + TKI - 1) // TKI
    ntile = sc_ref.shape[0]
    q_g = i * TQI + lax.broadcasted_iota(jnp.int32, (1, TQI), 1)
    lane = lax.broadcasted_iota(jnp.int32, (TQI, LANES), 1)

    for h in range(IDX_HEADS):
        qp = qi_ref[:, (h // 2) * LANES:(h // 2 + 1) * LANES]
        keep = (lane < IDX_DIM) if h % 2 == 0 else (lane >= IDX_DIM)
        qh_ref[h] = jnp.where(keep, qp, jnp.zeros_like(qp))

    def score_tile(kt, causal):
        kb = ki_ref[pl.ds(pl.multiple_of(kt * TKI, TKI), TKI), :]
        acc = jnp.zeros((TKI, TQI), F32)
        for h in range(IDX_HEADS):
            acc = acc + wt_ref[h:h + 1, :] * jnp.maximum(_dot_nt(kb, qh_ref[h]), 0.0)
        if causal:
            key_g = kt * TKI + lax.broadcasted_iota(jnp.int32, (TKI, TQI), 0)
            ok = key_g <= q_g
            lo_src = jnp.where(ok, acc, jnp.inf)
            acc = jnp.where(ok, acc, -jnp.inf)
        else:
            lo_src = acc
        sc_ref[kt] = acc
        return jnp.max(acc, axis=0, keepdims=True), jnp.min(lo_src, axis=0, keepdims=True)

    def score_body(kt, carry):
        mx, mn = carry
        tmx, tmn = score_tile(kt, False)
        return jnp.maximum(mx, tmx), jnp.minimum(mn, tmn)

    mx0 = jnp.full((1, TQI), -jnp.inf, F32)
    mn0 = jnp.full((1, TQI), jnp.inf, F32)
    mx, mn = lax.fori_loop(0, nkt - 1, score_body, (mx0, mn0))
    tmx, tmn = score_tile(nkt - 1, True)
    mx = jnp.maximum(mx, tmx)
    mn = jnp.minimum(mn, tmn)
    kf = float(top_k)
    sub = 8

    def count(pivot, strict):
        def tile(kt, cnt):
            for r in range(TKI // sub):
                blk = sc_ref[kt, r * sub:(r + 1) * sub, :]
                hit = (blk > pivot) if strict else (blk >= pivot)
                cnt = cnt + jnp.where(hit, 1.0, 0.0)
            return cnt

        group = 4

        def tiles(j, cnts):
            return tuple(tile(group * j + g, cnts[g]) for g in range(group))

        zero = jnp.zeros((sub, TQI), F32)
        cnts = lax.fori_loop(0, nkt // group, tiles, (zero,) * group)
        rest = lax.fori_loop(group * (nkt // group), nkt, tile, cnts[0])
        return jnp.sum(rest + sum(cnts[1:]), axis=0, keepdims=True)

    n_causal = (q_g + 1).astype(F32)
    all_sel = n_causal <= kf
    zeros = jnp.zeros((1, TQI), F32)
    state0 = dict(
        it=jnp.int32(0),
        lo=mn, hi=_key_float(_float_key(mx) + 1), clo=n_causal, chi=zeros,
        glo=jnp.log(jnp.maximum(n_causal, kf + 1.0) / kf), ghi=jnp.full((1, TQI), math.log(0.5 / kf), F32),
        thr=jnp.where(all_sel, NEG_BIG, 0.0).astype(F32),
        done=all_sel.astype(F32), tie=zeros, side=zeros, forced=zeros, use_forced=zeros,
    )

    def above(x):
        return jnp.where(jnp.abs(x) < F32_TINY, F32_TINY, _key_float(_float_key(x) + 1))

    def bracket_edges(lo, hi):
        def tile(kt, carry):
            a, b = carry
            for r in range(TKI // sub):
                blk = sc_ref[kt, r * sub:(r + 1) * sub, :]
                a = jnp.maximum(a, jnp.where(blk < hi, blk, -jnp.inf))
                b = jnp.minimum(b, jnp.where(blk >= lo, blk, jnp.inf))
            return a, b
        a, b = lax.fori_loop(0, nkt, tile, (jnp.full((sub, TQI), -jnp.inf, F32), jnp.full((sub, TQI), jnp.inf, F32)))
        return jnp.max(a, axis=0, keepdims=True), jnp.min(b, axis=0, keepdims=True)

    def cond(st):
        return jnp.logical_and(st["it"] < SEARCH_CAP, jnp.min(st["done"]) < 0.5)

    def count_step(st):
        it, lo, hi, glo, ghi = st["it"], st["lo"], st["hi"], st["glo"], st["ghi"]
        lo_k, hi_k = _float_key(lo), _float_key(hi)
        above_lo = above(lo)
        below_hi = jnp.where(hi == F32_TINY, 0.0, _key_float(hi_k - 1))
        probe = jnp.logical_and(st["done"] < 0.5, above_lo < hi)
        tie_now = jnp.logical_and(st["done"] < 0.5, above_lo >= hi)
        frac = jnp.where(it < 24, glo / (glo - ghi), 0.5)
        pf = lo + (hi - lo) * frac
        pf = jnp.where(it < 64, pf, _key_float((lo_k >> 1) + (hi_k >> 1) + (lo_k & hi_k & 1)))
        pf = jnp.where(it == 0, F32_TINY, jnp.where(jnp.logical_and(it == 1, hi == F32_TINY), 0.0, pf))
        pf = jnp.where(st["use_forced"] > 0.5, st["forced"], pf)
        pf = jnp.where(probe, jnp.minimum(jnp.maximum(pf, above_lo), below_hi), lo)
        c = count(pf, False)
        hit = jnp.logical_and(probe, c == kf)
        up = jnp.logical_and(probe, c > kf)
        dn = jnp.logical_and(probe, c < kf)
        g = jnp.log(jnp.maximum(c, 0.5) / kf)
        return dict(
            it=it + 1,
            lo=jnp.where(up, pf, lo), hi=jnp.where(dn, pf, hi),
            clo=jnp.where(up, c, st["clo"]), chi=jnp.where(dn, c, st["chi"]),
            glo=jnp.where(up, g, jnp.where(jnp.logical_and(dn, st["side"] < -0.5), 0.5 * glo, glo)),
            ghi=jnp.where(dn, g, jnp.where(jnp.logical_and(up, st["side"] > 0.5), 0.5 * ghi, ghi)),
            thr=jnp.where(hit, pf, jnp.where(tie_now, lo, st["thr"])),
            done=jnp.where(jnp.logical_or(hit, tie_now), 1.0, st["done"]),
            tie=jnp.where(tie_now, 1.0, st["tie"]),
            side=jnp.where(up, 1.0, jnp.where(dn, -1.0, st["side"])),
            forced=st["forced"], use_forced=zeros,
        )

    def edge_step(st):
        lo, hi = st["lo"], st["hi"]
        probe = jnp.logical_and(st["done"] < 0.5, above(lo) < hi)
        a, b = bracket_edges(lo, hi)
        one_above = st["chi"] == kf - 1.0
        one_below = st["clo"] == kf + 1.0
        new = dict(st)
        new.update(
            it=st["it"] + 1,
            lo=jnp.where(probe, b, lo), hi=jnp.where(probe, above(a), hi),
            forced=jnp.where(one_above, a, above(b)),
            use_forced=jnp.where(jnp.logical_and(probe, jnp.logical_or(one_above, one_below)), 1.0, 0.0),
        )
        return new

    def step(st):
        it = st["it"]
        is_edge = jnp.logical_and(it >= EDGE_FIRST, (it - EDGE_FIRST) % 3 == 0)
        return lax.cond(is_edge, edge_step, count_step, st)

    st = lax.fori_loop(0, WARM_PASSES, lambda _, s: count_step(s), state0)
    st = lax.while_loop(cond, step, st)
    thr = st["thr"]
    tie = st["tie"]
    any_tie = jnp.max(tie) > 0.5

    def emit(kt, sel):
        mask_ref[0, kt] = jnp.where(sel, 0.0, NEG_BIG).astype(mask_ref.dtype).T

    @pl.when(jnp.logical_not(any_tie))
    def _():
        def body(kt, carry):
            emit(kt, sc_ref[kt] >= thr)
            return carry
        lax.fori_loop(0, nkt, body, 0)

    @pl.when(any_tie)
    def _():
        need = jnp.where(tie > 0.5, kf - count(thr, True), float(2 * ntile * TKI))
        r = lax.broadcasted_iota(jnp.int32, (TKI, TKI), 0)
        c = lax.broadcasted_iota(jnp.int32, (TKI, TKI), 1)
        prefix = jnp.where(c <= r, 1.0, 0.0).astype(BF16)

        def body(kt, seen):
            s = sc_ref[kt]
            eq = s == thr
            rank = seen + _dot(prefix, jnp.where(eq, 1.0, 0.0).astype(BF16))
            emit(kt, jnp.logical_or(s > thr, jnp.logical_and(eq, rank <= need)))
            return seen + jnp.sum(jnp.where(eq, 1.0, 0.0), axis=0, keepdims=True)
        lax.fori_loop(0, nkt, body, jnp.zeros((1, TQI), F32))

    def fill(kt, carry):
        mask_ref[0, kt] = jnp.full((TQI, TKI), NEG_BIG, mask_ref.dtype)
        return carry
    lax.fori_loop(nkt, ntile, fill, 0)


def _index_mask(qi, wt, ki2, top_k):
    S = qi.shape[0]
    nq, nk = S // TQI, S // TKI
    return pl.pallas_call(
        functools.partial(_index_kernel, top_k=top_k),
        grid=(nq,),
        in_specs=[pl.BlockSpec((TQI, IDX_HEADS * IDX_DIM), lambda i: (i, 0)),
                  pl.BlockSpec((IDX_HEADS, TQI), lambda i: (0, i)),
                  pl.BlockSpec((S, LANES), lambda i: (0, 0))],
        out_specs=pl.BlockSpec((1, nk, TQI, TKI), lambda i: (i, 0, 0, 0)),
        out_shape=jax.ShapeDtypeStruct((nq, nk, TQI, TKI), BF16),
        scratch_shapes=[pltpu.VMEM((nk, TKI, TQI), F32),
                        pltpu.VMEM((IDX_HEADS, TQI, LANES), BF16)],
        compiler_params=_params(("parallel",)),
        name="index_mask",
    )(qi, wt, ki2)


def _attn_kernel(qt_ref, kt_ref, q_ref, k_ref, v_ref, m_ref, bd_ref, be_ref, o_ref, *state):
    m_sc, acc_sc = state[:N_HEADS], state[N_HEADS:]
    step = pl.program_id(0)
    qi = qt_ref[step]
    ki = kt_ref[step]
    nsub = TA // SUB

    @pl.when(ki == 0)
    def _():
        for h in range(N_HEADS):
            m_sc[h][...] = jnp.full((TA, LANES), NEG_BIG, F32)
            acc_sc[h][...] = jnp.zeros((TA, LANES), F32)

    lane = lax.broadcasted_iota(jnp.int32, (TA, LANES), 1)
    first = lane < HEAD_DIM

    def bias_tile(h, diagonal):
        zero = jnp.zeros((SUB, SUB), F32)
        rows = []
        for a in range(nsub):
            if diagonal:
                blks = [bd_ref[h] if b == a else (be_ref[h] if b == a - 1 else zero) for b in range(nsub)]
            else:
                blks = [be_ref[h] if (a == 0 and b == nsub - 1) else zero for b in range(nsub)]
            rows.append(jnp.concatenate(blks, axis=1))
        return jnp.concatenate(rows, axis=0)

    def update(near, diagonal):
        maskf = jnp.concatenate([m_ref[a, 0] for a in range(TA // TQI)], axis=0).astype(F32)

        def logits(h):
            cols = slice((h // 2) * LANES, (h // 2 + 1) * LANES)
            qp = q_ref[:, cols]
            mine = first if h % 2 == 0 else jnp.logical_not(first)
            s = _dot_nt(jnp.where(mine, qp, jnp.zeros_like(qp)), k_ref[:, cols]) + maskf
            return s + bias_tile(h, diagonal) if near else s

        s = logits(0)
        for h in range(N_HEADS):
            s_next = logits(h + 1) if h + 1 < N_HEADS else None
            vp = v_ref[:, (h // 2) * LANES:(h // 2 + 1) * LANES]
            mine = first if h % 2 == 0 else jnp.logical_not(first)
            vh = jnp.where(mine, vp, jnp.ones_like(vp))
            m_prev = m_sc[h][...]
            m_next = jnp.maximum(m_prev, jnp.max(s, axis=1, keepdims=True))
            pexp = jnp.exp2(s - jnp.concatenate([m_next] * (TA // LANES), axis=1)).astype(BF16)
            acc_sc[h][...] = jnp.exp2(m_prev - m_next) * acc_sc[h][...] + _dot(pexp, vh)
            m_sc[h][...] = m_next
            s = s_next

    @pl.when(ki < qi - 1)
    def _():
        update(False, False)

    @pl.when(ki == qi - 1)
    def _():
        update(True, False)

    @pl.when(ki == qi)
    def _():
        update(True, True)
        for p in range(N_HEADS // 2):
            a0, a1 = acc_sc[2 * p][...], acc_sc[2 * p + 1][...]
            d0 = pltpu.roll(a0, HEAD_DIM, axis=1)
            d1 = pltpu.roll(a1, HEAD_DIM, axis=1)
            o_ref[:, p * LANES:(p + 1) * LANES] = jnp.where(first, a0 / d0, a1 / d1).astype(o_ref.dtype)


def _attention(q, k, v, mask4, bias_d, bias_e):
    S = q.shape[0]
    nb = S // TA
    pairs = [(a, b) for a in range(nb) for b in range(a + 1)]
    qtab = jnp.asarray(np.array([a for a, _ in pairs], np.int32))
    ktab = jnp.asarray(np.array([b for _, b in pairs], np.int32))
    grid_spec = pltpu.PrefetchScalarGridSpec(
        num_scalar_prefetch=2,
        grid=(len(pairs),),
        in_specs=[pl.BlockSpec((TA, ATTN_WIDTH), lambda s, qt, kt: (qt[s], 0)),
                  pl.BlockSpec((TA, ATTN_WIDTH), lambda s, qt, kt: (kt[s], 0)),
                  pl.BlockSpec((TA, ATTN_WIDTH), lambda s, qt, kt: (kt[s], 0)),
                  pl.BlockSpec((TA // TQI, TA // TKI, TQI, TKI), lambda s, qt, kt: (qt[s], kt[s], 0, 0)),
                  pl.BlockSpec(bias_d.shape, lambda s, qt, kt: (0, 0, 0)),
                  pl.BlockSpec(bias_e.shape, lambda s, qt, kt: (0, 0, 0))],
        out_specs=pl.BlockSpec((TA, ATTN_WIDTH), lambda s, qt, kt: (qt[s], 0)),
        scratch_shapes=[pltpu.VMEM((TA, LANES), F32)] * (2 * N_HEADS),
    )
    return pl.pallas_call(
        _attn_kernel,
        grid_spec=grid_spec,
        out_shape=jax.ShapeDtypeStruct((S, ATTN_WIDTH), BF16),
        compiler_params=_params(("arbitrary",)),
        name="attn",
    )(qtab, ktab, q, k, v, mask4, bias_d, bias_e)


def _relative_bias_blocks(rel_bias):
    dist = jnp.arange(2 * SUB, dtype=jnp.int32)
    max_exact = N_BUCKETS // 2
    dist_f = jnp.maximum(dist, 1).astype(F32)
    large = max_exact + (jnp.log(dist_f / max_exact) / math.log(MAX_DISTANCE / max_exact)
                         * (N_BUCKETS - max_exact)).astype(jnp.int32)
    bucket = jnp.where(dist < max_exact, dist, jnp.minimum(large, N_BUCKETS - 1))
    table = ((rel_bias[bucket] - rel_bias[N_BUCKETS - 1]) * LOG2E).astype(F32)
    period = 2 * SUB

    def toeplitz(first_row_index):
        z = table[first_row_index].T
        cut = jnp.tile(z, (1, SUB))[:, :SUB * (period - 1)].reshape(-1, SUB, period - 1)
        return cut[:, :, :SUB]

    m = np.arange(period)
    return toeplitz((period - m) % period), toeplitz((SUB - m) % period)


def _pack_bf16_pair(a, b):
    def rnd(x):
        bits = lax.bitcast_convert_type(x, jnp.uint32)
        return bits + jnp.uint32(0x7FFF) + ((bits >> 16) & jnp.uint32(1))
    return (rnd(a) >> 16) | (rnd(b) & jnp.uint32(0xFFFF0000))


def _unpack_bf16_pair(p):
    lo = lax.bitcast_convert_type(p << 16, F32)
    hi = lax.bitcast_convert_type(p & jnp.uint32(0xFFFF0000), F32)
    return lo.astype(BF16), hi.astype(BF16)


def _post_kernel(ya_ref, sga_ref, pc_ref, x_ref, woa_ref, wout_ref, g_ref, b_ref, wr_ref, wrl_ref, br_ref,
                 x1_ref, x1p_ref, ridx_ref, rgate_ref):
    y_attn = _dot(ya_ref[...], woa_ref[...])
    merged = sga_ref[...] * y_attn + pc_ref[...]
    mix = _dot(merged.astype(BF16), wout_ref[...])
    x1 = _layer_norm(DEEPNORM_ALPHA * x_ref[...] + mix, g_ref[...], b_ref[...])
    x1_ref[...] = x1
    half = D_MODEL // 2
    x1p_ref[:, 0, :] = _pack_bf16_pair(x1[:, :half], x1[:, half:])

    x1_hi = x1.astype(BF16)
    x1_lo = (x1 - x1_hi.astype(F32)).astype(BF16)
    logits = (_dot(x1_hi, wr_ref[...]) + (_dot(x1_hi, wrl_ref[...]) + _dot(x1_lo, wr_ref[...]))) + br_ref[...]
    lane = lax.broadcasted_iota(jnp.int32, logits.shape, 1).astype(F32)
    cur = logits
    vals, idxs = [], []
    for _ in range(TOP_K_EXPERTS):
        m = jnp.max(cur, axis=1, keepdims=True)
        ix = jnp.min(jnp.where(cur == m, lane, float(LANES)), axis=1, keepdims=True)
        vals.append(m)
        idxs.append(ix)
        cur = jnp.where(lane == ix, -jnp.inf, cur)
    exps = [jnp.exp(v - vals[0]) for v in vals]
    denom = exps[0]
    for e in exps[1:]:
        denom = denom + e
    ridx = jnp.zeros_like(logits)
    rgate = jnp.zeros_like(logits)
    for j in range(TOP_K_EXPERTS):
        ridx = jnp.where(lane == float(j), idxs[j], ridx)
        rgate = jnp.where(lane == float(j), exps[j] / denom, rgate)
    ridx_ref[...] = ridx.T[:ridx_ref.shape[0], :].astype(jnp.int32)
    rgate_ref[...] = rgate


def _post(ya, sg, pc, x, woa, wout, g, b, wr, wrl, br):
    S = x.shape[0]
    full = lambda a: pl.BlockSpec(a.shape, lambda i: (0, 0))
    row = lambda n: pl.BlockSpec((TM, n), lambda i: (i, 0))
    return pl.pallas_call(
        _post_kernel,
        grid=(S // TM,),
        in_specs=[row(ATTN_WIDTH), row(D_MODEL), row(D_MODEL), row(D_MODEL)] + [full(a) for a in (woa, wout, g, b, wr, wrl, br)],
        out_specs=[row(D_MODEL), pl.BlockSpec((TM, 1, D_MODEL // 2), lambda i: (i, 0, 0)),
                   pl.BlockSpec((ROUTE_ROWS, TM), lambda i: (0, i)), row(LANES)],
        out_shape=[jax.ShapeDtypeStruct((S, D_MODEL), F32), jax.ShapeDtypeStruct((S, 1, D_MODEL // 2), jnp.uint32),
                   jax.ShapeDtypeStruct((ROUTE_ROWS, S), jnp.int32), jax.ShapeDtypeStruct((S, LANES), F32)],
        compiler_params=_params(("parallel",)),
        name="post",
    )(ya, sg, pc, x, woa, wout, g, b, wr, wrl, br)


def _expert_kernel(te_ref, nv_ref, nused_ref, tok_ref, tokn_ref, dst_ref, x_hbm, wgu_ref, bgu_ref, wd_ref, bd_ref,
                   ys_hbm, xs_buf, y_buf, wgu_sc, wd_sc, gsem, ssem):
    t = pl.program_id(0)
    nused = nused_ref[0]
    cur = t % 2
    nxt = 1 - cur

    def start_gather(idx_ref, slot):
        def body(r, carry):
            pltpu.make_async_copy(x_hbm.at[idx_ref[0, 0, r]], xs_buf.at[slot, pl.ds(r, 1)], gsem.at[slot]).start()
            return carry
        lax.fori_loop(0, TME, body, 0, unroll=DMA_UNROLL)

    def wait_gather(slot):
        pltpu.make_async_copy(xs_buf.at[slot], xs_buf.at[slot], gsem.at[slot]).wait()

    def wait_scatter(slot, n):
        pltpu.make_async_copy(ys_hbm.at[pl.ds(0, n)], ys_hbm.at[pl.ds(0, n)], ssem.at[slot]).wait()

    @pl.when(t == 0)
    def _():
        start_gather(tok_ref, 0)

    @pl.when(t + 1 < nused)
    def _():
        start_gather(tokn_ref, nxt)

    e = te_ref[t]
    e_prev = te_ref[jnp.maximum(t - 1, 0)]

    @pl.when(jnp.logical_or(t == 0, e != e_prev))
    def _():
        wgu_sc[...] = wgu_ref[0].astype(BF16)
        wd_sc[...] = wd_ref[0].astype(BF16)

    @pl.when(t < nused)
    def _():
        wait_gather(cur)

        @pl.when(t >= 2)
        def _():
            wait_scatter(cur, nv_ref[jnp.maximum(t - 2, 0)])

        half = D_MODEL // 2
        lo, hi = _unpack_bf16_pair(xs_buf[cur])
        gu = _dot(lo, wgu_sc[0:half, :]) + _dot(hi, wgu_sc[half:, :]) + bgu_ref[0]
        g = jnp.minimum(gu[:, :D_EXPERT], SWIGLU_LIMIT)
        u = jnp.clip(gu[:, D_EXPERT:], -SWIGLU_LIMIT, SWIGLU_LIMIT)
        act = (u + 1.0) * (g * _sigmoid(SWIGLU_ALPHA * g))
        y_buf[cur] = _dot(act.astype(BF16), wd_sc[...]) + bd_ref[0]

        def scatter(r, carry):
            pltpu.make_async_copy(y_buf.at[cur, pl.ds(r, 1)], ys_hbm.at[dst_ref[0, 0, r]], ssem.at[cur]).start()
            return carry

        def scatter_group(c, carry):
            for j in range(DMA_UNROLL):
                scatter(c * DMA_UNROLL + j, carry)
            return carry
        n_groups = nv_ref[t] // DMA_UNROLL
        lax.fori_loop(0, n_groups, scatter_group, 0)
        lax.fori_loop(n_groups * DMA_UNROLL, nv_ref[t], scatter, 0)

        @pl.when(t == nused - 1)
        def _():
            @pl.when(t >= 1)
            def _():
                wait_scatter(nxt, nv_ref[jnp.maximum(t - 1, 0)])
            wait_scatter(cur, nv_ref[t])


def _experts(tile_e, tile_nv, nused, row_token, row_dst, x1p, wgu, bgu, wd, bd, n_tiles, n_dst):
    idx = lambda a: a.reshape(n_tiles, 1, TME)
    smem_tile = lambda f: pl.BlockSpec((1, 1, TME), f, memory_space=pltpu.SMEM)
    grid_spec = pltpu.PrefetchScalarGridSpec(
        num_scalar_prefetch=3,
        grid=(n_tiles,),
        in_specs=[smem_tile(lambda t, te, nv, nu: (t, 0, 0)),
                  smem_tile(lambda t, te, nv, nu: (jnp.minimum(t + 1, n_tiles - 1), 0, 0)),
                  smem_tile(lambda t, te, nv, nu: (t, 0, 0)),
                  pl.BlockSpec(memory_space=pl.ANY),
                  pl.BlockSpec((1, D_MODEL, 2 * D_EXPERT), lambda t, te, nv, nu: (te[t], 0, 0)),
                  pl.BlockSpec((1, 1, 2 * D_EXPERT), lambda t, te, nv, nu: (te[t], 0, 0)),
                  pl.BlockSpec((1, D_EXPERT, D_MODEL), lambda t, te, nv, nu: (te[t], 0, 0)),
                  pl.BlockSpec((1, 1, D_MODEL), lambda t, te, nv, nu: (te[t], 0, 0))],
        out_specs=pl.BlockSpec(memory_space=pl.ANY),
        scratch_shapes=[pltpu.VMEM((2, TME, D_MODEL // 2), jnp.uint32), pltpu.VMEM((2, TME, D_MODEL), F32),
                        pltpu.VMEM((D_MODEL, 2 * D_EXPERT), BF16), pltpu.VMEM((D_EXPERT, D_MODEL), BF16),
                        pltpu.SemaphoreType.DMA((2,)), pltpu.SemaphoreType.DMA((2,))],
    )
    return pl.pallas_call(
        _expert_kernel,
        grid_spec=grid_spec,
        out_shape=jax.ShapeDtypeStruct((n_dst, 1, D_MODEL), F32),
        compiler_params=_params(("arbitrary",)),
        name="moe_experts",
    )(tile_e, tile_nv, nused, idx(row_token), idx(row_token), idx(row_dst), x1p, wgu, bgu, wd, bd)


def _routing_tables(ridx, S):
    n_flat = S * TOP_K_EXPERTS
    n_tiles = n_flat // TME + N_EXPERTS
    flat_e = ridx.reshape(n_flat)
    assert N_EXPERTS * n_flat < 2 ** 31
    keys = jnp.sort(flat_e * n_flat + jnp.arange(n_flat, dtype=jnp.int32))
    order = keys % n_flat
    grp_bound = jnp.searchsorted(keys, jnp.arange(N_EXPERTS + 1, dtype=jnp.int32) * n_flat).astype(jnp.int32)
    grp_start = grp_bound[:-1]
    counts = grp_bound[1:] - grp_start
    padded = ((counts + TME - 1) // TME) * TME
    pad_end = jnp.cumsum(padded)
    pad_start = pad_end - padded
    nused = (pad_end[-1] // TME).astype(jnp.int32).reshape(1)
    tile_row0 = jnp.arange(n_tiles, dtype=jnp.int32) * TME
    tile_e = jnp.minimum(jnp.sum(tile_row0[:, None] >= pad_end[None, :], axis=1), N_EXPERTS - 1).astype(jnp.int32)
    tile_rank0 = tile_row0 - pad_start[tile_e]
    tile_nv = jnp.where(tile_row0 < pad_end[-1], jnp.clip(counts[tile_e] - tile_rank0, 0, TME), 0).astype(jnp.int32)
    within = jnp.arange(TME, dtype=jnp.int32)[None, :]
    valid = within < tile_nv[:, None]
    src_flat = lax.optimization_barrier(order[jnp.clip((grp_start[tile_e] + tile_rank0)[:, None] + within, 0, n_flat - 1)])
    row_token = jnp.where(valid, src_flat % S, 0).astype(jnp.int32)
    row_dst = jnp.where(valid, src_flat, 0).astype(jnp.int32)
    return tile_e, tile_nv, nused, row_token, row_dst, n_tiles


def _final_kernel(x1_ref, y0_ref, y1_ref, y2_ref, y3_ref, rg_ref, p_ref, wpg_ref, wpp_ref, g_ref, b_ref, o_ref):
    h = DEEPNORM_ALPHA * x1_ref[...]
    rg = rg_ref[...]
    for j, y_ref in enumerate((y0_ref, y1_ref, y2_ref, y3_ref)):
        h = h + rg[:, j:j + 1] * y_ref[:, 0, :]
    ple = _sigmoid(_dot(h.astype(BF16), wpg_ref[...])) * _dot(p_ref[...].astype(BF16), wpp_ref[...])
    o_ref[...] = _layer_norm(h + ple, g_ref[...], b_ref[...])


def _final(x1, ys, rgate, p, wpg, wpp, g, b):
    S = x1.shape[0]
    full = lambda a: pl.BlockSpec(a.shape, lambda i: (0, 0))
    row = lambda n: pl.BlockSpec((TM, n), lambda i: (i, 0))
    return pl.pallas_call(
        _final_kernel,
        grid=(S // TM,),
        in_specs=[row(D_MODEL)]
        + [pl.BlockSpec((TM, 1, D_MODEL), lambda i, j=j: (j * (S // TM) + i, 0, 0)) for j in range(TOP_K_EXPERTS)]
        + [row(LANES), row(PLE_DIM), full(wpg), full(wpp), full(g), full(b)],
        out_specs=row(D_MODEL),
        out_shape=jax.ShapeDtypeStruct((S, D_MODEL), F32),
        compiler_params=_params(("parallel",)),
        name="final",
    )(x1, ys, ys, ys, ys, rgate, p, wpg, wpp, g, b)


def _layer(x, p, w_in, b_in, w_o_attn, w_dw, b_dw, conv_ln_g, conv_ln_b, w_o_conv, w_out, ln1_g, ln1_b,
           w_router, b_router, w_gate_up, b_gate_up, w_down, b_down, w_ple_gate, w_ple_proj, ln2_g, ln2_b,
           rel_bias):
    S = x.shape[0]
    assert S % TM == 0 and S % TA == 0 and S % TKI == 0 and (S * TOP_K_EXPERTS) % TME == 0
    top_k = min(TOPK_MAX, S // 4)
    row2 = lambda a: a.reshape(1, -1).astype(F32)

    o_q, o_qi, o_ki, o_cv, o_g = 0, 3 * ATTN_WIDTH, 3 * ATTN_WIDTH + IDX_HEADS * IDX_DIM, \
        3 * ATTN_WIDTH + IDX_HEADS * IDX_DIM + IDX_DIM + IDX_HEADS, \
        3 * ATTN_WIDTH + IDX_HEADS * IDX_DIM + IDX_DIM + IDX_HEADS + 2 * CONV_CH
    kw_pad = LANES - (IDX_DIM + IDX_HEADS)
    wkw = jnp.pad(w_in[:, o_ki:o_cv], ((0, 0), (0, kw_pad)))
    bkw = jnp.pad(b_in[o_ki:o_cv], (0, kw_pad))
    q, k, v, qi, kw, u, sg = _proj(
        x, w_in[:, o_q:o_qi].astype(BF16), row2(b_in[o_q:o_qi]),
        w_in[:, o_qi:o_ki].astype(BF16), row2(b_in[o_qi:o_ki]),
        wkw.astype(BF16), row2(bkw),
        w_in[:, o_cv:o_g].astype(BF16), row2(b_in[o_cv:o_g]),
        w_in[:, o_g:].astype(BF16), row2(b_in[o_g:]))

    part_conv = _conv(u, sg, w_dw, row2(b_dw), row2(conv_ln_g), row2(conv_ln_b), w_o_conv.astype(BF16))

    ki = kw[:, :IDX_DIM].astype(BF16)
    mask4 = _index_mask(qi, kw[:, IDX_DIM:IDX_DIM + IDX_HEADS].T, jnp.concatenate([ki, ki], axis=1), top_k)
    bias_d, bias_e = _relative_bias_blocks(rel_bias)
    y_attn = _attention(q, k, v, mask4, bias_d, bias_e)

    wr = jnp.pad(w_router, ((0, 0), (0, LANES - N_EXPERTS)))
    wr_hi = wr.astype(BF16)
    wr_lo = (wr - wr_hi.astype(F32)).astype(BF16)
    br = jnp.pad(b_router, (0, LANES - N_EXPERTS), constant_values=-jnp.inf)
    x1, x1p, ridx, rgate = _post(y_attn, sg, part_conv, x, w_o_attn.astype(BF16), w_out.astype(BF16),
                                 row2(ln1_g), row2(ln1_b), wr_hi, wr_lo, row2(br))

    tile_e, tile_nv, nused, row_token, row_dst, n_tiles = _routing_tables(ridx[:TOP_K_EXPERTS], S)
    ys = _experts(tile_e, tile_nv, nused, row_token, row_dst, x1p, w_gate_up, b_gate_up.reshape(N_EXPERTS, 1, -1),
                  w_down, b_down.reshape(N_EXPERTS, 1, -1), n_tiles, S * TOP_K_EXPERTS)
    return _final(x1, ys, rgate, p, w_ple_gate.astype(BF16), w_ple_proj.astype(BF16), row2(ln2_g), row2(ln2_b))


def kernel(x, p, w_in, b_in, w_o_attn, w_dw, b_dw, conv_ln_g, conv_ln_b, w_o_conv, w_out, ln1_g, ln1_b, w_router, b_router, w_gate_up, b_gate_up, w_down, b_down, w_ple_gate, w_ple_proj, ln2_g, ln2_b, rel_bias):
    assert x.shape[0] == 1 and p.shape[0] == DEPTH
    out = _layer(x[0], p[0, 0], w_in[0], b_in[0], w_o_attn[0], w_dw[0], b_dw[0], conv_ln_g[0], conv_ln_b[0],
                 w_o_conv[0], w_out[0], ln1_g[0], ln1_b[0], w_router[0], b_router[0], w_gate_up[0], b_gate_up[0],
                 w_down[0], b_down[0], w_ple_gate[0], w_ple_proj[0], ln2_g[0], ln2_b[0], rel_bias)
    return out[None]
```

```python
import functools
import math

import jax
import jax.numpy as jnp
import numpy as np
from jax import lax
from jax.experimental import pallas as pl
from jax.experimental.pallas import tpu as pltpu

F32 = jnp.float32
BF16 = jnp.bfloat16

D_MODEL = 1024
N_HEADS = 8
HEAD_DIM = 64
ATTN_WIDTH = N_HEADS * HEAD_DIM
ATTN_SCALE = HEAD_DIM ** -0.5
LOG2E = math.log2(math.e)
IDX_HEADS = 8
IDX_DIM = 64
IDX_SCALE = (IDX_HEADS ** -0.5) * (IDX_DIM ** -0.5)
TOPK_MAX = 256
CONV_CH = 512
CONV_WIDTH = 31
N_BUCKETS = 32
MAX_DISTANCE = 128
N_EXPERTS = 32
TOP_K_EXPERTS = 4
D_EXPERT = 1024
SWIGLU_LIMIT = 7.0
SWIGLU_ALPHA = 1.702
PLE_DIM = 256
LN_EPS = 1e-5
DEPTH = 1
DEEPNORM_ALPHA = (2 * DEPTH) ** 0.25

LANES = 128
NEG_BIG = -1e30
F32_TINY = float(np.finfo(np.float32).tiny)
VMEM_LIMIT = 56 * 1024 * 1024

TM = 512
TQI = 256
TKI = 512
WARM_PASSES = 8
EDGE_FIRST = 9
SEARCH_CAP = 128
TA = 512
SUB = 128
TME = 256
DMA_UNROLL = 8
ROUTE_ROWS = 8
CONV_HALO = 32


def _params(sem, vmem=VMEM_LIMIT):
    return pltpu.CompilerParams(dimension_semantics=sem, vmem_limit_bytes=vmem)


def _sigmoid(x):
    return 1.0 / (1.0 + jnp.exp(-x))


def _layer_norm(x, g, b):
    mu = jnp.mean(x, axis=-1, keepdims=True)
    xc = x - mu
    var = jnp.mean(xc * xc, axis=-1, keepdims=True)
    return xc * lax.rsqrt(var + LN_EPS) * g + b


def _dot(a, b):
    return jnp.dot(a, b, preferred_element_type=F32)


def _dot_nt(a, b):
    return lax.dot_general(a, b, (((1,), (1,)), ((), ())), preferred_element_type=F32)


def _proj_kernel(x_ref, wqkv_ref, bqkv_ref, wqi_ref, bqi_ref, wkw_ref, bkw_ref, wcv_ref, bcv_ref,
                 wg_ref, bg_ref, q_ref, k_ref, v_ref, qi_ref, kw_ref, u_ref, sg_ref):
    xb = x_ref[...].astype(BF16)
    qkv = _dot(xb, wqkv_ref[...]) + bqkv_ref[...]
    q_ref[...] = (qkv[:, :ATTN_WIDTH] * (ATTN_SCALE * LOG2E)).astype(BF16)
    k_ref[...] = qkv[:, ATTN_WIDTH:2 * ATTN_WIDTH].astype(BF16)
    v_ref[...] = qkv[:, 2 * ATTN_WIDTH:].astype(BF16)
    qi_ref[...] = (_dot(xb, wqi_ref[...]) + bqi_ref[...]).astype(BF16)
    kw = _dot(xb, wkw_ref[...]) + bkw_ref[...]
    lane = lax.broadcasted_iota(jnp.int32, kw.shape, 1)
    kw_ref[...] = jnp.where(lane >= IDX_DIM, kw * IDX_SCALE, kw)
    cv = _dot(xb, wcv_ref[...]) + bcv_ref[...]
    u_ref[...] = cv[:, :CONV_CH] * _sigmoid(cv[:, CONV_CH:])
    sg_ref[...] = _sigmoid(_dot(xb, wg_ref[...]) + bg_ref[...])


def _proj(x, wqkv, bqkv, wqi, bqi, wkw, bkw, wcv, bcv, wg, bg):
    S = x.shape[0]
    full = lambda a: pl.BlockSpec(a.shape, lambda i: (0, 0))
    row = lambda n: pl.BlockSpec((TM, n), lambda i: (i, 0))
    outs = [(ATTN_WIDTH, BF16)] * 3 + [(IDX_HEADS * IDX_DIM, BF16), (LANES, F32), (CONV_CH, F32),
                                        (2 * D_MODEL, F32)]
    return pl.pallas_call(
        _proj_kernel,
        grid=(S // TM,),
        in_specs=[row(D_MODEL)] + [full(a) for a in (wqkv, bqkv, wqi, bqi, wkw, bkw, wcv, bcv, wg, bg)],
        out_specs=[row(n) for n, _ in outs],
        out_shape=[jax.ShapeDtypeStruct((S, n), dt) for n, dt in outs],
        compiler_params=_params(("parallel",)),
        name="proj",
    )(x, wqkv, bqkv, wqi, bqi, wkw, bkw, wcv, bcv, wg, bg)


def _conv_kernel(u_ref, wdw_ref, bdw_ref, lng_ref, lnb_ref, wo_ref, sgc_ref, o_ref, buf_ref, sh_ref):
    @pl.when(pl.program_id(0) == 0)
    def _():
        buf_ref[0:CONV_HALO, :] = jnp.zeros((CONV_HALO, CONV_CH), F32)

    buf_ref[CONV_HALO:CONV_HALO + TM, :] = u_ref[...]
    base = CONV_HALO - (CONV_WIDTH - 1)
    acc = jnp.zeros((TM, CONV_CH), F32) + bdw_ref[...]
    sub = 8
    for r in range(sub):
        taps = [j for j in range(CONV_WIDTH) if (base + j) % sub == r]
        span = max(base + j - r for j in taps) + TM
        sh_ref[0:span, :] = buf_ref[r:r + span, :]
        for j in taps:
            off = base + j - r
            acc = acc + wdw_ref[j:j + 1, :] * sh_ref[off:off + TM, :]
    buf_ref[0:CONV_HALO, :] = buf_ref[TM:TM + CONV_HALO, :]
    y = _layer_norm(acc, lng_ref[...], lnb_ref[...])
    y = y * _sigmoid(y)
    o_ref[...] = sgc_ref[...] * _dot(y.astype(BF16), wo_ref[...])


def _conv(u, sg, wdw, bdw, lng, lnb, wo):
    S = u.shape[0]
    full = lambda a: pl.BlockSpec(a.shape, lambda i: (0, 0))
    return pl.pallas_call(
        _conv_kernel,
        grid=(S // TM,),
        in_specs=[pl.BlockSpec((TM, CONV_CH), lambda i: (i, 0)), full(wdw), full(bdw), full(lng), full(lnb),
                  full(wo), pl.BlockSpec((TM, D_MODEL), lambda i: (i, 1))],
        out_specs=pl.BlockSpec((TM, D_MODEL), lambda i: (i, 0)),
        out_shape=jax.ShapeDtypeStruct((S, D_MODEL), F32),
        scratch_shapes=[pltpu.VMEM((TM + CONV_HALO, CONV_CH), F32), pltpu.VMEM((TM + CONV_HALO, CONV_CH), F32)],
        compiler_params=_params(("arbitrary",)),
        name="conv",
    )(u, wdw, bdw, lng, lnb, wo, sg)


def _float_key(f):
    b = lax.bitcast_convert_type(f, jnp.int32)
    return b ^ ((b >> 31) & jnp.int32(0x7FFFFFFF))


def _key_float(k):
    b = k ^ ((k >> 31) & jnp.int32(0x7FFFFFFF))
    return lax.bitcast_convert_type(b, F32)


def _index_kernel(qi_ref, wt_ref, ki_ref, mask_ref, sc_ref, qh_ref, *, top_k):
    i = pl.program_id(0)
    nkt = (i * TQI + TQI + TKI - 1) // TKI
    ntile = sc_ref.shape[0]
    q_g = i * TQI + lax.broadcasted_iota(jnp.int32, (1, TQI), 1)
    lane = lax.broadcasted_iota(jnp.int32, (TQI, LANES), 1)

    for h in range(IDX_HEADS):
        qp = qi_ref[:, (h // 2) * LANES:(h // 2 + 1) * LANES]
        keep = (lane < IDX_DIM) if h % 2 == 0 else (lane >= IDX_DIM)
        qh_ref[h] = jnp.where(keep, qp, jnp.zeros_like(qp))

    def score_tile(kt, causal):
        kb = ki_ref[pl.ds(pl.multiple_of(kt * TKI, TKI), TKI), :]
        acc = jnp.zeros((TKI, TQI), F32)
        for h in range(IDX_HEADS):
            acc = acc + wt_ref[h:h + 1, :] * jnp.maximum(_dot_nt(kb, qh_ref[h]), 0.0)
        if causal:
            key_g = kt * TKI + lax.broadcasted_iota(jnp.int32, (TKI, TQI), 0)
            ok = key_g <= q_g
            lo_src = jnp.where(ok, acc, jnp.inf)
            acc = jnp.where(ok, acc, -jnp.inf)
        else:
            lo_src = acc
        sc_ref[kt] = acc
        return jnp.max(acc, axis=0, keepdims=True), jnp.min(lo_src, axis=0, keepdims=True)

    def score_body(kt, carry):
        mx, mn = carry
        tmx, tmn = score_tile(kt, False)
        return jnp.maximum(mx, tmx), jnp.minimum(mn, tmn)

    mx0 = jnp.full((1, TQI), -jnp.inf, F32)
    mn0 = jnp.full((1, TQI), jnp.inf, F32)
    n_pair = (nkt - 1) // 2
    mx, mn = lax.fori_loop(0, n_pair, lambda j, c: score_body(2 * j + 1, score_body(2 * j, c)), (mx0, mn0))
    mx, mn = lax.fori_loop(2 * n_pair, nkt - 1, score_body, (mx, mn))
    tmx, tmn = score_tile(nkt - 1, True)
    mx = jnp.maximum(mx, tmx)
    mn = jnp.minimum(mn, tmn)
    kf = float(top_k)
    sub = 8

    def count(pivot, strict):
        def tile(kt, cnt):
            for r in range(TKI // sub):
                blk = sc_ref[kt, r * sub:(r + 1) * sub, :]
                hit = (blk > pivot) if strict else (blk >= pivot)
                cnt = cnt + jnp.where(hit, 1.0, 0.0)
            return cnt

        group = 4

        def tiles(j, cnts):
            return tuple(tile(group * j + g, cnts[g]) for g in range(group))

        zero = jnp.zeros((sub, TQI), F32)
        cnts = lax.fori_loop(0, nkt // group, tiles, (zero,) * group)
        rest = lax.fori_loop(group * (nkt // group), nkt, tile, cnts[0])
        return jnp.sum(rest + sum(cnts[1:]), axis=0, keepdims=True)

    n_causal = (q_g + 1).astype(F32)
    all_sel = n_causal <= kf
    zeros = jnp.zeros((1, TQI), F32)
    state0 = dict(
        it=jnp.int32(0),
        lo=mn, hi=_key_float(_float_key(mx) + 1), clo=n_causal, chi=zeros,
        glo=jnp.log(jnp.maximum(n_causal, kf + 1.0) / kf), ghi=jnp.full((1, TQI), math.log(0.5 / kf), F32),
        thr=jnp.where(all_sel, NEG_BIG, 0.0).astype(F32),
        done=all_sel.astype(F32), tie=zeros, side=zeros, forced=zeros, use_forced=zeros,
    )

    def above(x):
        return jnp.where(jnp.abs(x) < F32_TINY, F32_TINY, _key_float(_float_key(x) + 1))

    def bracket_edges(lo, hi):
        def tile(kt, carry):
            a, b = carry
            for r in range(TKI // sub):
                blk = sc_ref[kt, r * sub:(r + 1) * sub, :]
                a = jnp.maximum(a, jnp.where(blk < hi, blk, -jnp.inf))
                b = jnp.minimum(b, jnp.where(blk >= lo, blk, jnp.inf))
            return a, b
        a, b = lax.fori_loop(0, nkt, tile, (jnp.full((sub, TQI), -jnp.inf, F32), jnp.full((sub, TQI), jnp.inf, F32)))
        return jnp.max(a, axis=0, keepdims=True), jnp.min(b, axis=0, keepdims=True)

    def cond(st):
        return jnp.logical_and(st["it"] < SEARCH_CAP, jnp.min(st["done"]) < 0.5)

    def count_step(st):
        it, lo, hi, glo, ghi = st["it"], st["lo"], st["hi"], st["glo"], st["ghi"]
        lo_k, hi_k = _float_key(lo), _float_key(hi)
        above_lo = above(lo)
        below_hi = jnp.where(hi == F32_TINY, 0.0, _key_float(hi_k - 1))
        probe = jnp.logical_and(st["done"] < 0.5, above_lo < hi)
        tie_now = jnp.logical_and(st["done"] < 0.5, above_lo >= hi)
        frac = jnp.where(it < 24, glo / (glo - ghi), 0.5)
        pf = lo + (hi - lo) * frac
        pf = jnp.where(it < 64, pf, _key_float((lo_k >> 1) + (hi_k >> 1) + (lo_k & hi_k & 1)))
        pf = jnp.where(it == 0, F32_TINY, jnp.where(jnp.logical_and(it == 1, hi == F32_TINY), 0.0, pf))
        pf = jnp.where(st["use_forced"] > 0.5, st["forced"], pf)
        pf = jnp.where(probe, jnp.minimum(jnp.maximum(pf, above_lo), below_hi), lo)
        c = count(pf, False)
        hit = jnp.logical_and(probe, c == kf)
        up = jnp.logical_and(probe, c > kf)
        dn = jnp.logical_and(probe, c < kf)
        g = jnp.log(jnp.maximum(c, 0.5) / kf)
        return dict(
            it=it + 1,
            lo=jnp.where(up, pf, lo), hi=jnp.where(dn, pf, hi),
            clo=jnp.where(up, c, st["clo"]), chi=jnp.where(dn, c, st["chi"]),
            glo=jnp.where(up, g, jnp.where(jnp.logical_and(dn, st["side"] < -0.5), 0.5 * glo, glo)),
            ghi=jnp.where(dn, g, jnp.where(jnp.logical_and(up, st["side"] > 0.5), 0.5 * ghi, ghi)),
            thr=jnp.where(hit, pf, jnp.where(tie_now, lo, st["thr"])),
            done=jnp.where(jnp.logical_or(hit, tie_now), 1.0, st["done"]),
            tie=jnp.where(tie_now, 1.0, st["tie"]),
            side=jnp.where(up, 1.0, jnp.where(dn, -1.0, st["side"])),
            forced=st["forced"], use_forced=zeros,
        )

    def edge_step(st):
        lo, hi = st["lo"], st["hi"]
        probe = jnp.logical_and(st["done"] < 0.5, above(lo) < hi)
        a, b = bracket_edges(lo, hi)
        one_above = st["chi"] == kf - 1.0
        one_below = st["clo"] == kf + 1.0
        new = dict(st)
        new.update(
            it=st["it"] + 1,
            lo=jnp.where(probe, b, lo), hi=jnp.where(probe, above(a), hi),
            forced=jnp.where(one_above, a, above(b)),
            use_forced=jnp.where(jnp.logical_and(probe, jnp.logical_or(one_above, one_below)), 1.0, 0.0),
        )
        return new

    def step(st):
        it = st["it"]
        is_edge = jnp.logical_and(it >= EDGE_FIRST, (it - EDGE_FIRST) % 3 == 0)
        return lax.cond(is_edge, edge_step, count_step, st)

    st = lax.fori_loop(0, WARM_PASSES, lambda _, s: count_step(s), state0)
    st = lax.while_loop(cond, step, st)
    thr = st["thr"]
    tie = st["tie"]
    any_tie = jnp.max(tie) > 0.5

    def emit(kt, sel):
        mask_ref[0, kt] = jnp.where(sel, 0.0, NEG_BIG).astype(mask_ref.dtype).T

    @pl.when(jnp.logical_not(any_tie))
    def _():
        def body(kt, carry):
            emit(kt, sc_ref[kt] >= thr)
            return carry

        def pair(j, carry):
            return body(2 * j + 1, body(2 * j, carry))
        lax.fori_loop(0, nkt // 2, pair, 0)
        lax.fori_loop(2 * (nkt // 2), nkt, body, 0)

    @pl.when(any_tie)
    def _():
        need = jnp.where(tie > 0.5, kf - count(thr, True), float(2 * ntile * TKI))
        r = lax.broadcasted_iota(jnp.int32, (TKI, TKI), 0)
        c = lax.broadcasted_iota(jnp.int32, (TKI, TKI), 1)
        prefix = jnp.where(c <= r, 1.0, 0.0).astype(BF16)

        def body(kt, seen):
            s = sc_ref[kt]
            eq = s == thr
            rank = seen + _dot(prefix, jnp.where(eq, 1.0, 0.0).astype(BF16))
            emit(kt, jnp.logical_or(s > thr, jnp.logical_and(eq, rank <= need)))
            return seen + jnp.sum(jnp.where(eq, 1.0, 0.0), axis=0, keepdims=True)
        lax.fori_loop(0, nkt, body, jnp.zeros((1, TQI), F32))

    def fill(kt, carry):
        mask_ref[0, kt] = jnp.full((TQI, TKI), NEG_BIG, mask_ref.dtype)
        return carry
    lax.fori_loop(nkt, ntile, fill, 0)


def _index_mask(qi, wt, ki2, top_k):
    S = qi.shape[0]
    nq, nk = S // TQI, S // TKI
    return pl.pallas_call(
        functools.partial(_index_kernel, top_k=top_k),
        grid=(nq,),
        in_specs=[pl.BlockSpec((TQI, IDX_HEADS * IDX_DIM), lambda i: (i, 0)),
                  pl.BlockSpec((IDX_HEADS, TQI), lambda i: (0, i)),
                  pl.BlockSpec((S, LANES), lambda i: (0, 0))],
        out_specs=pl.BlockSpec((1, nk, TQI, TKI), lambda i: (i, 0, 0, 0)),
        out_shape=jax.ShapeDtypeStruct((nq, nk, TQI, TKI), BF16),
        scratch_shapes=[pltpu.VMEM((nk, TKI, TQI), F32),
                        pltpu.VMEM((IDX_HEADS, TQI, LANES), BF16)],
        compiler_params=_params(("parallel",)),
        name="index_mask",
    )(qi, wt, ki2)


def _attn_kernel(qt_ref, kt_ref, q_ref, k_ref, v_ref, m_ref, bd_ref, be_ref, o_ref, *state):
    m_sc, acc_sc = state[:N_HEADS], state[N_HEADS:]
    step = pl.program_id(0)
    qi = qt_ref[step]
    ki = kt_ref[step]
    nsub = TA // SUB

    @pl.when(ki == 0)
    def _():
        for h in range(N_HEADS):
            m_sc[h][...] = jnp.full((TA, LANES), NEG_BIG, F32)
            acc_sc[h][...] = jnp.zeros((TA, LANES), F32)

    lane = lax.broadcasted_iota(jnp.int32, (TA, LANES), 1)
    first = lane < HEAD_DIM

    def bias_tile(h, diagonal):
        zero = jnp.zeros((SUB, SUB), F32)
        rows = []
        for a in range(nsub):
            if diagonal:
                blks = [bd_ref[h] if b == a else (be_ref[h] if b == a - 1 else zero) for b in range(nsub)]
            else:
                blks = [be_ref[h] if (a == 0 and b == nsub - 1) else zero for b in range(nsub)]
            rows.append(jnp.concatenate(blks, axis=1))
        return jnp.concatenate(rows, axis=0)

    def update(near, diagonal):
        maskf = jnp.concatenate([m_ref[a, 0] for a in range(TA // TQI)], axis=0).astype(F32)

        def logits(h):
            cols = slice((h // 2) * LANES, (h // 2 + 1) * LANES)
            qp = q_ref[:, cols]
            mine = first if h % 2 == 0 else jnp.logical_not(first)
            s = _dot_nt(jnp.where(mine, qp, jnp.zeros_like(qp)), k_ref[:, cols]) + maskf
            return s + bias_tile(h, diagonal) if near else s

        s = logits(0)
        for h in range(N_HEADS):
            s_next = logits(h + 1) if h + 1 < N_HEADS else None
            vp = v_ref[:, (h // 2) * LANES:(h // 2 + 1) * LANES]
            mine = first if h % 2 == 0 else jnp.logical_not(first)
            vh = jnp.where(mine, vp, jnp.ones_like(vp))
            m_prev = m_sc[h][...]
            m_next = jnp.maximum(m_prev, jnp.max(s, axis=1, keepdims=True))
            pexp = jnp.exp2(s - jnp.concatenate([m_next] * (TA // LANES), axis=1)).astype(BF16)
            acc_sc[h][...] = jnp.exp2(m_prev - m_next) * acc_sc[h][...] + _dot(pexp, vh)
            m_sc[h][...] = m_next
            s = s_next

    @pl.when(ki < qi - 1)
    def _():
        update(False, False)

    @pl.when(ki == qi - 1)
    def _():
        update(True, False)

    @pl.when(ki == qi)
    def _():
        update(True, True)
        for p in range(N_HEADS // 2):
            a0, a1 = acc_sc[2 * p][...], acc_sc[2 * p + 1][...]
            d0 = pltpu.roll(a0, HEAD_DIM, axis=1)
            d1 = pltpu.roll(a1, HEAD_DIM, axis=1)
            o_ref[:, p * LANES:(p + 1) * LANES] = jnp.where(first, a0 / d0, a1 / d1).astype(o_ref.dtype)


def _attention(q, k, v, mask4, bias_d, bias_e):
    S = q.shape[0]
    nb = S // TA
    pairs = [(a, b) for a in range(nb) for b in range(a + 1)]
    qtab = jnp.asarray(np.array([a for a, _ in pairs], np.int32))
    ktab = jnp.asarray(np.array([b for _, b in pairs], np.int32))
    grid_spec = pltpu.PrefetchScalarGridSpec(
        num_scalar_prefetch=2,
        grid=(len(pairs),),
        in_specs=[pl.BlockSpec((TA, ATTN_WIDTH), lambda s, qt, kt: (qt[s], 0)),
                  pl.BlockSpec((TA, ATTN_WIDTH), lambda s, qt, kt: (kt[s], 0)),
                  pl.BlockSpec((TA, ATTN_WIDTH), lambda s, qt, kt: (kt[s], 0)),
                  pl.BlockSpec((TA // TQI, TA // TKI, TQI, TKI), lambda s, qt, kt: (qt[s], kt[s], 0, 0)),
                  pl.BlockSpec(bias_d.shape, lambda s, qt, kt: (0, 0, 0)),
                  pl.BlockSpec(bias_e.shape, lambda s, qt, kt: (0, 0, 0))],
        out_specs=pl.BlockSpec((TA, ATTN_WIDTH), lambda s, qt, kt: (qt[s], 0)),
        scratch_shapes=[pltpu.VMEM((TA, LANES), F32)] * (2 * N_HEADS),
    )
    return pl.pallas_call(
        _attn_kernel,
        grid_spec=grid_spec,
        out_shape=jax.ShapeDtypeStruct((S, ATTN_WIDTH), BF16),
        compiler_params=_params(("arbitrary",)),
        name="attn",
    )(qtab, ktab, q, k, v, mask4, bias_d, bias_e)


def _relative_bias_blocks(rel_bias):
    dist = jnp.arange(2 * SUB, dtype=jnp.int32)
    max_exact = N_BUCKETS // 2
    dist_f = jnp.maximum(dist, 1).astype(F32)
    large = max_exact + (jnp.log(dist_f / max_exact) / math.log(MAX_DISTANCE / max_exact)
                         * (N_BUCKETS - max_exact)).astype(jnp.int32)
    bucket = jnp.where(dist < max_exact, dist, jnp.minimum(large, N_BUCKETS - 1))
    table = ((rel_bias[bucket] - rel_bias[N_BUCKETS - 1]) * LOG2E).astype(F32)
    period = 2 * SUB

    def toeplitz(first_row_index):
        z = table[first_row_index].T
        cut = jnp.tile(z, (1, SUB))[:, :SUB * (period - 1)].reshape(-1, SUB, period - 1)
        return cut[:, :, :SUB]

    m = np.arange(period)
    return toeplitz((period - m) % period), toeplitz((SUB - m) % period)


def _pack_bf16_pair(a, b):
    def rnd(x):
        bits = lax.bitcast_convert_type(x, jnp.uint32)
        return bits + jnp.uint32(0x7FFF) + ((bits >> 16) & jnp.uint32(1))
    return (rnd(a) >> 16) | (rnd(b) & jnp.uint32(0xFFFF0000))


def _unpack_bf16_pair(p):
    lo = lax.bitcast_convert_type(p << 16, F32)
    hi = lax.bitcast_convert_type(p & jnp.uint32(0xFFFF0000), F32)
    return lo.astype(BF16), hi.astype(BF16)


def _post_kernel(ya_ref, sga_ref, pc_ref, x_ref, woa_ref, wout_ref, g_ref, b_ref, wr_ref, wrl_ref, br_ref,
                 x1_ref, x1p_ref, ridx_ref, rgate_ref):
    y_attn = _dot(ya_ref[...], woa_ref[...])
    merged = sga_ref[...] * y_attn + pc_ref[...]
    mix = _dot(merged.astype(BF16), wout_ref[...])
    x1 = _layer_norm(DEEPNORM_ALPHA * x_ref[...] + mix, g_ref[...], b_ref[...])
    x1_ref[...] = x1
    half = D_MODEL // 2
    x1p_ref[:, 0, :] = _pack_bf16_pair(x1[:, :half], x1[:, half:])

    x1_hi = x1.astype(BF16)
    x1_lo = (x1 - x1_hi.astype(F32)).astype(BF16)
    logits = (_dot(x1_hi, wr_ref[...]) + (_dot(x1_hi, wrl_ref[...]) + _dot(x1_lo, wr_ref[...]))) + br_ref[...]
    lane = lax.broadcasted_iota(jnp.int32, logits.shape, 1).astype(F32)
    cur = logits
    vals, idxs = [], []
    for _ in range(TOP_K_EXPERTS):
        m = jnp.max(cur, axis=1, keepdims=True)
        ix = jnp.min(jnp.where(cur == m, lane, float(LANES)), axis=1, keepdims=True)
        vals.append(m)
        idxs.append(ix)
        cur = jnp.where(lane == ix, -jnp.inf, cur)
    exps = [jnp.exp(v - vals[0]) for v in vals]
    denom = exps[0]
    for e in exps[1:]:
        denom = denom + e
    ridx = jnp.zeros_like(logits)
    rgate = jnp.zeros_like(logits)
    for j in range(TOP_K_EXPERTS):
        ridx = jnp.where(lane == float(j), idxs[j], ridx)
        rgate = jnp.where(lane == float(j), exps[j] / denom, rgate)
    ridx_ref[...] = ridx.T[:ridx_ref.shape[0], :].astype(jnp.int32)
    rgate_ref[...] = rgate


def _post(ya, sg, pc, x, woa, wout, g, b, wr, wrl, br):
    S = x.shape[0]
    full = lambda a: pl.BlockSpec(a.shape, lambda i: (0, 0))
    row = lambda n: pl.BlockSpec((TM, n), lambda i: (i, 0))
    return pl.pallas_call(
        _post_kernel,
        grid=(S // TM,),
        in_specs=[row(ATTN_WIDTH), row(D_MODEL), row(D_MODEL), row(D_MODEL)] + [full(a) for a in (woa, wout, g, b, wr, wrl, br)],
        out_specs=[row(D_MODEL), pl.BlockSpec((TM, 1, D_MODEL // 2), lambda i: (i, 0, 0)),
                   pl.BlockSpec((ROUTE_ROWS, TM), lambda i: (0, i)), row(LANES)],
        out_shape=[jax.ShapeDtypeStruct((S, D_MODEL), F32), jax.ShapeDtypeStruct((S, 1, D_MODEL // 2), jnp.uint32),
                   jax.ShapeDtypeStruct((ROUTE_ROWS, S), jnp.int32), jax.ShapeDtypeStruct((S, LANES), F32)],
        compiler_params=_params(("parallel",)),
        name="post",
    )(ya, sg, pc, x, woa, wout, g, b, wr, wrl, br)


def _expert_kernel(te_ref, nv_ref, nused_ref, tok_ref, tokn_ref, dst_ref, x_hbm, wgu_ref, bgu_ref, wd_ref, bd_ref,
                   ys_hbm, xs_buf, y_buf, wgu_sc, wd_sc, gsem, ssem):
    t = pl.program_id(0)
    nused = nused_ref[0]
    cur = t % 2
    nxt = 1 - cur

    def start_gather(idx_ref, slot):
        def body(r, carry):
            pltpu.make_async_copy(x_hbm.at[idx_ref[0, 0, r]], xs_buf.at[slot, pl.ds(r, 1)], gsem.at[slot]).start()
            return carry
        lax.fori_loop(0, TME, body, 0, unroll=DMA_UNROLL)

    def wait_gather(slot):
        pltpu.make_async_copy(xs_buf.at[slot], xs_buf.at[slot], gsem.at[slot]).wait()

    def wait_scatter(slot, n):
        pltpu.make_async_copy(ys_hbm.at[pl.ds(0, n)], ys_hbm.at[pl.ds(0, n)], ssem.at[slot]).wait()

    @pl.when(t == 0)
    def _():
        start_gather(tok_ref, 0)

    @pl.when(t + 1 < nused)
    def _():
        start_gather(tokn_ref, nxt)

    e = te_ref[t]
    e_prev = te_ref[jnp.maximum(t - 1, 0)]

    @pl.when(jnp.logical_or(t == 0, e != e_prev))
    def _():
        wgu_sc[...] = wgu_ref[0].astype(BF16)
        wd_sc[...] = wd_ref[0].astype(BF16)

    @pl.when(t < nused)
    def _():
        wait_gather(cur)

        @pl.when(t >= 2)
        def _():
            wait_scatter(cur, nv_ref[jnp.maximum(t - 2, 0)])

        half = D_MODEL // 2
        lo, hi = _unpack_bf16_pair(xs_buf[cur])
        gu = _dot(lo, wgu_sc[0:half, :]) + _dot(hi, wgu_sc[half:, :]) + bgu_ref[0]
        g = jnp.minimum(gu[:, :D_EXPERT], SWIGLU_LIMIT)
        u = jnp.clip(gu[:, D_EXPERT:], -SWIGLU_LIMIT, SWIGLU_LIMIT)
        act = (u + 1.0) * (g * _sigmoid(SWIGLU_ALPHA * g))
        y_buf[cur] = _dot(act.astype(BF16), wd_sc[...]) + bd_ref[0]

        def scatter(r, carry):
            pltpu.make_async_copy(y_buf.at[cur, pl.ds(r, 1)], ys_hbm.at[dst_ref[0, 0, r]], ssem.at[cur]).start()
            return carry

        def scatter_group(c, carry):
            for j in range(DMA_UNROLL):
                scatter(c * DMA_UNROLL + j, carry)
            return carry
        n_groups = nv_ref[t] // DMA_UNROLL
        lax.fori_loop(0, n_groups, scatter_group, 0)
        lax.fori_loop(n_groups * DMA_UNROLL, nv_ref[t], scatter, 0)

        @pl.when(t == nused - 1)
        def _():
            @pl.when(t >= 1)
            def _():
                wait_scatter(nxt, nv_ref[jnp.maximum(t - 1, 0)])
            wait_scatter(cur, nv_ref[t])


def _experts(tile_e, tile_nv, nused, row_token, row_dst, x1p, wgu, bgu, wd, bd, n_tiles, n_dst):
    idx = lambda a: a.reshape(n_tiles, 1, TME)
    smem_tile = lambda f: pl.BlockSpec((1, 1, TME), f, memory_space=pltpu.SMEM)
    grid_spec = pltpu.PrefetchScalarGridSpec(
        num_scalar_prefetch=3,
        grid=(n_tiles,),
        in_specs=[smem_tile(lambda t, te, nv, nu: (t, 0, 0)),
                  smem_tile(lambda t, te, nv, nu: (jnp.minimum(t + 1, n_tiles - 1), 0, 0)),
                  smem_tile(lambda t, te, nv, nu: (t, 0, 0)),
                  pl.BlockSpec(memory_space=pl.ANY),
                  pl.BlockSpec((1, D_MODEL, 2 * D_EXPERT), lambda t, te, nv, nu: (te[t], 0, 0)),
                  pl.BlockSpec((1, 1, 2 * D_EXPERT), lambda t, te, nv, nu: (te[t], 0, 0)),
                  pl.BlockSpec((1, D_EXPERT, D_MODEL), lambda t, te, nv, nu: (te[t], 0, 0)),
                  pl.BlockSpec((1, 1, D_MODEL), lambda t, te, nv, nu: (te[t], 0, 0))],
        out_specs=pl.BlockSpec(memory_space=pl.ANY),
        scratch_shapes=[pltpu.VMEM((2, TME, D_MODEL // 2), jnp.uint32), pltpu.VMEM((2, TME, D_MODEL), F32),
                        pltpu.VMEM((D_MODEL, 2 * D_EXPERT), BF16), pltpu.VMEM((D_EXPERT, D_MODEL), BF16),
                        pltpu.SemaphoreType.DMA((2,)), pltpu.SemaphoreType.DMA((2,))],
    )
    return pl.pallas_call(
        _expert_kernel,
        grid_spec=grid_spec,
        out_shape=jax.ShapeDtypeStruct((n_dst, 1, D_MODEL), F32),
        compiler_params=_params(("arbitrary",)),
        name="moe_experts",
    )(tile_e, tile_nv, nused, idx(row_token), idx(row_token), idx(row_dst), x1p, wgu, bgu, wd, bd)


def _routing_tables(ridx, S):
    n_flat = S * TOP_K_EXPERTS
    n_tiles = n_flat // TME + N_EXPERTS
    flat_e = ridx.reshape(n_flat)
    assert N_EXPERTS * n_flat < 2 ** 31
    keys = jnp.sort(flat_e * n_flat + jnp.arange(n_flat, dtype=jnp.int32))
    order = keys % n_flat
    grp_bound = jnp.searchsorted(keys, jnp.arange(N_EXPERTS + 1, dtype=jnp.int32) * n_flat).astype(jnp.int32)
    grp_start = grp_bound[:-1]
    counts = grp_bound[1:] - grp_start
    padded = ((counts + TME - 1) // TME) * TME
    pad_end = jnp.cumsum(padded)
    pad_start = pad_end - padded
    nused = (pad_end[-1] // TME).astype(jnp.int32).reshape(1)
    tile_row0 = jnp.arange(n_tiles, dtype=jnp.int32) * TME
    tile_e = jnp.minimum(jnp.sum(tile_row0[:, None] >= pad_end[None, :], axis=1), N_EXPERTS - 1).astype(jnp.int32)
    tile_rank0 = tile_row0 - pad_start[tile_e]
    tile_nv = jnp.where(tile_row0 < pad_end[-1], jnp.clip(counts[tile_e] - tile_rank0, 0, TME), 0).astype(jnp.int32)
    within = jnp.arange(TME, dtype=jnp.int32)[None, :]
    valid = within < tile_nv[:, None]
    src_flat = lax.optimization_barrier(order[jnp.clip((grp_start[tile_e] + tile_rank0)[:, None] + within, 0, n_flat - 1)])
    row_token = jnp.where(valid, src_flat % S, 0).astype(jnp.int32)
    row_dst = jnp.where(valid, src_flat, 0).astype(jnp.int32)
    return tile_e, tile_nv, nused, row_token, row_dst, n_tiles


def _final_kernel(x1_ref, y0_ref, y1_ref, y2_ref, y3_ref, rg_ref, p_ref, wpg_ref, wpp_ref, g_ref, b_ref, o_ref):
    h = DEEPNORM_ALPHA * x1_ref[...]
    rg = rg_ref[...]
    for j, y_ref in enumerate((y0_ref, y1_ref, y2_ref, y3_ref)):
        h = h + rg[:, j:j + 1] * y_ref[:, 0, :]
    ple = _sigmoid(_dot(h.astype(BF16), wpg_ref[...])) * _dot(p_ref[...].astype(BF16), wpp_ref[...])
    o_ref[...] = _layer_norm(h + ple, g_ref[...], b_ref[...])


def _final(x1, ys, rgate, p, wpg, wpp, g, b):
    S = x1.shape[0]
    full = lambda a: pl.BlockSpec(a.shape, lambda i: (0, 0))
    row = lambda n: pl.BlockSpec((TM, n), lambda i: (i, 0))
    return pl.pallas_call(
        _final_kernel,
        grid=(S // TM,),
        in_specs=[row(D_MODEL)]
        + [pl.BlockSpec((TM, 1, D_MODEL), lambda i, j=j: (j * (S // TM) + i, 0, 0)) for j in range(TOP_K_EXPERTS)]
        + [row(LANES), row(PLE_DIM), full(wpg), full(wpp), full(g), full(b)],
        out_specs=row(D_MODEL),
        out_shape=jax.ShapeDtypeStruct((S, D_MODEL), F32),
        compiler_params=_params(("parallel",)),
        name="final",
    )(x1, ys, ys, ys, ys, rgate, p, wpg, wpp, g, b)


def _layer(x, p, w_in, b_in, w_o_attn, w_dw, b_dw, conv_ln_g, conv_ln_b, w_o_conv, w_out, ln1_g, ln1_b,
           w_router, b_router, w_gate_up, b_gate_up, w_down, b_down, w_ple_gate, w_ple_proj, ln2_g, ln2_b,
           rel_bias):
    S = x.shape[0]
    assert S % TM == 0 and S % TA == 0 and S % TKI == 0 and (S * TOP_K_EXPERTS) % TME == 0
    top_k = min(TOPK_MAX, S // 4)
    row2 = lambda a: a.reshape(1, -1).astype(F32)

    o_q, o_qi, o_ki, o_cv, o_g = 0, 3 * ATTN_WIDTH, 3 * ATTN_WIDTH + IDX_HEADS * IDX_DIM, \
        3 * ATTN_WIDTH + IDX_HEADS * IDX_DIM + IDX_DIM + IDX_HEADS, \
        3 * ATTN_WIDTH + IDX_HEADS * IDX_DIM + IDX_DIM + IDX_HEADS + 2 * CONV_CH
    kw_pad = LANES - (IDX_DIM + IDX_HEADS)
    wkw = jnp.pad(w_in[:, o_ki:o_cv], ((0, 0), (0, kw_pad)))
    bkw = jnp.pad(b_in[o_ki:o_cv], (0, kw_pad))
    q, k, v, qi, kw, u, sg = _proj(
        x, w_in[:, o_q:o_qi].astype(BF16), row2(b_in[o_q:o_qi]),
        w_in[:, o_qi:o_ki].astype(BF16), row2(b_in[o_qi:o_ki]),
        wkw.astype(BF16), row2(bkw),
        w_in[:, o_cv:o_g].astype(BF16), row2(b_in[o_cv:o_g]),
        w_in[:, o_g:].astype(BF16), row2(b_in[o_g:]))

    part_conv = _conv(u, sg, w_dw, row2(b_dw), row2(conv_ln_g), row2(conv_ln_b), w_o_conv.astype(BF16))

    ki = kw[:, :IDX_DIM].astype(BF16)
    mask4 = _index_mask(qi, kw[:, IDX_DIM:IDX_DIM + IDX_HEADS].T, jnp.concatenate([ki, ki], axis=1), top_k)
    bias_d, bias_e = _relative_bias_blocks(rel_bias)
    y_attn = _attention(q, k, v, mask4, bias_d, bias_e)

    wr = jnp.pad(w_router, ((0, 0), (0, LANES - N_EXPERTS)))
    wr_hi = wr.astype(BF16)
    wr_lo = (wr - wr_hi.astype(F32)).astype(BF16)
    br = jnp.pad(b_router, (0, LANES - N_EXPERTS), constant_values=-jnp.inf)
    x1, x1p, ridx, rgate = _post(y_attn, sg, part_conv, x, w_o_attn.astype(BF16), w_out.astype(BF16),
                                 row2(ln1_g), row2(ln1_b), wr_hi, wr_lo, row2(br))

    tile_e, tile_nv, nused, row_token, row_dst, n_tiles = _routing_tables(ridx[:TOP_K_EXPERTS], S)
    ys = _experts(tile_e, tile_nv, nused, row_token, row_dst, x1p, w_gate_up, b_gate_up.reshape(N_EXPERTS, 1, -1),
                  w_down, b_down.reshape(N_EXPERTS, 1, -1), n_tiles, S * TOP_K_EXPERTS)
    return _final(x1, ys, rgate, p, w_ple_gate.astype(BF16), w_ple_proj.astype(BF16), row2(ln2_g), row2(ln2_b))


def kernel(x, p, w_in, b_in, w_o_attn, w_dw, b_dw, conv_ln_g, conv_ln_b, w_o_conv, w_out, ln1_g, ln1_b, w_router, b_router, w_gate_up, b_gate_up, w_down, b_down, w_ple_gate, w_ple_proj, ln2_g, ln2_b, rel_bias):
    assert x.shape[0] == 1 and p.shape[0] == DEPTH
    out = _layer(x[0], p[0, 0], w_in[0], b_in[0], w_o_attn[0], w_dw[0], b_dw[0], conv_ln_g[0], conv_ln_b[0],
                 w_o_conv[0], w_out[0], ln1_g[0], ln1_b[0], w_router[0], b_router[0], w_gate_up[0], b_gate_up[0],
                 w_down[0], b_down[0], w_ple_gate[0], w_ple_proj[0], ln2_g[0], ln2_b[0], rel_bias)
    return out[None]
```

```python
import functools
import math

import jax
import jax.numpy as jnp
import numpy as np
from jax import lax
from jax.experimental import pallas as pl
from jax.experimental.pallas import tpu as pltpu

F32 = jnp.float32
BF16 = jnp.bfloat16

D_MODEL = 1024
N_HEADS = 8
HEAD_DIM = 64
ATTN_WIDTH = N_HEADS * HEAD_DIM
ATTN_SCALE = HEAD_DIM ** -0.5
LOG2E = math.log2(math.e)
IDX_HEADS = 8
IDX_DIM = 64
IDX_SCALE = (IDX_HEADS ** -0.5) * (IDX_DIM ** -0.5)
TOPK_MAX = 256
CONV_CH = 512
CONV_WIDTH = 31
N_BUCKETS = 32
MAX_DISTANCE = 128
N_EXPERTS = 32
TOP_K_EXPERTS = 4
D_EXPERT = 1024
SWIGLU_LIMIT = 7.0
SWIGLU_ALPHA = 1.702
PLE_DIM = 256
LN_EPS = 1e-5
DEPTH = 1
DEEPNORM_ALPHA = (2 * DEPTH) ** 0.25

LANES = 128
NEG_BIG = -1e30
F32_TINY = float(np.finfo(np.float32).tiny)
VMEM_LIMIT = 56 * 1024 * 1024

TM = 512
TQI = 256
TKI = 512
SCORE_GROUP = 4
WARM_PASSES = 8
EDGE_FIRST = 9
SEARCH_CAP = 128
TA = 512
SUB = 128
TME = 256
DMA_UNROLL = 8
ROUTE_ROWS = 8
CONV_HALO = 32


def _params(sem, vmem=VMEM_LIMIT):
    return pltpu.CompilerParams(dimension_semantics=sem, vmem_limit_bytes=vmem)


def _sigmoid(x):
    return 1.0 / (1.0 + jnp.exp(-x))


def _layer_norm(x, g, b):
    mu = jnp.mean(x, axis=-1, keepdims=True)
    xc = x - mu
    var = jnp.mean(xc * xc, axis=-1, keepdims=True)
    return xc * lax.rsqrt(var + LN_EPS) * g + b


def _dot(a, b):
    return jnp.dot(a, b, preferred_element_type=F32)


def _dot_nt(a, b):
    return lax.dot_general(a, b, (((1,), (1,)), ((), ())), preferred_element_type=F32)


def _proj_kernel(x_ref, wqkv_ref, bqkv_ref, wqi_ref, bqi_ref, wkw_ref, bkw_ref, wcv_ref, bcv_ref,
                 wg_ref, bg_ref, q_ref, k_ref, v_ref, qi_ref, kw_ref, u_ref, sg_ref):
    xb = x_ref[...].astype(BF16)
    qkv = _dot(xb, wqkv_ref[...]) + bqkv_ref[...]
    q_ref[...] = (qkv[:, :ATTN_WIDTH] * (ATTN_SCALE * LOG2E)).astype(BF16)
    k_ref[...] = qkv[:, ATTN_WIDTH:2 * ATTN_WIDTH].astype(BF16)
    v_ref[...] = qkv[:, 2 * ATTN_WIDTH:].astype(BF16)
    qi_ref[...] = (_dot(xb, wqi_ref[...]) + bqi_ref[...]).astype(BF16)
    kw = _dot(xb, wkw_ref[...]) + bkw_ref[...]
    lane = lax.broadcasted_iota(jnp.int32, kw.shape, 1)
    kw_ref[...] = jnp.where(lane >= IDX_DIM, kw * IDX_SCALE, kw)
    cv = _dot(xb, wcv_ref[...]) + bcv_ref[...]
    u_ref[...] = cv[:, :CONV_CH] * _sigmoid(cv[:, CONV_CH:])
    sg_ref[...] = _sigmoid(_dot(xb, wg_ref[...]) + bg_ref[...])


def _proj(x, wqkv, bqkv, wqi, bqi, wkw, bkw, wcv, bcv, wg, bg):
    S = x.shape[0]
    full = lambda a: pl.BlockSpec(a.shape, lambda i: (0, 0))
    row = lambda n: pl.BlockSpec((TM, n), lambda i: (i, 0))
    outs = [(ATTN_WIDTH, BF16)] * 3 + [(IDX_HEADS * IDX_DIM, BF16), (LANES, F32), (CONV_CH, F32),
                                        (2 * D_MODEL, F32)]
    return pl.pallas_call(
        _proj_kernel,
        grid=(S // TM,),
        in_specs=[row(D_MODEL)] + [full(a) for a in (wqkv, bqkv, wqi, bqi, wkw, bkw, wcv, bcv, wg, bg)],
        out_specs=[row(n) for n, _ in outs],
        out_shape=[jax.ShapeDtypeStruct((S, n), dt) for n, dt in outs],
        compiler_params=_params(("parallel",)),
        name="proj",
    )(x, wqkv, bqkv, wqi, bqi, wkw, bkw, wcv, bcv, wg, bg)


def _conv_kernel(u_ref, wdw_ref, bdw_ref, lng_ref, lnb_ref, wo_ref, sgc_ref, o_ref, buf_ref, sh_ref):
    @pl.when(pl.program_id(0) == 0)
    def _():
        buf_ref[0:CONV_HALO, :] = jnp.zeros((CONV_HALO, CONV_CH), F32)

    buf_ref[CONV_HALO:CONV_HALO + TM, :] = u_ref[...]
    base = CONV_HALO - (CONV_WIDTH - 1)
    acc = jnp.zeros((TM, CONV_CH), F32) + bdw_ref[...]
    sub = 8
    for r in range(sub):
        taps = [j for j in range(CONV_WIDTH) if (base + j) % sub == r]
        span = max(base + j - r for j in taps) + TM
        sh_ref[0:span, :] = buf_ref[r:r + span, :]
        for j in taps:
            off = base + j - r
            acc = acc + wdw_ref[j:j + 1, :] * sh_ref[off:off + TM, :]
    buf_ref[0:CONV_HALO, :] = buf_ref[TM:TM + CONV_HALO, :]
    y = _layer_norm(acc, lng_ref[...], lnb_ref[...])
    y = y * _sigmoid(y)
    o_ref[...] = sgc_ref[...] * _dot(y.astype(BF16), wo_ref[...])


def _conv(u, sg, wdw, bdw, lng, lnb, wo):
    S = u.shape[0]
    full = lambda a: pl.BlockSpec(a.shape, lambda i: (0, 0))
    return pl.pallas_call(
        _conv_kernel,
        grid=(S // TM,),
        in_specs=[pl.BlockSpec((TM, CONV_CH), lambda i: (i, 0)), full(wdw), full(bdw), full(lng), full(lnb),
                  full(wo), pl.BlockSpec((TM, D_MODEL), lambda i: (i, 1))],
        out_specs=pl.BlockSpec((TM, D_MODEL), lambda i: (i, 0)),
        out_shape=jax.ShapeDtypeStruct((S, D_MODEL), F32),
        scratch_shapes=[pltpu.VMEM((TM + CONV_HALO, CONV_CH), F32), pltpu.VMEM((TM + CONV_HALO, CONV_CH), F32)],
        compiler_params=_params(("arbitrary",)),
        name="conv",
    )(u, wdw, bdw, lng, lnb, wo, sg)


def _float_key(f):
    b = lax.bitcast_convert_type(f, jnp.int32)
    return b ^ ((b >> 31) & jnp.int32(0x7FFFFFFF))


def _key_float(k):
    b = k ^ ((k >> 31) & jnp.int32(0x7FFFFFFF))
    return lax.bitcast_convert_type(b, F32)


def _index_kernel(qi_ref, wt_ref, ki_ref, mask_ref, sc_ref, qh_ref, *, top_k):
    i = pl.program_id(0)
    nkt = (i * TQI + TQI + TKI - 1) // TKI
    ntile = sc_ref.shape[0]
    q_g = i * TQI + lax.broadcasted_iota(jnp.int32, (1, TQI), 1)
    lane = lax.broadcasted_iota(jnp.int32, (TQI, LANES), 1)

    for h in range(IDX_HEADS):
        qp = qi_ref[:, (h // 2) * LANES:(h // 2 + 1) * LANES]
        keep = (lane < IDX_DIM) if h % 2 == 0 else (lane >= IDX_DIM)
        qh_ref[h] = jnp.where(keep, qp, jnp.zeros_like(qp))

    def score_tile(kt, causal):
        kb = ki_ref[pl.ds(pl.multiple_of(kt * TKI, TKI), TKI), :]
        acc = jnp.zeros((TKI, TQI), F32)
        for h in range(IDX_HEADS):
            acc = acc + wt_ref[h:h + 1, :] * jnp.maximum(_dot_nt(kb, qh_ref[h]), 0.0)
        if causal:
            key_g = kt * TKI + lax.broadcasted_iota(jnp.int32, (TKI, TQI), 0)
            ok = key_g <= q_g
            lo_src = jnp.where(ok, acc, jnp.inf)
            acc = jnp.where(ok, acc, -jnp.inf)
        else:
            lo_src = acc
        sc_ref[kt] = acc
        return jnp.max(acc, axis=0, keepdims=True), jnp.min(lo_src, axis=0, keepdims=True)

    def score_body(kt, carry):
        mx, mn = carry
        tmx, tmn = score_tile(kt, False)
        return jnp.maximum(mx, tmx), jnp.minimum(mn, tmn)

    mx0 = jnp.full((1, TQI), -jnp.inf, F32)
    mn0 = jnp.full((1, TQI), jnp.inf, F32)
    def score_group(j, carry):
        for g in range(SCORE_GROUP):
            carry = score_body(SCORE_GROUP * j + g, carry)
        return carry
    n_grp = (nkt - 1) // SCORE_GROUP
    mx, mn = lax.fori_loop(0, n_grp, score_group, (mx0, mn0))
    mx, mn = lax.fori_loop(SCORE_GROUP * n_grp, nkt - 1, score_body, (mx, mn))
    tmx, tmn = score_tile(nkt - 1, True)
    mx = jnp.maximum(mx, tmx)
    mn = jnp.minimum(mn, tmn)
    kf = float(top_k)
    sub = 8

    def count(pivot, strict):
        def tile(kt, cnt):
            for r in range(TKI // sub):
                blk = sc_ref[kt, r * sub:(r + 1) * sub, :]
                hit = (blk > pivot) if strict else (blk >= pivot)
                cnt = cnt + jnp.where(hit, 1.0, 0.0)
            return cnt

        group = 4

        def tiles(j, cnts):
            return tuple(tile(group * j + g, cnts[g]) for g in range(group))

        zero = jnp.zeros((sub, TQI), F32)
        cnts = lax.fori_loop(0, nkt // group, tiles, (zero,) * group)
        rest = lax.fori_loop(group * (nkt // group), nkt, tile, cnts[0])
        return jnp.sum(rest + sum(cnts[1:]), axis=0, keepdims=True)

    n_causal = (q_g + 1).astype(F32)
    all_sel = n_causal <= kf
    zeros = jnp.zeros((1, TQI), F32)
    state0 = dict(
        it=jnp.int32(0),
        lo=mn, hi=_key_float(_float_key(mx) + 1), clo=n_causal, chi=zeros,
        glo=jnp.log(jnp.maximum(n_causal, kf + 1.0) / kf), ghi=jnp.full((1, TQI), math.log(0.5 / kf), F32),
        thr=jnp.where(all_sel, NEG_BIG, 0.0).astype(F32),
        done=all_sel.astype(F32), tie=zeros, side=zeros, forced=zeros, use_forced=zeros,
    )

    def above(x):
        return jnp.where(jnp.abs(x) < F32_TINY, F32_TINY, _key_float(_float_key(x) + 1))

    def bracket_edges(lo, hi):
        def tile(kt, carry):
            a, b = carry
            for r in range(TKI // sub):
                blk = sc_ref[kt, r * sub:(r + 1) * sub, :]
                a = jnp.maximum(a, jnp.where(blk < hi, blk, -jnp.inf))
                b = jnp.minimum(b, jnp.where(blk >= lo, blk, jnp.inf))
            return a, b
        a, b = lax.fori_loop(0, nkt, tile, (jnp.full((sub, TQI), -jnp.inf, F32), jnp.full((sub, TQI), jnp.inf, F32)))
        return jnp.max(a, axis=0, keepdims=True), jnp.min(b, axis=0, keepdims=True)

    def cond(st):
        return jnp.logical_and(st["it"] < SEARCH_CAP, jnp.min(st["done"]) < 0.5)

    def count_step(st):
        it, lo, hi, glo, ghi = st["it"], st["lo"], st["hi"], st["glo"], st["ghi"]
        lo_k, hi_k = _float_key(lo), _float_key(hi)
        above_lo = above(lo)
        below_hi = jnp.where(hi == F32_TINY, 0.0, _key_float(hi_k - 1))
        probe = jnp.logical_and(st["done"] < 0.5, above_lo < hi)
        tie_now = jnp.logical_and(st["done"] < 0.5, above_lo >= hi)
        frac = jnp.where(it < 24, glo / (glo - ghi), 0.5)
        pf = lo + (hi - lo) * frac
        pf = jnp.where(it < 64, pf, _key_float((lo_k >> 1) + (hi_k >> 1) + (lo_k & hi_k & 1)))
        pf = jnp.where(it == 0, F32_TINY, jnp.where(jnp.logical_and(it == 1, hi == F32_TINY), 0.0, pf))
        pf = jnp.where(st["use_forced"] > 0.5, st["forced"], pf)
        pf = jnp.where(probe, jnp.minimum(jnp.maximum(pf, above_lo), below_hi), lo)
        c = count(pf, False)
        hit = jnp.logical_and(probe, c == kf)
        up = jnp.logical_and(probe, c > kf)
        dn = jnp.logical_and(probe, c < kf)
        g = jnp.log(jnp.maximum(c, 0.5) / kf)
        return dict(
            it=it + 1,
            lo=jnp.where(up, pf, lo), hi=jnp.where(dn, pf, hi),
            clo=jnp.where(up, c, st["clo"]), chi=jnp.where(dn, c, st["chi"]),
            glo=jnp.where(up, g, jnp.where(jnp.logical_and(dn, st["side"] < -0.5), 0.5 * glo, glo)),
            ghi=jnp.where(dn, g, jnp.where(jnp.logical_and(up, st["side"] > 0.5), 0.5 * ghi, ghi)),
            thr=jnp.where(hit, pf, jnp.where(tie_now, lo, st["thr"])),
            done=jnp.where(jnp.logical_or(hit, tie_now), 1.0, st["done"]),
            tie=jnp.where(tie_now, 1.0, st["tie"]),
            side=jnp.where(up, 1.0, jnp.where(dn, -1.0, st["side"])),
            forced=st["forced"], use_forced=zeros,
        )

    def edge_step(st):
        lo, hi = st["lo"], st["hi"]
        probe = jnp.logical_and(st["done"] < 0.5, above(lo) < hi)
        a, b = bracket_edges(lo, hi)
        one_above = st["chi"] == kf - 1.0
        one_below = st["clo"] == kf + 1.0
        new = dict(st)
        new.update(
            it=st["it"] + 1,
            lo=jnp.where(probe, b, lo), hi=jnp.where(probe, above(a), hi),
            forced=jnp.where(one_above, a, above(b)),
            use_forced=jnp.where(jnp.logical_and(probe, jnp.logical_or(one_above, one_below)), 1.0, 0.0),
        )
        return new

    def step(st):
        it = st["it"]
        is_edge = jnp.logical_and(it >= EDGE_FIRST, (it - EDGE_FIRST) % 3 == 0)
        return lax.cond(is_edge, edge_step, count_step, st)

    st = lax.fori_loop(0, WARM_PASSES, lambda _, s: count_step(s), state0)
    st = lax.while_loop(cond, step, st)
    thr = st["thr"]
    tie = st["tie"]
    any_tie = jnp.max(tie) > 0.5

    def emit(kt, sel):
        mask_ref[0, kt] = jnp.where(sel, 0.0, NEG_BIG).astype(mask_ref.dtype).T

    @pl.when(jnp.logical_not(any_tie))
    def _():
        def body(kt, carry):
            emit(kt, sc_ref[kt] >= thr)
            return carry

        def pair(j, carry):
            return body(2 * j + 1, body(2 * j, carry))
        lax.fori_loop(0, nkt // 2, pair, 0)
        lax.fori_loop(2 * (nkt // 2), nkt, body, 0)

    @pl.when(any_tie)
    def _():
        need = jnp.where(tie > 0.5, kf - count(thr, True), float(2 * ntile * TKI))
        r = lax.broadcasted_iota(jnp.int32, (TKI, TKI), 0)
        c = lax.broadcasted_iota(jnp.int32, (TKI, TKI), 1)
        prefix = jnp.where(c <= r, 1.0, 0.0).astype(BF16)

        def body(kt, seen):
            s = sc_ref[kt]
            eq = s == thr
            rank = seen + _dot(prefix, jnp.where(eq, 1.0, 0.0).astype(BF16))
            emit(kt, jnp.logical_or(s > thr, jnp.logical_and(eq, rank <= need)))
            return seen + jnp.sum(jnp.where(eq, 1.0, 0.0), axis=0, keepdims=True)
        lax.fori_loop(0, nkt, body, jnp.zeros((1, TQI), F32))

    def fill(kt, carry):
        mask_ref[0, kt] = jnp.full((TQI, TKI), NEG_BIG, mask_ref.dtype)
        return carry
    lax.fori_loop(nkt, ntile, fill, 0)


def _index_mask(qi, wt, ki2, top_k):
    S = qi.shape[0]
    nq, nk = S // TQI, S // TKI
    return pl.pallas_call(
        functools.partial(_index_kernel, top_k=top_k),
        grid=(nq,),
        in_specs=[pl.BlockSpec((TQI, IDX_HEADS * IDX_DIM), lambda i: (i, 0)),
                  pl.BlockSpec((IDX_HEADS, TQI), lambda i: (0, i)),
                  pl.BlockSpec((S, LANES), lambda i: (0, 0))],
        out_specs=pl.BlockSpec((1, nk, TQI, TKI), lambda i: (i, 0, 0, 0)),
        out_shape=jax.ShapeDtypeStruct((nq, nk, TQI, TKI), BF16),
        scratch_shapes=[pltpu.VMEM((nk, TKI, TQI), F32),
                        pltpu.VMEM((IDX_HEADS, TQI, LANES), BF16)],
        compiler_params=_params(("parallel",)),
        name="index_mask",
    )(qi, wt, ki2)


def _attn_kernel(qt_ref, kt_ref, q_ref, k_ref, v_ref, m_ref, bd_ref, be_ref, o_ref, *state):
    m_sc, acc_sc = state[:N_HEADS], state[N_HEADS:]
    step = pl.program_id(0)
    qi = qt_ref[step]
    ki = kt_ref[step]
    nsub = TA // SUB

    @pl.when(ki == 0)
    def _():
        for h in range(N_HEADS):
            m_sc[h][...] = jnp.full((TA, LANES), NEG_BIG, F32)
            acc_sc[h][...] = jnp.zeros((TA, LANES), F32)

    lane = lax.broadcasted_iota(jnp.int32, (TA, LANES), 1)
    first = lane < HEAD_DIM

    def bias_tile(h, diagonal):
        zero = jnp.zeros((SUB, SUB), F32)
        rows = []
        for a in range(nsub):
            if diagonal:
                blks = [bd_ref[h] if b == a else (be_ref[h] if b == a - 1 else zero) for b in range(nsub)]
            else:
                blks = [be_ref[h] if (a == 0 and b == nsub - 1) else zero for b in range(nsub)]
            rows.append(jnp.concatenate(blks, axis=1))
        return jnp.concatenate(rows, axis=0)

    def update(near, diagonal):
        maskf = jnp.concatenate([m_ref[a, 0] for a in range(TA // TQI)], axis=0).astype(F32)

        def logits(h):
            cols = slice((h // 2) * LANES, (h // 2 + 1) * LANES)
            qp = q_ref[:, cols]
            mine = first if h % 2 == 0 else jnp.logical_not(first)
            s = _dot_nt(jnp.where(mine, qp, jnp.zeros_like(qp)), k_ref[:, cols]) + maskf
            return s + bias_tile(h, diagonal) if near else s

        s = logits(0)
        for h in range(N_HEADS):
            s_next = logits(h + 1) if h + 1 < N_HEADS else None
            vp = v_ref[:, (h // 2) * LANES:(h // 2 + 1) * LANES]
            mine = first if h % 2 == 0 else jnp.logical_not(first)
            vh = jnp.where(mine, vp, jnp.ones_like(vp))
            m_prev = m_sc[h][...]
            m_next = jnp.maximum(m_prev, jnp.max(s, axis=1, keepdims=True))
            pexp = jnp.exp2(s - jnp.concatenate([m_next] * (TA // LANES), axis=1)).astype(BF16)
            acc_sc[h][...] = jnp.exp2(m_prev - m_next) * acc_sc[h][...] + _dot(pexp, vh)
            m_sc[h][...] = m_next
            s = s_next

    @pl.when(ki < qi - 1)
    def _():
        update(False, False)

    @pl.when(ki == qi - 1)
    def _():
        update(True, False)

    @pl.when(ki == qi)
    def _():
        update(True, True)
        for p in range(N_HEADS // 2):
            a0, a1 = acc_sc[2 * p][...], acc_sc[2 * p + 1][...]
            d0 = pltpu.roll(a0, HEAD_DIM, axis=1)
            d1 = pltpu.roll(a1, HEAD_DIM, axis=1)
            o_ref[:, p * LANES:(p + 1) * LANES] = jnp.where(first, a0 / d0, a1 / d1).astype(o_ref.dtype)


def _attention(q, k, v, mask4, bias_d, bias_e):
    S = q.shape[0]
    nb = S // TA
    pairs = [(a, b) for a in range(nb) for b in range(a + 1)]
    qtab = jnp.asarray(np.array([a for a, _ in pairs], np.int32))
    ktab = jnp.asarray(np.array([b for _, b in pairs], np.int32))
    grid_spec = pltpu.PrefetchScalarGridSpec(
        num_scalar_prefetch=2,
        grid=(len(pairs),),
        in_specs=[pl.BlockSpec((TA, ATTN_WIDTH), lambda s, qt, kt: (qt[s], 0)),
                  pl.BlockSpec((TA, ATTN_WIDTH), lambda s, qt, kt: (kt[s], 0)),
                  pl.BlockSpec((TA, ATTN_WIDTH), lambda s, qt, kt: (kt[s], 0)),
                  pl.BlockSpec((TA // TQI, TA // TKI, TQI, TKI), lambda s, qt, kt: (qt[s], kt[s], 0, 0)),
                  pl.BlockSpec(bias_d.shape, lambda s, qt, kt: (0, 0, 0)),
                  pl.BlockSpec(bias_e.shape, lambda s, qt, kt: (0, 0, 0))],
        out_specs=pl.BlockSpec((TA, ATTN_WIDTH), lambda s, qt, kt: (qt[s], 0)),
        scratch_shapes=[pltpu.VMEM((TA, LANES), F32)] * (2 * N_HEADS),
    )
    return pl.pallas_call(
        _attn_kernel,
        grid_spec=grid_spec,
        out_shape=jax.ShapeDtypeStruct((S, ATTN_WIDTH), BF16),
        compiler_params=_params(("arbitrary",)),
        name="attn",
    )(qtab, ktab, q, k, v, mask4, bias_d, bias_e)


def _relative_bias_blocks(rel_bias):
    dist = jnp.arange(2 * SUB, dtype=jnp.int32)
    max_exact = N_BUCKETS // 2
    dist_f = jnp.maximum(dist, 1).astype(F32)
    large = max_exact + (jnp.log(dist_f / max_exact) / math.log(MAX_DISTANCE / max_exact)
                         * (N_BUCKETS - max_exact)).astype(jnp.int32)
    bucket = jnp.where(dist < max_exact, dist, jnp.minimum(large, N_BUCKETS - 1))
    table = ((rel_bias[bucket] - rel_bias[N_BUCKETS - 1]) * LOG2E).astype(F32)
    period = 2 * SUB

    def toeplitz(first_row_index):
        z = table[first_row_index].T
        cut = jnp.tile(z, (1, SUB))[:, :SUB * (period - 1)].reshape(-1, SUB, period - 1)
        return cut[:, :, :SUB]

    m = np.arange(period)
    return toeplitz((period - m) % period), toeplitz((SUB - m) % period)


def _pack_bf16_pair(a, b):
    def rnd(x):
        bits = lax.bitcast_convert_type(x, jnp.uint32)
        return bits + jnp.uint32(0x7FFF) + ((bits >> 16) & jnp.uint32(1))
    return (rnd(a) >> 16) | (rnd(b) & jnp.uint32(0xFFFF0000))


def _unpack_bf16_pair(p):
    lo = lax.bitcast_convert_type(p << 16, F32)
    hi = lax.bitcast_convert_type(p & jnp.uint32(0xFFFF0000), F32)
    return lo.astype(BF16), hi.astype(BF16)


def _post_kernel(ya_ref, sga_ref, pc_ref, x_ref, woa_ref, wout_ref, g_ref, b_ref, wr_ref, wrl_ref, br_ref,
                 x1_ref, x1p_ref, ridx_ref, rgate_ref):
    y_attn = _dot(ya_ref[...], woa_ref[...])
    merged = sga_ref[...] * y_attn + pc_ref[...]
    mix = _dot(merged.astype(BF16), wout_ref[...])
    x1 = _layer_norm(DEEPNORM_ALPHA * x_ref[...] + mix, g_ref[...], b_ref[...])
    x1_ref[...] = x1
    half = D_MODEL // 2
    x1p_ref[:, 0, :] = _pack_bf16_pair(x1[:, :half], x1[:, half:])

    x1_hi = x1.astype(BF16)
    x1_lo = (x1 - x1_hi.astype(F32)).astype(BF16)
    logits = (_dot(x1_hi, wr_ref[...]) + (_dot(x1_hi, wrl_ref[...]) + _dot(x1_lo, wr_ref[...]))) + br_ref[...]
    lane = lax.broadcasted_iota(jnp.int32, logits.shape, 1).astype(F32)
    cur = logits
    vals, idxs = [], []
    for _ in range(TOP_K_EXPERTS):
        m = jnp.max(cur, axis=1, keepdims=True)
        ix = jnp.min(jnp.where(cur == m, lane, float(LANES)), axis=1, keepdims=True)
        vals.append(m)
        idxs.append(ix)
        cur = jnp.where(lane == ix, -jnp.inf, cur)
    exps = [jnp.exp(v - vals[0]) for v in vals]
    denom = exps[0]
    for e in exps[1:]:
        denom = denom + e
    ridx = jnp.zeros_like(logits)
    rgate = jnp.zeros_like(logits)
    for j in range(TOP_K_EXPERTS):
        ridx = jnp.where(lane == float(j), idxs[j], ridx)
        rgate = jnp.where(lane == float(j), exps[j] / denom, rgate)
    ridx_ref[...] = ridx.T[:ridx_ref.shape[0], :].astype(jnp.int32)
    rgate_ref[...] = rgate


def _post(ya, sg, pc, x, woa, wout, g, b, wr, wrl, br):
    S = x.shape[0]
    full = lambda a: pl.BlockSpec(a.shape, lambda i: (0, 0))
    row = lambda n: pl.BlockSpec((TM, n), lambda i: (i, 0))
    return pl.pallas_call(
        _post_kernel,
        grid=(S // TM,),
        in_specs=[row(ATTN_WIDTH), row(D_MODEL), row(D_MODEL), row(D_MODEL)] + [full(a) for a in (woa, wout, g, b, wr, wrl, br)],
        out_specs=[row(D_MODEL), pl.BlockSpec((TM, 1, D_MODEL // 2), lambda i: (i, 0, 0)),
                   pl.BlockSpec((ROUTE_ROWS, TM), lambda i: (0, i)), row(LANES)],
        out_shape=[jax.ShapeDtypeStruct((S, D_MODEL), F32), jax.ShapeDtypeStruct((S, 1, D_MODEL // 2), jnp.uint32),
                   jax.ShapeDtypeStruct((ROUTE_ROWS, S), jnp.int32), jax.ShapeDtypeStruct((S, LANES), F32)],
        compiler_params=_params(("parallel",)),
        name="post",
    )(ya, sg, pc, x, woa, wout, g, b, wr, wrl, br)


def _expert_kernel(te_ref, nv_ref, nused_ref, tok_ref, tokn_ref, dst_ref, x_hbm, wgu_ref, bgu_ref, wd_ref, bd_ref,
                   ys_hbm, xs_buf, y_buf, wgu_sc, wd_sc, gsem, ssem):
    t = pl.program_id(0)
    nused = nused_ref[0]
    cur = t % 2
    nxt = 1 - cur

    def start_gather(idx_ref, slot):
        def body(r, carry):
            pltpu.make_async_copy(x_hbm.at[idx_ref[0, 0, r]], xs_buf.at[slot, pl.ds(r, 1)], gsem.at[slot]).start()
            return carry
        lax.fori_loop(0, TME, body, 0, unroll=DMA_UNROLL)

    def wait_gather(slot):
        pltpu.make_async_copy(xs_buf.at[slot], xs_buf.at[slot], gsem.at[slot]).wait()

    def wait_scatter(slot, n):
        pltpu.make_async_copy(ys_hbm.at[pl.ds(0, n)], ys_hbm.at[pl.ds(0, n)], ssem.at[slot]).wait()

    @pl.when(t == 0)
    def _():
        start_gather(tok_ref, 0)

    @pl.when(t + 1 < nused)
    def _():
        start_gather(tokn_ref, nxt)

    e = te_ref[t]
    e_prev = te_ref[jnp.maximum(t - 1, 0)]

    @pl.when(jnp.logical_or(t == 0, e != e_prev))
    def _():
        wgu_sc[...] = wgu_ref[0].astype(BF16)
        wd_sc[...] = wd_ref[0].astype(BF16)

    @pl.when(t < nused)
    def _():
        wait_gather(cur)

        @pl.when(t >= 2)
        def _():
            wait_scatter(cur, nv_ref[jnp.maximum(t - 2, 0)])

        half = D_MODEL // 2
        lo, hi = _unpack_bf16_pair(xs_buf[cur])
        gu = _dot(lo, wgu_sc[0:half, :]) + _dot(hi, wgu_sc[half:, :]) + bgu_ref[0]
        g = jnp.minimum(gu[:, :D_EXPERT], SWIGLU_LIMIT)
        u = jnp.clip(gu[:, D_EXPERT:], -SWIGLU_LIMIT, SWIGLU_LIMIT)
        act = (u + 1.0) * (g * _sigmoid(SWIGLU_ALPHA * g))
        y_buf[cur] = _dot(act.astype(BF16), wd_sc[...]) + bd_ref[0]

        def scatter(r, carry):
            pltpu.make_async_copy(y_buf.at[cur, pl.ds(r, 1)], ys_hbm.at[dst_ref[0, 0, r]], ssem.at[cur]).start()
            return carry

        def scatter_group(c, carry):
            for j in range(DMA_UNROLL):
                scatter(c * DMA_UNROLL + j, carry)
            return carry
        n_groups = nv_ref[t] // DMA_UNROLL
        lax.fori_loop(0, n_groups, scatter_group, 0)
        lax.fori_loop(n_groups * DMA_UNROLL, nv_ref[t], scatter, 0)

        @pl.when(t == nused - 1)
        def _():
            @pl.when(t >= 1)
            def _():
                wait_scatter(nxt, nv_ref[jnp.maximum(t - 1, 0)])
            wait_scatter(cur, nv_ref[t])


def _experts(tile_e, tile_nv, nused, row_token, row_dst, x1p, wgu, bgu, wd, bd, n_tiles, n_dst):
    idx = lambda a: a.reshape(n_tiles, 1, TME)
    smem_tile = lambda f: pl.BlockSpec((1, 1, TME), f, memory_space=pltpu.SMEM)
    grid_spec = pltpu.PrefetchScalarGridSpec(
        num_scalar_prefetch=3,
        grid=(n_tiles,),
        in_specs=[smem_tile(lambda t, te, nv, nu: (t, 0, 0)),
                  smem_tile(lambda t, te, nv, nu: (jnp.minimum(t + 1, n_tiles - 1), 0, 0)),
                  smem_tile(lambda t, te, nv, nu: (t, 0, 0)),
                  pl.BlockSpec(memory_space=pl.ANY),
                  pl.BlockSpec((1, D_MODEL, 2 * D_EXPERT), lambda t, te, nv, nu: (te[t], 0, 0)),
                  pl.BlockSpec((1, 1, 2 * D_EXPERT), lambda t, te, nv, nu: (te[t], 0, 0)),
                  pl.BlockSpec((1, D_EXPERT, D_MODEL), lambda t, te, nv, nu: (te[t], 0, 0)),
                  pl.BlockSpec((1, 1, D_MODEL), lambda t, te, nv, nu: (te[t], 0, 0))],
        out_specs=pl.BlockSpec(memory_space=pl.ANY),
        scratch_shapes=[pltpu.VMEM((2, TME, D_MODEL // 2), jnp.uint32), pltpu.VMEM((2, TME, D_MODEL), F32),
                        pltpu.VMEM((D_MODEL, 2 * D_EXPERT), BF16), pltpu.VMEM((D_EXPERT, D_MODEL), BF16),
                        pltpu.SemaphoreType.DMA((2,)), pltpu.SemaphoreType.DMA((2,))],
    )
    return pl.pallas_call(
        _expert_kernel,
        grid_spec=grid_spec,
        out_shape=jax.ShapeDtypeStruct((n_dst, 1, D_MODEL), F32),
        compiler_params=_params(("arbitrary",)),
        name="moe_experts",
    )(tile_e, tile_nv, nused, idx(row_token), idx(row_token), idx(row_dst), x1p, wgu, bgu, wd, bd)


def _routing_tables(ridx, S):
    n_flat = S * TOP_K_EXPERTS
    n_tiles = n_flat // TME + N_EXPERTS
    flat_e = ridx.reshape(n_flat)
    assert N_EXPERTS * n_flat < 2 ** 31
    keys = jnp.sort(flat_e * n_flat + jnp.arange(n_flat, dtype=jnp.int32))
    order = keys % n_flat
    grp_bound = jnp.searchsorted(keys, jnp.arange(N_EXPERTS + 1, dtype=jnp.int32) * n_flat).astype(jnp.int32)
    grp_start = grp_bound[:-1]
    counts = grp_bound[1:] - grp_start
    padded = ((counts + TME - 1) // TME) * TME
    pad_end = jnp.cumsum(padded)
    pad_start = pad_end - padded
    nused = (pad_end[-1] // TME).astype(jnp.int32).reshape(1)
    tile_row0 = jnp.arange(n_tiles, dtype=jnp.int32) * TME
    tile_e = jnp.minimum(jnp.sum(tile_row0[:, None] >= pad_end[None, :], axis=1), N_EXPERTS - 1).astype(jnp.int32)
    tile_rank0 = tile_row0 - pad_start[tile_e]
    tile_nv = jnp.where(tile_row0 < pad_end[-1], jnp.clip(counts[tile_e] - tile_rank0, 0, TME), 0).astype(jnp.int32)
    within = jnp.arange(TME, dtype=jnp.int32)[None, :]
    valid = within < tile_nv[:, None]
    src_flat = lax.optimization_barrier(order[jnp.clip((grp_start[tile_e] + tile_rank0)[:, None] + within, 0, n_flat - 1)])
    row_token = jnp.where(valid, src_flat % S, 0).astype(jnp.int32)
    row_dst = jnp.where(valid, src_flat, 0).astype(jnp.int32)
    return tile_e, tile_nv, nused, row_token, row_dst, n_tiles


def _final_kernel(x1_ref, y0_ref, y1_ref, y2_ref, y3_ref, rg_ref, p_ref, wpg_ref, wpp_ref, g_ref, b_ref, o_ref):
    h = DEEPNORM_ALPHA * x1_ref[...]
    rg = rg_ref[...]
    for j, y_ref in enumerate((y0_ref, y1_ref, y2_ref, y3_ref)):
        h = h + rg[:, j:j + 1] * y_ref[:, 0, :]
    ple = _sigmoid(_dot(h.astype(BF16), wpg_ref[...])) * _dot(p_ref[...].astype(BF16), wpp_ref[...])
    o_ref[...] = _layer_norm(h + ple, g_ref[...], b_ref[...])


def _final(x1, ys, rgate, p, wpg, wpp, g, b):
    S = x1.shape[0]
    full = lambda a: pl.BlockSpec(a.shape, lambda i: (0, 0))
    row = lambda n: pl.BlockSpec((TM, n), lambda i: (i, 0))
    return pl.pallas_call(
        _final_kernel,
        grid=(S // TM,),
        in_specs=[row(D_MODEL)]
        + [pl.BlockSpec((TM, 1, D_MODEL), lambda i, j=j: (j * (S // TM) + i, 0, 0)) for j in range(TOP_K_EXPERTS)]
        + [row(LANES), row(PLE_DIM), full(wpg), full(wpp), full(g), full(b)],
        out_specs=row(D_MODEL),
        out_shape=jax.ShapeDtypeStruct((S, D_MODEL), F32),
        compiler_params=_params(("parallel",)),
        name="final",
    )(x1, ys, ys, ys, ys, rgate, p, wpg, wpp, g, b)


def _layer(x, p, w_in, b_in, w_o_attn, w_dw, b_dw, conv_ln_g, conv_ln_b, w_o_conv, w_out, ln1_g, ln1_b,
           w_router, b_router, w_gate_up, b_gate_up, w_down, b_down, w_ple_gate, w_ple_proj, ln2_g, ln2_b,
           rel_bias):
    S = x.shape[0]
    assert S % TM == 0 and S % TA == 0 and S % TKI == 0 and (S * TOP_K_EXPERTS) % TME == 0
    top_k = min(TOPK_MAX, S // 4)
    row2 = lambda a: a.reshape(1, -1).astype(F32)

    o_q, o_qi, o_ki, o_cv, o_g = 0, 3 * ATTN_WIDTH, 3 * ATTN_WIDTH + IDX_HEADS * IDX_DIM, \
        3 * ATTN_WIDTH + IDX_HEADS * IDX_DIM + IDX_DIM + IDX_HEADS, \
        3 * ATTN_WIDTH + IDX_HEADS * IDX_DIM + IDX_DIM + IDX_HEADS + 2 * CONV_CH
    kw_pad = LANES - (IDX_DIM + IDX_HEADS)
    wkw = jnp.pad(w_in[:, o_ki:o_cv], ((0, 0), (0, kw_pad)))
    bkw = jnp.pad(b_in[o_ki:o_cv], (0, kw_pad))
    q, k, v, qi, kw, u, sg = _proj(
        x, w_in[:, o_q:o_qi].astype(BF16), row2(b_in[o_q:o_qi]),
        w_in[:, o_qi:o_ki].astype(BF16), row2(b_in[o_qi:o_ki]),
        wkw.astype(BF16), row2(bkw),
        w_in[:, o_cv:o_g].astype(BF16), row2(b_in[o_cv:o_g]),
        w_in[:, o_g:].astype(BF16), row2(b_in[o_g:]))

    part_conv = _conv(u, sg, w_dw, row2(b_dw), row2(conv_ln_g), row2(conv_ln_b), w_o_conv.astype(BF16))

    ki = kw[:, :IDX_DIM].astype(BF16)
    mask4 = _index_mask(qi, kw[:, IDX_DIM:IDX_DIM + IDX_HEADS].T, jnp.concatenate([ki, ki], axis=1), top_k)
    bias_d, bias_e = _relative_bias_blocks(rel_bias)
    y_attn = _attention(q, k, v, mask4, bias_d, bias_e)

    wr = jnp.pad(w_router, ((0, 0), (0, LANES - N_EXPERTS)))
    wr_hi = wr.astype(BF16)
    wr_lo = (wr - wr_hi.astype(F32)).astype(BF16)
    br = jnp.pad(b_router, (0, LANES - N_EXPERTS), constant_values=-jnp.inf)
    x1, x1p, ridx, rgate = _post(y_attn, sg, part_conv, x, w_o_attn.astype(BF16), w_out.astype(BF16),
                                 row2(ln1_g), row2(ln1_b), wr_hi, wr_lo, row2(br))

    tile_e, tile_nv, nused, row_token, row_dst, n_tiles = _routing_tables(ridx[:TOP_K_EXPERTS], S)
    ys = _experts(tile_e, tile_nv, nused, row_token, row_dst, x1p, w_gate_up, b_gate_up.reshape(N_EXPERTS, 1, -1),
                  w_down, b_down.reshape(N_EXPERTS, 1, -1), n_tiles, S * TOP_K_EXPERTS)
    return _final(x1, ys, rgate, p, w_ple_gate.astype(BF16), w_ple_proj.astype(BF16), row2(ln2_g), row2(ln2_b))


def kernel(x, p, w_in, b_in, w_o_attn, w_dw, b_dw, conv_ln_g, conv_ln_b, w_o_conv, w_out, ln1_g, ln1_b, w_router, b_router, w_gate_up, b_gate_up, w_down, b_down, w_ple_gate, w_ple_proj, ln2_g, ln2_b, rel_bias):
    assert x.shape[0] == 1 and p.shape[0] == DEPTH
    out = _layer(x[0], p[0, 0], w_in[0], b_in[0], w_o_attn[0], w_dw[0], b_dw[0], conv_ln_g[0], conv_ln_b[0],
                 w_o_conv[0], w_out[0], ln1_g[0], ln1_b[0], w_router[0], b_router[0], w_gate_up[0], b_gate_up[0],
                 w_down[0], b_down[0], w_ple_gate[0], w_ple_proj[0], ln2_g[0], ln2_b[0], rel_bias)
    return out[None]
```

```python
import functools
import math

import jax
import jax.numpy as jnp
import numpy as np
from jax import lax
from jax.experimental import pallas as pl
from jax.experimental.pallas import tpu as pltpu

F32 = jnp.float32
BF16 = jnp.bfloat16

D_MODEL = 1024
N_HEADS = 8
HEAD_DIM = 64
ATTN_WIDTH = N_HEADS * HEAD_DIM
ATTN_SCALE = HEAD_DIM ** -0.5
LOG2E = math.log2(math.e)
IDX_HEADS = 8
IDX_DIM = 64
IDX_SCALE = (IDX_HEADS ** -0.5) * (IDX_DIM ** -0.5)
TOPK_MAX = 256
CONV_CH = 512
CONV_WIDTH = 31
N_BUCKETS = 32
MAX_DISTANCE = 128
N_EXPERTS = 32
TOP_K_EXPERTS = 4
D_EXPERT = 1024
SWIGLU_LIMIT = 7.0
SWIGLU_ALPHA = 1.702
PLE_DIM = 256
LN_EPS = 1e-5
DEPTH = 1
DEEPNORM_ALPHA = (2 * DEPTH) ** 0.25

LANES = 128
NEG_BIG = -1e30
F32_TINY = float(np.finfo(np.float32).tiny)
VMEM_LIMIT = 56 * 1024 * 1024

TM = 512
TQI = 256
TKI = 512
SCORE_GROUP = 8
WARM_PASSES = 8
EDGE_FIRST = 9
SEARCH_CAP = 128
TA = 512
SUB = 128
TME = 256
DMA_UNROLL = 8
ROUTE_ROWS = 8
CONV_HALO = 32


def _params(sem, vmem=VMEM_LIMIT):
    return pltpu.CompilerParams(dimension_semantics=sem, vmem_limit_bytes=vmem)


def _sigmoid(x):
    return 1.0 / (1.0 + jnp.exp(-x))


def _layer_norm(x, g, b):
    mu = jnp.mean(x, axis=-1, keepdims=True)
    xc = x - mu
    var = jnp.mean(xc * xc, axis=-1, keepdims=True)
    return xc * lax.rsqrt(var + LN_EPS) * g + b


def _dot(a, b):
    return jnp.dot(a, b, preferred_element_type=F32)


def _dot_nt(a, b):
    return lax.dot_general(a, b, (((1,), (1,)), ((), ())), preferred_element_type=F32)


def _proj_kernel(x_ref, wqkv_ref, bqkv_ref, wqi_ref, bqi_ref, wkw_ref, bkw_ref, wcv_ref, bcv_ref,
                 wg_ref, bg_ref, q_ref, k_ref, v_ref, qi_ref, kw_ref, u_ref, sg_ref):
    xb = x_ref[...].astype(BF16)
    qkv = _dot(xb, wqkv_ref[...]) + bqkv_ref[...]
    q_ref[...] = (qkv[:, :ATTN_WIDTH] * (ATTN_SCALE * LOG2E)).astype(BF16)
    k_ref[...] = qkv[:, ATTN_WIDTH:2 * ATTN_WIDTH].astype(BF16)
    v_ref[...] = qkv[:, 2 * ATTN_WIDTH:].astype(BF16)
    qi_ref[...] = (_dot(xb, wqi_ref[...]) + bqi_ref[...]).astype(BF16)
    kw = _dot(xb, wkw_ref[...]) + bkw_ref[...]
    lane = lax.broadcasted_iota(jnp.int32, kw.shape, 1)
    kw_ref[...] = jnp.where(lane >= IDX_DIM, kw * IDX_SCALE, kw)
    cv = _dot(xb, wcv_ref[...]) + bcv_ref[...]
    u_ref[...] = cv[:, :CONV_CH] * _sigmoid(cv[:, CONV_CH:])
    sg_ref[...] = _sigmoid(_dot(xb, wg_ref[...]) + bg_ref[...])


def _proj(x, wqkv, bqkv, wqi, bqi, wkw, bkw, wcv, bcv, wg, bg):
    S = x.shape[0]
    full = lambda a: pl.BlockSpec(a.shape, lambda i: (0, 0))
    row = lambda n: pl.BlockSpec((TM, n), lambda i: (i, 0))
    outs = [(ATTN_WIDTH, BF16)] * 3 + [(IDX_HEADS * IDX_DIM, BF16), (LANES, F32), (CONV_CH, F32),
                                        (2 * D_MODEL, F32)]
    return pl.pallas_call(
        _proj_kernel,
        grid=(S // TM,),
        in_specs=[row(D_MODEL)] + [full(a) for a in (wqkv, bqkv, wqi, bqi, wkw, bkw, wcv, bcv, wg, bg)],
        out_specs=[row(n) for n, _ in outs],
        out_shape=[jax.ShapeDtypeStruct((S, n), dt) for n, dt in outs],
        compiler_params=_params(("parallel",)),
        name="proj",
    )(x, wqkv, bqkv, wqi, bqi, wkw, bkw, wcv, bcv, wg, bg)


def _conv_kernel(u_ref, wdw_ref, bdw_ref, lng_ref, lnb_ref, wo_ref, sgc_ref, o_ref, buf_ref, sh_ref):
    @pl.when(pl.program_id(0) == 0)
    def _():
        buf_ref[0:CONV_HALO, :] = jnp.zeros((CONV_HALO, CONV_CH), F32)

    buf_ref[CONV_HALO:CONV_HALO + TM, :] = u_ref[...]
    base = CONV_HALO - (CONV_WIDTH - 1)
    acc = jnp.zeros((TM, CONV_CH), F32) + bdw_ref[...]
    sub = 8
    for r in range(sub):
        taps = [j for j in range(CONV_WIDTH) if (base + j) % sub == r]
        span = max(base + j - r for j in taps) + TM
        sh_ref[0:span, :] = buf_ref[r:r + span, :]
        for j in taps:
            off = base + j - r
            acc = acc + wdw_ref[j:j + 1, :] * sh_ref[off:off + TM, :]
    buf_ref[0:CONV_HALO, :] = buf_ref[TM:TM + CONV_HALO, :]
    y = _layer_norm(acc, lng_ref[...], lnb_ref[...])
    y = y * _sigmoid(y)
    o_ref[...] = sgc_ref[...] * _dot(y.astype(BF16), wo_ref[...])


def _conv(u, sg, wdw, bdw, lng, lnb, wo):
    S = u.shape[0]
    full = lambda a: pl.BlockSpec(a.shape, lambda i: (0, 0))
    return pl.pallas_call(
        _conv_kernel,
        grid=(S // TM,),
        in_specs=[pl.BlockSpec((TM, CONV_CH), lambda i: (i, 0)), full(wdw), full(bdw), full(lng), full(lnb),
                  full(wo), pl.BlockSpec((TM, D_MODEL), lambda i: (i, 1))],
        out_specs=pl.BlockSpec((TM, D_MODEL), lambda i: (i, 0)),
        out_shape=jax.ShapeDtypeStruct((S, D_MODEL), F32),
        scratch_shapes=[pltpu.VMEM((TM + CONV_HALO, CONV_CH), F32), pltpu.VMEM((TM + CONV_HALO, CONV_CH), F32)],
        compiler_params=_params(("arbitrary",)),
        name="conv",
    )(u, wdw, bdw, lng, lnb, wo, sg)


def _float_key(f):
    b = lax.bitcast_convert_type(f, jnp.int32)
    return b ^ ((b >> 31) & jnp.int32(0x7FFFFFFF))


def _key_float(k):
    b = k ^ ((k >> 31) & jnp.int32(0x7FFFFFFF))
    return lax.bitcast_convert_type(b, F32)


def _index_kernel(qi_ref, wt_ref, ki_ref, mask_ref, sc_ref, qh_ref, *, top_k):
    i = pl.program_id(0)
    nkt = (i * TQI + TQI + TKI - 1) // TKI
    ntile = sc_ref.shape[0]
    q_g = i * TQI + lax.broadcasted_iota(jnp.int32, (1, TQI), 1)
    lane = lax.broadcasted_iota(jnp.int32, (TQI, LANES), 1)

    for h in range(IDX_HEADS):
        qp = qi_ref[:, (h // 2) * LANES:(h // 2 + 1) * LANES]
        keep = (lane < IDX_DIM) if h % 2 == 0 else (lane >= IDX_DIM)
        qh_ref[h] = jnp.where(keep, qp, jnp.zeros_like(qp))

    def score_tile(kt, causal):
        kb = ki_ref[pl.ds(pl.multiple_of(kt * TKI, TKI), TKI), :]
        acc = jnp.zeros((TKI, TQI), F32)
        for h in range(IDX_HEADS):
            acc = acc + wt_ref[h:h + 1, :] * jnp.maximum(_dot_nt(kb, qh_ref[h]), 0.0)
        if causal:
            key_g = kt * TKI + lax.broadcasted_iota(jnp.int32, (TKI, TQI), 0)
            ok = key_g <= q_g
            lo_src = jnp.where(ok, acc, jnp.inf)
            acc = jnp.where(ok, acc, -jnp.inf)
        else:
            lo_src = acc
        sc_ref[kt] = acc
        return jnp.max(acc, axis=0, keepdims=True), jnp.min(lo_src, axis=0, keepdims=True)

    def score_body(kt, carry):
        mx, mn = carry
        tmx, tmn = score_tile(kt, False)
        return jnp.maximum(mx, tmx), jnp.minimum(mn, tmn)

    mx0 = jnp.full((1, TQI), -jnp.inf, F32)
    mn0 = jnp.full((1, TQI), jnp.inf, F32)
    def score_group(j, carry):
        for g in range(SCORE_GROUP):
            carry = score_body(SCORE_GROUP * j + g, carry)
        return carry
    n_grp = (nkt - 1) // SCORE_GROUP
    mx, mn = lax.fori_loop(0, n_grp, score_group, (mx0, mn0))
    mx, mn = lax.fori_loop(SCORE_GROUP * n_grp, nkt - 1, score_body, (mx, mn))
    tmx, tmn = score_tile(nkt - 1, True)
    mx = jnp.maximum(mx, tmx)
    mn = jnp.minimum(mn, tmn)
    kf = float(top_k)
    sub = 8

    def count(pivot, strict):
        def tile(kt, cnt):
            for r in range(TKI // sub):
                blk = sc_ref[kt, r * sub:(r + 1) * sub, :]
                hit = (blk > pivot) if strict else (blk >= pivot)
                cnt = cnt + jnp.where(hit, 1.0, 0.0)
            return cnt

        group = 4

        def tiles(j, cnts):
            return tuple(tile(group * j + g, cnts[g]) for g in range(group))

        zero = jnp.zeros((sub, TQI), F32)
        cnts = lax.fori_loop(0, nkt // group, tiles, (zero,) * group)
        rest = lax.fori_loop(group * (nkt // group), nkt, tile, cnts[0])
        return jnp.sum(rest + sum(cnts[1:]), axis=0, keepdims=True)

    n_causal = (q_g + 1).astype(F32)
    all_sel = n_causal <= kf
    zeros = jnp.zeros((1, TQI), F32)
    state0 = dict(
        it=jnp.int32(0),
        lo=mn, hi=_key_float(_float_key(mx) + 1), clo=n_causal, chi=zeros,
        glo=jnp.log(jnp.maximum(n_causal, kf + 1.0) / kf), ghi=jnp.full((1, TQI), math.log(0.5 / kf), F32),
        thr=jnp.where(all_sel, NEG_BIG, 0.0).astype(F32),
        done=all_sel.astype(F32), tie=zeros, side=zeros, forced=zeros, use_forced=zeros,
    )

    def above(x):
        return jnp.where(jnp.abs(x) < F32_TINY, F32_TINY, _key_float(_float_key(x) + 1))

    def bracket_edges(lo, hi):
        def tile(kt, carry):
            a, b = carry
            for r in range(TKI // sub):
                blk = sc_ref[kt, r * sub:(r + 1) * sub, :]
                a = jnp.maximum(a, jnp.where(blk < hi, blk, -jnp.inf))
                b = jnp.minimum(b, jnp.where(blk >= lo, blk, jnp.inf))
            return a, b
        a, b = lax.fori_loop(0, nkt, tile, (jnp.full((sub, TQI), -jnp.inf, F32), jnp.full((sub, TQI), jnp.inf, F32)))
        return jnp.max(a, axis=0, keepdims=True), jnp.min(b, axis=0, keepdims=True)

    def cond(st):
        return jnp.logical_and(st["it"] < SEARCH_CAP, jnp.min(st["done"]) < 0.5)

    def count_step(st):
        it, lo, hi, glo, ghi = st["it"], st["lo"], st["hi"], st["glo"], st["ghi"]
        lo_k, hi_k = _float_key(lo), _float_key(hi)
        above_lo = above(lo)
        below_hi = jnp.where(hi == F32_TINY, 0.0, _key_float(hi_k - 1))
        probe = jnp.logical_and(st["done"] < 0.5, above_lo < hi)
        tie_now = jnp.logical_and(st["done"] < 0.5, above_lo >= hi)
        frac = jnp.where(it < 24, glo / (glo - ghi), 0.5)
        pf = lo + (hi - lo) * frac
        pf = jnp.where(it < 64, pf, _key_float((lo_k >> 1) + (hi_k >> 1) + (lo_k & hi_k & 1)))
        pf = jnp.where(it == 0, F32_TINY, jnp.where(jnp.logical_and(it == 1, hi == F32_TINY), 0.0, pf))
        pf = jnp.where(st["use_forced"] > 0.5, st["forced"], pf)
        pf = jnp.where(probe, jnp.minimum(jnp.maximum(pf, above_lo), below_hi), lo)
        c = count(pf, False)
        hit = jnp.logical_and(probe, c == kf)
        up = jnp.logical_and(probe, c > kf)
        dn = jnp.logical_and(probe, c < kf)
        g = jnp.log(jnp.maximum(c, 0.5) / kf)
        return dict(
            it=it + 1,
            lo=jnp.where(up, pf, lo), hi=jnp.where(dn, pf, hi),
            clo=jnp.where(up, c, st["clo"]), chi=jnp.where(dn, c, st["chi"]),
            glo=jnp.where(up, g, jnp.where(jnp.logical_and(dn, st["side"] < -0.5), 0.5 * glo, glo)),
            ghi=jnp.where(dn, g, jnp.where(jnp.logical_and(up, st["side"] > 0.5), 0.5 * ghi, ghi)),
            thr=jnp.where(hit, pf, jnp.where(tie_now, lo, st["thr"])),
            done=jnp.where(jnp.logical_or(hit, tie_now), 1.0, st["done"]),
            tie=jnp.where(tie_now, 1.0, st["tie"]),
            side=jnp.where(up, 1.0, jnp.where(dn, -1.0, st["side"])),
            forced=st["forced"], use_forced=zeros,
        )

    def edge_step(st):
        lo, hi = st["lo"], st["hi"]
        probe = jnp.logical_and(st["done"] < 0.5, above(lo) < hi)
        a, b = bracket_edges(lo, hi)
        one_above = st["chi"] == kf - 1.0
        one_below = st["clo"] == kf + 1.0
        new = dict(st)
        new.update(
            it=st["it"] + 1,
            lo=jnp.where(probe, b, lo), hi=jnp.where(probe, above(a), hi),
            forced=jnp.where(one_above, a, above(b)),
            use_forced=jnp.where(jnp.logical_and(probe, jnp.logical_or(one_above, one_below)), 1.0, 0.0),
        )
        return new

    def step(st):
        it = st["it"]
        is_edge = jnp.logical_and(it >= EDGE_FIRST, (it - EDGE_FIRST) % 3 == 0)
        return lax.cond(is_edge, edge_step, count_step, st)

    st = lax.fori_loop(0, WARM_PASSES, lambda _, s: count_step(s), state0)
    st = lax.while_loop(cond, step, st)
    thr = st["thr"]
    tie = st["tie"]
    any_tie = jnp.max(tie) > 0.5

    def emit(kt, sel):
        mask_ref[0, kt] = jnp.where(sel, 0.0, NEG_BIG).astype(mask_ref.dtype).T

    @pl.when(jnp.logical_not(any_tie))
    def _():
        def body(kt, carry):
            emit(kt, sc_ref[kt] >= thr)
            return carry

        def pair(j, carry):
            return body(2 * j + 1, body(2 * j, carry))
        lax.fori_loop(0, nkt // 2, pair, 0)
        lax.fori_loop(2 * (nkt // 2), nkt, body, 0)

    @pl.when(any_tie)
    def _():
        need = jnp.where(tie > 0.5, kf - count(thr, True), float(2 * ntile * TKI))
        r = lax.broadcasted_iota(jnp.int32, (TKI, TKI), 0)
        c = lax.broadcasted_iota(jnp.int32, (TKI, TKI), 1)
        prefix = jnp.where(c <= r, 1.0, 0.0).astype(BF16)

        def body(kt, seen):
            s = sc_ref[kt]
            eq = s == thr
            rank = seen + _dot(prefix, jnp.where(eq, 1.0, 0.0).astype(BF16))
            emit(kt, jnp.logical_or(s > thr, jnp.logical_and(eq, rank <= need)))
            return seen + jnp.sum(jnp.where(eq, 1.0, 0.0), axis=0, keepdims=True)
        lax.fori_loop(0, nkt, body, jnp.zeros((1, TQI), F32))

    def fill(kt, carry):
        mask_ref[0, kt] = jnp.full((TQI, TKI), NEG_BIG, mask_ref.dtype)
        return carry
    lax.fori_loop(nkt, ntile, fill, 0)


def _index_mask(qi, wt, ki2, top_k):
    S = qi.shape[0]
    nq, nk = S // TQI, S // TKI
    return pl.pallas_call(
        functools.partial(_index_kernel, top_k=top_k),
        grid=(nq,),
        in_specs=[pl.BlockSpec((TQI, IDX_HEADS * IDX_DIM), lambda i: (i, 0)),
                  pl.BlockSpec((IDX_HEADS, TQI), lambda i: (0, i)),
                  pl.BlockSpec((S, LANES), lambda i: (0, 0))],
        out_specs=pl.BlockSpec((1, nk, TQI, TKI), lambda i: (i, 0, 0, 0)),
        out_shape=jax.ShapeDtypeStruct((nq, nk, TQI, TKI), BF16),
        scratch_shapes=[pltpu.VMEM((nk, TKI, TQI), F32),
                        pltpu.VMEM((IDX_HEADS, TQI, LANES), BF16)],
        compiler_params=_params(("parallel",)),
        name="index_mask",
    )(qi, wt, ki2)


def _attn_kernel(qt_ref, kt_ref, q_ref, k_ref, v_ref, m_ref, bd_ref, be_ref, o_ref, *state):
    m_sc, acc_sc = state[:N_HEADS], state[N_HEADS:]
    step = pl.program_id(0)
    qi = qt_ref[step]
    ki = kt_ref[step]
    nsub = TA // SUB

    @pl.when(ki == 0)
    def _():
        for h in range(N_HEADS):
            m_sc[h][...] = jnp.full((TA, LANES), NEG_BIG, F32)
            acc_sc[h][...] = jnp.zeros((TA, LANES), F32)

    lane = lax.broadcasted_iota(jnp.int32, (TA, LANES), 1)
    first = lane < HEAD_DIM

    def bias_tile(h, diagonal):
        zero = jnp.zeros((SUB, SUB), F32)
        rows = []
        for a in range(nsub):
            if diagonal:
                blks = [bd_ref[h] if b == a else (be_ref[h] if b == a - 1 else zero) for b in range(nsub)]
            else:
                blks = [be_ref[h] if (a == 0 and b == nsub - 1) else zero for b in range(nsub)]
            rows.append(jnp.concatenate(blks, axis=1))
        return jnp.concatenate(rows, axis=0)

    def update(near, diagonal):
        maskf = jnp.concatenate([m_ref[a, 0] for a in range(TA // TQI)], axis=0).astype(F32)

        def logits(h):
            cols = slice((h // 2) * LANES, (h // 2 + 1) * LANES)
            qp = q_ref[:, cols]
            mine = first if h % 2 == 0 else jnp.logical_not(first)
            s = _dot_nt(jnp.where(mine, qp, jnp.zeros_like(qp)), k_ref[:, cols]) + maskf
            return s + bias_tile(h, diagonal) if near else s

        s = logits(0)
        for h in range(N_HEADS):
            s_next = logits(h + 1) if h + 1 < N_HEADS else None
            vp = v_ref[:, (h // 2) * LANES:(h // 2 + 1) * LANES]
            mine = first if h % 2 == 0 else jnp.logical_not(first)
            vh = jnp.where(mine, vp, jnp.ones_like(vp))
            m_prev = m_sc[h][...]
            m_next = jnp.maximum(m_prev, jnp.max(s, axis=1, keepdims=True))
            pexp = jnp.exp2(s - jnp.concatenate([m_next] * (TA // LANES), axis=1)).astype(BF16)
            acc_sc[h][...] = jnp.exp2(m_prev - m_next) * acc_sc[h][...] + _dot(pexp, vh)
            m_sc[h][...] = m_next
            s = s_next

    @pl.when(ki < qi - 1)
    def _():
        update(False, False)

    @pl.when(ki == qi - 1)
    def _():
        update(True, False)

    @pl.when(ki == qi)
    def _():
        update(True, True)
        for p in range(N_HEADS // 2):
            a0, a1 = acc_sc[2 * p][...], acc_sc[2 * p + 1][...]
            d0 = pltpu.roll(a0, HEAD_DIM, axis=1)
            d1 = pltpu.roll(a1, HEAD_DIM, axis=1)
            o_ref[:, p * LANES:(p + 1) * LANES] = jnp.where(first, a0 / d0, a1 / d1).astype(o_ref.dtype)


def _attention(q, k, v, mask4, bias_d, bias_e):
    S = q.shape[0]
    nb = S // TA
    pairs = [(a, b) for a in range(nb) for b in range(a + 1)]
    qtab = jnp.asarray(np.array([a for a, _ in pairs], np.int32))
    ktab = jnp.asarray(np.array([b for _, b in pairs], np.int32))
    grid_spec = pltpu.PrefetchScalarGridSpec(
        num_scalar_prefetch=2,
        grid=(len(pairs),),
        in_specs=[pl.BlockSpec((TA, ATTN_WIDTH), lambda s, qt, kt: (qt[s], 0)),
                  pl.BlockSpec((TA, ATTN_WIDTH), lambda s, qt, kt: (kt[s], 0)),
                  pl.BlockSpec((TA, ATTN_WIDTH), lambda s, qt, kt: (kt[s], 0)),
                  pl.BlockSpec((TA // TQI, TA // TKI, TQI, TKI), lambda s, qt, kt: (qt[s], kt[s], 0, 0)),
                  pl.BlockSpec(bias_d.shape, lambda s, qt, kt: (0, 0, 0)),
                  pl.BlockSpec(bias_e.shape, lambda s, qt, kt: (0, 0, 0))],
        out_specs=pl.BlockSpec((TA, ATTN_WIDTH), lambda s, qt, kt: (qt[s], 0)),
        scratch_shapes=[pltpu.VMEM((TA, LANES), F32)] * (2 * N_HEADS),
    )
    return pl.pallas_call(
        _attn_kernel,
        grid_spec=grid_spec,
        out_shape=jax.ShapeDtypeStruct((S, ATTN_WIDTH), BF16),
        compiler_params=_params(("arbitrary",)),
        name="attn",
    )(qtab, ktab, q, k, v, mask4, bias_d, bias_e)


def _relative_bias_blocks(rel_bias):
    dist = jnp.arange(2 * SUB, dtype=jnp.int32)
    max_exact = N_BUCKETS // 2
    dist_f = jnp.maximum(dist, 1).astype(F32)
    large = max_exact + (jnp.log(dist_f / max_exact) / math.log(MAX_DISTANCE / max_exact)
                         * (N_BUCKETS - max_exact)).astype(jnp.int32)
    bucket = jnp.where(dist < max_exact, dist, jnp.minimum(large, N_BUCKETS - 1))
    table = ((rel_bias[bucket] - rel_bias[N_BUCKETS - 1]) * LOG2E).astype(F32)
    period = 2 * SUB

    def toeplitz(first_row_index):
        z = table[first_row_index].T
        cut = jnp.tile(z, (1, SUB))[:, :SUB * (period - 1)].reshape(-1, SUB, period - 1)
        return cut[:, :, :SUB]

    m = np.arange(period)
    return toeplitz((period - m) % period), toeplitz((SUB - m) % period)


def _pack_bf16_pair(a, b):
    def rnd(x):
        bits = lax.bitcast_convert_type(x, jnp.uint32)
        return bits + jnp.uint32(0x7FFF) + ((bits >> 16) & jnp.uint32(1))
    return (rnd(a) >> 16) | (rnd(b) & jnp.uint32(0xFFFF0000))


def _unpack_bf16_pair(p):
    lo = lax.bitcast_convert_type(p << 16, F32)
    hi = lax.bitcast_convert_type(p & jnp.uint32(0xFFFF0000), F32)
    return lo.astype(BF16), hi.astype(BF16)


def _post_kernel(ya_ref, sga_ref, pc_ref, x_ref, woa_ref, wout_ref, g_ref, b_ref, wr_ref, wrl_ref, br_ref,
                 x1_ref, x1p_ref, ridx_ref, rgate_ref):
    y_attn = _dot(ya_ref[...], woa_ref[...])
    merged = sga_ref[...] * y_attn + pc_ref[...]
    mix = _dot(merged.astype(BF16), wout_ref[...])
    x1 = _layer_norm(DEEPNORM_ALPHA * x_ref[...] + mix, g_ref[...], b_ref[...])
    x1_ref[...] = x1
    half = D_MODEL // 2
    x1p_ref[:, 0, :] = _pack_bf16_pair(x1[:, :half], x1[:, half:])

    x1_hi = x1.astype(BF16)
    x1_lo = (x1 - x1_hi.astype(F32)).astype(BF16)
    logits = (_dot(x1_hi, wr_ref[...]) + (_dot(x1_hi, wrl_ref[...]) + _dot(x1_lo, wr_ref[...]))) + br_ref[...]
    lane = lax.broadcasted_iota(jnp.int32, logits.shape, 1).astype(F32)
    cur = logits
    vals, idxs = [], []
    for _ in range(TOP_K_EXPERTS):
        m = jnp.max(cur, axis=1, keepdims=True)
        ix = jnp.min(jnp.where(cur == m, lane, float(LANES)), axis=1, keepdims=True)
        vals.append(m)
        idxs.append(ix)
        cur = jnp.where(lane == ix, -jnp.inf, cur)
    exps = [jnp.exp(v - vals[0]) for v in vals]
    denom = exps[0]
    for e in exps[1:]:
        denom = denom + e
    ridx = jnp.zeros_like(logits)
    rgate = jnp.zeros_like(logits)
    for j in range(TOP_K_EXPERTS):
        ridx = jnp.where(lane == float(j), idxs[j], ridx)
        rgate = jnp.where(lane == float(j), exps[j] / denom, rgate)
    ridx_ref[...] = ridx.T[:ridx_ref.shape[0], :].astype(jnp.int32)
    rgate_ref[...] = rgate


def _post(ya, sg, pc, x, woa, wout, g, b, wr, wrl, br):
    S = x.shape[0]
    full = lambda a: pl.BlockSpec(a.shape, lambda i: (0, 0))
    row = lambda n: pl.BlockSpec((TM, n), lambda i: (i, 0))
    return pl.pallas_call(
        _post_kernel,
        grid=(S // TM,),
        in_specs=[row(ATTN_WIDTH), row(D_MODEL), row(D_MODEL), row(D_MODEL)] + [full(a) for a in (woa, wout, g, b, wr, wrl, br)],
        out_specs=[row(D_MODEL), pl.BlockSpec((TM, 1, D_MODEL // 2), lambda i: (i, 0, 0)),
                   pl.BlockSpec((ROUTE_ROWS, TM), lambda i: (0, i)), row(LANES)],
        out_shape=[jax.ShapeDtypeStruct((S, D_MODEL), F32), jax.ShapeDtypeStruct((S, 1, D_MODEL // 2), jnp.uint32),
                   jax.ShapeDtypeStruct((ROUTE_ROWS, S), jnp.int32), jax.ShapeDtypeStruct((S, LANES), F32)],
        compiler_params=_params(("parallel",)),
        name="post",
    )(ya, sg, pc, x, woa, wout, g, b, wr, wrl, br)


def _expert_kernel(te_ref, nv_ref, nused_ref, tok_ref, tokn_ref, dst_ref, x_hbm, wgu_ref, bgu_ref, wd_ref, bd_ref,
                   ys_hbm, xs_buf, y_buf, wgu_sc, wd_sc, gsem, ssem):
    t = pl.program_id(0)
    nused = nused_ref[0]
    cur = t % 2
    nxt = 1 - cur

    def start_gather(idx_ref, slot):
        def body(r, carry):
            pltpu.make_async_copy(x_hbm.at[idx_ref[0, 0, r]], xs_buf.at[slot, pl.ds(r, 1)], gsem.at[slot]).start()
            return carry
        lax.fori_loop(0, TME, body, 0, unroll=DMA_UNROLL)

    def wait_gather(slot):
        pltpu.make_async_copy(xs_buf.at[slot], xs_buf.at[slot], gsem.at[slot]).wait()

    def wait_scatter(slot, n):
        pltpu.make_async_copy(ys_hbm.at[pl.ds(0, n)], ys_hbm.at[pl.ds(0, n)], ssem.at[slot]).wait()

    @pl.when(t == 0)
    def _():
        start_gather(tok_ref, 0)

    @pl.when(t + 1 < nused)
    def _():
        start_gather(tokn_ref, nxt)

    e = te_ref[t]
    e_prev = te_ref[jnp.maximum(t - 1, 0)]

    @pl.when(jnp.logical_or(t == 0, e != e_prev))
    def _():
        wgu_sc[...] = wgu_ref[0].astype(BF16)
        wd_sc[...] = wd_ref[0].astype(BF16)

    @pl.when(t < nused)
    def _():
        wait_gather(cur)

        @pl.when(t >= 2)
        def _():
            wait_scatter(cur, nv_ref[jnp.maximum(t - 2, 0)])

        half = D_MODEL // 2
        lo, hi = _unpack_bf16_pair(xs_buf[cur])
        gu = _dot(lo, wgu_sc[0:half, :]) + _dot(hi, wgu_sc[half:, :]) + bgu_ref[0]
        g = jnp.minimum(gu[:, :D_EXPERT], SWIGLU_LIMIT)
        u = jnp.clip(gu[:, D_EXPERT:], -SWIGLU_LIMIT, SWIGLU_LIMIT)
        act = (u + 1.0) * (g * _sigmoid(SWIGLU_ALPHA * g))
        y_buf[cur] = _dot(act.astype(BF16), wd_sc[...]) + bd_ref[0]

        def scatter(r, carry):
            pltpu.make_async_copy(y_buf.at[cur, pl.ds(r, 1)], ys_hbm.at[dst_ref[0, 0, r]], ssem.at[cur]).start()
            return carry

        def scatter_group(c, carry):
            for j in range(DMA_UNROLL):
                scatter(c * DMA_UNROLL + j, carry)
            return carry
        n_groups = nv_ref[t] // DMA_UNROLL
        lax.fori_loop(0, n_groups, scatter_group, 0)
        lax.fori_loop(n_groups * DMA_UNROLL, nv_ref[t], scatter, 0)

        @pl.when(t == nused - 1)
        def _():
            @pl.when(t >= 1)
            def _():
                wait_scatter(nxt, nv_ref[jnp.maximum(t - 1, 0)])
            wait_scatter(cur, nv_ref[t])


def _experts(tile_e, tile_nv, nused, row_token, row_dst, x1p, wgu, bgu, wd, bd, n_tiles, n_dst):
    idx = lambda a: a.reshape(n_tiles, 1, TME)
    smem_tile = lambda f: pl.BlockSpec((1, 1, TME), f, memory_space=pltpu.SMEM)
    grid_spec = pltpu.PrefetchScalarGridSpec(
        num_scalar_prefetch=3,
        grid=(n_tiles,),
        in_specs=[smem_tile(lambda t, te, nv, nu: (t, 0, 0)),
                  smem_tile(lambda t, te, nv, nu: (jnp.minimum(t + 1, n_tiles - 1), 0, 0)),
                  smem_tile(lambda t, te, nv, nu: (t, 0, 0)),
                  pl.BlockSpec(memory_space=pl.ANY),
                  pl.BlockSpec((1, D_MODEL, 2 * D_EXPERT), lambda t, te, nv, nu: (te[t], 0, 0)),
                  pl.BlockSpec((1, 1, 2 * D_EXPERT), lambda t, te, nv, nu: (te[t], 0, 0)),
                  pl.BlockSpec((1, D_EXPERT, D_MODEL), lambda t, te, nv, nu: (te[t], 0, 0)),
                  pl.BlockSpec((1, 1, D_MODEL), lambda t, te, nv, nu: (te[t], 0, 0))],
        out_specs=pl.BlockSpec(memory_space=pl.ANY),
        scratch_shapes=[pltpu.VMEM((2, TME, D_MODEL // 2), jnp.uint32), pltpu.VMEM((2, TME, D_MODEL), F32),
                        pltpu.VMEM((D_MODEL, 2 * D_EXPERT), BF16), pltpu.VMEM((D_EXPERT, D_MODEL), BF16),
                        pltpu.SemaphoreType.DMA((2,)), pltpu.SemaphoreType.DMA((2,))],
    )
    return pl.pallas_call(
        _expert_kernel,
        grid_spec=grid_spec,
        out_shape=jax.ShapeDtypeStruct((n_dst, 1, D_MODEL), F32),
        compiler_params=_params(("arbitrary",)),
        name="moe_experts",
    )(tile_e, tile_nv, nused, idx(row_token), idx(row_token), idx(row_dst), x1p, wgu, bgu, wd, bd)


def _routing_tables(ridx, S):
    n_flat = S * TOP_K_EXPERTS
    n_tiles = n_flat // TME + N_EXPERTS
    flat_e = ridx.reshape(n_flat)
    assert N_EXPERTS * n_flat < 2 ** 31
    keys = jnp.sort(flat_e * n_flat + jnp.arange(n_flat, dtype=jnp.int32))
    order = keys % n_flat
    grp_bound = jnp.searchsorted(keys, jnp.arange(N_EXPERTS + 1, dtype=jnp.int32) * n_flat).astype(jnp.int32)
    grp_start = grp_bound[:-1]
    counts = grp_bound[1:] - grp_start
    padded = ((counts + TME - 1) // TME) * TME
    pad_end = jnp.cumsum(padded)
    pad_start = pad_end - padded
    nused = (pad_end[-1] // TME).astype(jnp.int32).reshape(1)
    tile_row0 = jnp.arange(n_tiles, dtype=jnp.int32) * TME
    tile_e = jnp.minimum(jnp.sum(tile_row0[:, None] >= pad_end[None, :], axis=1), N_EXPERTS - 1).astype(jnp.int32)
    tile_rank0 = tile_row0 - pad_start[tile_e]
    tile_nv = jnp.where(tile_row0 < pad_end[-1], jnp.clip(counts[tile_e] - tile_rank0, 0, TME), 0).astype(jnp.int32)
    within = jnp.arange(TME, dtype=jnp.int32)[None, :]
    valid = within < tile_nv[:, None]
    src_flat = lax.optimization_barrier(order[jnp.clip((grp_start[tile_e] + tile_rank0)[:, None] + within, 0, n_flat - 1)])
    row_token = jnp.where(valid, src_flat % S, 0).astype(jnp.int32)
    row_dst = jnp.where(valid, src_flat, 0).astype(jnp.int32)
    return tile_e, tile_nv, nused, row_token, row_dst, n_tiles


def _final_kernel(x1_ref, y0_ref, y1_ref, y2_ref, y3_ref, rg_ref, p_ref, wpg_ref, wpp_ref, g_ref, b_ref, o_ref):
    h = DEEPNORM_ALPHA * x1_ref[...]
    rg = rg_ref[...]
    for j, y_ref in enumerate((y0_ref, y1_ref, y2_ref, y3_ref)):
        h = h + rg[:, j:j + 1] * y_ref[:, 0, :]
    ple = _sigmoid(_dot(h.astype(BF16), wpg_ref[...])) * _dot(p_ref[...].astype(BF16), wpp_ref[...])
    o_ref[...] = _layer_norm(h + ple, g_ref[...], b_ref[...])


def _final(x1, ys, rgate, p, wpg, wpp, g, b):
    S = x1.shape[0]
    full = lambda a: pl.BlockSpec(a.shape, lambda i: (0, 0))
    row = lambda n: pl.BlockSpec((TM, n), lambda i: (i, 0))
    return pl.pallas_call(
        _final_kernel,
        grid=(S // TM,),
        in_specs=[row(D_MODEL)]
        + [pl.BlockSpec((TM, 1, D_MODEL), lambda i, j=j: (j * (S // TM) + i, 0, 0)) for j in range(TOP_K_EXPERTS)]
        + [row(LANES), row(PLE_DIM), full(wpg), full(wpp), full(g), full(b)],
        out_specs=row(D_MODEL),
        out_shape=jax.ShapeDtypeStruct((S, D_MODEL), F32),
        compiler_params=_params(("parallel",)),
        name="final",
    )(x1, ys, ys, ys, ys, rgate, p, wpg, wpp, g, b)


def _layer(x, p, w_in, b_in, w_o_attn, w_dw, b_dw, conv_ln_g, conv_ln_b, w_o_conv, w_out, ln1_g, ln1_b,
           w_router, b_router, w_gate_up, b_gate_up, w_down, b_down, w_ple_gate, w_ple_proj, ln2_g, ln2_b,
           rel_bias):
    S = x.shape[0]
    assert S % TM == 0 and S % TA == 0 and S % TKI == 0 and (S * TOP_K_EXPERTS) % TME == 0
    top_k = min(TOPK_MAX, S // 4)
    row2 = lambda a: a.reshape(1, -1).astype(F32)

    o_q, o_qi, o_ki, o_cv, o_g = 0, 3 * ATTN_WIDTH, 3 * ATTN_WIDTH + IDX_HEADS * IDX_DIM, \
        3 * ATTN_WIDTH + IDX_HEADS * IDX_DIM + IDX_DIM + IDX_HEADS, \
        3 * ATTN_WIDTH + IDX_HEADS * IDX_DIM + IDX_DIM + IDX_HEADS + 2 * CONV_CH
    kw_pad = LANES - (IDX_DIM + IDX_HEADS)
    wkw = jnp.pad(w_in[:, o_ki:o_cv], ((0, 0), (0, kw_pad)))
    bkw = jnp.pad(b_in[o_ki:o_cv], (0, kw_pad))
    q, k, v, qi, kw, u, sg = _proj(
        x, w_in[:, o_q:o_qi].astype(BF16), row2(b_in[o_q:o_qi]),
        w_in[:, o_qi:o_ki].astype(BF16), row2(b_in[o_qi:o_ki]),
        wkw.astype(BF16), row2(bkw),
        w_in[:, o_cv:o_g].astype(BF16), row2(b_in[o_cv:o_g]),
        w_in[:, o_g:].astype(BF16), row2(b_in[o_g:]))

    part_conv = _conv(u, sg, w_dw, row2(b_dw), row2(conv_ln_g), row2(conv_ln_b), w_o_conv.astype(BF16))

    ki = kw[:, :IDX_DIM].astype(BF16)
    mask4 = _index_mask(qi, kw[:, IDX_DIM:IDX_DIM + IDX_HEADS].T, jnp.concatenate([ki, ki], axis=1), top_k)
    bias_d, bias_e = _relative_bias_blocks(rel_bias)
    y_attn = _attention(q, k, v, mask4, bias_d, bias_e)

    wr = jnp.pad(w_router, ((0, 0), (0, LANES - N_EXPERTS)))
    wr_hi = wr.astype(BF16)
    wr_lo = (wr - wr_hi.astype(F32)).astype(BF16)
    br = jnp.pad(b_router, (0, LANES - N_EXPERTS), constant_values=-jnp.inf)
    x1, x1p, ridx, rgate = _post(y_attn, sg, part_conv, x, w_o_attn.astype(BF16), w_out.astype(BF16),
                                 row2(ln1_g), row2(ln1_b), wr_hi, wr_lo, row2(br))

    tile_e, tile_nv, nused, row_token, row_dst, n_tiles = _routing_tables(ridx[:TOP_K_EXPERTS], S)
    ys = _experts(tile_e, tile_nv, nused, row_token, row_dst, x1p, w_gate_up, b_gate_up.reshape(N_EXPERTS, 1, -1),
                  w_down, b_down.reshape(N_EXPERTS, 1, -1), n_tiles, S * TOP_K_EXPERTS)
    return _final(x1, ys, rgate, p, w_ple_gate.astype(BF16), w_ple_proj.astype(BF16), row2(ln2_g), row2(ln2_b))


def kernel(x, p, w_in, b_in, w_o_attn, w_dw, b_dw, conv_ln_g, conv_ln_b, w_o_conv, w_out, ln1_g, ln1_b, w_router, b_router, w_gate_up, b_gate_up, w_down, b_down, w_ple_gate, w_ple_proj, ln2_g, ln2_b, rel_bias):
    assert x.shape[0] == 1 and p.shape[0] == DEPTH
    out = _layer(x[0], p[0, 0], w_in[0], b_in[0], w_o_attn[0], w_dw[0], b_dw[0], conv_ln_g[0], conv_ln_b[0],
                 w_o_conv[0], w_out[0], ln1_g[0], ln1_b[0], w_router[0], b_router[0], w_gate_up[0], b_gate_up[0],
                 w_down[0], b_down[0], w_ple_gate[0], w_ple_proj[0], ln2_g[0], ln2_b[0], rel_bias)
    return out[None]
```

```python
import functools
import math

import jax
import jax.numpy as jnp
import numpy as np
from jax import lax
from jax.experimental import pallas as pl
from jax.experimental.pallas import tpu as pltpu

F32 = jnp.float32
BF16 = jnp.bfloat16

D_MODEL = 1024
N_HEADS = 8
HEAD_DIM = 64
ATTN_WIDTH = N_HEADS * HEAD_DIM
ATTN_SCALE = HEAD_DIM ** -0.5
LOG2E = math.log2(math.e)
IDX_HEADS = 8
IDX_DIM = 64
IDX_SCALE = (IDX_HEADS ** -0.5) * (IDX_DIM ** -0.5)
TOPK_MAX = 256
CONV_CH = 512
CONV_WIDTH = 31
N_BUCKETS = 32
MAX_DISTANCE = 128
N_EXPERTS = 32
TOP_K_EXPERTS = 4
D_EXPERT = 1024
SWIGLU_LIMIT = 7.0
SWIGLU_ALPHA = 1.702
PLE_DIM = 256
LN_EPS = 1e-5
DEPTH = 1
DEEPNORM_ALPHA = (2 * DEPTH) ** 0.25

LANES = 128
NEG_BIG = -1e30
F32_TINY = float(np.finfo(np.float32).tiny)
VMEM_LIMIT = 56 * 1024 * 1024

TM = 512
TQI = 256
TKI = 512
SCORE_GROUP = 4
WARM_PASSES = 8
EDGE_FIRST = 9
SEARCH_CAP = 128
TA = 512
SUB = 128
TME = 256
DMA_UNROLL = 8
ROUTE_ROWS = 8
CONV_HALO = 32


def _params(sem, vmem=VMEM_LIMIT):
    return pltpu.CompilerParams(dimension_semantics=sem, vmem_limit_bytes=vmem)


def _sigmoid(x):
    return 1.0 / (1.0 + jnp.exp(-x))


def _layer_norm(x, g, b):
    mu = jnp.mean(x, axis=-1, keepdims=True)
    xc = x - mu
    var = jnp.mean(xc * xc, axis=-1, keepdims=True)
    return xc * lax.rsqrt(var + LN_EPS) * g + b


def _dot(a, b):
    return jnp.dot(a, b, preferred_element_type=F32)


def _dot_nt(a, b):
    return lax.dot_general(a, b, (((1,), (1,)), ((), ())), preferred_element_type=F32)


def _proj_kernel(x_ref, wqkv_ref, bqkv_ref, wqi_ref, bqi_ref, wkw_ref, bkw_ref, wcv_ref, bcv_ref,
                 wg_ref, bg_ref, q_ref, k_ref, v_ref, qi_ref, kw_ref, u_ref, sg_ref):
    xb = x_ref[...].astype(BF16)
    qkv = _dot(xb, wqkv_ref[...]) + bqkv_ref[...]
    q_ref[...] = (qkv[:, :ATTN_WIDTH] * (ATTN_SCALE * LOG2E)).astype(BF16)
    k_ref[...] = qkv[:, ATTN_WIDTH:2 * ATTN_WIDTH].astype(BF16)
    v_ref[...] = qkv[:, 2 * ATTN_WIDTH:].astype(BF16)
    qi_ref[...] = (_dot(xb, wqi_ref[...]) + bqi_ref[...]).astype(BF16)
    kw = _dot(xb, wkw_ref[...]) + bkw_ref[...]
    lane = lax.broadcasted_iota(jnp.int32, kw.shape, 1)
    kw_ref[...] = jnp.where(lane >= IDX_DIM, kw * IDX_SCALE, kw)
    cv = _dot(xb, wcv_ref[...]) + bcv_ref[...]
    u_ref[...] = cv[:, :CONV_CH] * _sigmoid(cv[:, CONV_CH:])
    sg_ref[...] = _sigmoid(_dot(xb, wg_ref[...]) + bg_ref[...])


def _proj(x, wqkv, bqkv, wqi, bqi, wkw, bkw, wcv, bcv, wg, bg):
    S = x.shape[0]
    full = lambda a: pl.BlockSpec(a.shape, lambda i: (0, 0))
    row = lambda n: pl.BlockSpec((TM, n), lambda i: (i, 0))
    outs = [(ATTN_WIDTH, BF16)] * 3 + [(IDX_HEADS * IDX_DIM, BF16), (LANES, F32), (CONV_CH, F32),
                                        (2 * D_MODEL, F32)]
    return pl.pallas_call(
        _proj_kernel,
        grid=(S // TM,),
        in_specs=[row(D_MODEL)] + [full(a) for a in (wqkv, bqkv, wqi, bqi, wkw, bkw, wcv, bcv, wg, bg)],
        out_specs=[row(n) for n, _ in outs],
        out_shape=[jax.ShapeDtypeStruct((S, n), dt) for n, dt in outs],
        compiler_params=_params(("parallel",)),
        name="proj",
    )(x, wqkv, bqkv, wqi, bqi, wkw, bkw, wcv, bcv, wg, bg)


def _conv_kernel(u_ref, wdw_ref, bdw_ref, lng_ref, lnb_ref, wo_ref, sgc_ref, o_ref, buf_ref, sh_ref):
    @pl.when(pl.program_id(0) == 0)
    def _():
        buf_ref[0:CONV_HALO, :] = jnp.zeros((CONV_HALO, CONV_CH), F32)

    buf_ref[CONV_HALO:CONV_HALO + TM, :] = u_ref[...]
    base = CONV_HALO - (CONV_WIDTH - 1)
    acc = jnp.zeros((TM, CONV_CH), F32) + bdw_ref[...]
    sub = 8
    for r in range(sub):
        taps = [j for j in range(CONV_WIDTH) if (base + j) % sub == r]
        span = max(base + j - r for j in taps) + TM
        sh_ref[0:span, :] = buf_ref[r:r + span, :]
        for j in taps:
            off = base + j - r
            acc = acc + wdw_ref[j:j + 1, :] * sh_ref[off:off + TM, :]
    buf_ref[0:CONV_HALO, :] = buf_ref[TM:TM + CONV_HALO, :]
    y = _layer_norm(acc, lng_ref[...], lnb_ref[...])
    y = y * _sigmoid(y)
    o_ref[...] = sgc_ref[...] * _dot(y.astype(BF16), wo_ref[...])


def _conv(u, sg, wdw, bdw, lng, lnb, wo):
    S = u.shape[0]
    full = lambda a: pl.BlockSpec(a.shape, lambda i: (0, 0))
    return pl.pallas_call(
        _conv_kernel,
        grid=(S // TM,),
        in_specs=[pl.BlockSpec((TM, CONV_CH), lambda i: (i, 0)), full(wdw), full(bdw), full(lng), full(lnb),
                  full(wo), pl.BlockSpec((TM, D_MODEL), lambda i: (i, 1))],
        out_specs=pl.BlockSpec((TM, D_MODEL), lambda i: (i, 0)),
        out_shape=jax.ShapeDtypeStruct((S, D_MODEL), F32),
        scratch_shapes=[pltpu.VMEM((TM + CONV_HALO, CONV_CH), F32), pltpu.VMEM((TM + CONV_HALO, CONV_CH), F32)],
        compiler_params=_params(("arbitrary",)),
        name="conv",
    )(u, wdw, bdw, lng, lnb, wo, sg)


def _float_key(f):
    b = lax.bitcast_convert_type(f, jnp.int32)
    return b ^ ((b >> 31) & jnp.int32(0x7FFFFFFF))


def _key_float(k):
    b = k ^ ((k >> 31) & jnp.int32(0x7FFFFFFF))
    return lax.bitcast_convert_type(b, F32)


def _index_kernel(qi_ref, wt_ref, ki_ref, mask_ref, sc_ref, qh_ref, *, top_k):
    i = pl.program_id(0)
    nkt = (i * TQI + TQI + TKI - 1) // TKI
    ntile = sc_ref.shape[0]
    q_g = i * TQI + lax.broadcasted_iota(jnp.int32, (1, TQI), 1)
    lane = lax.broadcasted_iota(jnp.int32, (TQI, LANES), 1)

    for h in range(IDX_HEADS):
        qp = qi_ref[:, (h // 2) * LANES:(h // 2 + 1) * LANES]
        keep = (lane < IDX_DIM) if h % 2 == 0 else (lane >= IDX_DIM)
        qh_ref[h] = jnp.where(keep, qp, jnp.zeros_like(qp))

    def score_tile(kt, causal):
        kb = ki_ref[pl.ds(pl.multiple_of(kt * TKI, TKI), TKI), :]
        acc = jnp.zeros((TKI, TQI), F32)
        for h in range(IDX_HEADS):
            acc = acc + wt_ref[h:h + 1, :] * jnp.maximum(_dot_nt(kb, qh_ref[h]), 0.0)
        if causal:
            key_g = kt * TKI + lax.broadcasted_iota(jnp.int32, (TKI, TQI), 0)
            ok = key_g <= q_g
            lo_src = jnp.where(ok, acc, jnp.inf)
            acc = jnp.where(ok, acc, -jnp.inf)
        else:
            lo_src = acc
        sc_ref[kt] = acc
        return jnp.max(acc, axis=0, keepdims=True), jnp.min(lo_src, axis=0, keepdims=True)

    def score_body(kt, carry):
        mx, mn = carry
        tmx, tmn = score_tile(kt, False)
        return jnp.maximum(mx, tmx), jnp.minimum(mn, tmn)

    mx0 = jnp.full((1, TQI), -jnp.inf, F32)
    mn0 = jnp.full((1, TQI), jnp.inf, F32)
    def score_group(j, carry):
        for g in range(SCORE_GROUP):
            carry = score_body(SCORE_GROUP * j + g, carry)
        return carry
    n_grp = (nkt - 1) // SCORE_GROUP
    mx, mn = lax.fori_loop(0, n_grp, score_group, (mx0, mn0))
    mx, mn = lax.fori_loop(SCORE_GROUP * n_grp, nkt - 1, score_body, (mx, mn))
    tmx, tmn = score_tile(nkt - 1, True)
    mx = jnp.maximum(mx, tmx)
    mn = jnp.minimum(mn, tmn)
    kf = float(top_k)
    sub = 8

    def count(pivot, strict):
        def tile(kt, cnt):
            for r in range(TKI // sub):
                blk = sc_ref[kt, r * sub:(r + 1) * sub, :]
                hit = (blk > pivot) if strict else (blk >= pivot)
                cnt = cnt + jnp.where(hit, 1.0, 0.0)
            return cnt

        group = 4

        def tiles(j, cnts):
            return tuple(tile(group * j + g, cnts[g]) for g in range(group))

        zero = jnp.zeros((sub, TQI), F32)
        cnts = lax.fori_loop(0, nkt // group, tiles, (zero,) * group)
        rest = lax.fori_loop(group * (nkt // group), nkt, tile, cnts[0])
        return jnp.sum(rest + sum(cnts[1:]), axis=0, keepdims=True)

    n_causal = (q_g + 1).astype(F32)
    all_sel = n_causal <= kf
    zeros = jnp.zeros((1, TQI), F32)
    state0 = dict(
        it=jnp.int32(0),
        lo=mn, hi=_key_float(_float_key(mx) + 1), clo=n_causal, chi=zeros,
        glo=jnp.log(jnp.maximum(n_causal, kf + 1.0) / kf), ghi=jnp.full((1, TQI), math.log(0.5 / kf), F32),
        thr=jnp.where(all_sel, NEG_BIG, 0.0).astype(F32),
        done=all_sel.astype(F32), tie=zeros, side=zeros, forced=zeros, use_forced=zeros,
    )

    def above(x):
        return jnp.where(jnp.abs(x) < F32_TINY, F32_TINY, _key_float(_float_key(x) + 1))

    def bracket_edges(lo, hi):
        def tile(kt, carry):
            a, b = carry
            for r in range(TKI // sub):
                blk = sc_ref[kt, r * sub:(r + 1) * sub, :]
                a = jnp.maximum(a, jnp.where(blk < hi, blk, -jnp.inf))
                b = jnp.minimum(b, jnp.where(blk >= lo, blk, jnp.inf))
            return a, b
        a, b = lax.fori_loop(0, nkt, tile, (jnp.full((sub, TQI), -jnp.inf, F32), jnp.full((sub, TQI), jnp.inf, F32)))
        return jnp.max(a, axis=0, keepdims=True), jnp.min(b, axis=0, keepdims=True)

    def cond(st):
        return jnp.logical_and(st["it"] < SEARCH_CAP, jnp.min(st["done"]) < 0.5)

    def count_step(st):
        it, lo, hi, glo, ghi = st["it"], st["lo"], st["hi"], st["glo"], st["ghi"]
        lo_k, hi_k = _float_key(lo), _float_key(hi)
        above_lo = above(lo)
        below_hi = jnp.where(hi == F32_TINY, 0.0, _key_float(hi_k - 1))
        probe = jnp.logical_and(st["done"] < 0.5, above_lo < hi)
        tie_now = jnp.logical_and(st["done"] < 0.5, above_lo >= hi)
        frac = jnp.where(it < 24, glo / (glo - ghi), 0.5)
        pf = lo + (hi - lo) * frac
        pf = jnp.where(it < 64, pf, _key_float((lo_k >> 1) + (hi_k >> 1) + (lo_k & hi_k & 1)))
        pf = jnp.where(it == 0, F32_TINY, jnp.where(jnp.logical_and(it == 1, hi == F32_TINY), 0.0, pf))
        pf = jnp.where(st["use_forced"] > 0.5, st["forced"], pf)
        pf = jnp.where(probe, jnp.minimum(jnp.maximum(pf, above_lo), below_hi), lo)
        c = count(pf, False)
        hit = jnp.logical_and(probe, c == kf)
        up = jnp.logical_and(probe, c > kf)
        dn = jnp.logical_and(probe, c < kf)
        g = jnp.log(jnp.maximum(c, 0.5) / kf)
        return dict(
            it=it + 1,
            lo=jnp.where(up, pf, lo), hi=jnp.where(dn, pf, hi),
            clo=jnp.where(up, c, st["clo"]), chi=jnp.where(dn, c, st["chi"]),
            glo=jnp.where(up, g, jnp.where(jnp.logical_and(dn, st["side"] < -0.5), 0.5 * glo, glo)),
            ghi=jnp.where(dn, g, jnp.where(jnp.logical_and(up, st["side"] > 0.5), 0.5 * ghi, ghi)),
            thr=jnp.where(hit, pf, jnp.where(tie_now, lo, st["thr"])),
            done=jnp.where(jnp.logical_or(hit, tie_now), 1.0, st["done"]),
            tie=jnp.where(tie_now, 1.0, st["tie"]),
            side=jnp.where(up, 1.0, jnp.where(dn, -1.0, st["side"])),
            forced=st["forced"], use_forced=zeros,
        )

    def edge_step(st):
        lo, hi = st["lo"], st["hi"]
        probe = jnp.logical_and(st["done"] < 0.5, above(lo) < hi)
        a, b = bracket_edges(lo, hi)
        one_above = st["chi"] == kf - 1.0
        one_below = st["clo"] == kf + 1.0
        new = dict(st)
        new.update(
            it=st["it"] + 1,
            lo=jnp.where(probe, b, lo), hi=jnp.where(probe, above(a), hi),
            forced=jnp.where(one_above, a, above(b)),
            use_forced=jnp.where(jnp.logical_and(probe, jnp.logical_or(one_above, one_below)), 1.0, 0.0),
        )
        return new

    def step(st):
        it = st["it"]
        is_edge = jnp.logical_and(it >= EDGE_FIRST, (it - EDGE_FIRST) % 3 == 0)
        return lax.cond(is_edge, edge_step, count_step, st)

    st = lax.fori_loop(0, WARM_PASSES, lambda _, s: count_step(s), state0)
    st = lax.while_loop(cond, step, st)
    thr = st["thr"]
    tie = st["tie"]
    any_tie = jnp.max(tie) > 0.5

    def emit(kt, sel):
        mask_ref[0, kt] = jnp.where(sel, 0.0, NEG_BIG).astype(mask_ref.dtype).T

    @pl.when(jnp.logical_not(any_tie))
    def _():
        def body(kt, carry):
            emit(kt, sc_ref[kt] >= thr)
            return carry

        def pair(j, carry):
            return body(2 * j + 1, body(2 * j, carry))
        lax.fori_loop(0, nkt // 2, pair, 0)
        lax.fori_loop(2 * (nkt // 2), nkt, body, 0)

    @pl.when(any_tie)
    def _():
        need = jnp.where(tie > 0.5, kf - count(thr, True), float(2 * ntile * TKI))
        r = lax.broadcasted_iota(jnp.int32, (TKI, TKI), 0)
        c = lax.broadcasted_iota(jnp.int32, (TKI, TKI), 1)
        prefix = jnp.where(c <= r, 1.0, 0.0).astype(BF16)

        def body(kt, seen):
            s = sc_ref[kt]
            eq = s == thr
            rank = seen + _dot(prefix, jnp.where(eq, 1.0, 0.0).astype(BF16))
            emit(kt, jnp.logical_or(s > thr, jnp.logical_and(eq, rank <= need)))
            return seen + jnp.sum(jnp.where(eq, 1.0, 0.0), axis=0, keepdims=True)
        lax.fori_loop(0, nkt, body, jnp.zeros((1, TQI), F32))

    def fill(kt, carry):
        mask_ref[0, kt] = jnp.full((TQI, TKI), NEG_BIG, mask_ref.dtype)
        return carry
    lax.fori_loop(nkt, ntile, fill, 0)


def _index_mask(qi, wt, ki2, top_k):
    S = qi.shape[0]
    nq, nk = S // TQI, S // TKI
    return pl.pallas_call(
        functools.partial(_index_kernel, top_k=top_k),
        grid=(nq,),
        in_specs=[pl.BlockSpec((TQI, IDX_HEADS * IDX_DIM), lambda i: (i, 0)),
                  pl.BlockSpec((IDX_HEADS, TQI), lambda i: (0, i)),
                  pl.BlockSpec((S, LANES), lambda i: (0, 0))],
        out_specs=pl.BlockSpec((1, nk, TQI, TKI), lambda i: (i, 0, 0, 0)),
        out_shape=jax.ShapeDtypeStruct((nq, nk, TQI, TKI), BF16),
        scratch_shapes=[pltpu.VMEM((nk, TKI, TQI), F32),
                        pltpu.VMEM((IDX_HEADS, TQI, LANES), BF16)],
        compiler_params=_params(("parallel",)),
        name="index_mask",
    )(qi, wt, ki2)


def _attn_kernel(qt_ref, kt_ref, q_ref, k_ref, v_ref, m_ref, bd_ref, be_ref, o_ref, *state):
    m_sc, acc_sc = state[:N_HEADS], state[N_HEADS:]
    step = pl.program_id(0)
    qi = qt_ref[step]
    ki = kt_ref[step]
    nsub = TA // SUB

    @pl.when(ki == 0)
    def _():
        for h in range(N_HEADS):
            m_sc[h][...] = jnp.full((TA, LANES), NEG_BIG, F32)
            acc_sc[h][...] = jnp.zeros((TA, LANES), F32)

    lane = lax.broadcasted_iota(jnp.int32, (TA, LANES), 1)
    first = lane < HEAD_DIM

    def bias_tile(h, diagonal):
        zero = jnp.zeros((SUB, SUB), F32)
        rows = []
        for a in range(nsub):
            if diagonal:
                blks = [bd_ref[h] if b == a else (be_ref[h] if b == a - 1 else zero) for b in range(nsub)]
            else:
                blks = [be_ref[h] if (a == 0 and b == nsub - 1) else zero for b in range(nsub)]
            rows.append(jnp.concatenate(blks, axis=1))
        return jnp.concatenate(rows, axis=0)

    def update(near, diagonal):
        maskf = jnp.concatenate([m_ref[a, 0] for a in range(TA // TQI)], axis=0).astype(F32)

        def logits(h):
            cols = slice((h // 2) * LANES, (h // 2 + 1) * LANES)
            qp = q_ref[:, cols]
            mine = first if h % 2 == 0 else jnp.logical_not(first)
            s = _dot_nt(jnp.where(mine, qp, jnp.zeros_like(qp)), k_ref[:, cols]) + maskf
            return s + bias_tile(h, diagonal) if near else s

        s = logits(0)
        for h in range(N_HEADS):
            s_next = logits(h + 1) if h + 1 < N_HEADS else None
            vp = v_ref[:, (h // 2) * LANES:(h // 2 + 1) * LANES]
            mine = first if h % 2 == 0 else jnp.logical_not(first)
            vh = jnp.where(mine, vp, jnp.ones_like(vp))
            m_prev = m_sc[h][...]
            m_next = jnp.maximum(m_prev, jnp.max(s, axis=1, keepdims=True))
            pexp = jnp.exp2(s - jnp.concatenate([m_next] * (TA // LANES), axis=1)).astype(BF16)
            acc_sc[h][...] = jnp.exp2(m_prev - m_next) * acc_sc[h][...] + _dot(pexp, vh)
            m_sc[h][...] = m_next
            s = s_next

    @pl.when(ki < qi - 1)
    def _():
        update(False, False)

    @pl.when(ki == qi - 1)
    def _():
        update(True, False)

    @pl.when(ki == qi)
    def _():
        update(True, True)
        for p in range(N_HEADS // 2):
            a0, a1 = acc_sc[2 * p][...], acc_sc[2 * p + 1][...]
            d0 = pltpu.roll(a0, HEAD_DIM, axis=1)
            d1 = pltpu.roll(a1, HEAD_DIM, axis=1)
            o_ref[:, p * LANES:(p + 1) * LANES] = jnp.where(first, a0 / d0, a1 / d1).astype(o_ref.dtype)


def _attention(q, k, v, mask4, bias_d, bias_e):
    S = q.shape[0]
    nb = S // TA
    pairs = [(a, b) for a in range(nb) for b in range(a + 1)]
    qtab = jnp.asarray(np.array([a for a, _ in pairs], np.int32))
    ktab = jnp.asarray(np.array([b for _, b in pairs], np.int32))
    grid_spec = pltpu.PrefetchScalarGridSpec(
        num_scalar_prefetch=2,
        grid=(len(pairs),),
        in_specs=[pl.BlockSpec((TA, ATTN_WIDTH), lambda s, qt, kt: (qt[s], 0)),
                  pl.BlockSpec((TA, ATTN_WIDTH), lambda s, qt, kt: (kt[s], 0)),
                  pl.BlockSpec((TA, ATTN_WIDTH), lambda s, qt, kt: (kt[s], 0)),
                  pl.BlockSpec((TA // TQI, TA // TKI, TQI, TKI), lambda s, qt, kt: (qt[s], kt[s], 0, 0)),
                  pl.BlockSpec(bias_d.shape, lambda s, qt, kt: (0, 0, 0)),
                  pl.BlockSpec(bias_e.shape, lambda s, qt, kt: (0, 0, 0))],
        out_specs=pl.BlockSpec((TA, ATTN_WIDTH), lambda s, qt, kt: (qt[s], 0)),
        scratch_shapes=[pltpu.VMEM((TA, LANES), F32)] * (2 * N_HEADS),
    )
    return pl.pallas_call(
        _attn_kernel,
        grid_spec=grid_spec,
        out_shape=jax.ShapeDtypeStruct((S, ATTN_WIDTH), BF16),
        compiler_params=_params(("arbitrary",)),
        name="attn",
    )(qtab, ktab, q, k, v, mask4, bias_d, bias_e)


def _relative_bias_blocks(rel_bias):
    dist = jnp.arange(2 * SUB, dtype=jnp.int32)
    max_exact = N_BUCKETS // 2
    dist_f = jnp.maximum(dist, 1).astype(F32)
    large = max_exact + (jnp.log(dist_f / max_exact) / math.log(MAX_DISTANCE / max_exact)
                         * (N_BUCKETS - max_exact)).astype(jnp.int32)
    bucket = jnp.where(dist < max_exact, dist, jnp.minimum(large, N_BUCKETS - 1))
    table = ((rel_bias[bucket] - rel_bias[N_BUCKETS - 1]) * LOG2E).astype(F32)
    period = 2 * SUB

    def toeplitz(first_row_index):
        z = table[first_row_index].T
        cut = jnp.tile(z, (1, SUB))[:, :SUB * (period - 1)].reshape(-1, SUB, period - 1)
        return cut[:, :, :SUB]

    m = np.arange(period)
    return toeplitz((period - m) % period), toeplitz((SUB - m) % period)


def _pack_bf16_pair(a, b):
    def rnd(x):
        bits = lax.bitcast_convert_type(x, jnp.uint32)
        return bits + jnp.uint32(0x7FFF) + ((bits >> 16) & jnp.uint32(1))
    return (rnd(a) >> 16) | (rnd(b) & jnp.uint32(0xFFFF0000))


def _unpack_bf16_pair(p):
    lo = lax.bitcast_convert_type(p << 16, F32)
    hi = lax.bitcast_convert_type(p & jnp.uint32(0xFFFF0000), F32)
    return lo.astype(BF16), hi.astype(BF16)


def _post_kernel(ya_ref, sga_ref, pc_ref, x_ref, woa_ref, wout_ref, g_ref, b_ref, wr_ref, wrl_ref, br_ref,
                 x1_ref, x1p_ref, ridx_ref, rgate_ref):
    y_attn = _dot(ya_ref[...], woa_ref[...])
    merged = sga_ref[...] * y_attn + pc_ref[...]
    mix = _dot(merged.astype(BF16), wout_ref[...])
    x1 = _layer_norm(DEEPNORM_ALPHA * x_ref[...] + mix, g_ref[...], b_ref[...])
    x1_ref[...] = x1
    half = D_MODEL // 2
    x1p_ref[:, 0, :] = _pack_bf16_pair(x1[:, :half], x1[:, half:])

    x1_hi = x1.astype(BF16)
    x1_lo = (x1 - x1_hi.astype(F32)).astype(BF16)
    logits = (_dot(x1_hi, wr_ref[...]) + (_dot(x1_hi, wrl_ref[...]) + _dot(x1_lo, wr_ref[...]))) + br_ref[...]
    lane = lax.broadcasted_iota(jnp.int32, logits.shape, 1).astype(F32)
    cur = logits
    vals, idxs = [], []
    for _ in range(TOP_K_EXPERTS):
        m = jnp.max(cur, axis=1, keepdims=True)
        ix = jnp.min(jnp.where(cur == m, lane, float(LANES)), axis=1, keepdims=True)
        vals.append(m)
        idxs.append(ix)
        cur = jnp.where(lane == ix, -jnp.inf, cur)
    exps = [jnp.exp(v - vals[0]) for v in vals]
    denom = exps[0]
    for e in exps[1:]:
        denom = denom + e
    ridx = jnp.zeros_like(logits)
    rgate = jnp.zeros_like(logits)
    for j in range(TOP_K_EXPERTS):
        ridx = jnp.where(lane == float(j), idxs[j], ridx)
        rgate = jnp.where(lane == float(j), exps[j] / denom, rgate)
    ridx_ref[...] = ridx.T[:ridx_ref.shape[0], :].astype(jnp.int32)
    rgate_ref[...] = rgate


def _post(ya, sg, pc, x, woa, wout, g, b, wr, wrl, br):
    S = x.shape[0]
    full = lambda a: pl.BlockSpec(a.shape, lambda i: (0, 0))
    row = lambda n: pl.BlockSpec((TM, n), lambda i: (i, 0))
    return pl.pallas_call(
        _post_kernel,
        grid=(S // TM,),
        in_specs=[row(ATTN_WIDTH), row(D_MODEL), row(D_MODEL), row(D_MODEL)] + [full(a) for a in (woa, wout, g, b, wr, wrl, br)],
        out_specs=[row(D_MODEL), pl.BlockSpec((TM, 1, D_MODEL // 2), lambda i: (i, 0, 0)),
                   pl.BlockSpec((ROUTE_ROWS, TM), lambda i: (0, i)), row(LANES)],
        out_shape=[jax.ShapeDtypeStruct((S, D_MODEL), F32), jax.ShapeDtypeStruct((S, 1, D_MODEL // 2), jnp.uint32),
                   jax.ShapeDtypeStruct((ROUTE_ROWS, S), jnp.int32), jax.ShapeDtypeStruct((S, LANES), F32)],
        compiler_params=_params(("parallel",)),
        name="post",
    )(ya, sg, pc, x, woa, wout, g, b, wr, wrl, br)


def _expert_kernel(te_ref, nv_ref, nused_ref, tok_ref, tokn_ref, dst_ref, x_hbm, wgu_ref, bgu_ref, wd_ref, bd_ref,
                   ys_hbm, xs_buf, y_buf, wgu_sc, wd_sc, gsem, ssem):
    t = pl.program_id(0)
    nused = nused_ref[0]
    cur = t % 2
    nxt = 1 - cur

    def start_gather(idx_ref, slot):
        def group(c, carry):
            for j in range(DMA_UNROLL):
                r = c * DMA_UNROLL + j
                pltpu.make_async_copy(x_hbm.at[idx_ref[0, 0, r]], xs_buf.at[slot, pl.ds(r, 1)],
                                      gsem.at[slot]).start(priority=j % 2)
            return carry
        lax.fori_loop(0, TME // DMA_UNROLL, group, 0)

    def wait_gather(slot):
        pltpu.make_async_copy(xs_buf.at[slot], xs_buf.at[slot], gsem.at[slot]).wait()

    def wait_scatter(slot, n):
        pltpu.make_async_copy(ys_hbm.at[pl.ds(0, n)], ys_hbm.at[pl.ds(0, n)], ssem.at[slot]).wait()

    @pl.when(t == 0)
    def _():
        start_gather(tok_ref, 0)

    @pl.when(t + 1 < nused)
    def _():
        start_gather(tokn_ref, nxt)

    e = te_ref[t]
    e_prev = te_ref[jnp.maximum(t - 1, 0)]

    @pl.when(jnp.logical_or(t == 0, e != e_prev))
    def _():
        wgu_sc[...] = wgu_ref[0].astype(BF16)
        wd_sc[...] = wd_ref[0].astype(BF16)

    @pl.when(t < nused)
    def _():
        wait_gather(cur)

        @pl.when(t >= 2)
        def _():
            wait_scatter(cur, nv_ref[jnp.maximum(t - 2, 0)])

        half = D_MODEL // 2
        lo, hi = _unpack_bf16_pair(xs_buf[cur])
        gu = _dot(lo, wgu_sc[0:half, :]) + _dot(hi, wgu_sc[half:, :]) + bgu_ref[0]
        g = jnp.minimum(gu[:, :D_EXPERT], SWIGLU_LIMIT)
        u = jnp.clip(gu[:, D_EXPERT:], -SWIGLU_LIMIT, SWIGLU_LIMIT)
        act = (u + 1.0) * (g * _sigmoid(SWIGLU_ALPHA * g))
        y_buf[cur] = _dot(act.astype(BF16), wd_sc[...]) + bd_ref[0]

        def scatter(r, carry, priority=0):
            pltpu.make_async_copy(y_buf.at[cur, pl.ds(r, 1)], ys_hbm.at[dst_ref[0, 0, r]],
                                  ssem.at[cur]).start(priority=priority)
            return carry

        def scatter_group(c, carry):
            for j in range(DMA_UNROLL):
                scatter(c * DMA_UNROLL + j, carry, priority=j % 2)
            return carry
        n_groups = nv_ref[t] // DMA_UNROLL
        lax.fori_loop(0, n_groups, scatter_group, 0)
        lax.fori_loop(n_groups * DMA_UNROLL, nv_ref[t], scatter, 0)

        @pl.when(t == nused - 1)
        def _():
            @pl.when(t >= 1)
            def _():
                wait_scatter(nxt, nv_ref[jnp.maximum(t - 1, 0)])
            wait_scatter(cur, nv_ref[t])


def _experts(tile_e, tile_nv, nused, row_token, row_dst, x1p, wgu, bgu, wd, bd, n_tiles, n_dst):
    idx = lambda a: a.reshape(n_tiles, 1, TME)
    smem_tile = lambda f: pl.BlockSpec((1, 1, TME), f, memory_space=pltpu.SMEM)
    grid_spec = pltpu.PrefetchScalarGridSpec(
        num_scalar_prefetch=3,
        grid=(n_tiles,),
        in_specs=[smem_tile(lambda t, te, nv, nu: (t, 0, 0)),
                  smem_tile(lambda t, te, nv, nu: (jnp.minimum(t + 1, n_tiles - 1), 0, 0)),
                  smem_tile(lambda t, te, nv, nu: (t, 0, 0)),
                  pl.BlockSpec(memory_space=pl.ANY),
                  pl.BlockSpec((1, D_MODEL, 2 * D_EXPERT), lambda t, te, nv, nu: (te[t], 0, 0)),
                  pl.BlockSpec((1, 1, 2 * D_EXPERT), lambda t, te, nv, nu: (te[t], 0, 0)),
                  pl.BlockSpec((1, D_EXPERT, D_MODEL), lambda t, te, nv, nu: (te[t], 0, 0)),
                  pl.BlockSpec((1, 1, D_MODEL), lambda t, te, nv, nu: (te[t], 0, 0))],
        out_specs=pl.BlockSpec(memory_space=pl.ANY),
        scratch_shapes=[pltpu.VMEM((2, TME, D_MODEL // 2), jnp.uint32), pltpu.VMEM((2, TME, D_MODEL), F32),
                        pltpu.VMEM((D_MODEL, 2 * D_EXPERT), BF16), pltpu.VMEM((D_EXPERT, D_MODEL), BF16),
                        pltpu.SemaphoreType.DMA((2,)), pltpu.SemaphoreType.DMA((2,))],
    )
    return pl.pallas_call(
        _expert_kernel,
        grid_spec=grid_spec,
        out_shape=jax.ShapeDtypeStruct((n_dst, 1, D_MODEL), F32),
        compiler_params=_params(("arbitrary",)),
        name="moe_experts",
    )(tile_e, tile_nv, nused, idx(row_token), idx(row_token), idx(row_dst), x1p, wgu, bgu, wd, bd)


def _routing_tables(ridx, S):
    n_flat = S * TOP_K_EXPERTS
    n_tiles = n_flat // TME + N_EXPERTS
    flat_e = ridx.reshape(n_flat)
    assert N_EXPERTS * n_flat < 2 ** 31
    keys = jnp.sort(flat_e * n_flat + jnp.arange(n_flat, dtype=jnp.int32))
    order = keys % n_flat
    grp_bound = jnp.searchsorted(keys, jnp.arange(N_EXPERTS + 1, dtype=jnp.int32) * n_flat).astype(jnp.int32)
    grp_start = grp_bound[:-1]
    counts = grp_bound[1:] - grp_start
    padded = ((counts + TME - 1) // TME) * TME
    pad_end = jnp.cumsum(padded)
    pad_start = pad_end - padded
    nused = (pad_end[-1] // TME).astype(jnp.int32).reshape(1)
    tile_row0 = jnp.arange(n_tiles, dtype=jnp.int32) * TME
    tile_e = jnp.minimum(jnp.sum(tile_row0[:, None] >= pad_end[None, :], axis=1), N_EXPERTS - 1).astype(jnp.int32)
    tile_rank0 = tile_row0 - pad_start[tile_e]
    tile_nv = jnp.where(tile_row0 < pad_end[-1], jnp.clip(counts[tile_e] - tile_rank0, 0, TME), 0).astype(jnp.int32)
    within = jnp.arange(TME, dtype=jnp.int32)[None, :]
    valid = within < tile_nv[:, None]
    src_flat = lax.optimization_barrier(order[jnp.clip((grp_start[tile_e] + tile_rank0)[:, None] + within, 0, n_flat - 1)])
    row_token = jnp.where(valid, src_flat % S, 0).astype(jnp.int32)
    row_dst = jnp.where(valid, src_flat, 0).astype(jnp.int32)
    return tile_e, tile_nv, nused, row_token, row_dst, n_tiles


def _final_kernel(x1_ref, y0_ref, y1_ref, y2_ref, y3_ref, rg_ref, p_ref, wpg_ref, wpp_ref, g_ref, b_ref, o_ref):
    h = DEEPNORM_ALPHA * x1_ref[...]
    rg = rg_ref[...]
    for j, y_ref in enumerate((y0_ref, y1_ref, y2_ref, y3_ref)):
        h = h + rg[:, j:j + 1] * y_ref[:, 0, :]
    ple = _sigmoid(_dot(h.astype(BF16), wpg_ref[...])) * _dot(p_ref[...].astype(BF16), wpp_ref[...])
    o_ref[...] = _layer_norm(h + ple, g_ref[...], b_ref[...])


def _final(x1, ys, rgate, p, wpg, wpp, g, b):
    S = x1.shape[0]
    full = lambda a: pl.BlockSpec(a.shape, lambda i: (0, 0))
    row = lambda n: pl.BlockSpec((TM, n), lambda i: (i, 0))
    return pl.pallas_call(
        _final_kernel,
        grid=(S // TM,),
        in_specs=[row(D_MODEL)]
        + [pl.BlockSpec((TM, 1, D_MODEL), lambda i, j=j: (j * (S // TM) + i, 0, 0)) for j in range(TOP_K_EXPERTS)]
        + [row(LANES), row(PLE_DIM), full(wpg), full(wpp), full(g), full(b)],
        out_specs=row(D_MODEL),
        out_shape=jax.ShapeDtypeStruct((S, D_MODEL), F32),
        compiler_params=_params(("parallel",)),
        name="final",
    )(x1, ys, ys, ys, ys, rgate, p, wpg, wpp, g, b)


def _layer(x, p, w_in, b_in, w_o_attn, w_dw, b_dw, conv_ln_g, conv_ln_b, w_o_conv, w_out, ln1_g, ln1_b,
           w_router, b_router, w_gate_up, b_gate_up, w_down, b_down, w_ple_gate, w_ple_proj, ln2_g, ln2_b,
           rel_bias):
    S = x.shape[0]
    assert S % TM == 0 and S % TA == 0 and S % TKI == 0 and (S * TOP_K_EXPERTS) % TME == 0
    top_k = min(TOPK_MAX, S // 4)
    row2 = lambda a: a.reshape(1, -1).astype(F32)

    o_q, o_qi, o_ki, o_cv, o_g = 0, 3 * ATTN_WIDTH, 3 * ATTN_WIDTH + IDX_HEADS * IDX_DIM, \
        3 * ATTN_WIDTH + IDX_HEADS * IDX_DIM + IDX_DIM + IDX_HEADS, \
        3 * ATTN_WIDTH + IDX_HEADS * IDX_DIM + IDX_DIM + IDX_HEADS + 2 * CONV_CH
    kw_pad = LANES - (IDX_DIM + IDX_HEADS)
    wkw = jnp.pad(w_in[:, o_ki:o_cv], ((0, 0), (0, kw_pad)))
    bkw = jnp.pad(b_in[o_ki:o_cv], (0, kw_pad))
    q, k, v, qi, kw, u, sg = _proj(
        x, w_in[:, o_q:o_qi].astype(BF16), row2(b_in[o_q:o_qi]),
        w_in[:, o_qi:o_ki].astype(BF16), row2(b_in[o_qi:o_ki]),
        wkw.astype(BF16), row2(bkw),
        w_in[:, o_cv:o_g].astype(BF16), row2(b_in[o_cv:o_g]),
        w_in[:, o_g:].astype(BF16), row2(b_in[o_g:]))

    part_conv = _conv(u, sg, w_dw, row2(b_dw), row2(conv_ln_g), row2(conv_ln_b), w_o_conv.astype(BF16))

    ki = kw[:, :IDX_DIM].astype(BF16)
    mask4 = _index_mask(qi, kw[:, IDX_DIM:IDX_DIM + IDX_HEADS].T, jnp.concatenate([ki, ki], axis=1), top_k)
    bias_d, bias_e = _relative_bias_blocks(rel_bias)
    y_attn = _attention(q, k, v, mask4, bias_d, bias_e)

    wr = jnp.pad(w_router, ((0, 0), (0, LANES - N_EXPERTS)))
    wr_hi = wr.astype(BF16)
    wr_lo = (wr - wr_hi.astype(F32)).astype(BF16)
    br = jnp.pad(b_router, (0, LANES - N_EXPERTS), constant_values=-jnp.inf)
    x1, x1p, ridx, rgate = _post(y_attn, sg, part_conv, x, w_o_attn.astype(BF16), w_out.astype(BF16),
                                 row2(ln1_g), row2(ln1_b), wr_hi, wr_lo, row2(br))

    tile_e, tile_nv, nused, row_token, row_dst, n_tiles = _routing_tables(ridx[:TOP_K_EXPERTS], S)
    ys = _experts(tile_e, tile_nv, nused, row_token, row_dst, x1p, w_gate_up, b_gate_up.reshape(N_EXPERTS, 1, -1),
                  w_down, b_down.reshape(N_EXPERTS, 1, -1), n_tiles, S * TOP_K_EXPERTS)
    return _final(x1, ys, rgate, p, w_ple_gate.astype(BF16), w_ple_proj.astype(BF16), row2(ln2_g), row2(ln2_b))


def kernel(x, p, w_in, b_in, w_o_attn, w_dw, b_dw, conv_ln_g, conv_ln_b, w_o_conv, w_out, ln1_g, ln1_b, w_router, b_router, w_gate_up, b_gate_up, w_down, b_down, w_ple_gate, w_ple_proj, ln2_g, ln2_b, rel_bias):
    assert x.shape[0] == 1 and p.shape[0] == DEPTH
    out = _layer(x[0], p[0, 0], w_in[0], b_in[0], w_o_attn[0], w_dw[0], b_dw[0], conv_ln_g[0], conv_ln_b[0],
                 w_o_conv[0], w_out[0], ln1_g[0], ln1_b[0], w_router[0], b_router[0], w_gate_up[0], b_gate_up[0],
                 w_down[0], b_down[0], w_ple_gate[0], w_ple_proj[0], ln2_g[0], ln2_b[0], rel_bias)
    return out[None]
```
